```python
import jax, jax.numpy as jnp
from jax import lax
import numpy as np

D_MODEL = 1024
BATCH = 16
SEQ = 2048
DEPTH = 1

HEAD_DIM = 64
N_SLOTS = 8
DILATED_GROUPS = ((128, 1), (512, 4), (2048, 16))
N_ATTN_HEADS = N_SLOTS * len(DILATED_GROUPS)
ATTN_WIDTH = N_ATTN_HEADS * HEAD_DIM
ATTN_OUT_WIDTH = N_SLOTS * HEAD_DIM
LRU_WIDTH = D_MODEL
LRU_BLOCKS = 16
LRU_BLOCK_DIM = LRU_WIDTH // LRU_BLOCKS
CONV_WIDTH = 4
LRU_C = 8.0
N_BRANCHES = 2
IN_COLS = 3 * ATTN_WIDTH + 2 * LRU_WIDTH + N_BRANCHES * D_MODEL
IN_SPLITS = [ATTN_WIDTH, 2 * ATTN_WIDTH, 3 * ATTN_WIDTH,
             3 * ATTN_WIDTH + LRU_WIDTH, 3 * ATTN_WIDTH + 2 * LRU_WIDTH]
N_EXPERT_GROUPS = 4
EXPERTS_PER_GROUP = 8
N_EXPERTS = N_EXPERT_GROUPS * EXPERTS_PER_GROUP
TOP_K_INNER = 2
D_EXPERT = D_MODEL // 2
MOE_BLOCK = 128
EPS = 1e-6

kernel_name = "hybrid_dilated_attn_rglru_hmoe_adaln"


def rms_norm(x, g):
    xf = x.astype(jnp.float32)
    y = xf * lax.rsqrt(jnp.mean(xf * xf, axis=-1, keepdims=True) + EPS)
    return (y * g.astype(jnp.float32)).astype(x.dtype)


def dilated_window_group(q, k, v, window, dilation):
    b, s, h, dh = q.shape
    span = window // dilation
    n_sub = s // dilation
    n_blk = -(-n_sub // span)
    pad = n_blk * span - n_sub

    def to_blocks(t):
        t = t.reshape(b, n_sub, dilation, h, dh)
        t = jnp.pad(t, ((0, 0), (0, pad), (0, 0), (0, 0), (0, 0)))
        return t.reshape(b, n_blk, span, dilation, h, dh)

    def with_prev(t):
        prev = jnp.pad(t, ((0, 0), (1, 0), (0, 0), (0, 0), (0, 0), (0, 0)))[:, :-1]
        return jnp.concatenate([prev, t], axis=2)

    qb = to_blocks(q)
    kk = with_prev(to_blocks(k))
    vv = with_prev(to_blocks(v))
    scores = jnp.einsum('bnqphd,bnkphd->bnqphk', qb, kk,
                        preferred_element_type=jnp.float32) * (dh ** -0.5)
    qi = jnp.arange(span)[:, None]
    ki = jnp.arange(2 * span)[None, :]
    dist = span + qi - ki
    band = (dist >= 0) & (dist <= span)
    blk = jnp.arange(n_blk)[:, None, None]
    valid = band[None] & ((blk > 0) | (ki[None] >= span))
    scores = jnp.where(valid[None, :, :, None, None, :], scores, -jnp.inf)
    m = jnp.max(scores, axis=-1, keepdims=True)
    e = jnp.exp(scores - m)
    den = jnp.sum(e, axis=-1)
    num = jnp.einsum('bnqphk,bnkphd->bnqphd', e, vv.astype(jnp.float32))
    out = num / den[..., None]
    lse = m[..., 0] + jnp.log(den)
    out = out.reshape(b, n_blk * span, dilation, h, dh)[:, :n_sub].reshape(b, s, h, dh)
    lse = lse.reshape(b, n_blk * span, dilation, h)[:, :n_sub].reshape(b, s, h)
    return out, lse


def dilated_attention(q, k, v):
    outs, lses = [], []
    for g, (window, dilation) in enumerate(DILATED_GROUPS):
        sl = slice(g * N_SLOTS, (g + 1) * N_SLOTS)
        o, l = dilated_window_group(q[:, :, sl], k[:, :, sl], v[:, :, sl], window, dilation)
        outs.append(o)
        lses.append(l)
    out = jnp.stack(outs, axis=0)
    wts = jax.nn.softmax(jnp.stack(lses, axis=0), axis=0)
    return jnp.sum(wts[..., None] * out, axis=0)


def causal_depthwise_conv(x, w, b):
    c = x.shape[-1]
    y = lax.conv_general_dilated(x, w[:, None, :].astype(x.dtype), window_strides=(1,),
                                 padding=[(CONV_WIDTH - 1, 0)],
                                 dimension_numbers=('NWC', 'WIO', 'NWC'),
                                 feature_group_count=c)
    return y + b


def block_diag(x, w, b):
    xb = x.reshape(*x.shape[:-1], LRU_BLOCKS, LRU_BLOCK_DIM)
    y = jnp.einsum('bsni,nio->bsno', xb, w.astype(x.dtype))
    return y.reshape(x.shape) + b.astype(x.dtype)


def rg_lru(x, wx, bx, wa, ba, lam):
    xf = x.astype(jnp.float32)
    gate_i = jax.nn.sigmoid(block_diag(xf, wx, bx))
    gate_r = jax.nn.sigmoid(block_diag(xf, wa, ba))
    log_a = -LRU_C * gate_r * jax.nn.softplus(-lam.astype(jnp.float32))
    a = jnp.exp(log_a)
    mult = jnp.sqrt(-jnp.expm1(2.0 * log_a))
    b_in = mult * gate_i * xf

    def combine(left, right):
        a1, b1 = left
        a2, b2 = right
        return a1 * a2, a2 * b1 + b2

    _, h = lax.associative_scan(combine, (a, b_in), axis=1)
    return h.astype(x.dtype)


def expert_ffn_blocks(t, expert_id, weights, w1, w3, w2):
    n_tok, d = t.shape
    flat_e = expert_id.reshape(-1)
    flat_w = weights.reshape(-1)
    n_assign = flat_e.shape[0]
    order = jnp.argsort(flat_e)
    e_sorted = flat_e[order]
    tok_sorted = order // TOP_K_INNER
    sizes = jnp.bincount(flat_e, length=N_EXPERTS)
    starts = jnp.cumsum(sizes) - sizes
    padded = (sizes + MOE_BLOCK - 1) // MOE_BLOCK * MOE_BLOCK
    pad_ends = jnp.cumsum(padded)
    pad_starts = pad_ends - padded
    dest = pad_starts[e_sorted] + jnp.arange(n_assign) - starts[e_sorted]
    n_blocks = (n_assign + N_EXPERTS * (MOE_BLOCK - 1) + MOE_BLOCK - 1) // MOE_BLOCK
    xp = jnp.zeros((n_blocks * MOE_BLOCK, d), t.dtype).at[dest].set(t[tok_sorted])
    blk_e = jnp.minimum(jnp.searchsorted(pad_ends, jnp.arange(n_blocks) * MOE_BLOCK,
                                         side='right'), N_EXPERTS - 1)

    def run(args):
        xb, e = args
        hb = jax.nn.silu(xb @ w1[e]) * (xb @ w3[e])
        return hb @ w2[e]

    yp = lax.map(run, (xp.reshape(n_blocks, MOE_BLOCK, d), blk_e)).reshape(-1, d)
    ys = yp[dest] * flat_w[order][:, None].astype(yp.dtype)
    return jnp.zeros((n_tok, d), yp.dtype).at[tok_sorted].add(ys)


def hierarchical_moe(h, w_grp, b_grp, w_exp, b_exp, w1, w3, w2):
    b, s, d = h.shape
    t = h.reshape(b * s, d)
    grp_logits = (t @ w_grp + b_grp).astype(jnp.float32)
    grp_prob = jax.nn.softmax(grp_logits, axis=-1)
    grp_idx = jnp.argmax(grp_logits, axis=-1)
    grp_gate = jnp.take_along_axis(grp_prob, grp_idx[:, None], axis=-1)
    exp_logits = (t @ w_exp + b_exp).astype(jnp.float32).reshape(-1, N_EXPERT_GROUPS, EXPERTS_PER_GROUP)
    in_grp = jnp.take_along_axis(exp_logits, grp_idx[:, None, None], axis=1)[:, 0]
    top_val, top_idx = lax.top_k(in_grp, TOP_K_INNER)
    weights = grp_gate * jax.nn.softmax(top_val, axis=-1)
    expert_id = grp_idx[:, None].astype(jnp.int32) * EXPERTS_PER_GROUP + top_idx.astype(jnp.int32)
    out = expert_ffn_blocks(t, expert_id, weights, w1, w3, w2)
    return out.reshape(b, s, d)


def setup_inputs(seed: int = 0) -> dict:
    key = jax.random.key(seed)
    ks = jax.random.split(key, 25)
    f32 = jnp.float32
    L, D = DEPTH, D_MODEL

    def nrm(k, shape, scale):
        return jax.random.normal(k, shape, f32) * scale

    a0 = jax.random.uniform(ks[12], (L, LRU_WIDTH), f32, 0.9, 0.999)
    return {
        "x": nrm(ks[0], (BATCH, SEQ, D), 1.0),
        "c": nrm(ks[1], (BATCH, D), 1.0),
        "w_mod": nrm(ks[2], (L, D, 6 * D), 0.1 * D ** -0.5),
        "b_mod": nrm(ks[3], (L, 6 * D), 0.01),
        "norm1_g": 1.0 + nrm(ks[4], (L, D), 0.02),
        "w_in": nrm(ks[5], (L, D, IN_COLS), D ** -0.5),
        "conv_w": nrm(ks[6], (L, CONV_WIDTH, LRU_WIDTH), CONV_WIDTH ** -0.5),
        "conv_b": nrm(ks[7], (L, LRU_WIDTH), 0.01),
        "lru_wx": nrm(ks[8], (L, LRU_BLOCKS, LRU_BLOCK_DIM, LRU_BLOCK_DIM), LRU_BLOCK_DIM ** -0.5),
        "lru_bx": nrm(ks[9], (L, LRU_WIDTH), 0.01),
        "lru_wa": nrm(ks[10], (L, LRU_BLOCKS, LRU_BLOCK_DIM, LRU_BLOCK_DIM), LRU_BLOCK_DIM ** -0.5),
        "lru_ba": nrm(ks[11], (L, LRU_WIDTH), 0.01),
        "lru_lambda": jnp.log(a0) - jnp.log1p(-a0),
        "w_attn_o": nrm(ks[13], (L, ATTN_OUT_WIDTH, D), ATTN_OUT_WIDTH ** -0.5),
        "w_lru_o": nrm(ks[14], (L, LRU_WIDTH, D), LRU_WIDTH ** -0.5),
        "w_out": nrm(ks[15], (L, D, D), D ** -0.5),
        "norm2_g": 1.0 + nrm(ks[16], (L, D), 0.02),
        "w_grp": nrm(ks[17], (L, D, N_EXPERT_GROUPS), D ** -0.5),
        "b_grp": nrm(ks[18], (L, N_EXPERT_GROUPS), 0.01),
        "w_exp": nrm(ks[19], (L, D, N_EXPERTS), D ** -0.5),
        "b_exp": nrm(ks[20], (L, N_EXPERTS), 0.01),
        "w1": nrm(ks[21], (L, N_EXPERTS, D, D_EXPERT), D ** -0.5),
        "w3": nrm(ks[22], (L, N_EXPERTS, D, D_EXPERT), D ** -0.5),
        "w2": nrm(ks[23], (L, N_EXPERTS, D_EXPERT, D), D_EXPERT ** -0.5),
        "norm_f_g": 1.0 + nrm(ks[24], (D,), 0.02),
    }


def reference(x, c, w_mod, b_mod, norm1_g, w_in, conv_w, conv_b, lru_wx, lru_bx, lru_wa,
              lru_ba, lru_lambda, w_attn_o, w_lru_o, w_out, norm2_g, w_grp, b_grp, w_exp,
              b_exp, w1, w3, w2, norm_f_g):
    b, s, _ = x.shape
    c_act = jax.nn.silu(c)
    for l in range(DEPTH):
        mod = c_act @ w_mod[l] + b_mod[l]
        shift1, scale1, gate1, shift2, scale2, gate2 = jnp.split(mod[:, None, :], 6, axis=-1)

        h = rms_norm(x, norm1_g[l]) * (1.0 + scale1) + shift1
        proj = h @ w_in[l]
        q, k, v, xr, yr, gl = jnp.split(proj, IN_SPLITS, axis=-1)
        q = q.reshape(b, s, N_ATTN_HEADS, HEAD_DIM)
        k = k.reshape(b, s, N_ATTN_HEADS, HEAD_DIM)
        v = v.reshape(b, s, N_ATTN_HEADS, HEAD_DIM)
        attn = dilated_attention(q, k, v).reshape(b, s, ATTN_OUT_WIDTH).astype(x.dtype)

        xr = causal_depthwise_conv(xr, conv_w[l], conv_b[l])
        lru = rg_lru(xr, lru_wx[l], lru_bx[l], lru_wa[l], lru_ba[l], lru_lambda[l]) * jax.nn.gelu(yr)

        gates = jax.nn.sigmoid(gl.astype(jnp.float32)).astype(x.dtype)
        g_attn, g_lru = jnp.split(gates, N_BRANCHES, axis=-1)
        mixed = g_attn * (attn @ w_attn_o[l]) + g_lru * (lru @ w_lru_o[l])
        x = x + (1.0 + gate1) * (mixed @ w_out[l])

        h = rms_norm(x, norm2_g[l]) * (1.0 + scale2) + shift2
        x = x + (1.0 + gate2) * hierarchical_moe(h, w_grp[l], b_grp[l], w_exp[l], b_exp[l],
                                                 w1[l], w3[l], w2[l])
    return rms_norm(x, norm_f_g)
```

```python
import functools

import jax
import jax.numpy as jnp
from jax import lax
from jax.experimental import pallas as pl
from jax.experimental.pallas import tpu as pltpu

F32 = jnp.float32
BF16 = jnp.bfloat16

D_MODEL = 1024
HEAD_DIM = 64
N_SLOTS = 8
SPAN = 128
DILATIONS = (1, 4, 16)
GROUP_COLS = 3 * N_SLOTS * HEAD_DIM
ATTN_WIDTH = len(DILATIONS) * N_SLOTS * HEAD_DIM
ATTN_OUT = N_SLOTS * HEAD_DIM
LRU_WIDTH = D_MODEL
LRU_BLOCK_DIM = 64
CONV_WIDTH = 4
LRU_C = 8.0
N_GROUPS = 4
EXPERTS_PER_GROUP = 8
N_EXPERTS = N_GROUPS * EXPERTS_PER_GROUP
D_EXPERT = D_MODEL // 2
EPS = 1e-6
LANE = 128
VMEM_LIMIT = 56 * 1024 * 1024


def _cparams(sem, vmem=None):
    return pltpu.CompilerParams(dimension_semantics=sem, vmem_limit_bytes=vmem)


def _resident(shape):
    nd = len(shape)
    return pl.BlockSpec(shape, lambda *_: (0,) * nd, pipeline_mode=pl.Buffered(1))


def _mod_kernel(c_ref, w_ref, b_ref, o_ref):
    c = c_ref[...]
    ca = c * jax.nn.sigmoid(c)
    o_ref[...] = jnp.dot(ca.astype(BF16), w_ref[...].astype(BF16),
                         preferred_element_type=F32) + b_ref[...]


def _modulation(c, w_mod, b_mod):
    b, d = c.shape
    n = w_mod.shape[1]
    tn = n // 4
    return pl.pallas_call(
        _mod_kernel,
        grid=(n // tn,),
        in_specs=[pl.BlockSpec((b, d), lambda j: (0, 0)),
                  pl.BlockSpec((d, tn), lambda j: (0, j)),
                  pl.BlockSpec((1, tn), lambda j: (0, j))],
        out_specs=pl.BlockSpec((b, tn), lambda j: (0, j)),
        out_shape=jax.ShapeDtypeStruct((b, n), F32),
        compiler_params=_cparams(("arbitrary",)),
        name="modulation",
    )(c, w_mod, b_mod.reshape(1, n))


def _rms_mod(x, g, scale, shift):
    ms = jnp.mean(x * x, axis=-1, keepdims=True)
    return x * lax.rsqrt(ms + EPS) * g * (1.0 + scale) + shift


def _proj_kernel(x_ref, mod_ref, g_ref, w_ref, qkv0_ref, qkv1_ref, qkv2_ref,
                 xr_ref, yr_ref, gl_ref, hs_ref, *, tm):
    d_model = x_ref.shape[-1]
    m = mod_ref[0]
    h = _rms_mod(x_ref[0], g_ref[...], m[:, d_model:2 * d_model], m[:, 0:d_model])

    def mm(hv, lo, hi):
        return jnp.dot(hv, w_ref[:, lo:hi], preferred_element_type=F32)

    hb = h.astype(BF16)
    c0 = len(DILATIONS) * GROUP_COLS
    qkv0_ref[0] = mm(hb, 0, GROUP_COLS).astype(BF16)
    xr_ref[0] = mm(hb, c0, c0 + LRU_WIDTH).astype(BF16)
    yr_ref[0] = mm(hb, c0 + LRU_WIDTH, c0 + 2 * LRU_WIDTH).astype(BF16)
    gl_ref[0] = mm(hb, c0 + 2 * LRU_WIDTH, c0 + 2 * LRU_WIDTH + 2 * d_model).astype(BF16)

    n_slab = d_model // LANE
    for j in range(n_slab):
        hs_ref[j] = h[:, j * LANE:(j + 1) * LANE]
    for g, out_ref in ((1, qkv1_ref), (2, qkv2_ref)):
        d = DILATIONS[g]
        rows = tm // d
        hp = jnp.concatenate(
            [jnp.concatenate([hs_ref[j, pl.ds(p, rows, stride=d), :] for j in range(n_slab)], axis=1)
             for p in range(d)], axis=0).astype(BF16)
        res = mm(hp, g * GROUP_COLS, (g + 1) * GROUP_COLS).astype(BF16)
        for p in range(d):
            out_ref[p] = res[p * rows:(p + 1) * rows]


def _projection(x, mod3, g1, w_r, *, tm=256):
    b, s, d = x.shape
    n = w_r.shape[1]
    assert s % tm == 0 and tm % (16 * DILATIONS[-1]) == 0
    out_shape = [jax.ShapeDtypeStruct((b * dd, s // dd, GROUP_COLS), BF16) for dd in DILATIONS]
    out_shape += [jax.ShapeDtypeStruct((b, s, LRU_WIDTH), BF16),
                  jax.ShapeDtypeStruct((b, s, LRU_WIDTH), BF16),
                  jax.ShapeDtypeStruct((b, s, 2 * d), BF16)]
    out_specs = [pl.BlockSpec((dd, tm // dd, GROUP_COLS), lambda bi, i: (bi, i, 0)) for dd in DILATIONS]
    out_specs += [pl.BlockSpec((1, tm, LRU_WIDTH), lambda bi, i: (bi, i, 0)),
                  pl.BlockSpec((1, tm, LRU_WIDTH), lambda bi, i: (bi, i, 0)),
                  pl.BlockSpec((1, tm, 2 * d), lambda bi, i: (bi, i, 0))]
    return pl.pallas_call(
        functools.partial(_proj_kernel, tm=tm),
        grid=(b, s // tm),
        in_specs=[pl.BlockSpec((1, tm, d), lambda bi, i: (bi, i, 0)),
                  pl.BlockSpec((1, 1, mod3.shape[-1]), lambda bi, i: (bi, 0, 0)),
                  pl.BlockSpec((1, d), lambda bi, i: (0, 0)),
                  _resident((d, n))],
        out_specs=out_specs,
        out_shape=out_shape,
        scratch_shapes=[pltpu.VMEM((d // LANE, tm, LANE), F32)],
        compiler_params=_cparams(("parallel", "parallel"), VMEM_LIMIT),
        name="projection",
    )(x, mod3, g1, w_r)


def _attn_kernel(q0, k0, v0, q1, k1, v1, q2, k2, v2, o_ref, acc_ref, lse_ref, *, seq):
    hcols = o_ref.shape[-1]
    n_head = hcols // HEAD_DIM
    head_of_lane = lax.broadcasted_iota(jnp.int32, (SPAN, hcols), 1) // HEAD_DIM
    head_mask_f = [jnp.where(head_of_lane == h, 1.0, 0.0).astype(F32) for h in range(n_head)]
    head_mask_b = [mk.astype(BF16) for mk in head_mask_f]
    qi = lax.broadcasted_iota(jnp.int32, (n_head * SPAN, 2 * SPAN), 0) % SPAN
    ki = lax.broadcasted_iota(jnp.int32, (n_head * SPAN, 2 * SPAN), 1)
    band = (ki >= qi) & (ki <= qi + SPAN)

    for g, (q_ref, k_ref, v_ref) in enumerate(((q0, k0, v0), (q1, k1, v1), (q2, k2, v2))):
        d = DILATIONS[g]
        n_blk = seq // d // SPAN

        def tile(n, carry, q_ref=q_ref, k_ref=k_ref, v_ref=v_ref, d=d, n_blk=n_blk, g=g):
            p = n // n_blk
            blk = n % n_blk
            r0 = pl.multiple_of(blk * SPAN, SPAN)
            rp = pl.multiple_of(jnp.maximum(blk - 1, 0) * SPAN, SPAN)
            q = q_ref[p, pl.ds(r0, SPAN), :]
            kk = jnp.concatenate([k_ref[p, pl.ds(rp, SPAN), :], k_ref[p, pl.ds(r0, SPAN), :]], axis=0)
            vv = jnp.concatenate([v_ref[p, pl.ds(rp, SPAN), :], v_ref[p, pl.ds(r0, SPAN), :]], axis=0)
            qs = jnp.concatenate([q * head_mask_b[h] for h in range(n_head)], axis=0)
            sc = lax.dot_general(qs, kk, (((1,), (1,)), ((), ())), preferred_element_type=F32)
            valid = band & (ki >= jnp.where(blk > 0, 0, SPAN))
            sc = jnp.where(valid, sc, -jnp.inf)
            mx = jnp.max(sc, axis=-1, keepdims=True)
            e = jnp.exp(sc - mx)
            den = jnp.sum(e, axis=-1, keepdims=True)
            pv = jnp.dot(e.astype(BF16), vv, preferred_element_type=F32) / den
            lse = mx + jnp.log(den)
            o = pv[0:SPAN] * head_mask_f[0]
            l = lse[0:SPAN] * head_mask_f[0]
            for h in range(1, n_head):
                o = o + pv[h * SPAN:(h + 1) * SPAN] * head_mask_f[h]
                l = l + lse[h * SPAN:(h + 1) * SPAN] * head_mask_f[h]
            start = p + d * r0
            for j in range(hcols // LANE):
                rows = pl.ds(start, SPAN, stride=d) if d > 1 else pl.ds(start, SPAN)
                acc_ref[g, j, rows, :] = o[:, j * LANE:(j + 1) * LANE]
                lse_ref[g, j, rows, :] = l[:, j * LANE:(j + 1) * LANE]
            return carry

        lax.fori_loop(0, seq // SPAN, tile, 0)

    chunk = 256

    def combine(c, carry):
        r = pl.multiple_of(c * chunk, chunk)
        for j in range(hcols // LANE):
            ls = [lse_ref[g, j, pl.ds(r, chunk), :] for g in range(len(DILATIONS))]
            mx = jnp.maximum(jnp.maximum(ls[0], ls[1]), ls[2])
            ws = [jnp.exp(v - mx) for v in ls]
            num = ws[0] * acc_ref[0, j, pl.ds(r, chunk), :]
            for g in range(1, len(DILATIONS)):
                num = num + ws[g] * acc_ref[g, j, pl.ds(r, chunk), :]
            o_ref[0, pl.ds(r, chunk), j * LANE:(j + 1) * LANE] = (num / (ws[0] + ws[1] + ws[2])).astype(BF16)
        return carry

    lax.fori_loop(0, seq // chunk, combine, 0)


def _attention(qkvs, b, s):
    hcols = 4 * HEAD_DIM
    n_hg = ATTN_OUT // hcols
    ncb = ATTN_OUT // hcols
    in_specs, args = [], []
    for g, d in enumerate(DILATIONS):
        for part in range(3):
            in_specs.append(pl.BlockSpec((d, s // d, hcols),
                                         lambda bi, hg, part=part: (bi, 0, part * ncb + hg)))
            args.append(qkvs[g])
    return pl.pallas_call(
        functools.partial(_attn_kernel, seq=s),
        grid=(b, n_hg),
        in_specs=in_specs,
        out_specs=pl.BlockSpec((1, s, hcols), lambda bi, hg: (bi, 0, hg)),
        out_shape=jax.ShapeDtypeStruct((b, s, ATTN_OUT), BF16),
        scratch_shapes=[pltpu.VMEM((len(DILATIONS), hcols // LANE, s, LANE), F32),
                        pltpu.VMEM((len(DILATIONS), hcols // LANE, s, LANE), F32)],
        compiler_params=_cparams(("parallel", "parallel"), VMEM_LIMIT),
        name="dilated_attention",
    )(*args)


def _gelu_tanh(y):
    return y * (0.5 * (1.0 + jnp.tanh(0.7978845608028654 * (y + 0.044715 * (y * y * y)))))


def _expm1(u):
    e = jnp.exp(u)
    em1 = e - 1.0
    near = jnp.abs(u) < 0.5
    safe_log = jnp.log(jnp.where(near & (e != 1.0), e, 2.0))
    return jnp.where(near, jnp.where(e == 1.0, u, em1 * u / safe_log), em1)


def _lru_kernel(xr_ref, yr_ref, cw_ref, cb_ref, wg_ref, bx_ref, ba_ref, lam_ref, o_ref,
                a_ref, b_ref):
    seq, tc = xr_ref.shape[1], xr_ref.shape[2]
    x = xr_ref[0].astype(F32)
    row = lax.broadcasted_iota(jnp.int32, (seq, tc), 0)
    cw = cw_ref[...]
    xc = x * cw[CONV_WIDTH - 1:CONV_WIDTH] + cb_ref[...]
    for k in range(1, CONV_WIDTH):
        xs = jnp.where(row >= k, pltpu.roll(x, k, axis=0), 0.0)
        xc = xc + xs * cw[CONV_WIDTH - 1 - k:CONV_WIDTH - k]
    gates = jnp.dot(xc.astype(BF16), wg_ref[0], preferred_element_type=F32)
    gate_i = jax.nn.sigmoid(gates[:, :tc] + bx_ref[...])
    gate_r = jax.nn.sigmoid(gates[:, tc:] + ba_ref[...])
    nl = -lam_ref[...]
    softplus = jnp.maximum(nl, 0.0) + jnp.log1p(jnp.exp(-jnp.abs(nl)))
    log_a = (-LRU_C) * gate_r * softplus
    a = jnp.exp(log_a)
    bv = jnp.sqrt(-_expm1(2.0 * log_a)) * gate_i * xc
    r8 = row % 8
    for k in (1, 2, 4):
        a_s = pltpu.roll(a, k, axis=0)
        b_s = pltpu.roll(bv, k, axis=0)
        take = r8 >= k
        bv = jnp.where(take, a * b_s + bv, bv)
        a = jnp.where(take, a * a_s, a)
    a_ref[...] = a
    b_ref[...] = bv

    def group(i, carry):
        r = pl.multiple_of(i * 8, 8)
        h = a_ref[pl.ds(r, 8), :] * carry + b_ref[pl.ds(r, 8), :]
        b_ref[pl.ds(r, 8), :] = h
        return jnp.broadcast_to(h[7:8, :], h.shape)

    lax.fori_loop(0, seq // 8, group, jnp.zeros((8, tc), F32), unroll=8)
    o_ref[0] = (b_ref[...] * _gelu_tanh(yr_ref[0].astype(F32))).astype(BF16)


def _lru_gate_weights(wx, wa, tc):
    nb, bd, _ = wx.shape
    per = tc // bd
    eye = jnp.eye(per, dtype=wx.dtype)

    def bdiag(w):
        w = w.reshape(nb // per, per, bd, bd)
        return jnp.einsum('cpio,pq->cpiqo', w, eye).reshape(nb // per, tc, tc)

    return jnp.concatenate([bdiag(wx), bdiag(wa)], axis=-1).astype(BF16)


def _lru_branch(xr, yr, conv_w, conv_b, wx, bx, wa, ba, lam, *, tc=256):
    b, s, c = xr.shape
    wg = _lru_gate_weights(wx, wa, tc)
    row = lambda v: v.reshape(1, c)
    tile = pl.BlockSpec((1, s, tc), lambda bi, ci: (bi, 0, ci))
    vec = pl.BlockSpec((1, tc), lambda bi, ci: (0, ci))
    return pl.pallas_call(
        _lru_kernel,
        grid=(b, c // tc),
        in_specs=[tile, tile,
                  pl.BlockSpec((CONV_WIDTH, tc), lambda bi, ci: (0, ci)), vec,
                  pl.BlockSpec((1, tc, 2 * tc), lambda bi, ci: (ci, 0, 0)),
                  vec, vec, vec],
        out_specs=tile,
        out_shape=jax.ShapeDtypeStruct((b, s, c), BF16),
        scratch_shapes=[pltpu.VMEM((s, tc), F32), pltpu.VMEM((s, tc), F32)],
        compiler_params=_cparams(("parallel", "parallel"), VMEM_LIMIT),
        name="rg_lru",
    )(xr, yr, conv_w, row(conv_b), wg, row(bx), row(ba), row(lam))


def _prep_w_in(w_in):
    a = ATTN_WIDTH
    gw = N_SLOTS * HEAD_DIM
    q = w_in[:, :a] * (HEAD_DIM ** -0.5)
    k = w_in[:, a:2 * a]
    v = w_in[:, 2 * a:3 * a]
    parts = []
    for g in range(len(DILATIONS)):
        sl = slice(g * gw, (g + 1) * gw)
        parts += [q[:, sl], k[:, sl], v[:, sl]]
    parts.append(w_in[:, 3 * a:])
    return jnp.concatenate(parts, axis=1).astype(BF16)


def _pack_rows(v):
    half = v.shape[-1] // 2
    lo = pltpu.bitcast(v[:, :half].astype(BF16).astype(F32), jnp.uint32)
    hi = pltpu.bitcast(v[:, half:].astype(BF16).astype(F32), jnp.uint32)
    return (hi & jnp.uint32(0xFFFF0000)) | (lo >> 16)


def _unpack_rows(p):
    lo = pltpu.bitcast(p << 16, F32)
    hi = pltpu.bitcast(p & jnp.uint32(0xFFFF0000), F32)
    return jnp.concatenate([lo, hi], axis=-1)


ROUTE_COLS = 8
EXPERT_LANE0 = N_GROUPS


def _mix_kernel(attn_ref, lru_ref, gl_ref, x_ref, mod_ref, wa_ref, wl_ref, wo_ref, g2_ref,
                wr_ref, br_ref, x1_ref, h2_ref, route_ref, cnt_ref, cnt_acc):
    d = x_ref.shape[-1]
    tm = x_ref.shape[1]

    @pl.when((pl.program_id(0) == 0) & (pl.program_id(1) == 0))
    def _():
        cnt_acc[...] = jnp.zeros_like(cnt_acc)

    m = mod_ref[0]
    gate1, shift2, scale2 = m[:, 2 * d:3 * d], m[:, 3 * d:4 * d], m[:, 4 * d:5 * d]
    ya = jnp.dot(attn_ref[0], wa_ref[...], preferred_element_type=F32)
    yl = jnp.dot(lru_ref[0], wl_ref[...], preferred_element_type=F32)
    glv = gl_ref[0].astype(F32)
    mixed = jax.nn.sigmoid(glv[:, :d]) * ya + jax.nn.sigmoid(glv[:, d:]) * yl
    y = jnp.dot(mixed.astype(BF16), wo_ref[...], preferred_element_type=F32)
    x1 = x_ref[0] + (1.0 + gate1) * y
    x1_ref[0] = x1
    h2 = _rms_mod(x1, g2_ref[...], scale2, shift2)
    h2_ref[0] = _pack_rows(h2)
    logits = jnp.dot(h2.astype(BF16), wr_ref[...], preferred_element_type=F32) + br_ref[...]

    lane = lax.broadcasted_iota(jnp.int32, logits.shape, 1)
    neg = -jnp.inf
    nl = logits.shape[-1]

    def top(vals):
        mx = jnp.max(vals, axis=-1, keepdims=True)
        idx = jnp.min(jnp.where(vals == mx, lane, nl), axis=-1, keepdims=True)
        return mx, idx

    is_grp = lane < N_GROUPS
    gmax, gidx = top(jnp.where(is_grp, logits, neg))
    grp_gate = 1.0 / jnp.sum(jnp.where(is_grp, jnp.exp(logits - gmax), 0.0), axis=-1, keepdims=True)
    lo = EXPERT_LANE0 + EXPERTS_PER_GROUP * gidx
    el = jnp.where((lane >= lo) & (lane < lo + EXPERTS_PER_GROUP), logits, neg)
    v1, i1 = top(el)
    v2, i2 = top(jnp.where(lane == i1, neg, el))
    e21 = jnp.exp(v2 - v1)
    wt1 = grp_gate / (1.0 + e21)
    wt2 = wt1 * e21

    oh1 = jnp.where(lane == i1, 1.0, 0.0)
    oh2 = jnp.where(lane == i2, 1.0, 0.0)
    ohs = oh1 + oh2
    rr = lax.broadcasted_iota(jnp.int32, (tm, tm), 0)
    cc = lax.broadcasted_iota(jnp.int32, (tm, tm), 1)
    earlier = jnp.where(cc < rr, 1.0, 0.0).astype(BF16)
    before = jnp.dot(earlier, ohs.astype(BF16), preferred_element_type=F32) + cnt_acc[...]
    rank1 = jnp.sum(oh1 * before, axis=-1, keepdims=True)
    rank2 = jnp.sum(oh2 * before, axis=-1, keepdims=True)
    cnt_acc[...] = cnt_acc[...] + jnp.sum(ohs, axis=0, keepdims=True)
    cnt_ref[...] = cnt_acc[...]

    cols = [(i1 - EXPERT_LANE0).astype(F32), (i2 - EXPERT_LANE0).astype(F32), rank1, rank2, wt1, wt2]
    slab = jnp.zeros(logits.shape, F32)
    for j, v in enumerate(cols):
        slab = jnp.where(lane == j, v, slab)
    route_ref[0] = slab[:, :ROUTE_COLS]


def _mix_route(attn, lru, gl, x, mod3, wa, wl, wo, g2, wr, br, *, tm=256):
    b, s, d = x.shape
    tok = lambda w: pl.BlockSpec((1, tm, w), lambda bi, i: (bi, i, 0))
    return pl.pallas_call(
        _mix_kernel,
        grid=(b, s // tm),
        in_specs=[tok(attn.shape[-1]), tok(d), tok(2 * d), tok(d),
                  pl.BlockSpec((1, 1, mod3.shape[-1]), lambda bi, i: (bi, 0, 0)),
                  _resident(wa.shape), _resident(wl.shape), _resident(wo.shape),
                  pl.BlockSpec((1, d), lambda bi, i: (0, 0)),
                  _resident(wr.shape),
                  pl.BlockSpec((1, LANE), lambda bi, i: (0, 0))],
        out_specs=[tok(d), tok(d // 2), tok(ROUTE_COLS),
                   pl.BlockSpec((1, LANE), lambda bi, i: (0, 0))],
        out_shape=[jax.ShapeDtypeStruct((b, s, d), F32),
                   jax.ShapeDtypeStruct((b, s, d // 2), jnp.uint32),
                   jax.ShapeDtypeStruct((b, s, ROUTE_COLS), F32),
                   jax.ShapeDtypeStruct((1, LANE), F32)],
        scratch_shapes=[pltpu.VMEM((1, LANE), F32)],
        compiler_params=_cparams(("arbitrary", "arbitrary"), VMEM_LIMIT),
        name="mix_route",
    )(attn, lru, gl, x, mod3, wa, wl, wo, g2, wr, br)


TOP_K = 2
EXPERT_BLOCK = 256


def _dispatch_kernel(dest_ref, h_ref, xp_in, xp_out, sem):
    del xp_in
    tm = h_ref.shape[0]

    def row_copy(r, k):
        return pltpu.make_async_copy(h_ref.at[pl.ds(r, 1)],
                                     xp_out.at[pl.ds(dest_ref[TOP_K * r + k], 1)], sem)

    def start(r, carry):
        for k in range(TOP_K):
            row_copy(r, k).start()
        return carry

    def wait(r, carry):
        for k in range(TOP_K):
            row_copy(r, k).wait()
        return carry

    lax.fori_loop(0, tm, start, 0)
    lax.fori_loop(0, tm, wait, 0)


def _dispatch(h2p, dest, n_slots, *, tm=512):
    t, w = h2p.shape
    xp0 = jnp.zeros((n_slots, w), h2p.dtype)
    return pl.pallas_call(
        _dispatch_kernel,
        grid=(t // tm,),
        in_specs=[pl.BlockSpec((TOP_K * tm,), lambda i: (i,), memory_space=pltpu.SMEM),
                  pl.BlockSpec((tm, w), lambda i: (i, 0)),
                  pl.BlockSpec(memory_space=pl.ANY)],
        out_specs=pl.BlockSpec(memory_space=pl.ANY),
        out_shape=jax.ShapeDtypeStruct((n_slots, w), h2p.dtype),
        scratch_shapes=[pltpu.SemaphoreType.DMA],
        input_output_aliases={2: 0},
        compiler_params=_cparams(("arbitrary",)),
        name="dispatch",
    )(dest, h2p, xp0)


def _expert_kernel(be_ref, nu_ref, x_ref, w1_ref, w3_ref, w2_ref, y_ref, wb1, wb3, wb2):
    j = pl.program_id(0)

    @pl.when(j < nu_ref[0])
    def _():
        @pl.when((j == 0) | (be_ref[j] != be_ref[jnp.maximum(j - 1, 0)]))
        def _():
            wb1[...] = w1_ref[0].astype(BF16)
            wb3[...] = w3_ref[0].astype(BF16)
            wb2[...] = w2_ref[0].astype(BF16)

        xb = _unpack_rows(x_ref[...]).astype(BF16)
        a = jnp.dot(xb, wb1[...], preferred_element_type=F32)
        g = jnp.dot(xb, wb3[...], preferred_element_type=F32)
        hm = (a * jax.nn.sigmoid(a) * g).astype(BF16)
        y_ref[...] = _pack_rows(jnp.dot(hm, wb2[...], preferred_element_type=F32))

    @pl.when(j >= nu_ref[0])
    def _():
        y_ref[...] = jnp.zeros_like(y_ref)


def _experts(xp, blk_e, n_used, w1, w3, w2):
    n_slots, w = xp.shape
    ne, d, de = w1.shape
    nb = n_slots // EXPERT_BLOCK
    last = lambda j, nu: jnp.minimum(j, nu[0] - 1)
    grid_spec = pltpu.PrefetchScalarGridSpec(
        num_scalar_prefetch=2,
        grid=(nb,),
        in_specs=[pl.BlockSpec((EXPERT_BLOCK, w), lambda j, be, nu: (last(j, nu), 0)),
                  pl.BlockSpec((1, d, de), lambda j, be, nu: (be[j], 0, 0)),
                  pl.BlockSpec((1, d, de), lambda j, be, nu: (be[j], 0, 0)),
                  pl.BlockSpec((1, de, d), lambda j, be, nu: (be[j], 0, 0))],
        out_specs=pl.BlockSpec((EXPERT_BLOCK, w), lambda j, be, nu: (j, 0)),
        scratch_shapes=[pltpu.VMEM((d, de), BF16), pltpu.VMEM((d, de), BF16), pltpu.VMEM((de, d), BF16)])
    return pl.pallas_call(
        _expert_kernel,
        grid_spec=grid_spec,
        out_shape=jax.ShapeDtypeStruct((n_slots, w), xp.dtype),
        compiler_params=_cparams(("arbitrary",), VMEM_LIMIT),
        name="experts",
    )(blk_e, n_used, xp, w1, w3, w2)


def _combine_kernel(dest_ref, route_ref, x1_ref, mod_ref, gf_ref, yp_hbm, o_ref, ybuf, sem):
    tm, d = x1_ref.shape[1], x1_ref.shape[2]

    def row_copy(r, k):
        return pltpu.make_async_copy(yp_hbm.at[pl.ds(dest_ref[TOP_K * r + k], 1)],
                                     ybuf.at[k, pl.ds(r, 1)], sem)

    def start(r, carry):
        for k in range(TOP_K):
            row_copy(r, k).start()
        return carry

    def wait(r, carry):
        for k in range(TOP_K):
            row_copy(r, k).wait()
        return carry

    lax.fori_loop(0, tm, start, 0)
    lax.fori_loop(0, tm, wait, 0)
    route = route_ref[0]
    moe = _unpack_rows(ybuf[0]) * route[:, 4:5] + _unpack_rows(ybuf[1]) * route[:, 5:6]
    gate2 = mod_ref[0][:, 5 * d:6 * d]
    xo = x1_ref[0] + (1.0 + gate2) * moe
    ms = jnp.mean(xo * xo, axis=-1, keepdims=True)
    o_ref[0] = xo * lax.rsqrt(ms + EPS) * gf_ref[...]


def _combine(dest, route, x1, mod3, gf, yp, *, tm=512):
    b, s, d = x1.shape
    spt = s // tm
    return pl.pallas_call(
        _combine_kernel,
        grid=(b, spt),
        in_specs=[pl.BlockSpec((TOP_K * tm,), lambda bi, i: (bi * spt + i,), memory_space=pltpu.SMEM),
                  pl.BlockSpec((1, tm, ROUTE_COLS), lambda bi, i: (bi, i, 0)),
                  pl.BlockSpec((1, tm, d), lambda bi, i: (bi, i, 0)),
                  pl.BlockSpec((1, 1, mod3.shape[-1]), lambda bi, i: (bi, 0, 0)),
                  pl.BlockSpec((1, d), lambda bi, i: (0, 0)),
                  pl.BlockSpec(memory_space=pl.ANY)],
        out_specs=pl.BlockSpec((1, tm, d), lambda bi, i: (bi, i, 0)),
        out_shape=jax.ShapeDtypeStruct((b, s, d), F32),
        scratch_shapes=[pltpu.VMEM((TOP_K, tm, d // 2), jnp.uint32), pltpu.SemaphoreType.DMA],
        compiler_params=_cparams(("arbitrary", "arbitrary")),
        name="combine",
    )(dest, route, x1, mod3, gf, yp)


def _slot_plan(route, counts, n_tok):
    sizes = counts[0, EXPERT_LANE0:EXPERT_LANE0 + N_EXPERTS].astype(jnp.int32)
    padded = (sizes + EXPERT_BLOCK - 1) // EXPERT_BLOCK * EXPERT_BLOCK
    pad_ends = jnp.cumsum(padded)
    pad_starts = pad_ends - padded
    eid = route[..., 0:TOP_K].astype(jnp.int32).reshape(n_tok, TOP_K)
    rank = route[..., TOP_K:2 * TOP_K].astype(jnp.int32).reshape(n_tok, TOP_K)
    dest = (pad_starts[eid] + rank).reshape(n_tok * TOP_K)
    n_blocks = (n_tok * TOP_K + N_EXPERTS * (EXPERT_BLOCK - 1) + EXPERT_BLOCK - 1) // EXPERT_BLOCK
    n_used = pad_ends[-1] // EXPERT_BLOCK
    blk = jnp.minimum(jnp.arange(n_blocks), n_used - 1)
    blk_e = jnp.minimum(jnp.searchsorted(pad_ends, blk * EXPERT_BLOCK, side='right'), N_EXPERTS - 1)
    return dest, blk_e.astype(jnp.int32), n_used.reshape(1).astype(jnp.int32), n_blocks


def kernel(x, c, w_mod, b_mod, norm1_g, w_in, conv_w, conv_b, lru_wx, lru_bx, lru_wa, lru_ba, lru_lambda, w_attn_o, w_lru_o, w_out, norm2_g, w_grp, b_grp, w_exp, b_exp, w1, w3, w2, norm_f_g):
    b, s, d = x.shape
    assert d == D_MODEL and s == SPAN * DILATIONS[-1] and w_mod.shape[0] == 1
    mod3 = _modulation(c, w_mod[0], b_mod[0]).reshape(b, 1, 6 * d)
    qkv0, qkv1, qkv2, xr, yr, gl = _projection(x, mod3, norm1_g[0].reshape(1, d), _prep_w_in(w_in[0]))
    attn = _attention((qkv0, qkv1, qkv2), b, s)
    lru = _lru_branch(xr, yr, conv_w[0], conv_b[0], lru_wx[0], lru_bx[0], lru_wa[0], lru_ba[0], lru_lambda[0])

    n_router = N_GROUPS + N_EXPERTS
    wr = jnp.pad(jnp.concatenate([w_grp[0], w_exp[0]], axis=1), ((0, 0), (0, LANE - n_router))).astype(BF16)
    br = jnp.pad(jnp.concatenate([b_grp[0], b_exp[0]]), (0, LANE - n_router)).reshape(1, LANE)
    x1, h2p, route, counts = _mix_route(
        attn, lru, gl, x, mod3, w_attn_o[0].astype(BF16), w_lru_o[0].astype(BF16), w_out[0].astype(BF16),
        norm2_g[0].reshape(1, d), wr, br)

    n_tok = b * s
    dest, blk_e, n_used, n_blocks = _slot_plan(route, counts, n_tok)
    xp = _dispatch(h2p.reshape(n_tok, d // 2), dest, n_blocks * EXPERT_BLOCK)
    yp = _experts(xp, blk_e, n_used, w1[0], w3[0], w2[0])
    return _combine(dest, route, x1, mod3, norm_f_g.reshape(1, d), yp)
```

```python
import functools

import jax
import jax.numpy as jnp
from jax import lax
from jax.experimental import pallas as pl
from jax.experimental.pallas import tpu as pltpu

F32 = jnp.float32
BF16 = jnp.bfloat16

D_MODEL = 1024
HEAD_DIM = 64
N_SLOTS = 8
SPAN = 128
DILATIONS = (1, 4, 16)
GROUP_COLS = 3 * N_SLOTS * HEAD_DIM
ATTN_WIDTH = len(DILATIONS) * N_SLOTS * HEAD_DIM
ATTN_OUT = N_SLOTS * HEAD_DIM
LRU_WIDTH = D_MODEL
LRU_BLOCK_DIM = 64
CONV_WIDTH = 4
LRU_C = 8.0
N_GROUPS = 4
EXPERTS_PER_GROUP = 8
N_EXPERTS = N_GROUPS * EXPERTS_PER_GROUP
D_EXPERT = D_MODEL // 2
EPS = 1e-6
LANE = 128
VMEM_LIMIT = 56 * 1024 * 1024


def _cparams(sem, vmem=None):
    return pltpu.CompilerParams(dimension_semantics=sem, vmem_limit_bytes=vmem)


def _resident(shape):
    nd = len(shape)
    return pl.BlockSpec(shape, lambda *_: (0,) * nd, pipeline_mode=pl.Buffered(1))


def _mod_kernel(c_ref, w_ref, b_ref, o_ref):
    c = c_ref[...]
    ca = c * jax.nn.sigmoid(c)
    o_ref[...] = jnp.dot(ca.astype(BF16), w_ref[...].astype(BF16),
                         preferred_element_type=F32) + b_ref[...]


def _modulation(c, w_mod, b_mod):
    b, d = c.shape
    n = w_mod.shape[1]
    tn = n // 4
    return pl.pallas_call(
        _mod_kernel,
        grid=(n // tn,),
        in_specs=[pl.BlockSpec((b, d), lambda j: (0, 0)),
                  pl.BlockSpec((d, tn), lambda j: (0, j)),
                  pl.BlockSpec((1, tn), lambda j: (0, j))],
        out_specs=pl.BlockSpec((b, tn), lambda j: (0, j)),
        out_shape=jax.ShapeDtypeStruct((b, n), F32),
        compiler_params=_cparams(("arbitrary",)),
        name="modulation",
    )(c, w_mod, b_mod.reshape(1, n))


def _rms_mod(x, g, scale, shift):
    ms = jnp.mean(x * x, axis=-1, keepdims=True)
    return x * lax.rsqrt(ms + EPS) * g * (1.0 + scale) + shift


def _proj_kernel(x_ref, mod_ref, g_ref, w_ref, qkv0_ref, qkv1_ref, qkv2_ref,
                 xr_ref, yr_ref, gl_ref, hs_ref, *, tm):
    d_model = x_ref.shape[-1]
    m = mod_ref[0]
    h = _rms_mod(x_ref[0], g_ref[...], m[:, d_model:2 * d_model], m[:, 0:d_model])

    def mm(hv, lo, hi):
        return jnp.dot(hv, w_ref[:, lo:hi], preferred_element_type=F32)

    hb = h.astype(BF16)
    c0 = len(DILATIONS) * GROUP_COLS
    qkv0_ref[0] = mm(hb, 0, GROUP_COLS).astype(BF16)
    xr_ref[0] = mm(hb, c0, c0 + LRU_WIDTH).astype(BF16)
    yr_ref[0] = mm(hb, c0 + LRU_WIDTH, c0 + 2 * LRU_WIDTH).astype(BF16)
    gl_ref[0] = mm(hb, c0 + 2 * LRU_WIDTH, c0 + 2 * LRU_WIDTH + 2 * d_model).astype(BF16)

    n_slab = d_model // LANE
    for j in range(n_slab):
        hs_ref[j] = h[:, j * LANE:(j + 1) * LANE]
    for g, out_ref in ((1, qkv1_ref), (2, qkv2_ref)):
        d = DILATIONS[g]
        rows = tm // d
        hp = jnp.concatenate(
            [jnp.concatenate([hs_ref[j, pl.ds(p, rows, stride=d), :] for j in range(n_slab)], axis=1)
             for p in range(d)], axis=0).astype(BF16)
        res = mm(hp, g * GROUP_COLS, (g + 1) * GROUP_COLS).astype(BF16)
        for p in range(d):
            out_ref[p] = res[p * rows:(p + 1) * rows]


def _projection(x, mod3, g1, w_r, *, tm=256):
    b, s, d = x.shape
    n = w_r.shape[1]
    assert s % tm == 0 and tm % (16 * DILATIONS[-1]) == 0
    out_shape = [jax.ShapeDtypeStruct((b * dd, s // dd, GROUP_COLS), BF16) for dd in DILATIONS]
    out_shape += [jax.ShapeDtypeStruct((b, s, LRU_WIDTH), BF16),
                  jax.ShapeDtypeStruct((b, s, LRU_WIDTH), BF16),
                  jax.ShapeDtypeStruct((b, s, 2 * d), BF16)]
    out_specs = [pl.BlockSpec((dd, tm // dd, GROUP_COLS), lambda bi, i: (bi, i, 0)) for dd in DILATIONS]
    out_specs += [pl.BlockSpec((1, tm, LRU_WIDTH), lambda bi, i: (bi, i, 0)),
                  pl.BlockSpec((1, tm, LRU_WIDTH), lambda bi, i: (bi, i, 0)),
                  pl.BlockSpec((1, tm, 2 * d), lambda bi, i: (bi, i, 0))]
    return pl.pallas_call(
        functools.partial(_proj_kernel, tm=tm),
        grid=(b, s // tm),
        in_specs=[pl.BlockSpec((1, tm, d), lambda bi, i: (bi, i, 0)),
                  pl.BlockSpec((1, 1, mod3.shape[-1]), lambda bi, i: (bi, 0, 0)),
                  pl.BlockSpec((1, d), lambda bi, i: (0, 0)),
                  _resident((d, n))],
        out_specs=out_specs,
        out_shape=out_shape,
        scratch_shapes=[pltpu.VMEM((d // LANE, tm, LANE), F32)],
        compiler_params=_cparams(("parallel", "parallel"), VMEM_LIMIT),
        name="projection",
    )(x, mod3, g1, w_r)


def _attn_kernel(q0, k0, v0, q1, k1, v1, q2, k2, v2, o_ref, acc_ref, lse_ref, bias_ref, *, seq):
    hcols = o_ref.shape[-1]
    n_head = hcols // HEAD_DIM
    head_of_lane = lax.broadcasted_iota(jnp.int32, (SPAN, hcols), 1) // HEAD_DIM
    head_mask_b = [jnp.where(head_of_lane == h, 1.0, 0.0).astype(BF16) for h in range(n_head)]

    def by_head(parts):
        out = parts[n_head - 1]
        for h in range(n_head - 2, -1, -1):
            out = jnp.where(head_of_lane == h, parts[h], out)
        return out

    qi = lax.broadcasted_iota(jnp.int32, (n_head * SPAN, 2 * SPAN), 0) % SPAN
    ki = lax.broadcasted_iota(jnp.int32, (n_head * SPAN, 2 * SPAN), 1)
    band = (ki >= qi) & (ki <= qi + SPAN)
    bias_ref[0] = jnp.where(band, 0.0, -jnp.inf)
    bias_ref[1] = jnp.where(band & (ki >= SPAN), 0.0, -jnp.inf)

    for g, (q_ref, k_ref, v_ref) in enumerate(((q0, k0, v0), (q1, k1, v1), (q2, k2, v2))):
        d = DILATIONS[g]
        n_blk = seq // d // SPAN

        def tile(n, carry, q_ref=q_ref, k_ref=k_ref, v_ref=v_ref, d=d, n_blk=n_blk, g=g):
            p = n // n_blk
            blk = n % n_blk
            r0 = pl.multiple_of(blk * SPAN, SPAN)
            rp = pl.multiple_of(jnp.maximum(blk - 1, 0) * SPAN, SPAN)
            q = q_ref[p, pl.ds(r0, SPAN), :]
            kk = jnp.concatenate([k_ref[p, pl.ds(rp, SPAN), :], k_ref[p, pl.ds(r0, SPAN), :]], axis=0)
            vv = jnp.concatenate([v_ref[p, pl.ds(rp, SPAN), :], v_ref[p, pl.ds(r0, SPAN), :]], axis=0)
            qs = jnp.concatenate([q * head_mask_b[h] for h in range(n_head)], axis=0)
            sc = lax.dot_general(qs, kk, (((1,), (1,)), ((), ())), preferred_element_type=F32)
            sc = sc + bias_ref[jnp.where(blk > 0, 0, 1)]
            mx = jnp.max(sc, axis=-1, keepdims=True)
            e = jnp.exp(sc - mx)
            den = jnp.sum(e, axis=-1, keepdims=True)
            pv = jnp.dot(e.astype(BF16), vv, preferred_element_type=F32)
            lse = mx + jnp.log(den)
            rows_of = lambda a: [a[h * SPAN:(h + 1) * SPAN] for h in range(n_head)]
            o = by_head(rows_of(pv)) / by_head(rows_of(den))
            l = by_head(rows_of(lse))
            start = p + d * r0
            for j in range(hcols // LANE):
                rows = pl.ds(start, SPAN, stride=d) if d > 1 else pl.ds(start, SPAN)
                acc_ref[g, j, rows, :] = o[:, j * LANE:(j + 1) * LANE]
                lse_ref[g, j, rows, :] = l[:, j * LANE:(j + 1) * LANE]
            return carry

        lax.fori_loop(0, seq // SPAN, tile, 0, unroll=8)

    chunk = 256

    def combine(c, carry):
        r = pl.multiple_of(c * chunk, chunk)
        for j in range(hcols // LANE):
            ls = [lse_ref[g, j, pl.ds(r, chunk), :] for g in range(len(DILATIONS))]
            mx = jnp.maximum(jnp.maximum(ls[0], ls[1]), ls[2])
            ws = [jnp.exp(v - mx) for v in ls]
            num = ws[0] * acc_ref[0, j, pl.ds(r, chunk), :]
            for g in range(1, len(DILATIONS)):
                num = num + ws[g] * acc_ref[g, j, pl.ds(r, chunk), :]
            o_ref[0, pl.ds(r, chunk), j * LANE:(j + 1) * LANE] = (num / (ws[0] + ws[1] + ws[2])).astype(BF16)
        return carry

    lax.fori_loop(0, seq // chunk, combine, 0)


def _attention(qkvs, b, s):
    hcols = 4 * HEAD_DIM
    n_hg = ATTN_OUT // hcols
    ncb = ATTN_OUT // hcols
    in_specs, args = [], []
    for g, d in enumerate(DILATIONS):
        for part in range(3):
            in_specs.append(pl.BlockSpec((d, s // d, hcols),
                                         lambda bi, hg, part=part: (bi, 0, part * ncb + hg)))
            args.append(qkvs[g])
    return pl.pallas_call(
        functools.partial(_attn_kernel, seq=s),
        grid=(b, n_hg),
        in_specs=in_specs,
        out_specs=pl.BlockSpec((1, s, hcols), lambda bi, hg: (bi, 0, hg)),
        out_shape=jax.ShapeDtypeStruct((b, s, ATTN_OUT), BF16),
        scratch_shapes=[pltpu.VMEM((len(DILATIONS), hcols // LANE, s, LANE), F32),
                        pltpu.VMEM((len(DILATIONS), hcols // LANE, s, LANE), F32),
                        pltpu.VMEM((2, (hcols // HEAD_DIM) * SPAN, 2 * SPAN), F32)],
        compiler_params=_cparams(("parallel", "parallel"), VMEM_LIMIT),
        name="dilated_attention",
    )(*args)


def _gelu_tanh(y):
    return y * (0.5 * (1.0 + jnp.tanh(0.7978845608028654 * (y + 0.044715 * (y * y * y)))))


def _expm1(u):
    e = jnp.exp(u)
    em1 = e - 1.0
    near = jnp.abs(u) < 0.5
    safe_log = jnp.log(jnp.where(near & (e != 1.0), e, 2.0))
    return jnp.where(near, jnp.where(e == 1.0, u, em1 * u / safe_log), em1)


def _lru_kernel(xr_ref, yr_ref, cw_ref, cb_ref, wg_ref, bx_ref, ba_ref, lam_ref, o_ref,
                a_ref, b_ref):
    seq, tc = xr_ref.shape[1], xr_ref.shape[2]
    x = xr_ref[0].astype(F32)
    row = lax.broadcasted_iota(jnp.int32, (seq, tc), 0)
    cw = cw_ref[...]
    xc = x * cw[CONV_WIDTH - 1:CONV_WIDTH] + cb_ref[...]
    for k in range(1, CONV_WIDTH):
        xs = jnp.where(row >= k, pltpu.roll(x, k, axis=0), 0.0)
        xc = xc + xs * cw[CONV_WIDTH - 1 - k:CONV_WIDTH - k]
    gates = jnp.dot(xc.astype(BF16), wg_ref[0], preferred_element_type=F32)
    gate_i = jax.nn.sigmoid(gates[:, :tc] + bx_ref[...])
    gate_r = jax.nn.sigmoid(gates[:, tc:] + ba_ref[...])
    nl = -lam_ref[...]
    softplus = jnp.maximum(nl, 0.0) + jnp.log1p(jnp.exp(-jnp.abs(nl)))
    log_a = (-LRU_C) * gate_r * softplus
    a = jnp.exp(log_a)
    bv = jnp.sqrt(-_expm1(2.0 * log_a)) * gate_i * xc
    r8 = row % 8
    for k in (1, 2, 4):
        a_s = pltpu.roll(a, k, axis=0)
        b_s = pltpu.roll(bv, k, axis=0)
        take = r8 >= k
        bv = jnp.where(take, a * b_s + bv, bv)
        a = jnp.where(take, a * a_s, a)
    a_ref[...] = a
    b_ref[...] = bv

    def group(i, carry):
        r = pl.multiple_of(i * 8, 8)
        h = a_ref[pl.ds(r, 8), :] * carry + b_ref[pl.ds(r, 8), :]
        b_ref[pl.ds(r, 8), :] = h
        return jnp.broadcast_to(h[7:8, :], h.shape)

    lax.fori_loop(0, seq // 8, group, jnp.zeros((8, tc), F32), unroll=8)
    o_ref[0] = (b_ref[...] * _gelu_tanh(yr_ref[0].astype(F32))).astype(BF16)


def _lru_gate_weights(wx, wa, tc):
    nb, bd, _ = wx.shape
    per = tc // bd
    eye = jnp.eye(per, dtype=wx.dtype)

    def bdiag(w):
        w = w.reshape(nb // per, per, bd, bd)
        return jnp.einsum('cpio,pq->cpiqo', w, eye).reshape(nb // per, tc, tc)

    return jnp.concatenate([bdiag(wx), bdiag(wa)], axis=-1).astype(BF16)


def _lru_branch(xr, yr, conv_w, conv_b, wx, bx, wa, ba, lam, *, tc=256):
    b, s, c = xr.shape
    wg = _lru_gate_weights(wx, wa, tc)
    row = lambda v: v.reshape(1, c)
    tile = pl.BlockSpec((1, s, tc), lambda bi, ci: (bi, 0, ci))
    vec = pl.BlockSpec((1, tc), lambda bi, ci: (0, ci))
    return pl.pallas_call(
        _lru_kernel,
        grid=(b, c // tc),
        in_specs=[tile, tile,
                  pl.BlockSpec((CONV_WIDTH, tc), lambda bi, ci: (0, ci)), vec,
                  pl.BlockSpec((1, tc, 2 * tc), lambda bi, ci: (ci, 0, 0)),
                  vec, vec, vec],
        out_specs=tile,
        out_shape=jax.ShapeDtypeStruct((b, s, c), BF16),
        scratch_shapes=[pltpu.VMEM((s, tc), F32), pltpu.VMEM((s, tc), F32)],
        compiler_params=_cparams(("parallel", "parallel"), VMEM_LIMIT),
        name="rg_lru",
    )(xr, yr, conv_w, row(conv_b), wg, row(bx), row(ba), row(lam))


def _prep_w_in(w_in):
    a = ATTN_WIDTH
    gw = N_SLOTS * HEAD_DIM
    q = w_in[:, :a] * (HEAD_DIM ** -0.5)
    k = w_in[:, a:2 * a]
    v = w_in[:, 2 * a:3 * a]
    parts = []
    for g in range(len(DILATIONS)):
        sl = slice(g * gw, (g + 1) * gw)
        parts += [q[:, sl], k[:, sl], v[:, sl]]
    parts.append(w_in[:, 3 * a:])
    return jnp.concatenate(parts, axis=1).astype(BF16)


def _pack_rows(v):
    half = v.shape[-1] // 2
    lo = pltpu.bitcast(v[:, :half].astype(BF16).astype(F32), jnp.uint32)
    hi = pltpu.bitcast(v[:, half:].astype(BF16).astype(F32), jnp.uint32)
    return (hi & jnp.uint32(0xFFFF0000)) | (lo >> 16)


def _unpack_rows(p):
    lo = pltpu.bitcast(p << 16, F32)
    hi = pltpu.bitcast(p & jnp.uint32(0xFFFF0000), F32)
    return jnp.concatenate([lo, hi], axis=-1)


ROUTE_COLS = 8
EXPERT_LANE0 = N_GROUPS


def _mix_kernel(attn_ref, lru_ref, gl_ref, x_ref, mod_ref, wa_ref, wl_ref, wo_ref, g2_ref,
                wr_ref, br_ref, x1_ref, h2_ref, route_ref, cnt_ref, cnt_acc):
    d = x_ref.shape[-1]
    tm = x_ref.shape[1]

    @pl.when((pl.program_id(0) == 0) & (pl.program_id(1) == 0))
    def _():
        cnt_acc[...] = jnp.zeros_like(cnt_acc)

    m = mod_ref[0]
    gate1, shift2, scale2 = m[:, 2 * d:3 * d], m[:, 3 * d:4 * d], m[:, 4 * d:5 * d]
    ya = jnp.dot(attn_ref[0], wa_ref[...], preferred_element_type=F32)
    yl = jnp.dot(lru_ref[0], wl_ref[...], preferred_element_type=F32)
    glv = gl_ref[0].astype(F32)
    mixed = jax.nn.sigmoid(glv[:, :d]) * ya + jax.nn.sigmoid(glv[:, d:]) * yl
    y = jnp.dot(mixed.astype(BF16), wo_ref[...], preferred_element_type=F32)
    x1 = x_ref[0] + (1.0 + gate1) * y
    x1_ref[0] = x1
    h2 = _rms_mod(x1, g2_ref[...], scale2, shift2)
    h2_ref[0] = _pack_rows(h2)
    logits = jnp.dot(h2.astype(BF16), wr_ref[...], preferred_element_type=F32) + br_ref[...]

    lane = lax.broadcasted_iota(jnp.int32, logits.shape, 1)
    neg = -jnp.inf
    nl = logits.shape[-1]

    def top(vals):
        mx = jnp.max(vals, axis=-1, keepdims=True)
        idx = jnp.min(jnp.where(vals == mx, lane, nl), axis=-1, keepdims=True)
        return mx, idx

    is_grp = lane < N_GROUPS
    gmax, gidx = top(jnp.where(is_grp, logits, neg))
    grp_gate = 1.0 / jnp.sum(jnp.where(is_grp, jnp.exp(logits - gmax), 0.0), axis=-1, keepdims=True)
    lo = EXPERT_LANE0 + EXPERTS_PER_GROUP * gidx
    el = jnp.where((lane >= lo) & (lane < lo + EXPERTS_PER_GROUP), logits, neg)
    v1, i1 = top(el)
    v2, i2 = top(jnp.where(lane == i1, neg, el))
    e21 = jnp.exp(v2 - v1)
    wt1 = grp_gate / (1.0 + e21)
    wt2 = wt1 * e21

    oh1 = jnp.where(lane == i1, 1.0, 0.0)
    oh2 = jnp.where(lane == i2, 1.0, 0.0)
    ohs = oh1 + oh2
    rr = lax.broadcasted_iota(jnp.int32, (tm, tm), 0)
    cc = lax.broadcasted_iota(jnp.int32, (tm, tm), 1)
    earlier = jnp.where(cc < rr, 1.0, 0.0).astype(BF16)
    before = jnp.dot(earlier, ohs.astype(BF16), preferred_element_type=F32) + cnt_acc[...]
    rank1 = jnp.sum(oh1 * before, axis=-1, keepdims=True)
    rank2 = jnp.sum(oh2 * before, axis=-1, keepdims=True)
    cnt_acc[...] = cnt_acc[...] + jnp.sum(ohs, axis=0, keepdims=True)
    cnt_ref[...] = cnt_acc[...]

    cols = [(i1 - EXPERT_LANE0).astype(F32), (i2 - EXPERT_LANE0).astype(F32), rank1, rank2, wt1, wt2]
    slab = jnp.zeros(logits.shape, F32)
    for j, v in enumerate(cols):
        slab = jnp.where(lane == j, v, slab)
    route_ref[0] = slab[:, :ROUTE_COLS]


def _mix_route(attn, lru, gl, x, mod3, wa, wl, wo, g2, wr, br, *, tm=512):
    b, s, d = x.shape
    tok = lambda w: pl.BlockSpec((1, tm, w), lambda bi, i: (bi, i, 0))
    return pl.pallas_call(
        _mix_kernel,
        grid=(b, s // tm),
        in_specs=[tok(attn.shape[-1]), tok(d), tok(2 * d), tok(d),
                  pl.BlockSpec((1, 1, mod3.shape[-1]), lambda bi, i: (bi, 0, 0)),
                  _resident(wa.shape), _resident(wl.shape), _resident(wo.shape),
                  pl.BlockSpec((1, d), lambda bi, i: (0, 0)),
                  _resident(wr.shape),
                  pl.BlockSpec((1, LANE), lambda bi, i: (0, 0))],
        out_specs=[tok(d), tok(d // 2), tok(ROUTE_COLS),
                   pl.BlockSpec((1, LANE), lambda bi, i: (0, 0))],
        out_shape=[jax.ShapeDtypeStruct((b, s, d), F32),
                   jax.ShapeDtypeStruct((b, s, d // 2), jnp.uint32),
                   jax.ShapeDtypeStruct((b, s, ROUTE_COLS), F32),
                   jax.ShapeDtypeStruct((1, LANE), F32)],
        scratch_shapes=[pltpu.VMEM((1, LANE), F32)],
        compiler_params=_cparams(("arbitrary", "arbitrary"), VMEM_LIMIT),
        name="mix_route",
    )(attn, lru, gl, x, mod3, wa, wl, wo, g2, wr, br)


TOP_K = 2
EXPERT_BLOCK = 256


def _dispatch_kernel(dest_ref, h_ref, xp_in, xp_out, sem):
    del xp_in
    tm = h_ref.shape[0]

    def row_copy(r, k):
        return pltpu.make_async_copy(h_ref.at[pl.ds(r, 1)],
                                     xp_out.at[pl.ds(dest_ref[TOP_K * r + k], 1)], sem)

    def start(r, carry):
        for k in range(TOP_K):
            row_copy(r, k).start()
        return carry

    def wait(r, carry):
        for k in range(TOP_K):
            row_copy(r, k).wait()
        return carry

    lax.fori_loop(0, tm, start, 0)
    lax.fori_loop(0, tm, wait, 0)


def _dispatch(h2p, dest, n_slots, *, tm=512):
    t, w = h2p.shape
    xp0 = jnp.zeros((n_slots, w), h2p.dtype)
    return pl.pallas_call(
        _dispatch_kernel,
        grid=(t // tm,),
        in_specs=[pl.BlockSpec((TOP_K * tm,), lambda i: (i,), memory_space=pltpu.SMEM),
                  pl.BlockSpec((tm, w), lambda i: (i, 0)),
                  pl.BlockSpec(memory_space=pl.ANY)],
        out_specs=pl.BlockSpec(memory_space=pl.ANY),
        out_shape=jax.ShapeDtypeStruct((n_slots, w), h2p.dtype),
        scratch_shapes=[pltpu.SemaphoreType.DMA],
        input_output_aliases={2: 0},
        compiler_params=_cparams(("arbitrary",)),
        name="dispatch",
    )(dest, h2p, xp0)


def _expert_kernel(be_ref, nu_ref, x_ref, w1_ref, w3_ref, w2_ref, y_ref, wb1, wb3, wb2):
    j = pl.program_id(0)

    @pl.when(j < nu_ref[0])
    def _():
        @pl.when((j == 0) | (be_ref[j] != be_ref[jnp.maximum(j - 1, 0)]))
        def _():
            wb1[...] = w1_ref[0].astype(BF16)
            wb3[...] = w3_ref[0].astype(BF16)
            wb2[...] = w2_ref[0].astype(BF16)

        xb = _unpack_rows(x_ref[...]).astype(BF16)
        a = jnp.dot(xb, wb1[...], preferred_element_type=F32)
        g = jnp.dot(xb, wb3[...], preferred_element_type=F32)
        hm = (a * jax.nn.sigmoid(a) * g).astype(BF16)
        y_ref[...] = _pack_rows(jnp.dot(hm, wb2[...], preferred_element_type=F32))

    @pl.when(j >= nu_ref[0])
    def _():
        y_ref[...] = jnp.zeros_like(y_ref)


def _experts(xp, blk_e, n_used, w1, w3, w2):
    n_slots, w = xp.shape
    ne, d, de = w1.shape
    nb = n_slots // EXPERT_BLOCK
    last = lambda j, nu: jnp.minimum(j, nu[0] - 1)
    grid_spec = pltpu.PrefetchScalarGridSpec(
        num_scalar_prefetch=2,
        grid=(nb,),
        in_specs=[pl.BlockSpec((EXPERT_BLOCK, w), lambda j, be, nu: (last(j, nu), 0)),
                  pl.BlockSpec((1, d, de), lambda j, be, nu: (be[j], 0, 0)),
                  pl.BlockSpec((1, d, de), lambda j, be, nu: (be[j], 0, 0)),
                  pl.BlockSpec((1, de, d), lambda j, be, nu: (be[j], 0, 0))],
        out_specs=pl.BlockSpec((EXPERT_BLOCK, w), lambda j, be, nu: (j, 0)),
        scratch_shapes=[pltpu.VMEM((d, de), BF16), pltpu.VMEM((d, de), BF16), pltpu.VMEM((de, d), BF16)])
    return pl.pallas_call(
        _expert_kernel,
        grid_spec=grid_spec,
        out_shape=jax.ShapeDtypeStruct((n_slots, w), xp.dtype),
        compiler_params=_cparams(("arbitrary",), VMEM_LIMIT),
        name="experts",
    )(blk_e, n_used, xp, w1, w3, w2)


def _combine_kernel(dest_ref, route_ref, x1_ref, mod_ref, gf_ref, yp_hbm, o_ref, ybuf, sem):
    tm, d = x1_ref.shape[1], x1_ref.shape[2]

    def row_copy(r, k):
        return pltpu.make_async_copy(yp_hbm.at[pl.ds(dest_ref[TOP_K * r + k], 1)],
                                     ybuf.at[k, pl.ds(r, 1)], sem)

    def start(r, carry):
        for k in range(TOP_K):
            row_copy(r, k).start()
        return carry

    def wait(r, carry):
        for k in range(TOP_K):
            row_copy(r, k).wait()
        return carry

    lax.fori_loop(0, tm, start, 0)
    lax.fori_loop(0, tm, wait, 0)
    route = route_ref[0]
    moe = _unpack_rows(ybuf[0]) * route[:, 4:5] + _unpack_rows(ybuf[1]) * route[:, 5:6]
    gate2 = mod_ref[0][:, 5 * d:6 * d]
    xo = x1_ref[0] + (1.0 + gate2) * moe
    ms = jnp.mean(xo * xo, axis=-1, keepdims=True)
    o_ref[0] = xo * lax.rsqrt(ms + EPS) * gf_ref[...]


def _combine(dest, route, x1, mod3, gf, yp, *, tm=512):
    b, s, d = x1.shape
    spt = s // tm
    return pl.pallas_call(
        _combine_kernel,
        grid=(b, spt),
        in_specs=[pl.BlockSpec((TOP_K * tm,), lambda bi, i: (bi * spt + i,), memory_space=pltpu.SMEM),
                  pl.BlockSpec((1, tm, ROUTE_COLS), lambda bi, i: (bi, i, 0)),
                  pl.BlockSpec((1, tm, d), lambda bi, i: (bi, i, 0)),
                  pl.BlockSpec((1, 1, mod3.shape[-1]), lambda bi, i: (bi, 0, 0)),
                  pl.BlockSpec((1, d), lambda bi, i: (0, 0)),
                  pl.BlockSpec(memory_space=pl.ANY)],
        out_specs=pl.BlockSpec((1, tm, d), lambda bi, i: (bi, i, 0)),
        out_shape=jax.ShapeDtypeStruct((b, s, d), F32),
        scratch_shapes=[pltpu.VMEM((TOP_K, tm, d // 2), jnp.uint32), pltpu.SemaphoreType.DMA],
        compiler_params=_cparams(("arbitrary", "arbitrary")),
        name="combine",
    )(dest, route, x1, mod3, gf, yp)


def _slot_plan(route, counts, n_tok):
    sizes = counts[0, EXPERT_LANE0:EXPERT_LANE0 + N_EXPERTS].astype(jnp.int32)
    padded = (sizes + EXPERT_BLOCK - 1) // EXPERT_BLOCK * EXPERT_BLOCK
    pad_ends = jnp.cumsum(padded)
    pad_starts = pad_ends - padded
    eid = route[..., 0:TOP_K].astype(jnp.int32).reshape(n_tok, TOP_K)
    rank = route[..., TOP_K:2 * TOP_K].astype(jnp.int32).reshape(n_tok, TOP_K)
    dest = (pad_starts[eid] + rank).reshape(n_tok * TOP_K)
    n_blocks = (n_tok * TOP_K + N_EXPERTS * (EXPERT_BLOCK - 1) + EXPERT_BLOCK - 1) // EXPERT_BLOCK
    n_used = pad_ends[-1] // EXPERT_BLOCK
    blk = jnp.minimum(jnp.arange(n_blocks), n_used - 1)
    blk_e = jnp.minimum(jnp.sum(pad_ends[None, :] <= (blk * EXPERT_BLOCK)[:, None], axis=1), N_EXPERTS - 1)
    return dest, blk_e.astype(jnp.int32), n_used.reshape(1).astype(jnp.int32), n_blocks


def kernel(x, c, w_mod, b_mod, norm1_g, w_in, conv_w, conv_b, lru_wx, lru_bx, lru_wa, lru_ba, lru_lambda, w_attn_o, w_lru_o, w_out, norm2_g, w_grp, b_grp, w_exp, b_exp, w1, w3, w2, norm_f_g):
    b, s, d = x.shape
    assert d == D_MODEL and s == SPAN * DILATIONS[-1] and w_mod.shape[0] == 1
    mod3 = _modulation(c, w_mod[0], b_mod[0]).reshape(b, 1, 6 * d)
    qkv0, qkv1, qkv2, xr, yr, gl = _projection(x, mod3, norm1_g[0].reshape(1, d), _prep_w_in(w_in[0]))
    attn = _attention((qkv0, qkv1, qkv2), b, s)
    lru = _lru_branch(xr, yr, conv_w[0], conv_b[0], lru_wx[0], lru_bx[0], lru_wa[0], lru_ba[0], lru_lambda[0])

    n_router = N_GROUPS + N_EXPERTS
    wr = jnp.pad(jnp.concatenate([w_grp[0], w_exp[0]], axis=1), ((0, 0), (0, LANE - n_router))).astype(BF16)
    br = jnp.pad(jnp.concatenate([b_grp[0], b_exp[0]]), (0, LANE - n_router)).reshape(1, LANE)
    x1, h2p, route, counts = _mix_route(
        attn, lru, gl, x, mod3, w_attn_o[0].astype(BF16), w_lru_o[0].astype(BF16), w_out[0].astype(BF16),
        norm2_g[0].reshape(1, d), wr, br)

    n_tok = b * s
    dest, blk_e, n_used, n_blocks = _slot_plan(route, counts, n_tok)
    xp = _dispatch(h2p.reshape(n_tok, d // 2), dest, n_blocks * EXPERT_BLOCK)
    yp = _experts(xp, blk_e, n_used, w1[0], w3[0], w2[0])
    return _combine(dest, route, x1, mod3, norm_f_g.reshape(1, d), yp)
```

```python
import functools
import math

import jax
import jax.numpy as jnp
from jax import lax
from jax.experimental import pallas as pl
from jax.experimental.pallas import tpu as pltpu
from jax.experimental.pallas import tpu_sc as plsc

F32 = jnp.float32
BF16 = jnp.bfloat16

D_MODEL = 1024
HEAD_DIM = 64
N_SLOTS = 8
SPAN = 128
DILATIONS = (1, 4, 16)
GROUP_COLS = 3 * N_SLOTS * HEAD_DIM
ATTN_WIDTH = len(DILATIONS) * N_SLOTS * HEAD_DIM
ATTN_OUT = N_SLOTS * HEAD_DIM
LRU_WIDTH = D_MODEL
LRU_BLOCK_DIM = 64
CONV_WIDTH = 4
LRU_C = 8.0
N_GROUPS = 4
EXPERTS_PER_GROUP = 8
N_EXPERTS = N_GROUPS * EXPERTS_PER_GROUP
D_EXPERT = D_MODEL // 2
EPS = 1e-6
LANE = 128
VMEM_LIMIT = 56 * 1024 * 1024


def _cparams(sem, vmem=None):
    return pltpu.CompilerParams(dimension_semantics=sem, vmem_limit_bytes=vmem)


def _resident(shape):
    nd = len(shape)
    return pl.BlockSpec(shape, lambda *_: (0,) * nd, pipeline_mode=pl.Buffered(1))


def _mod_kernel(c_ref, w_ref, b_ref, o_ref):
    c = c_ref[...]
    ca = c * jax.nn.sigmoid(c)
    o_ref[...] = jnp.dot(ca.astype(BF16), w_ref[...].astype(BF16),
                         preferred_element_type=F32) + b_ref[...]


def _modulation(c, w_mod, b_mod):
    b, d = c.shape
    n = w_mod.shape[1]
    tn = n // 4
    return pl.pallas_call(
        _mod_kernel,
        grid=(n // tn,),
        in_specs=[pl.BlockSpec((b, d), lambda j: (0, 0)),
                  pl.BlockSpec((d, tn), lambda j: (0, j)),
                  pl.BlockSpec((1, tn), lambda j: (0, j))],
        out_specs=pl.BlockSpec((b, tn), lambda j: (0, j)),
        out_shape=jax.ShapeDtypeStruct((b, n), F32),
        compiler_params=_cparams(("arbitrary",)),
        name="modulation",
    )(c, w_mod, b_mod.reshape(1, n))


def _rms_mod(x, g, scale, shift):
    ms = jnp.mean(x * x, axis=-1, keepdims=True)
    return x * lax.rsqrt(ms + EPS) * g * (1.0 + scale) + shift


def _proj_kernel(x_ref, mod_ref, g_ref, w_ref, qkv0_ref, qkv1_ref, qkv2_ref,
                 xr_ref, yr_ref, gl_ref, hs_ref, *, tm):
    d_model = x_ref.shape[-1]
    m = mod_ref[0]
    h = _rms_mod(x_ref[0], g_ref[...], m[:, d_model:2 * d_model], m[:, 0:d_model])

    def mm(hv, lo, hi):
        return jnp.dot(hv, w_ref[:, lo:hi], preferred_element_type=F32)

    hb = h.astype(BF16)
    c0 = len(DILATIONS) * GROUP_COLS
    qkv0_ref[0] = mm(hb, 0, GROUP_COLS).astype(BF16)
    xr_ref[0] = mm(hb, c0, c0 + LRU_WIDTH).astype(BF16)
    yr_ref[0] = mm(hb, c0 + LRU_WIDTH, c0 + 2 * LRU_WIDTH).astype(BF16)
    gl_ref[0] = mm(hb, c0 + 2 * LRU_WIDTH, c0 + 2 * LRU_WIDTH + 2 * d_model).astype(BF16)

    n_slab = d_model // LANE
    for j in range(n_slab):
        hs_ref[j] = h[:, j * LANE:(j + 1) * LANE]
    for g, out_ref in ((1, qkv1_ref), (2, qkv2_ref)):
        d = DILATIONS[g]
        rows = tm // d
        hp = jnp.concatenate(
            [jnp.concatenate([hs_ref[j, pl.ds(p, rows, stride=d), :] for j in range(n_slab)], axis=1)
             for p in range(d)], axis=0).astype(BF16)
        res = mm(hp, g * GROUP_COLS, (g + 1) * GROUP_COLS).astype(BF16)
        for p in range(d):
            out_ref[p] = res[p * rows:(p + 1) * rows]


def _projection(x, mod3, g1, w_r, *, tm=256):
    b, s, d = x.shape
    n = w_r.shape[1]
    assert s % tm == 0 and tm % (16 * DILATIONS[-1]) == 0
    out_shape = [jax.ShapeDtypeStruct((b * dd, s // dd, GROUP_COLS), BF16) for dd in DILATIONS]
    out_shape += [jax.ShapeDtypeStruct((b, s, LRU_WIDTH), BF16),
                  jax.ShapeDtypeStruct((b, s, LRU_WIDTH), BF16),
                  jax.ShapeDtypeStruct((b, s, 2 * d), BF16)]
    out_specs = [pl.BlockSpec((dd, tm // dd, GROUP_COLS), lambda bi, i: (bi, i, 0)) for dd in DILATIONS]
    out_specs += [pl.BlockSpec((1, tm, LRU_WIDTH), lambda bi, i: (bi, i, 0)),
                  pl.BlockSpec((1, tm, LRU_WIDTH), lambda bi, i: (bi, i, 0)),
                  pl.BlockSpec((1, tm, 2 * d), lambda bi, i: (bi, i, 0))]
    return pl.pallas_call(
        functools.partial(_proj_kernel, tm=tm),
        grid=(b, s // tm),
        in_specs=[pl.BlockSpec((1, tm, d), lambda bi, i: (bi, i, 0)),
                  pl.BlockSpec((1, 1, mod3.shape[-1]), lambda bi, i: (bi, 0, 0)),
                  pl.BlockSpec((1, d), lambda bi, i: (0, 0)),
                  _resident((d, n))],
        out_specs=out_specs,
        out_shape=out_shape,
        scratch_shapes=[pltpu.VMEM((d // LANE, tm, LANE), F32)],
        compiler_params=_cparams(("parallel", "parallel"), VMEM_LIMIT),
        name="projection",
    )(x, mod3, g1, w_r)


def _attn_kernel(q0, k0, v0, q1, k1, v1, q2, k2, v2, o_ref, acc_ref, lse_ref, bias_ref, *, seq):
    hcols = o_ref.shape[-1]
    n_head = hcols // HEAD_DIM
    head_of_lane = lax.broadcasted_iota(jnp.int32, (SPAN, hcols), 1) // HEAD_DIM
    head_mask_b = [jnp.where(head_of_lane == h, 1.0, 0.0).astype(BF16) for h in range(n_head)]

    def by_head(parts):
        out = parts[n_head - 1]
        for h in range(n_head - 2, -1, -1):
            out = jnp.where(head_of_lane == h, parts[h], out)
        return out

    qi = lax.broadcasted_iota(jnp.int32, (n_head * SPAN, 2 * SPAN), 0) % SPAN
    ki = lax.broadcasted_iota(jnp.int32, (n_head * SPAN, 2 * SPAN), 1)
    band = (ki >= qi) & (ki <= qi + SPAN)
    bias_ref[0] = jnp.where(band, 0.0, -jnp.inf)
    bias_ref[1] = jnp.where(band & (ki >= SPAN), 0.0, -jnp.inf)

    for g, (q_ref, k_ref, v_ref) in enumerate(((q0, k0, v0), (q1, k1, v1), (q2, k2, v2))):
        d = DILATIONS[g]
        n_blk = seq // d // SPAN

        def tile(n, carry, q_ref=q_ref, k_ref=k_ref, v_ref=v_ref, d=d, n_blk=n_blk, g=g):
            p = n // n_blk
            blk = n % n_blk
            r0 = pl.multiple_of(blk * SPAN, SPAN)
            rp = pl.multiple_of(jnp.maximum(blk - 1, 0) * SPAN, SPAN)
            q = q_ref[p, pl.ds(r0, SPAN), :]
            kk = jnp.concatenate([k_ref[p, pl.ds(rp, SPAN), :], k_ref[p, pl.ds(r0, SPAN), :]], axis=0)
            vv = jnp.concatenate([v_ref[p, pl.ds(rp, SPAN), :], v_ref[p, pl.ds(r0, SPAN), :]], axis=0)
            qs = jnp.concatenate([q * head_mask_b[h] for h in range(n_head)], axis=0)
            sc = lax.dot_general(qs, kk, (((1,), (1,)), ((), ())), preferred_element_type=F32)
            sc = sc + bias_ref[jnp.where(blk > 0, 0, 1)]
            mx = jnp.max(sc, axis=-1, keepdims=True)
            e = jnp.exp(sc - mx)
            den = jnp.sum(e, axis=-1, keepdims=True)
            pv = jnp.dot(e.astype(BF16), vv, preferred_element_type=F32)
            lse = mx + jnp.log(den)
            rows_of = lambda a: [a[h * SPAN:(h + 1) * SPAN] for h in range(n_head)]
            o = by_head(rows_of(pv)) / by_head(rows_of(den))
            l = by_head(rows_of(lse))
            start = p + d * r0
            for j in range(hcols // LANE):
                rows = pl.ds(start, SPAN, stride=d) if d > 1 else pl.ds(start, SPAN)
                acc_ref[g, j, rows, :] = o[:, j * LANE:(j + 1) * LANE]
                lse_ref[g, j, rows, :] = l[:, j * LANE:(j + 1) * LANE]
            return carry

        lax.fori_loop(0, seq // SPAN, tile, 0, unroll=8)

    chunk = 256

    def combine(c, carry):
        r = pl.multiple_of(c * chunk, chunk)
        for j in range(hcols // LANE):
            ls = [lse_ref[g, j, pl.ds(r, chunk), :] for g in range(len(DILATIONS))]
            mx = jnp.maximum(jnp.maximum(ls[0], ls[1]), ls[2])
            ws = [jnp.exp(v - mx) for v in ls]
            num = ws[0] * acc_ref[0, j, pl.ds(r, chunk), :]
            for g in range(1, len(DILATIONS)):
                num = num + ws[g] * acc_ref[g, j, pl.ds(r, chunk), :]
            o_ref[0, pl.ds(r, chunk), j * LANE:(j + 1) * LANE] = (num / (ws[0] + ws[1] + ws[2])).astype(BF16)
        return carry

    lax.fori_loop(0, seq // chunk, combine, 0)


def _attention(qkvs, b, s):
    hcols = 4 * HEAD_DIM
    n_hg = ATTN_OUT // hcols
    ncb = ATTN_OUT // hcols
    in_specs, args = [], []
    for g, d in enumerate(DILATIONS):
        for part in range(3):
            in_specs.append(pl.BlockSpec((d, s // d, hcols),
                                         lambda bi, hg, part=part: (bi, 0, part * ncb + hg)))
            args.append(qkvs[g])
    return pl.pallas_call(
        functools.partial(_attn_kernel, seq=s),
        grid=(b, n_hg),
        in_specs=in_specs,
        out_specs=pl.BlockSpec((1, s, hcols), lambda bi, hg: (bi, 0, hg)),
        out_shape=jax.ShapeDtypeStruct((b, s, ATTN_OUT), BF16),
        scratch_shapes=[pltpu.VMEM((len(DILATIONS), hcols // LANE, s, LANE), F32),
                        pltpu.VMEM((len(DILATIONS), hcols // LANE, s, LANE), F32),
                        pltpu.VMEM((2, (hcols // HEAD_DIM) * SPAN, 2 * SPAN), F32)],
        compiler_params=_cparams(("parallel", "parallel"), VMEM_LIMIT),
        name="dilated_attention",
    )(*args)


def _gelu_tanh(y):
    return y * (0.5 * (1.0 + jnp.tanh(0.7978845608028654 * (y + 0.044715 * (y * y * y)))))


def _expm1(u):
    e = jnp.exp(u)
    em1 = e - 1.0
    near = jnp.abs(u) < 0.5
    safe_log = jnp.log(jnp.where(near & (e != 1.0), e, 2.0))
    return jnp.where(near, jnp.where(e == 1.0, u, em1 * u / safe_log), em1)


def _lru_kernel(xr_ref, yr_ref, cw_ref, cb_ref, wg_ref, bx_ref, ba_ref, lam_ref, o_ref,
                a_ref, b_ref):
    seq, tc = xr_ref.shape[1], xr_ref.shape[2]
    x = xr_ref[0].astype(F32)
    row = lax.broadcasted_iota(jnp.int32, (seq, tc), 0)
    cw = cw_ref[...]
    xc = x * cw[CONV_WIDTH - 1:CONV_WIDTH] + cb_ref[...]
    for k in range(1, CONV_WIDTH):
        xs = jnp.where(row >= k, pltpu.roll(x, k, axis=0), 0.0)
        xc = xc + xs * cw[CONV_WIDTH - 1 - k:CONV_WIDTH - k]
    gates = jnp.dot(xc.astype(BF16), wg_ref[0], preferred_element_type=F32)
    gate_i = jax.nn.sigmoid(gates[:, :tc] + bx_ref[...])
    gate_r = jax.nn.sigmoid(gates[:, tc:] + ba_ref[...])
    nl = -lam_ref[...]
    softplus = jnp.maximum(nl, 0.0) + jnp.log1p(jnp.exp(-jnp.abs(nl)))
    log_a = (-LRU_C) * gate_r * softplus
    a = jnp.exp(log_a)
    bv = jnp.sqrt(-_expm1(2.0 * log_a)) * gate_i * xc
    r8 = row % 8
    for k in (1, 2, 4):
        a_s = pltpu.roll(a, k, axis=0)
        b_s = pltpu.roll(bv, k, axis=0)
        take = r8 >= k
        bv = jnp.where(take, a * b_s + bv, bv)
        a = jnp.where(take, a * a_s, a)
    a_ref[...] = a
    b_ref[...] = bv

    def group(i, carry):
        r = pl.multiple_of(i * 8, 8)
        h = a_ref[pl.ds(r, 8), :] * carry + b_ref[pl.ds(r, 8), :]
        b_ref[pl.ds(r, 8), :] = h
        return jnp.broadcast_to(h[7:8, :], h.shape)

    lax.fori_loop(0, seq // 8, group, jnp.zeros((8, tc), F32), unroll=8)
    o_ref[0] = (b_ref[...] * _gelu_tanh(yr_ref[0].astype(F32))).astype(BF16)


def _lru_gate_weights(wx, wa, tc):
    nb, bd, _ = wx.shape
    per = tc // bd
    eye = jnp.eye(per, dtype=wx.dtype)

    def bdiag(w):
        w = w.reshape(nb // per, per, bd, bd)
        return jnp.einsum('cpio,pq->cpiqo', w, eye).reshape(nb // per, tc, tc)

    return jnp.concatenate([bdiag(wx), bdiag(wa)], axis=-1).astype(BF16)


def _lru_branch(xr, yr, conv_w, conv_b, wx, bx, wa, ba, lam, *, tc=256):
    b, s, c = xr.shape
    wg = _lru_gate_weights(wx, wa, tc)
    row = lambda v: v.reshape(1, c)
    tile = pl.BlockSpec((1, s, tc), lambda bi, ci: (bi, 0, ci))
    vec = pl.BlockSpec((1, tc), lambda bi, ci: (0, ci))
    return pl.pallas_call(
        _lru_kernel,
        grid=(b, c // tc),
        in_specs=[tile, tile,
                  pl.BlockSpec((CONV_WIDTH, tc), lambda bi, ci: (0, ci)), vec,
                  pl.BlockSpec((1, tc, 2 * tc), lambda bi, ci: (ci, 0, 0)),
                  vec, vec, vec],
        out_specs=tile,
        out_shape=jax.ShapeDtypeStruct((b, s, c), BF16),
        scratch_shapes=[pltpu.VMEM((s, tc), F32), pltpu.VMEM((s, tc), F32)],
        compiler_params=_cparams(("parallel", "parallel"), VMEM_LIMIT),
        name="rg_lru",
    )(xr, yr, conv_w, row(conv_b), wg, row(bx), row(ba), row(lam))


def _prep_w_in(w_in):
    a = ATTN_WIDTH
    gw = N_SLOTS * HEAD_DIM
    q = w_in[:, :a] * (HEAD_DIM ** -0.5)
    k = w_in[:, a:2 * a]
    v = w_in[:, 2 * a:3 * a]
    parts = []
    for g in range(len(DILATIONS)):
        sl = slice(g * gw, (g + 1) * gw)
        parts += [q[:, sl], k[:, sl], v[:, sl]]
    parts.append(w_in[:, 3 * a:])
    return jnp.concatenate(parts, axis=1).astype(BF16)


ROW_SUBLANES = D_MODEL // LANE


def _store_tile_rows(ref, v):
    n = v.shape[0]
    for j in range(ROW_SUBLANES):
        ref[pl.ds(j, n, stride=ROW_SUBLANES), :] = v[:, j * LANE:(j + 1) * LANE]


def _load_tile_rows(ref):
    n = ref.shape[0] // ROW_SUBLANES
    return jnp.concatenate([ref[pl.ds(j, n, stride=ROW_SUBLANES), :] for j in range(ROW_SUBLANES)], axis=-1)


SC_CORES, SC_SUBCORES = 2, 16
SC_CHUNK = 64


def _sc_gather_rows(table, idx):
    n = idx.shape[0]
    per_worker = n // (SC_CORES * SC_SUBCORES)
    n_chunks = per_worker // SC_CHUNK
    assert n_chunks * SC_CHUNK * SC_CORES * SC_SUBCORES == n
    mesh = plsc.VectorSubcoreMesh(core_axis_name="c", subcore_axis_name="s")

    def body(table_hbm, idx_hbm, out_hbm, idx_v, rows_v, sem):
        base = (lax.axis_index("s") * SC_CORES + lax.axis_index("c")) * per_worker

        @pl.loop(0, n_chunks)
        def _(i):
            off = pl.multiple_of(base + i * SC_CHUNK, SC_CHUNK)
            pltpu.sync_copy(idx_hbm.at[pl.ds(off, SC_CHUNK)], idx_v)
            pltpu.async_copy(table_hbm.at[idx_v], rows_v, sem).wait()
            pltpu.sync_copy(rows_v, out_hbm.at[pl.ds(off, SC_CHUNK)])

    return pl.kernel(
        body, mesh=mesh,
        out_type=jax.ShapeDtypeStruct((n,) + table.shape[1:], table.dtype),
        scratch_types=[pltpu.VMEM((SC_CHUNK,), jnp.int32),
                       pltpu.VMEM((SC_CHUNK,) + table.shape[1:], table.dtype),
                       pltpu.SemaphoreType.DMA],
        name="sc_gather_rows",
    )(table, idx)


ROUTE_COLS = 8
EXPERT_LANE0 = N_GROUPS


def _mix_kernel(attn_ref, lru_ref, gl_ref, x_ref, mod_ref, wa_ref, wl_ref, wo_ref, g2_ref,
                wr_ref, br_ref, x1_ref, h2_ref, route_ref, cnt_ref, cnt_acc):
    d = x_ref.shape[-1]
    tm = x_ref.shape[1]

    @pl.when((pl.program_id(0) == 0) & (pl.program_id(1) == 0))
    def _():
        cnt_acc[...] = jnp.zeros_like(cnt_acc)

    m = mod_ref[0]
    gate1, shift2, scale2 = m[:, 2 * d:3 * d], m[:, 3 * d:4 * d], m[:, 4 * d:5 * d]
    ya = jnp.dot(attn_ref[0], wa_ref[...], preferred_element_type=F32)
    yl = jnp.dot(lru_ref[0], wl_ref[...], preferred_element_type=F32)
    glv = gl_ref[0].astype(F32)
    mixed = jax.nn.sigmoid(glv[:, :d]) * ya + jax.nn.sigmoid(glv[:, d:]) * yl
    y = jnp.dot(mixed.astype(BF16), wo_ref[...], preferred_element_type=F32)
    x1 = x_ref[0] + (1.0 + gate1) * y
    x1_ref[0] = x1
    h2 = _rms_mod(x1, g2_ref[...], scale2, shift2)
    _store_tile_rows(h2_ref, h2)
    logits = jnp.dot(h2.astype(BF16), wr_ref[...], preferred_element_type=F32) + br_ref[...]

    lane = lax.broadcasted_iota(jnp.int32, logits.shape, 1)
    neg = -jnp.inf
    nl = logits.shape[-1]

    def top(vals):
        mx = jnp.max(vals, axis=-1, keepdims=True)
        idx = jnp.min(jnp.where(vals == mx, lane, nl), axis=-1, keepdims=True)
        return mx, idx

    is_grp = lane < N_GROUPS
    gmax, gidx = top(jnp.where(is_grp, logits, neg))
    grp_gate = 1.0 / jnp.sum(jnp.where(is_grp, jnp.exp(logits - gmax), 0.0), axis=-1, keepdims=True)
    lo = EXPERT_LANE0 + EXPERTS_PER_GROUP * gidx
    el = jnp.where((lane >= lo) & (lane < lo + EXPERTS_PER_GROUP), logits, neg)
    v1, i1 = top(el)
    v2, i2 = top(jnp.where(lane == i1, neg, el))
    e21 = jnp.exp(v2 - v1)
    wt1 = grp_gate / (1.0 + e21)
    wt2 = wt1 * e21

    oh1 = jnp.where(lane == i1, 1.0, 0.0)
    oh2 = jnp.where(lane == i2, 1.0, 0.0)
    ohs = oh1 + oh2
    rr = lax.broadcasted_iota(jnp.int32, (tm, tm), 0)
    cc = lax.broadcasted_iota(jnp.int32, (tm, tm), 1)
    earlier = jnp.where(cc < rr, 1.0, 0.0).astype(BF16)
    before = jnp.dot(earlier, ohs.astype(BF16), preferred_element_type=F32) + cnt_acc[...]
    rank1 = jnp.sum(oh1 * before, axis=-1, keepdims=True)
    rank2 = jnp.sum(oh2 * before, axis=-1, keepdims=True)
    cnt_acc[...] = cnt_acc[...] + jnp.sum(ohs, axis=0, keepdims=True)
    cnt_ref[...] = cnt_acc[...]

    cols = [(i1 - EXPERT_LANE0).astype(F32), (i2 - EXPERT_LANE0).astype(F32), rank1, rank2, wt1, wt2]
    slab = jnp.zeros(logits.shape, F32)
    for j, v in enumerate(cols):
        slab = jnp.where(lane == j, v, slab)
    route_ref[0] = slab[:, :ROUTE_COLS]


def _mix_route(attn, lru, gl, x, mod3, wa, wl, wo, g2, wr, br, *, tm=512):
    b, s, d = x.shape
    tok = lambda w: pl.BlockSpec((1, tm, w), lambda bi, i: (bi, i, 0))
    return pl.pallas_call(
        _mix_kernel,
        grid=(b, s // tm),
        in_specs=[tok(attn.shape[-1]), tok(d), tok(2 * d), tok(d),
                  pl.BlockSpec((1, 1, mod3.shape[-1]), lambda bi, i: (bi, 0, 0)),
                  _resident(wa.shape), _resident(wl.shape), _resident(wo.shape),
                  pl.BlockSpec((1, d), lambda bi, i: (0, 0)),
                  _resident(wr.shape),
                  pl.BlockSpec((1, LANE), lambda bi, i: (0, 0))],
        out_specs=[tok(d),
                   pl.BlockSpec((tm * ROW_SUBLANES, LANE), lambda bi, i: (bi * (s // tm) + i, 0)),
                   tok(ROUTE_COLS),
                   pl.BlockSpec((1, LANE), lambda bi, i: (0, 0))],
        out_shape=[jax.ShapeDtypeStruct((b, s, d), F32),
                   jax.ShapeDtypeStruct((b * s * ROW_SUBLANES, LANE), F32),
                   jax.ShapeDtypeStruct((b, s, ROUTE_COLS), F32),
                   jax.ShapeDtypeStruct((1, LANE), F32)],
        scratch_shapes=[pltpu.VMEM((1, LANE), F32)],
        compiler_params=_cparams(("arbitrary", "arbitrary"), VMEM_LIMIT),
        name="mix_route",
    )(attn, lru, gl, x, mod3, wa, wl, wo, g2, wr, br)


TOP_K = 2
EXPERT_BLOCK = 256


def _expert_kernel(be_ref, nu_ref, x_ref, w1_ref, w3_ref, w2_ref, y_ref, wb1, wb3, wb2):
    j = pl.program_id(0)

    @pl.when(j < nu_ref[0])
    def _():
        @pl.when((j == 0) | (be_ref[j] != be_ref[jnp.maximum(j - 1, 0)]))
        def _():
            wb1[...] = w1_ref[0].astype(BF16)
            wb3[...] = w3_ref[0].astype(BF16)
            wb2[...] = w2_ref[0].astype(BF16)

        xb = _load_tile_rows(x_ref).astype(BF16)
        a = jnp.dot(xb, wb1[...], preferred_element_type=F32)
        g = jnp.dot(xb, wb3[...], preferred_element_type=F32)
        hm = (a * jax.nn.sigmoid(a) * g).astype(BF16)
        _store_tile_rows(y_ref, jnp.dot(hm, wb2[...], preferred_element_type=F32))

    @pl.when(j >= nu_ref[0])
    def _():
        y_ref[...] = jnp.zeros_like(y_ref)


def _experts(xp, blk_e, n_used, w1, w3, w2):
    ne, d, de = w1.shape
    nb = xp.shape[0] // (EXPERT_BLOCK * ROW_SUBLANES)
    rows = (EXPERT_BLOCK * ROW_SUBLANES, LANE)
    last = lambda j, nu: jnp.minimum(j, nu[0] - 1)
    grid_spec = pltpu.PrefetchScalarGridSpec(
        num_scalar_prefetch=2,
        grid=(nb,),
        in_specs=[pl.BlockSpec(rows, lambda j, be, nu: (last(j, nu), 0)),
                  pl.BlockSpec((1, d, de), lambda j, be, nu: (be[j], 0, 0)),
                  pl.BlockSpec((1, d, de), lambda j, be, nu: (be[j], 0, 0)),
                  pl.BlockSpec((1, de, d), lambda j, be, nu: (be[j], 0, 0))],
        out_specs=pl.BlockSpec(rows, lambda j, be, nu: (j, 0)),
        scratch_shapes=[pltpu.VMEM((d, de), BF16), pltpu.VMEM((d, de), BF16), pltpu.VMEM((de, d), BF16)])
    return pl.pallas_call(
        _expert_kernel,
        grid_spec=grid_spec,
        out_shape=jax.ShapeDtypeStruct(xp.shape, F32),
        compiler_params=_cparams(("arbitrary",), VMEM_LIMIT),
        name="experts",
    )(blk_e, n_used, xp, w1, w3, w2)


def _combine_kernel(y0_ref, y1_ref, route_ref, x1_ref, mod_ref, gf_ref, o_ref):
    d = x1_ref.shape[2]
    route = route_ref[0]
    moe = _load_tile_rows(y0_ref) * route[:, 4:5] + _load_tile_rows(y1_ref) * route[:, 5:6]
    gate2 = mod_ref[0][:, 5 * d:6 * d]
    xo = x1_ref[0] + (1.0 + gate2) * moe
    ms = jnp.mean(xo * xo, axis=-1, keepdims=True)
    o_ref[0] = xo * lax.rsqrt(ms + EPS) * gf_ref[...]


def _combine(yg, route, x1, mod3, gf, *, tm=256):
    b, s, d = x1.shape
    spt = s // tm
    nt = b * spt
    rows = (tm * ROW_SUBLANES, LANE)
    return pl.pallas_call(
        _combine_kernel,
        grid=(b, spt),
        in_specs=[pl.BlockSpec(rows, lambda bi, i: (bi * spt + i, 0)),
                  pl.BlockSpec(rows, lambda bi, i: (nt + bi * spt + i, 0)),
                  pl.BlockSpec((1, tm, ROUTE_COLS), lambda bi, i: (bi, i, 0)),
                  pl.BlockSpec((1, tm, d), lambda bi, i: (bi, i, 0)),
                  pl.BlockSpec((1, 1, mod3.shape[-1]), lambda bi, i: (bi, 0, 0)),
                  pl.BlockSpec((1, d), lambda bi, i: (0, 0))],
        out_specs=pl.BlockSpec((1, tm, d), lambda bi, i: (bi, i, 0)),
        out_shape=jax.ShapeDtypeStruct((b, s, d), F32),
        compiler_params=_cparams(("parallel", "parallel"), VMEM_LIMIT),
        name="combine",
    )(yg, yg, route, x1, mod3, gf)


def _slot_plan(route, counts, n_tok):
    sizes = counts[0, EXPERT_LANE0:EXPERT_LANE0 + N_EXPERTS].astype(jnp.int32)
    padded = (sizes + EXPERT_BLOCK - 1) // EXPERT_BLOCK * EXPERT_BLOCK
    pad_ends = jnp.cumsum(padded)
    pad_starts = pad_ends - padded
    eid = route[..., 0:TOP_K].astype(jnp.int32).reshape(n_tok, TOP_K)
    rank = route[..., TOP_K:2 * TOP_K].astype(jnp.int32).reshape(n_tok, TOP_K)
    dest = (pad_starts[eid] + rank).T.reshape(TOP_K * n_tok)
    n_blocks = (n_tok * TOP_K + N_EXPERTS * (EXPERT_BLOCK - 1) + EXPERT_BLOCK - 1) // EXPERT_BLOCK
    gran = SC_CORES * SC_SUBCORES * SC_CHUNK // math.gcd(SC_CORES * SC_SUBCORES * SC_CHUNK, EXPERT_BLOCK)
    n_blocks = (n_blocks + gran - 1) // gran * gran
    n_used = pad_ends[-1] // EXPERT_BLOCK
    blk = jnp.minimum(jnp.arange(n_blocks), n_used - 1)
    blk_e = jnp.minimum(jnp.sum(pad_ends[None, :] <= (blk * EXPERT_BLOCK)[:, None], axis=1), N_EXPERTS - 1)
    tok = jnp.tile(jnp.arange(n_tok, dtype=jnp.int32), TOP_K)
    src = jnp.zeros((n_blocks * EXPERT_BLOCK,), jnp.int32).at[dest].set(tok, unique_indices=True)
    return dest, src, blk_e.astype(jnp.int32), n_used.reshape(1).astype(jnp.int32)


def kernel(x, c, w_mod, b_mod, norm1_g, w_in, conv_w, conv_b, lru_wx, lru_bx, lru_wa, lru_ba, lru_lambda, w_attn_o, w_lru_o, w_out, norm2_g, w_grp, b_grp, w_exp, b_exp, w1, w3, w2, norm_f_g):
    b, s, d = x.shape
    assert d == D_MODEL and s == SPAN * DILATIONS[-1] and w_mod.shape[0] == 1
    mod3 = _modulation(c, w_mod[0], b_mod[0]).reshape(b, 1, 6 * d)
    qkv0, qkv1, qkv2, xr, yr, gl = _projection(x, mod3, norm1_g[0].reshape(1, d), _prep_w_in(w_in[0]))
    attn = _attention((qkv0, qkv1, qkv2), b, s)
    lru = _lru_branch(xr, yr, conv_w[0], conv_b[0], lru_wx[0], lru_bx[0], lru_wa[0], lru_ba[0], lru_lambda[0])

    n_router = N_GROUPS + N_EXPERTS
    wr = jnp.pad(jnp.concatenate([w_grp[0], w_exp[0]], axis=1), ((0, 0), (0, LANE - n_router))).astype(BF16)
    br = jnp.pad(jnp.concatenate([b_grp[0], b_exp[0]]), (0, LANE - n_router)).reshape(1, LANE)
    x1, h2, route, counts = _mix_route(
        attn, lru, gl, x, mod3, w_attn_o[0].astype(BF16), w_lru_o[0].astype(BF16), w_out[0].astype(BF16),
        norm2_g[0].reshape(1, d), wr, br)

    n_tok = b * s
    dest, src, blk_e, n_used = _slot_plan(route, counts, n_tok)
    as_rows = lambda a: a.reshape(-1, ROW_SUBLANES, LANE)
    as_tiles = lambda a: a.reshape(-1, LANE)
    xp = as_tiles(_sc_gather_rows(as_rows(h2), src))
    yp = _experts(xp, blk_e, n_used, w1[0], w3[0], w2[0])
    yg = as_tiles(_sc_gather_rows(as_rows(yp), dest))
    return _combine(yg, route, x1, mod3, norm_f_g.reshape(1, d))
```

```python
import functools
import math

import jax
import jax.numpy as jnp
from jax import lax
from jax.experimental import pallas as pl
from jax.experimental.pallas import tpu as pltpu
from jax.experimental.pallas import tpu_sc as plsc

F32 = jnp.float32
BF16 = jnp.bfloat16

D_MODEL = 1024
HEAD_DIM = 64
N_SLOTS = 8
SPAN = 128
DILATIONS = (1, 4, 16)
GROUP_COLS = 3 * N_SLOTS * HEAD_DIM
ATTN_WIDTH = len(DILATIONS) * N_SLOTS * HEAD_DIM
ATTN_OUT = N_SLOTS * HEAD_DIM
LRU_WIDTH = D_MODEL
LRU_BLOCK_DIM = 64
CONV_WIDTH = 4
LRU_C = 8.0
N_GROUPS = 4
EXPERTS_PER_GROUP = 8
N_EXPERTS = N_GROUPS * EXPERTS_PER_GROUP
D_EXPERT = D_MODEL // 2
EPS = 1e-6
LANE = 128
VMEM_LIMIT = 56 * 1024 * 1024


def _cparams(sem, vmem=None):
    return pltpu.CompilerParams(dimension_semantics=sem, vmem_limit_bytes=vmem)


def _resident(shape):
    nd = len(shape)
    return pl.BlockSpec(shape, lambda *_: (0,) * nd, pipeline_mode=pl.Buffered(1))


def _mod_kernel(c_ref, w_ref, b_ref, o_ref):
    c = c_ref[...]
    ca = c * jax.nn.sigmoid(c)
    o_ref[...] = jnp.dot(ca.astype(BF16), w_ref[...].astype(BF16),
                         preferred_element_type=F32) + b_ref[...]


def _modulation(c, w_mod, b_mod):
    b, d = c.shape
    n = w_mod.shape[1]
    tn = n // 4
    return pl.pallas_call(
        _mod_kernel,
        grid=(n // tn,),
        in_specs=[pl.BlockSpec((b, d), lambda j: (0, 0)),
                  pl.BlockSpec((d, tn), lambda j: (0, j)),
                  pl.BlockSpec((1, tn), lambda j: (0, j))],
        out_specs=pl.BlockSpec((b, tn), lambda j: (0, j)),
        out_shape=jax.ShapeDtypeStruct((b, n), F32),
        compiler_params=_cparams(("arbitrary",)),
        name="modulation",
    )(c, w_mod, b_mod.reshape(1, n))


def _rms_mod(x, g, scale, shift):
    ms = jnp.mean(x * x, axis=-1, keepdims=True)
    return x * lax.rsqrt(ms + EPS) * g * (1.0 + scale) + shift


def _proj_kernel(x_ref, mod_ref, g_ref, w_ref, qkv0_ref, qkv1_ref, qkv2_ref,
                 xr_ref, yr_ref, gl_ref, hs_ref, *, tm):
    d_model = x_ref.shape[-1]
    m = mod_ref[0]
    h = _rms_mod(x_ref[0], g_ref[...], m[:, d_model:2 * d_model], m[:, 0:d_model])

    def mm(hv, lo, hi):
        return jnp.dot(hv, w_ref[:, lo:hi], preferred_element_type=F32)

    hb = h.astype(BF16)
    c0 = len(DILATIONS) * GROUP_COLS
    qkv0_ref[0] = mm(hb, 0, GROUP_COLS).astype(BF16)
    xr_ref[0] = mm(hb, c0, c0 + LRU_WIDTH).astype(BF16)
    yr_ref[0] = mm(hb, c0 + LRU_WIDTH, c0 + 2 * LRU_WIDTH).astype(BF16)
    gl_ref[0] = mm(hb, c0 + 2 * LRU_WIDTH, c0 + 2 * LRU_WIDTH + 2 * d_model).astype(BF16)

    n_slab = d_model // LANE
    for j in range(n_slab):
        hs_ref[j] = h[:, j * LANE:(j + 1) * LANE]
    for g, out_ref in ((1, qkv1_ref), (2, qkv2_ref)):
        d = DILATIONS[g]
        rows = tm // d
        hp = jnp.concatenate(
            [jnp.concatenate([hs_ref[j, pl.ds(p, rows, stride=d), :] for j in range(n_slab)], axis=1)
             for p in range(d)], axis=0).astype(BF16)
        res = mm(hp, g * GROUP_COLS, (g + 1) * GROUP_COLS).astype(BF16)
        for p in range(d):
            out_ref[p] = res[p * rows:(p + 1) * rows]


def _projection(x, mod3, g1, w_r, *, tm=256):
    b, s, d = x.shape
    n = w_r.shape[1]
    assert s % tm == 0 and tm % (16 * DILATIONS[-1]) == 0
    out_shape = [jax.ShapeDtypeStruct((b * dd, s // dd, GROUP_COLS), BF16) for dd in DILATIONS]
    out_shape += [jax.ShapeDtypeStruct((b, s, LRU_WIDTH), BF16),
                  jax.ShapeDtypeStruct((b, s, LRU_WIDTH), BF16),
                  jax.ShapeDtypeStruct((b, s, 2 * d), BF16)]
    out_specs = [pl.BlockSpec((dd, tm // dd, GROUP_COLS), lambda bi, i: (bi, i, 0)) for dd in DILATIONS]
    out_specs += [pl.BlockSpec((1, tm, LRU_WIDTH), lambda bi, i: (bi, i, 0)),
                  pl.BlockSpec((1, tm, LRU_WIDTH), lambda bi, i: (bi, i, 0)),
                  pl.BlockSpec((1, tm, 2 * d), lambda bi, i: (bi, i, 0))]
    return pl.pallas_call(
        functools.partial(_proj_kernel, tm=tm),
        grid=(b, s // tm),
        in_specs=[pl.BlockSpec((1, tm, d), lambda bi, i: (bi, i, 0)),
                  pl.BlockSpec((1, 1, mod3.shape[-1]), lambda bi, i: (bi, 0, 0)),
                  pl.BlockSpec((1, d), lambda bi, i: (0, 0)),
                  _resident((d, n))],
        out_specs=out_specs,
        out_shape=out_shape,
        scratch_shapes=[pltpu.VMEM((d // LANE, tm, LANE), F32)],
        compiler_params=_cparams(("parallel", "parallel"), VMEM_LIMIT),
        name="projection",
    )(x, mod3, g1, w_r)


def _attn_kernel(q0, k0, v0, q1, k1, v1, q2, k2, v2, o_ref, acc_ref, lse_ref, bias_ref, *, seq):
    hcols = o_ref.shape[-1]
    n_head = hcols // HEAD_DIM
    head_of_lane = lax.broadcasted_iota(jnp.int32, (SPAN, hcols), 1) // HEAD_DIM
    head_mask_b = [jnp.where(head_of_lane == h, 1.0, 0.0).astype(BF16) for h in range(n_head)]

    def by_head(parts):
        out = parts[n_head - 1]
        for h in range(n_head - 2, -1, -1):
            out = jnp.where(head_of_lane == h, parts[h], out)
        return out

    qi = lax.broadcasted_iota(jnp.int32, (n_head * SPAN, 2 * SPAN), 0) % SPAN
    ki = lax.broadcasted_iota(jnp.int32, (n_head * SPAN, 2 * SPAN), 1)
    band = (ki >= qi) & (ki <= qi + SPAN)
    bias_ref[0] = jnp.where(band, 0.0, -jnp.inf)
    bias_ref[1] = jnp.where(band & (ki >= SPAN), 0.0, -jnp.inf)

    for g, (q_ref, k_ref, v_ref) in enumerate(((q0, k0, v0), (q1, k1, v1), (q2, k2, v2))):
        d = DILATIONS[g]
        n_blk = seq // d // SPAN

        def tile(n, carry, q_ref=q_ref, k_ref=k_ref, v_ref=v_ref, d=d, n_blk=n_blk, g=g):
            p = n // n_blk
            blk = n % n_blk
            r0 = pl.multiple_of(blk * SPAN, SPAN)
            rp = pl.multiple_of(jnp.maximum(blk - 1, 0) * SPAN, SPAN)
            q = q_ref[p, pl.ds(r0, SPAN), :]
            kk = jnp.concatenate([k_ref[p, pl.ds(rp, SPAN), :], k_ref[p, pl.ds(r0, SPAN), :]], axis=0)
            vv = jnp.concatenate([v_ref[p, pl.ds(rp, SPAN), :], v_ref[p, pl.ds(r0, SPAN), :]], axis=0)
            qs = jnp.concatenate([q * head_mask_b[h] for h in range(n_head)], axis=0)
            sc = lax.dot_general(qs, kk, (((1,), (1,)), ((), ())), preferred_element_type=F32)
            sc = sc + bias_ref[jnp.where(blk > 0, 0, 1)]
            mx = jnp.max(sc, axis=-1, keepdims=True)
            e = jnp.exp(sc - mx)
            den = jnp.sum(e, axis=-1, keepdims=True)
            pv = jnp.dot(e.astype(BF16), vv, preferred_element_type=F32)
            lse = mx + jnp.log(den)
            rows_of = lambda a: [a[h * SPAN:(h + 1) * SPAN] for h in range(n_head)]
            o = by_head(rows_of(pv)) / by_head(rows_of(den))
            l = by_head(rows_of(lse))
            start = p + d * r0
            for j in range(hcols // LANE):
                rows = pl.ds(start, SPAN, stride=d) if d > 1 else pl.ds(start, SPAN)
                acc_ref[g, j, rows, :] = o[:, j * LANE:(j + 1) * LANE]
                lse_ref[g, j, rows, :] = l[:, j * LANE:(j + 1) * LANE]
            return carry

        lax.fori_loop(0, seq // SPAN, tile, 0, unroll=8)

    chunk = 256

    def combine(c, carry):
        r = pl.multiple_of(c * chunk, chunk)
        for j in range(hcols // LANE):
            ls = [lse_ref[g, j, pl.ds(r, chunk), :] for g in range(len(DILATIONS))]
            mx = jnp.maximum(jnp.maximum(ls[0], ls[1]), ls[2])
            ws = [jnp.exp(v - mx) for v in ls]
            num = ws[0] * acc_ref[0, j, pl.ds(r, chunk), :]
            for g in range(1, len(DILATIONS)):
                num = num + ws[g] * acc_ref[g, j, pl.ds(r, chunk), :]
            o_ref[0, pl.ds(r, chunk), j * LANE:(j + 1) * LANE] = (num / (ws[0] + ws[1] + ws[2])).astype(BF16)
        return carry

    lax.fori_loop(0, seq // chunk, combine, 0)


def _attention(qkvs, b, s):
    hcols = 4 * HEAD_DIM
    n_hg = ATTN_OUT // hcols
    ncb = ATTN_OUT // hcols
    in_specs, args = [], []
    for g, d in enumerate(DILATIONS):
        for part in range(3):
            in_specs.append(pl.BlockSpec((d, s // d, hcols),
                                         lambda bi, hg, part=part: (bi, 0, part * ncb + hg)))
            args.append(qkvs[g])
    return pl.pallas_call(
        functools.partial(_attn_kernel, seq=s),
        grid=(b, n_hg),
        in_specs=in_specs,
        out_specs=pl.BlockSpec((1, s, hcols), lambda bi, hg: (bi, 0, hg)),
        out_shape=jax.ShapeDtypeStruct((b, s, ATTN_OUT), BF16),
        scratch_shapes=[pltpu.VMEM((len(DILATIONS), hcols // LANE, s, LANE), F32),
                        pltpu.VMEM((len(DILATIONS), hcols // LANE, s, LANE), F32),
                        pltpu.VMEM((2, (hcols // HEAD_DIM) * SPAN, 2 * SPAN), F32)],
        compiler_params=_cparams(("parallel", "parallel"), VMEM_LIMIT),
        name="dilated_attention",
    )(*args)


def _gelu_tanh(y):
    return y * (0.5 * (1.0 + jnp.tanh(0.7978845608028654 * (y + 0.044715 * (y * y * y)))))


def _expm1(u):
    e = jnp.exp(u)
    em1 = e - 1.0
    near = jnp.abs(u) < 0.5
    safe_log = jnp.log(jnp.where(near & (e != 1.0), e, 2.0))
    return jnp.where(near, jnp.where(e == 1.0, u, em1 * u / safe_log), em1)


def _lru_kernel(xr_ref, yr_ref, cw_ref, cb_ref, wg_ref, bx_ref, ba_ref, lam_ref, o_ref,
                a_ref, b_ref):
    seq, tc = xr_ref.shape[1], xr_ref.shape[2]
    x = xr_ref[0].astype(F32)
    row = lax.broadcasted_iota(jnp.int32, (seq, tc), 0)
    cw = cw_ref[...]
    xc = x * cw[CONV_WIDTH - 1:CONV_WIDTH] + cb_ref[...]
    for k in range(1, CONV_WIDTH):
        xs = jnp.where(row >= k, pltpu.roll(x, k, axis=0), 0.0)
        xc = xc + xs * cw[CONV_WIDTH - 1 - k:CONV_WIDTH - k]
    gates = jnp.dot(xc.astype(BF16), wg_ref[0], preferred_element_type=F32)
    gate_i = jax.nn.sigmoid(gates[:, :tc] + bx_ref[...])
    gate_r = jax.nn.sigmoid(gates[:, tc:] + ba_ref[...])
    nl = -lam_ref[...]
    softplus = jnp.maximum(nl, 0.0) + jnp.log1p(jnp.exp(-jnp.abs(nl)))
    log_a = (-LRU_C) * gate_r * softplus
    a = jnp.exp(log_a)
    bv = jnp.sqrt(-_expm1(2.0 * log_a)) * gate_i * xc
    r8 = row % 8
    for k in (1, 2, 4):
        a_s = pltpu.roll(a, k, axis=0)
        b_s = pltpu.roll(bv, k, axis=0)
        take = r8 >= k
        bv = jnp.where(take, a * b_s + bv, bv)
        a = jnp.where(take, a * a_s, a)
    a_ref[...] = a
    b_ref[...] = bv

    def group(i, carry):
        r = pl.multiple_of(i * 8, 8)
        h = a_ref[pl.ds(r, 8), :] * carry + b_ref[pl.ds(r, 8), :]
        b_ref[pl.ds(r, 8), :] = h
        return jnp.broadcast_to(h[7:8, :], h.shape)

    lax.fori_loop(0, seq // 8, group, jnp.zeros((8, tc), F32), unroll=8)
    o_ref[0] = (b_ref[...] * _gelu_tanh(yr_ref[0].astype(F32))).astype(BF16)


def _lru_gate_weights(wx, wa, tc):
    nb, bd, _ = wx.shape
    per = tc // bd
    eye = jnp.eye(per, dtype=wx.dtype)

    def bdiag(w):
        w = w.reshape(nb // per, per, bd, bd)
        return jnp.einsum('cpio,pq->cpiqo', w, eye).reshape(nb // per, tc, tc)

    return jnp.concatenate([bdiag(wx), bdiag(wa)], axis=-1).astype(BF16)


def _lru_branch(xr, yr, conv_w, conv_b, wx, bx, wa, ba, lam, *, tc=256):
    b, s, c = xr.shape
    wg = _lru_gate_weights(wx, wa, tc)
    row = lambda v: v.reshape(1, c)
    tile = pl.BlockSpec((1, s, tc), lambda bi, ci: (bi, 0, ci))
    vec = pl.BlockSpec((1, tc), lambda bi, ci: (0, ci))
    return pl.pallas_call(
        _lru_kernel,
        grid=(b, c // tc),
        in_specs=[tile, tile,
                  pl.BlockSpec((CONV_WIDTH, tc), lambda bi, ci: (0, ci)), vec,
                  pl.BlockSpec((1, tc, 2 * tc), lambda bi, ci: (ci, 0, 0)),
                  vec, vec, vec],
        out_specs=tile,
        out_shape=jax.ShapeDtypeStruct((b, s, c), BF16),
        scratch_shapes=[pltpu.VMEM((s, tc), F32), pltpu.VMEM((s, tc), F32)],
        compiler_params=_cparams(("parallel", "parallel"), VMEM_LIMIT),
        name="rg_lru",
    )(xr, yr, conv_w, row(conv_b), wg, row(bx), row(ba), row(lam))


def _prep_w_in(w_in):
    a = ATTN_WIDTH
    gw = N_SLOTS * HEAD_DIM
    q = w_in[:, :a] * (HEAD_DIM ** -0.5)
    k = w_in[:, a:2 * a]
    v = w_in[:, 2 * a:3 * a]
    parts = []
    for g in range(len(DILATIONS)):
        sl = slice(g * gw, (g + 1) * gw)
        parts += [q[:, sl], k[:, sl], v[:, sl]]
    parts.append(w_in[:, 3 * a:])
    return jnp.concatenate(parts, axis=1).astype(BF16)


ROW_SUBLANES = D_MODEL // LANE


def _store_tile_rows(ref, v):
    n = v.shape[0]
    for j in range(ROW_SUBLANES):
        ref[pl.ds(j, n, stride=ROW_SUBLANES), :] = v[:, j * LANE:(j + 1) * LANE]


def _load_tile_rows(ref):
    n = ref.shape[0] // ROW_SUBLANES
    return jnp.concatenate([ref[pl.ds(j, n, stride=ROW_SUBLANES), :] for j in range(ROW_SUBLANES)], axis=-1)


SC_CORES, SC_SUBCORES = 2, 16
SC_CHUNK = 64


def _sc_gather_rows(table, idx):
    n = idx.shape[0]
    per_worker = n // (SC_CORES * SC_SUBCORES)
    n_chunks = per_worker // SC_CHUNK
    assert n_chunks * SC_CHUNK * SC_CORES * SC_SUBCORES == n
    mesh = plsc.VectorSubcoreMesh(core_axis_name="c", subcore_axis_name="s")

    def body(table_hbm, idx_hbm, out_hbm, idx_v, rows_v, sem):
        base = (lax.axis_index("s") * SC_CORES + lax.axis_index("c")) * per_worker

        @pl.loop(0, n_chunks)
        def _(i):
            off = pl.multiple_of(base + i * SC_CHUNK, SC_CHUNK)
            pltpu.sync_copy(idx_hbm.at[pl.ds(off, SC_CHUNK)], idx_v)
            pltpu.async_copy(table_hbm.at[idx_v], rows_v, sem).wait()
            pltpu.sync_copy(rows_v, out_hbm.at[pl.ds(off, SC_CHUNK)])

    return pl.kernel(
        body, mesh=mesh,
        out_type=jax.ShapeDtypeStruct((n,) + table.shape[1:], table.dtype),
        scratch_types=[pltpu.VMEM((SC_CHUNK,), jnp.int32),
                       pltpu.VMEM((SC_CHUNK,) + table.shape[1:], table.dtype),
                       pltpu.SemaphoreType.DMA],
        name="sc_gather_rows",
    )(table, idx)


def _sc_scatter_rows(rows, idx, n_out):
    n_rows = rows.shape[0]
    n_choice = idx.shape[0] // n_rows
    per_worker = n_rows // (SC_CORES * SC_SUBCORES)
    n_chunks = per_worker // SC_CHUNK
    assert n_chunks * SC_CHUNK * SC_CORES * SC_SUBCORES == n_rows and n_choice * n_rows == idx.shape[0]
    mesh = plsc.VectorSubcoreMesh(core_axis_name="c", subcore_axis_name="s")

    def body(rows_hbm, idx_hbm, out_hbm, idx_v, rows_v):
        base = (lax.axis_index("s") * SC_CORES + lax.axis_index("c")) * per_worker

        @pl.loop(0, n_chunks)
        def _(i):
            off = pl.multiple_of(base + i * SC_CHUNK, SC_CHUNK)
            pltpu.sync_copy(rows_hbm.at[pl.ds(off, SC_CHUNK)], rows_v)
            for k in range(n_choice):
                pltpu.sync_copy(idx_hbm.at[pl.ds(k * n_rows + off, SC_CHUNK)], idx_v)
                pltpu.sync_copy(rows_v, out_hbm.at[idx_v])

    return pl.kernel(
        body, mesh=mesh,
        out_type=jax.ShapeDtypeStruct((n_out,) + rows.shape[1:], rows.dtype),
        scratch_types=[pltpu.VMEM((SC_CHUNK,), jnp.int32),
                       pltpu.VMEM((SC_CHUNK,) + rows.shape[1:], rows.dtype)],
        name="sc_scatter_rows",
    )(rows, idx)


ROUTE_COLS = 8
EXPERT_LANE0 = N_GROUPS


def _mix_kernel(attn_ref, lru_ref, gl_ref, x_ref, mod_ref, wa_ref, wl_ref, wo_ref, g2_ref,
                wr_ref, br_ref, x1_ref, h2_ref, route_ref, cnt_ref, cnt_acc):
    d = x_ref.shape[-1]
    tm = x_ref.shape[1]

    @pl.when((pl.program_id(0) == 0) & (pl.program_id(1) == 0))
    def _():
        cnt_acc[...] = jnp.zeros_like(cnt_acc)

    m = mod_ref[0]
    gate1, shift2, scale2 = m[:, 2 * d:3 * d], m[:, 3 * d:4 * d], m[:, 4 * d:5 * d]
    ya = jnp.dot(attn_ref[0], wa_ref[...], preferred_element_type=F32)
    yl = jnp.dot(lru_ref[0], wl_ref[...], preferred_element_type=F32)
    glv = gl_ref[0].astype(F32)
    mixed = jax.nn.sigmoid(glv[:, :d]) * ya + jax.nn.sigmoid(glv[:, d:]) * yl
    y = jnp.dot(mixed.astype(BF16), wo_ref[...], preferred_element_type=F32)
    x1 = x_ref[0] + (1.0 + gate1) * y
    x1_ref[0] = x1
    h2 = _rms_mod(x1, g2_ref[...], scale2, shift2)
    _store_tile_rows(h2_ref, h2)
    logits = jnp.dot(h2.astype(BF16), wr_ref[...], preferred_element_type=F32) + br_ref[...]

    lane = lax.broadcasted_iota(jnp.int32, logits.shape, 1)
    neg = -jnp.inf
    nl = logits.shape[-1]

    def top(vals):
        mx = jnp.max(vals, axis=-1, keepdims=True)
        idx = jnp.min(jnp.where(vals == mx, lane, nl), axis=-1, keepdims=True)
        return mx, idx

    is_grp = lane < N_GROUPS
    gmax, gidx = top(jnp.where(is_grp, logits, neg))
    grp_gate = 1.0 / jnp.sum(jnp.where(is_grp, jnp.exp(logits - gmax), 0.0), axis=-1, keepdims=True)
    lo = EXPERT_LANE0 + EXPERTS_PER_GROUP * gidx
    el = jnp.where((lane >= lo) & (lane < lo + EXPERTS_PER_GROUP), logits, neg)
    v1, i1 = top(el)
    v2, i2 = top(jnp.where(lane == i1, neg, el))
    e21 = jnp.exp(v2 - v1)
    wt1 = grp_gate / (1.0 + e21)
    wt2 = wt1 * e21

    oh1 = jnp.where(lane == i1, 1.0, 0.0)
    oh2 = jnp.where(lane == i2, 1.0, 0.0)
    ohs = oh1 + oh2
    rr = lax.broadcasted_iota(jnp.int32, (tm, tm), 0)
    cc = lax.broadcasted_iota(jnp.int32, (tm, tm), 1)
    earlier = jnp.where(cc < rr, 1.0, 0.0).astype(BF16)
    before = jnp.dot(earlier, ohs.astype(BF16), preferred_element_type=F32) + cnt_acc[...]
    rank1 = jnp.sum(oh1 * before, axis=-1, keepdims=True)
    rank2 = jnp.sum(oh2 * before, axis=-1, keepdims=True)
    cnt_acc[...] = cnt_acc[...] + jnp.sum(ohs, axis=0, keepdims=True)
    cnt_ref[...] = cnt_acc[...]

    cols = [(i1 - EXPERT_LANE0).astype(F32), (i2 - EXPERT_LANE0).astype(F32), rank1, rank2, wt1, wt2]
    slab = jnp.zeros(logits.shape, F32)
    for j, v in enumerate(cols):
        slab = jnp.where(lane == j, v, slab)
    route_ref[0] = slab[:, :ROUTE_COLS]


def _mix_route(attn, lru, gl, x, mod3, wa, wl, wo, g2, wr, br, *, tm=512):
    b, s, d = x.shape
    tok = lambda w: pl.BlockSpec((1, tm, w), lambda bi, i: (bi, i, 0))
    return pl.pallas_call(
        _mix_kernel,
        grid=(b, s // tm),
        in_specs=[tok(attn.shape[-1]), tok(d), tok(2 * d), tok(d),
                  pl.BlockSpec((1, 1, mod3.shape[-1]), lambda bi, i: (bi, 0, 0)),
                  _resident(wa.shape), _resident(wl.shape), _resident(wo.shape),
                  pl.BlockSpec((1, d), lambda bi, i: (0, 0)),
                  _resident(wr.shape),
                  pl.BlockSpec((1, LANE), lambda bi, i: (0, 0))],
        out_specs=[tok(d),
                   pl.BlockSpec((tm * ROW_SUBLANES, LANE), lambda bi, i: (bi * (s // tm) + i, 0)),
                   tok(ROUTE_COLS),
                   pl.BlockSpec((1, LANE), lambda bi, i: (0, 0))],
        out_shape=[jax.ShapeDtypeStruct((b, s, d), F32),
                   jax.ShapeDtypeStruct((b * s * ROW_SUBLANES, LANE), F32),
                   jax.ShapeDtypeStruct((b, s, ROUTE_COLS), F32),
                   jax.ShapeDtypeStruct((1, LANE), F32)],
        scratch_shapes=[pltpu.VMEM((1, LANE), F32)],
        compiler_params=_cparams(("arbitrary", "arbitrary"), VMEM_LIMIT),
        name="mix_route",
    )(attn, lru, gl, x, mod3, wa, wl, wo, g2, wr, br)


TOP_K = 2
EXPERT_BLOCK = 256


def _expert_kernel(be_ref, nu_ref, x_ref, w1_ref, w3_ref, w2_ref, y_ref, wb1, wb3, wb2):
    j = pl.program_id(0)

    @pl.when(j < nu_ref[0])
    def _():
        @pl.when((j == 0) | (be_ref[j] != be_ref[jnp.maximum(j - 1, 0)]))
        def _():
            wb1[...] = w1_ref[0].astype(BF16)
            wb3[...] = w3_ref[0].astype(BF16)
            wb2[...] = w2_ref[0].astype(BF16)

        xb = _load_tile_rows(x_ref).astype(BF16)
        a = jnp.dot(xb, wb1[...], preferred_element_type=F32)
        g = jnp.dot(xb, wb3[...], preferred_element_type=F32)
        hm = (a * jax.nn.sigmoid(a) * g).astype(BF16)
        _store_tile_rows(y_ref, jnp.dot(hm, wb2[...], preferred_element_type=F32))

    @pl.when(j >= nu_ref[0])
    def _():
        y_ref[...] = jnp.zeros_like(y_ref)


def _experts(xp, blk_e, n_used, w1, w3, w2):
    ne, d, de = w1.shape
    nb = xp.shape[0] // (EXPERT_BLOCK * ROW_SUBLANES)
    rows = (EXPERT_BLOCK * ROW_SUBLANES, LANE)
    last = lambda j, nu: jnp.minimum(j, nu[0] - 1)
    grid_spec = pltpu.PrefetchScalarGridSpec(
        num_scalar_prefetch=2,
        grid=(nb,),
        in_specs=[pl.BlockSpec(rows, lambda j, be, nu: (last(j, nu), 0)),
                  pl.BlockSpec((1, d, de), lambda j, be, nu: (be[j], 0, 0)),
                  pl.BlockSpec((1, d, de), lambda j, be, nu: (be[j], 0, 0)),
                  pl.BlockSpec((1, de, d), lambda j, be, nu: (be[j], 0, 0))],
        out_specs=pl.BlockSpec(rows, lambda j, be, nu: (j, 0)),
        scratch_shapes=[pltpu.VMEM((d, de), BF16), pltpu.VMEM((d, de), BF16), pltpu.VMEM((de, d), BF16)])
    return pl.pallas_call(
        _expert_kernel,
        grid_spec=grid_spec,
        out_shape=jax.ShapeDtypeStruct(xp.shape, F32),
        compiler_params=_cparams(("arbitrary",), VMEM_LIMIT),
        name="experts",
    )(blk_e, n_used, xp, w1, w3, w2)


def _combine_kernel(y0_ref, y1_ref, route_ref, x1_ref, mod_ref, gf_ref, o_ref):
    d = x1_ref.shape[2]
    route = route_ref[0]
    moe = _load_tile_rows(y0_ref) * route[:, 4:5] + _load_tile_rows(y1_ref) * route[:, 5:6]
    gate2 = mod_ref[0][:, 5 * d:6 * d]
    xo = x1_ref[0] + (1.0 + gate2) * moe
    ms = jnp.mean(xo * xo, axis=-1, keepdims=True)
    o_ref[0] = xo * lax.rsqrt(ms + EPS) * gf_ref[...]


def _combine(yg, route, x1, mod3, gf, *, tm=256):
    b, s, d = x1.shape
    spt = s // tm
    nt = b * spt
    rows = (tm * ROW_SUBLANES, LANE)
    return pl.pallas_call(
        _combine_kernel,
        grid=(b, spt),
        in_specs=[pl.BlockSpec(rows, lambda bi, i: (bi * spt + i, 0)),
                  pl.BlockSpec(rows, lambda bi, i: (nt + bi * spt + i, 0)),
                  pl.BlockSpec((1, tm, ROUTE_COLS), lambda bi, i: (bi, i, 0)),
                  pl.BlockSpec((1, tm, d), lambda bi, i: (bi, i, 0)),
                  pl.BlockSpec((1, 1, mod3.shape[-1]), lambda bi, i: (bi, 0, 0)),
                  pl.BlockSpec((1, d), lambda bi, i: (0, 0))],
        out_specs=pl.BlockSpec((1, tm, d), lambda bi, i: (bi, i, 0)),
        out_shape=jax.ShapeDtypeStruct((b, s, d), F32),
        compiler_params=_cparams(("parallel", "parallel"), VMEM_LIMIT),
        name="combine",
    )(yg, yg, route, x1, mod3, gf)


def _slot_plan(route, counts, n_tok):
    sizes = counts[0, EXPERT_LANE0:EXPERT_LANE0 + N_EXPERTS].astype(jnp.int32)
    padded = (sizes + EXPERT_BLOCK - 1) // EXPERT_BLOCK * EXPERT_BLOCK
    pad_ends = jnp.cumsum(padded)
    pad_starts = pad_ends - padded
    eid = route[..., 0:TOP_K].astype(jnp.int32).reshape(n_tok, TOP_K)
    rank = route[..., TOP_K:2 * TOP_K].astype(jnp.int32).reshape(n_tok, TOP_K)
    dest = (pad_starts[eid] + rank).T.reshape(TOP_K * n_tok)
    n_blocks = (n_tok * TOP_K + N_EXPERTS * (EXPERT_BLOCK - 1) + EXPERT_BLOCK - 1) // EXPERT_BLOCK
    gran = SC_CORES * SC_SUBCORES * SC_CHUNK // math.gcd(SC_CORES * SC_SUBCORES * SC_CHUNK, EXPERT_BLOCK)
    n_blocks = (n_blocks + gran - 1) // gran * gran
    n_used = pad_ends[-1] // EXPERT_BLOCK
    blk = jnp.minimum(jnp.arange(n_blocks), n_used - 1)
    blk_e = jnp.minimum(jnp.sum(pad_ends[None, :] <= (blk * EXPERT_BLOCK)[:, None], axis=1), N_EXPERTS - 1)
    return dest, n_blocks * EXPERT_BLOCK, blk_e.astype(jnp.int32), n_used.reshape(1).astype(jnp.int32)


def kernel(x, c, w_mod, b_mod, norm1_g, w_in, conv_w, conv_b, lru_wx, lru_bx, lru_wa, lru_ba, lru_lambda, w_attn_o, w_lru_o, w_out, norm2_g, w_grp, b_grp, w_exp, b_exp, w1, w3, w2, norm_f_g):
    b, s, d = x.shape
    assert d == D_MODEL and s == SPAN * DILATIONS[-1] and w_mod.shape[0] == 1
    mod3 = _modulation(c, w_mod[0], b_mod[0]).reshape(b, 1, 6 * d)
    qkv0, qkv1, qkv2, xr, yr, gl = _projection(x, mod3, norm1_g[0].reshape(1, d), _prep_w_in(w_in[0]))
    attn = _attention((qkv0, qkv1, qkv2), b, s)
    lru = _lru_branch(xr, yr, conv_w[0], conv_b[0], lru_wx[0], lru_bx[0], lru_wa[0], lru_ba[0], lru_lambda[0])

    n_router = N_GROUPS + N_EXPERTS
    wr = jnp.pad(jnp.concatenate([w_grp[0], w_exp[0]], axis=1), ((0, 0), (0, LANE - n_router))).astype(BF16)
    br = jnp.pad(jnp.concatenate([b_grp[0], b_exp[0]]), (0, LANE - n_router)).reshape(1, LANE)
    x1, h2, route, counts = _mix_route(
        attn, lru, gl, x, mod3, w_attn_o[0].astype(BF16), w_lru_o[0].astype(BF16), w_out[0].astype(BF16),
        norm2_g[0].reshape(1, d), wr, br)

    n_tok = b * s
    dest, n_slots, blk_e, n_used = _slot_plan(route, counts, n_tok)
    as_rows = lambda a: a.reshape(-1, ROW_SUBLANES, LANE)
    as_tiles = lambda a: a.reshape(-1, LANE)
    xp = as_tiles(_sc_scatter_rows(as_rows(h2), dest, n_slots))
    yp = _experts(xp, blk_e, n_used, w1[0], w3[0], w2[0])
    yg = as_tiles(_sc_gather_rows(as_rows(yp), dest))
    return _combine(yg, route, x1, mod3, norm_f_g.reshape(1, d))
```

```python
import functools
import math

import jax
import jax.numpy as jnp
from jax import lax
from jax.experimental import pallas as pl
from jax.experimental.pallas import tpu as pltpu
from jax.experimental.pallas import tpu_sc as plsc

F32 = jnp.float32
BF16 = jnp.bfloat16

D_MODEL = 1024
HEAD_DIM = 64
N_SLOTS = 8
SPAN = 128
DILATIONS = (1, 4, 16)
GROUP_COLS = 3 * N_SLOTS * HEAD_DIM
ATTN_WIDTH = len(DILATIONS) * N_SLOTS * HEAD_DIM
ATTN_OUT = N_SLOTS * HEAD_DIM
LRU_WIDTH = D_MODEL
LRU_BLOCK_DIM = 64
CONV_WIDTH = 4
CONV_TAIL = 8
LRU_C = 8.0
N_GROUPS = 4
EXPERTS_PER_GROUP = 8
N_EXPERTS = N_GROUPS * EXPERTS_PER_GROUP
D_EXPERT = D_MODEL // 2
EPS = 1e-6
LANE = 128
VMEM_LIMIT = 56 * 1024 * 1024


def _cparams(sem, vmem=None):
    return pltpu.CompilerParams(dimension_semantics=sem, vmem_limit_bytes=vmem)


def _resident(shape):
    nd = len(shape)
    return pl.BlockSpec(shape, lambda *_: (0,) * nd, pipeline_mode=pl.Buffered(1))


def _mod_kernel(c_ref, w_ref, b_ref, o_ref):
    c = c_ref[...]
    ca = c * jax.nn.sigmoid(c)
    o_ref[...] = jnp.dot(ca.astype(BF16), w_ref[...].astype(BF16),
                         preferred_element_type=F32) + b_ref[...]


def _modulation(c, w_mod, b_mod):
    b, d = c.shape
    n = w_mod.shape[1]
    tn = n // 4
    return pl.pallas_call(
        _mod_kernel,
        grid=(n // tn,),
        in_specs=[pl.BlockSpec((b, d), lambda j: (0, 0)),
                  pl.BlockSpec((d, tn), lambda j: (0, j)),
                  pl.BlockSpec((1, tn), lambda j: (0, j))],
        out_specs=pl.BlockSpec((b, tn), lambda j: (0, j)),
        out_shape=jax.ShapeDtypeStruct((b, n), F32),
        compiler_params=_cparams(("arbitrary",)),
        name="modulation",
    )(c, w_mod, b_mod.reshape(1, n))


def _rms_mod(x, g, scale, shift):
    ms = jnp.mean(x * x, axis=-1, keepdims=True)
    return x * lax.rsqrt(ms + EPS) * g * (1.0 + scale) + shift


def _gelu_tanh(y):
    return y * (0.5 * (1.0 + jnp.tanh(0.7978845608028654 * (y + 0.044715 * (y * y * y)))))


def _proj_kernel(x_ref, mod_ref, g_ref, w_ref, cw_ref, cb_ref, qkv0_ref, qkv1_ref, qkv2_ref,
                 xc_ref, gy_ref, gl_ref, hs_ref, tail_ref, *, tm):
    d_model = x_ref.shape[-1]
    m = mod_ref[0]
    h = _rms_mod(x_ref[0], g_ref[...], m[:, d_model:2 * d_model], m[:, 0:d_model])

    def mm(hv, lo, hi):
        return jnp.dot(hv, w_ref[:, lo:hi], preferred_element_type=F32)

    hb = h.astype(BF16)
    c0 = len(DILATIONS) * GROUP_COLS
    qkv0_ref[0] = mm(hb, 0, GROUP_COLS).astype(BF16)

    @pl.when(pl.program_id(1) == 0)
    def _():
        tail_ref[...] = jnp.zeros_like(tail_ref)

    xr = mm(hb, c0, c0 + LRU_WIDTH)
    xe = jnp.concatenate([tail_ref[...], xr], axis=0)
    tail_ref[...] = xr[tm - CONV_TAIL:, :]
    cw = cw_ref[...]
    xc = xr * cw[CONV_WIDTH - 1:CONV_WIDTH] + cb_ref[...]
    for k in range(1, CONV_WIDTH):
        xc = xc + xe[CONV_TAIL - k:CONV_TAIL - k + tm, :] * cw[CONV_WIDTH - 1 - k:CONV_WIDTH - k]
    xc_ref[0] = xc.astype(BF16)
    gy_ref[0] = _gelu_tanh(mm(hb, c0 + LRU_WIDTH, c0 + 2 * LRU_WIDTH)).astype(BF16)
    gl_ref[0] = mm(hb, c0 + 2 * LRU_WIDTH, c0 + 2 * LRU_WIDTH + 2 * d_model).astype(BF16)

    n_slab = d_model // LANE
    for j in range(n_slab):
        hs_ref[j] = h[:, j * LANE:(j + 1) * LANE]
    for g, out_ref in ((1, qkv1_ref), (2, qkv2_ref)):
        d = DILATIONS[g]
        rows = tm // d
        hp = jnp.concatenate(
            [jnp.concatenate([hs_ref[j, pl.ds(p, rows, stride=d), :] for j in range(n_slab)], axis=1)
             for p in range(d)], axis=0).astype(BF16)
        res = mm(hp, g * GROUP_COLS, (g + 1) * GROUP_COLS).astype(BF16)
        for p in range(d):
            out_ref[p] = res[p * rows:(p + 1) * rows]


def _projection(x, mod3, g1, w_r, conv_w, conv_b, *, tm=256):
    b, s, d = x.shape
    n = w_r.shape[1]
    assert s % tm == 0 and tm % (16 * DILATIONS[-1]) == 0 and CONV_TAIL >= CONV_WIDTH - 1
    out_shape = [jax.ShapeDtypeStruct((b * dd, s // dd, GROUP_COLS), BF16) for dd in DILATIONS]
    out_shape += [jax.ShapeDtypeStruct((b, s, LRU_WIDTH), BF16),
                  jax.ShapeDtypeStruct((b, s, LRU_WIDTH), BF16),
                  jax.ShapeDtypeStruct((b, s, 2 * d), BF16)]
    out_specs = [pl.BlockSpec((dd, tm // dd, GROUP_COLS), lambda bi, i: (bi, i, 0)) for dd in DILATIONS]
    out_specs += [pl.BlockSpec((1, tm, LRU_WIDTH), lambda bi, i: (bi, i, 0)),
                  pl.BlockSpec((1, tm, LRU_WIDTH), lambda bi, i: (bi, i, 0)),
                  pl.BlockSpec((1, tm, 2 * d), lambda bi, i: (bi, i, 0))]
    return pl.pallas_call(
        functools.partial(_proj_kernel, tm=tm),
        grid=(b, s // tm),
        in_specs=[pl.BlockSpec((1, tm, d), lambda bi, i: (bi, i, 0)),
                  pl.BlockSpec((1, 1, mod3.shape[-1]), lambda bi, i: (bi, 0, 0)),
                  pl.BlockSpec((1, d), lambda bi, i: (0, 0)),
                  _resident((d, n)),
                  pl.BlockSpec((CONV_WIDTH, LRU_WIDTH), lambda bi, i: (0, 0)),
                  pl.BlockSpec((1, LRU_WIDTH), lambda bi, i: (0, 0))],
        out_specs=out_specs,
        out_shape=out_shape,
        scratch_shapes=[pltpu.VMEM((d // LANE, tm, LANE), F32), pltpu.VMEM((CONV_TAIL, LRU_WIDTH), F32)],
        compiler_params=_cparams(("parallel", "arbitrary"), VMEM_LIMIT),
        name="projection",
    )(x, mod3, g1, w_r, conv_w, conv_b.reshape(1, LRU_WIDTH))


def _attn_kernel(q0, k0, v0, q1, k1, v1, q2, k2, v2, o_ref, acc_ref, lse_ref, bias_ref, *, seq):
    hcols = o_ref.shape[-1]
    n_head = hcols // HEAD_DIM
    head_of_lane = lax.broadcasted_iota(jnp.int32, (SPAN, hcols), 1) // HEAD_DIM
    head_mask_b = [jnp.where(head_of_lane == h, 1.0, 0.0).astype(BF16) for h in range(n_head)]

    def by_head(parts):
        out = parts[n_head - 1]
        for h in range(n_head - 2, -1, -1):
            out = jnp.where(head_of_lane == h, parts[h], out)
        return out

    qi = lax.broadcasted_iota(jnp.int32, (n_head * SPAN, 2 * SPAN), 0) % SPAN
    ki = lax.broadcasted_iota(jnp.int32, (n_head * SPAN, 2 * SPAN), 1)
    band = (ki >= qi) & (ki <= qi + SPAN)
    bias_ref[0] = jnp.where(band, 0.0, -jnp.inf)
    bias_ref[1] = jnp.where(band & (ki >= SPAN), 0.0, -jnp.inf)

    for g, (q_ref, k_ref, v_ref) in enumerate(((q0, k0, v0), (q1, k1, v1), (q2, k2, v2))):
        d = DILATIONS[g]
        n_blk = seq // d // SPAN

        def tile(n, carry, q_ref=q_ref, k_ref=k_ref, v_ref=v_ref, d=d, n_blk=n_blk, g=g):
            p = n // n_blk
            blk = n % n_blk
            r0 = pl.multiple_of(blk * SPAN, SPAN)
            rp = pl.multiple_of(jnp.maximum(blk - 1, 0) * SPAN, SPAN)
            q = q_ref[p, pl.ds(r0, SPAN), :]
            kk = jnp.concatenate([k_ref[p, pl.ds(rp, SPAN), :], k_ref[p, pl.ds(r0, SPAN), :]], axis=0)
            vv = jnp.concatenate([v_ref[p, pl.ds(rp, SPAN), :], v_ref[p, pl.ds(r0, SPAN), :]], axis=0)
            qs = jnp.concatenate([q * head_mask_b[h] for h in range(n_head)], axis=0)
            sc = lax.dot_general(qs, kk, (((1,), (1,)), ((), ())), preferred_element_type=F32)
            sc = sc + bias_ref[jnp.where(blk > 0, 0, 1)]
            mx = jnp.max(sc, axis=-1, keepdims=True)
            e = jnp.exp(sc - mx)
            den = jnp.sum(e, axis=-1, keepdims=True)
            pv = jnp.dot(e.astype(BF16), vv, preferred_element_type=F32)
            lse = mx + jnp.log(den)
            rows_of = lambda a: [a[h * SPAN:(h + 1) * SPAN] for h in range(n_head)]
            o = by_head(rows_of(pv)) / by_head(rows_of(den))
            l = by_head(rows_of(lse))
            start = p + d * r0
            for j in range(hcols // LANE):
                rows = pl.ds(start, SPAN, stride=d) if d > 1 else pl.ds(start, SPAN)
                acc_ref[g, j, rows, :] = o[:, j * LANE:(j + 1) * LANE]
                lse_ref[g, j, rows, :] = l[:, j * LANE:(j + 1) * LANE]
            return carry

        lax.fori_loop(0, seq // SPAN, tile, 0, unroll=8)

    chunk = 256

    def combine(c, carry):
        r = pl.multiple_of(c * chunk, chunk)
        for j in range(hcols // LANE):
            ls = [lse_ref[g, j, pl.ds(r, chunk), :] for g in range(len(DILATIONS))]
            mx = jnp.maximum(jnp.maximum(ls[0], ls[1]), ls[2])
            ws = [jnp.exp(v - mx) for v in ls]
            num = ws[0] * acc_ref[0, j, pl.ds(r, chunk), :]
            for g in range(1, len(DILATIONS)):
                num = num + ws[g] * acc_ref[g, j, pl.ds(r, chunk), :]
            o_ref[0, pl.ds(r, chunk), j * LANE:(j + 1) * LANE] = (num / (ws[0] + ws[1] + ws[2])).astype(BF16)
        return carry

    lax.fori_loop(0, seq // chunk, combine, 0)


def _attention(qkvs, b, s):
    hcols = 4 * HEAD_DIM
    n_hg = ATTN_OUT // hcols
    ncb = ATTN_OUT // hcols
    in_specs, args = [], []
    for g, d in enumerate(DILATIONS):
        for part in range(3):
            in_specs.append(pl.BlockSpec((d, s // d, hcols),
                                         lambda bi, hg, part=part: (bi, 0, part * ncb + hg)))
            args.append(qkvs[g])
    return pl.pallas_call(
        functools.partial(_attn_kernel, seq=s),
        grid=(b, n_hg),
        in_specs=in_specs,
        out_specs=pl.BlockSpec((1, s, hcols), lambda bi, hg: (bi, 0, hg)),
        out_shape=jax.ShapeDtypeStruct((b, s, ATTN_OUT), BF16),
        scratch_shapes=[pltpu.VMEM((len(DILATIONS), hcols // LANE, s, LANE), F32),
                        pltpu.VMEM((len(DILATIONS), hcols // LANE, s, LANE), F32),
                        pltpu.VMEM((2, (hcols // HEAD_DIM) * SPAN, 2 * SPAN), F32)],
        compiler_params=_cparams(("parallel", "parallel"), VMEM_LIMIT),
        name="dilated_attention",
    )(*args)


def _lru_kernel(xc_ref, gy_ref, wg_ref, bx_ref, ba_ref, lam_ref, o_ref, a_ref, b_ref, h_ref, *, pitch):
    nb, ts, tc = xc_ref.shape
    nl = tc // LANE

    @pl.when(pl.program_id(1) == 0)
    def _():
        h_ref[...] = jnp.zeros_like(h_ref)

    xb = xc_ref[...].reshape(nb * ts, tc)
    xc = xb.astype(F32)
    gates = jnp.dot(xb, wg_ref[0], preferred_element_type=F32)
    gate_i = jax.nn.sigmoid(gates[:, :tc] + bx_ref[...])
    gate_r = jax.nn.sigmoid(gates[:, tc:] + ba_ref[...])
    neg_lam = -lam_ref[...]
    softplus = jnp.maximum(neg_lam, 0.0) + jnp.log1p(jnp.exp(-jnp.abs(neg_lam)))
    log_a = (-LRU_C) * gate_r * softplus
    a = jnp.exp(log_a)
    bv = jnp.sqrt(jnp.tanh(-log_a) * (1.0 + a * a)) * gate_i * xc
    for bi in range(nb):
        for j in range(nl):
            a_ref[j, pl.ds(bi * pitch, ts), :] = a[bi * ts:(bi + 1) * ts, j * LANE:(j + 1) * LANE]
            b_ref[j, pl.ds(bi * pitch, ts), :] = bv[bi * ts:(bi + 1) * ts, j * LANE:(j + 1) * LANE]

    def step(t, hs):
        out = []
        for j in range(nl):
            rows = pl.ds(t, nb, stride=pitch)
            h = a_ref[j, rows, :] * hs[j] + b_ref[j, rows, :]
            b_ref[j, rows, :] = h
            out.append(h)
        return tuple(out)

    hs = lax.fori_loop(0, ts, step, tuple(h_ref[j] for j in range(nl)), unroll=8)
    for j in range(nl):
        h_ref[j] = hs[j]
    for bi in range(nb):
        h = jnp.concatenate([b_ref[j, pl.ds(bi * pitch, ts), :] for j in range(nl)], axis=1)
        o_ref[bi] = (h * gy_ref[bi].astype(F32)).astype(BF16)


def _lru_gate_weights(wx, wa, tc):
    nb, bd, _ = wx.shape
    per = tc // bd
    eye = jnp.eye(per, dtype=wx.dtype)

    def bdiag(w):
        w = w.reshape(nb // per, per, bd, bd)
        return jnp.einsum('cpio,pq->cpiqo', w, eye).reshape(nb // per, tc, tc)

    return jnp.concatenate([bdiag(wx), bdiag(wa)], axis=-1).astype(BF16)


def _lru_branch(xc, gy, wx, bx, wa, ba, lam, *, tc=256, ts=128):
    b, s, c = xc.shape
    assert s % ts == 0 and c % tc == 0
    wg = _lru_gate_weights(wx, wa, tc)
    row = lambda v: v.reshape(1, c)
    tile = pl.BlockSpec((b, ts, tc), lambda ci, ti: (0, ti, ci))
    vec = pl.BlockSpec((1, tc), lambda ci, ti: (0, ci))
    pitch = ts + 8
    return pl.pallas_call(
        functools.partial(_lru_kernel, pitch=pitch),
        grid=(c // tc, s // ts),
        in_specs=[tile, tile,
                  pl.BlockSpec((1, tc, 2 * tc), lambda ci, ti: (ci, 0, 0)),
                  vec, vec, vec],
        out_specs=tile,
        out_shape=jax.ShapeDtypeStruct((b, s, c), BF16),
        scratch_shapes=[pltpu.VMEM((tc // LANE, b * pitch, LANE), F32),
                        pltpu.VMEM((tc // LANE, b * pitch, LANE), F32),
                        pltpu.VMEM((tc // LANE, b, LANE), F32)],
        compiler_params=_cparams(("parallel", "arbitrary"), VMEM_LIMIT),
        name="rg_lru",
    )(xc, gy, wg, row(bx), row(ba), row(lam))


def _prep_w_in(w_in):
    a = ATTN_WIDTH
    gw = N_SLOTS * HEAD_DIM
    q = w_in[:, :a] * (HEAD_DIM ** -0.5)
    k = w_in[:, a:2 * a]
    v = w_in[:, 2 * a:3 * a]
    parts = []
    for g in range(len(DILATIONS)):
        sl = slice(g * gw, (g + 1) * gw)
        parts += [q[:, sl], k[:, sl], v[:, sl]]
    parts.append(w_in[:, 3 * a:])
    return jnp.concatenate(parts, axis=1).astype(BF16)


ROW_SUBLANES = D_MODEL // 2 // LANE


def _store_tile_rows(ref, v, row0=0):
    n, half = v.shape[0], v.shape[1] // 2
    lo = pltpu.bitcast(v[:, :half].astype(BF16).astype(F32), jnp.uint32)
    hi = pltpu.bitcast(v[:, half:].astype(BF16).astype(F32), jnp.uint32)
    words = (hi & jnp.uint32(0xFFFF0000)) | (lo >> 16)
    for j in range(ROW_SUBLANES):
        ref[pl.ds(row0 * ROW_SUBLANES + j, n, stride=ROW_SUBLANES), :] = words[:, j * LANE:(j + 1) * LANE]


def _load_tile_rows(ref):
    n = ref.shape[0] // ROW_SUBLANES
    words = [ref[pl.ds(j, n, stride=ROW_SUBLANES), :] for j in range(ROW_SUBLANES)]
    lo = [pltpu.bitcast(w << 16, F32) for w in words]
    hi = [pltpu.bitcast(w & jnp.uint32(0xFFFF0000), F32) for w in words]
    return jnp.concatenate(lo + hi, axis=-1)


SC_CORES, SC_SUBCORES = 2, 16
SC_CHUNK = 128


def _sc_gather_rows(table, idx):
    n = idx.shape[0]
    per_worker = n // (SC_CORES * SC_SUBCORES)
    n_chunks = per_worker // SC_CHUNK
    assert n_chunks * SC_CHUNK * SC_CORES * SC_SUBCORES == n
    mesh = plsc.VectorSubcoreMesh(core_axis_name="c", subcore_axis_name="s")

    def body(table_hbm, idx_hbm, out_hbm, idx_v, rows_v, sem):
        base = (lax.axis_index("s") * SC_CORES + lax.axis_index("c")) * per_worker

        @pl.loop(0, n_chunks)
        def _(i):
            off = pl.multiple_of(base + i * SC_CHUNK, SC_CHUNK)
            pltpu.sync_copy(idx_hbm.at[pl.ds(off, SC_CHUNK)], idx_v)
            pltpu.async_copy(table_hbm.at[idx_v], rows_v, sem).wait()
            pltpu.sync_copy(rows_v, out_hbm.at[pl.ds(off, SC_CHUNK)])

    return pl.kernel(
        body, mesh=mesh,
        out_type=jax.ShapeDtypeStruct((n,) + table.shape[1:], table.dtype),
        scratch_types=[pltpu.VMEM((SC_CHUNK,), jnp.int32),
                       pltpu.VMEM((SC_CHUNK,) + table.shape[1:], table.dtype),
                       pltpu.SemaphoreType.DMA],
        name="sc_gather_rows",
    )(table, idx)


def _sc_scatter_rows(rows, idx, n_out):
    n_rows = rows.shape[0]
    n_choice = idx.shape[0] // n_rows
    per_worker = n_rows // (SC_CORES * SC_SUBCORES)
    n_chunks = per_worker // SC_CHUNK
    assert n_chunks * SC_CHUNK * SC_CORES * SC_SUBCORES == n_rows and n_choice * n_rows == idx.shape[0]
    mesh = plsc.VectorSubcoreMesh(core_axis_name="c", subcore_axis_name="s")

    def body(rows_hbm, idx_hbm, out_hbm, idx_v, rows_v):
        base = (lax.axis_index("s") * SC_CORES + lax.axis_index("c")) * per_worker

        @pl.loop(0, n_chunks)
        def _(i):
            off = pl.multiple_of(base + i * SC_CHUNK, SC_CHUNK)
            pltpu.sync_copy(rows_hbm.at[pl.ds(off, SC_CHUNK)], rows_v)
            for k in range(n_choice):
                pltpu.sync_copy(idx_hbm.at[pl.ds(k * n_rows + off, SC_CHUNK)], idx_v)
                pltpu.sync_copy(rows_v, out_hbm.at[idx_v])

    return pl.kernel(
        body, mesh=mesh,
        out_type=jax.ShapeDtypeStruct((n_out,) + rows.shape[1:], rows.dtype),
        scratch_types=[pltpu.VMEM((SC_CHUNK,), jnp.int32),
                       pltpu.VMEM((SC_CHUNK,) + rows.shape[1:], rows.dtype)],
        name="sc_scatter_rows",
    )(rows, idx)


ROUTE_COLS = 8
EXPERT_LANE0 = N_GROUPS
MIX_SUBTILES = 1


def _mix_kernel(attn_ref, lru_ref, gl_ref, x_ref, mod_ref, wa_ref, wl_ref, wo_ref, g2_ref,
                wr_ref, br_ref, x1_ref, h2_ref, route_ref, cnt_ref, cnt_acc):
    d = x_ref.shape[-1]
    tm = x_ref.shape[1]

    @pl.when((pl.program_id(0) == 0) & (pl.program_id(1) == 0))
    def _():
        cnt_acc[...] = jnp.zeros_like(cnt_acc)

    sub = tm // MIX_SUBTILES
    for r0 in range(0, tm, sub):
        _mix_rows(pl.ds(r0, sub), r0, sub, attn_ref, lru_ref, gl_ref, x_ref, mod_ref, wa_ref, wl_ref, wo_ref,
                  g2_ref, wr_ref, br_ref, x1_ref, h2_ref, route_ref, cnt_acc)
    cnt_ref[...] = cnt_acc[...]


def _mix_rows(rows, r0, tm, attn_ref, lru_ref, gl_ref, x_ref, mod_ref, wa_ref, wl_ref, wo_ref, g2_ref,
              wr_ref, br_ref, x1_ref, h2_ref, route_ref, cnt_acc):
    d = x_ref.shape[-1]
    m = mod_ref[0]
    gate1, shift2, scale2 = m[:, 2 * d:3 * d], m[:, 3 * d:4 * d], m[:, 4 * d:5 * d]
    ya = jnp.dot(attn_ref[0, rows, :], wa_ref[...], preferred_element_type=F32)
    yl = jnp.dot(lru_ref[0, rows, :], wl_ref[...], preferred_element_type=F32)
    gates = jax.nn.sigmoid(gl_ref[0, rows, :].astype(F32))
    mixed = gates[:, :d] * ya + gates[:, d:] * yl
    y = jnp.dot(mixed.astype(BF16), wo_ref[...], preferred_element_type=F32)
    x1 = x_ref[0, rows, :] + (1.0 + gate1) * y
    x1_ref[0, rows, :] = x1
    h2 = _rms_mod(x1, g2_ref[...], scale2, shift2)
    _store_tile_rows(h2_ref, h2, r0)
    logits = jnp.dot(h2.astype(BF16), wr_ref[...], preferred_element_type=F32) + br_ref[...]

    lane = lax.broadcasted_iota(jnp.int32, logits.shape, 1)
    neg = -jnp.inf
    nl = logits.shape[-1]

    def top(vals):
        mx = jnp.max(vals, axis=-1, keepdims=True)
        idx = jnp.min(jnp.where(vals == mx, lane, nl), axis=-1, keepdims=True)
        return mx, idx

    is_grp = lane < N_GROUPS
    gmax, gidx = top(jnp.where(is_grp, logits, neg))
    grp_gate = 1.0 / jnp.sum(jnp.where(is_grp, jnp.exp(logits - gmax), 0.0), axis=-1, keepdims=True)
    lo = EXPERT_LANE0 + EXPERTS_PER_GROUP * gidx
    el = jnp.where((lane >= lo) & (lane < lo + EXPERTS_PER_GROUP), logits, neg)
    v1, i1 = top(el)
    v2, i2 = top(jnp.where(lane == i1, neg, el))
    e21 = jnp.exp(v2 - v1)
    wt1 = grp_gate / (1.0 + e21)
    wt2 = wt1 * e21

    oh1 = jnp.where(lane == i1, 1.0, 0.0)
    oh2 = jnp.where(lane == i2, 1.0, 0.0)
    ohs = oh1 + oh2
    rr = lax.broadcasted_iota(jnp.int32, (tm, tm), 0)
    cc = lax.broadcasted_iota(jnp.int32, (tm, tm), 1)
    earlier = jnp.where(cc < rr, 1.0, 0.0).astype(BF16)
    before = jnp.dot(earlier, ohs.astype(BF16), preferred_element_type=F32) + cnt_acc[...]
    rank1 = jnp.sum(oh1 * before, axis=-1, keepdims=True)
    rank2 = jnp.sum(oh2 * before, axis=-1, keepdims=True)
    cnt_acc[...] = cnt_acc[...] + jnp.sum(ohs, axis=0, keepdims=True)

    cols = [(i1 - EXPERT_LANE0).astype(F32), (i2 - EXPERT_LANE0).astype(F32), rank1, rank2, wt1, wt2]
    slab = jnp.zeros(logits.shape, F32)
    for j, v in enumerate(cols):
        slab = jnp.where(lane == j, v, slab)
    route_ref[0, rows, :] = slab[:, :ROUTE_COLS]


def _mix_route(attn, lru, gl, x, mod3, wa, wl, wo, g2, wr, br, b0, nb, *, tm=512):
    _, s, d = x.shape
    tok_in = lambda w: pl.BlockSpec((1, tm, w), lambda bi, i: (b0 + bi, i, 0))
    tok = lambda w: pl.BlockSpec((1, tm, w), lambda bi, i: (bi, i, 0))
    return pl.pallas_call(
        _mix_kernel,
        grid=(nb, s // tm),
        in_specs=[tok_in(attn.shape[-1]), tok_in(d), tok_in(2 * d), tok_in(d),
                  pl.BlockSpec((1, 1, mod3.shape[-1]), lambda bi, i: (b0 + bi, 0, 0)),
                  _resident(wa.shape), _resident(wl.shape), _resident(wo.shape),
                  pl.BlockSpec((1, d), lambda bi, i: (0, 0)),
                  _resident(wr.shape),
                  pl.BlockSpec((1, LANE), lambda bi, i: (0, 0))],
        out_specs=[tok(d),
                   pl.BlockSpec((tm * ROW_SUBLANES, LANE), lambda bi, i: (bi * (s // tm) + i, 0)),
                   tok(ROUTE_COLS),
                   pl.BlockSpec((1, LANE), lambda bi, i: (0, 0))],
        out_shape=[jax.ShapeDtypeStruct((nb, s, d), F32),
                   jax.ShapeDtypeStruct((nb * s * ROW_SUBLANES, LANE), jnp.uint32),
                   jax.ShapeDtypeStruct((nb, s, ROUTE_COLS), F32),
                   jax.ShapeDtypeStruct((1, LANE), F32)],
        scratch_shapes=[pltpu.VMEM((1, LANE), F32)],
        compiler_params=_cparams(("arbitrary", "arbitrary"), VMEM_LIMIT),
        name="mix_route",
    )(attn, lru, gl, x, mod3, wa, wl, wo, g2, wr, br)


TOP_K = 2
EXPERT_BLOCK = 256
MOE_BATCH_RANGES = 2


def _expert_kernel(be_ref, nu_ref, x_ref, w1_ref, w3_ref, w2_ref, y_ref, wb1, wb3, wb2):
    j = pl.program_id(0)

    @pl.when(j < nu_ref[0])
    def _():
        @pl.when((j == 0) | (be_ref[j] != be_ref[jnp.maximum(j - 1, 0)]))
        def _():
            wb1[...] = w1_ref[0].astype(BF16)
            wb3[...] = w3_ref[0].astype(BF16)
            wb2[...] = w2_ref[0].astype(BF16)

        xb = _load_tile_rows(x_ref).astype(BF16)
        a = jnp.dot(xb, wb1[...], preferred_element_type=F32)
        g = jnp.dot(xb, wb3[...], preferred_element_type=F32)
        hm = (a * jax.nn.sigmoid(a) * g).astype(BF16)
        _store_tile_rows(y_ref, jnp.dot(hm, wb2[...], preferred_element_type=F32))

    @pl.when(j >= nu_ref[0])
    def _():
        y_ref[...] = jnp.zeros_like(y_ref)


def _experts(xp, blk_e, n_used, w1, w3, w2):
    ne, d, de = w1.shape
    nb = xp.shape[0] // (EXPERT_BLOCK * ROW_SUBLANES)
    rows = (EXPERT_BLOCK * ROW_SUBLANES, LANE)
    last = lambda j, nu: jnp.minimum(j, nu[0] - 1)
    grid_spec = pltpu.PrefetchScalarGridSpec(
        num_scalar_prefetch=2,
        grid=(nb,),
        in_specs=[pl.BlockSpec(rows, lambda j, be, nu: (last(j, nu), 0)),
                  pl.BlockSpec((1, d, de), lambda j, be, nu: (be[j], 0, 0)),
                  pl.BlockSpec((1, d, de), lambda j, be, nu: (be[j], 0, 0)),
                  pl.BlockSpec((1, de, d), lambda j, be, nu: (be[j], 0, 0))],
        out_specs=pl.BlockSpec(rows, lambda j, be, nu: (j, 0)),
        scratch_shapes=[pltpu.VMEM((d, de), BF16), pltpu.VMEM((d, de), BF16), pltpu.VMEM((de, d), BF16)])
    return pl.pallas_call(
        _expert_kernel,
        grid_spec=grid_spec,
        out_shape=jax.ShapeDtypeStruct(xp.shape, xp.dtype),
        compiler_params=_cparams(("arbitrary",), VMEM_LIMIT),
        name="experts",
    )(blk_e, n_used, xp, w1, w3, w2)


def _combine_kernel(y0_ref, y1_ref, route_ref, x1_ref, mod_ref, gf_ref, *rest):
    o_ref = rest[-1]
    d = x1_ref.shape[2]
    route = route_ref[0]
    moe = _load_tile_rows(y0_ref) * route[:, 4:5] + _load_tile_rows(y1_ref) * route[:, 5:6]
    gate2 = mod_ref[0][:, 5 * d:6 * d]
    xo = x1_ref[0] + (1.0 + gate2) * moe
    ms = jnp.mean(xo * xo, axis=-1, keepdims=True)
    o_ref[0] = xo * lax.rsqrt(ms + EPS) * gf_ref[...]


def _combine(yg, route, x1, mod3, gf, b0, out_prev, *, tm=256):
    nb, s, d = x1.shape
    b_all = mod3.shape[0]
    spt = s // tm
    nt = nb * spt
    rows = (tm * ROW_SUBLANES, LANE)
    in_specs = [pl.BlockSpec(rows, lambda bi, i: (bi * spt + i, 0)),
                pl.BlockSpec(rows, lambda bi, i: (nt + bi * spt + i, 0)),
                pl.BlockSpec((1, tm, ROUTE_COLS), lambda bi, i: (bi, i, 0)),
                pl.BlockSpec((1, tm, d), lambda bi, i: (bi, i, 0)),
                pl.BlockSpec((1, 1, mod3.shape[-1]), lambda bi, i: (b0 + bi, 0, 0)),
                pl.BlockSpec((1, d), lambda bi, i: (0, 0))]
    args = [yg, yg, route, x1, mod3, gf]
    aliases = {}
    if out_prev is not None:
        in_specs.append(pl.BlockSpec(memory_space=pl.ANY))
        aliases = {len(args): 0}
        args.append(out_prev)
    return pl.pallas_call(
        _combine_kernel,
        grid=(nb, spt),
        in_specs=in_specs,
        out_specs=pl.BlockSpec((1, tm, d), lambda bi, i: (b0 + bi, i, 0)),
        out_shape=jax.ShapeDtypeStruct((b_all, s, d), F32),
        input_output_aliases=aliases,
        compiler_params=_cparams(("parallel", "parallel"), VMEM_LIMIT),
        name="combine",
    )(*args)


def _slot_plan(route, counts, n_tok):
    sizes = counts[0, EXPERT_LANE0:EXPERT_LANE0 + N_EXPERTS].astype(jnp.int32)
    padded = (sizes + EXPERT_BLOCK - 1) // EXPERT_BLOCK * EXPERT_BLOCK
    pad_ends = jnp.cumsum(padded)
    pad_starts = pad_ends - padded
    eid = route[..., 0:TOP_K].astype(jnp.int32).reshape(n_tok, TOP_K)
    rank = route[..., TOP_K:2 * TOP_K].astype(jnp.int32).reshape(n_tok, TOP_K)
    dest = (pad_starts[eid] + rank).T.reshape(TOP_K * n_tok)
    n_blocks = (n_tok * TOP_K + N_EXPERTS * (EXPERT_BLOCK - 1) + EXPERT_BLOCK - 1) // EXPERT_BLOCK
    gran = SC_CORES * SC_SUBCORES * SC_CHUNK // math.gcd(SC_CORES * SC_SUBCORES * SC_CHUNK, EXPERT_BLOCK)
    n_blocks = (n_blocks + gran - 1) // gran * gran
    n_used = pad_ends[-1] // EXPERT_BLOCK
    blk = jnp.minimum(jnp.arange(n_blocks), n_used - 1)
    blk_e = jnp.minimum(jnp.sum(pad_ends[None, :] <= (blk * EXPERT_BLOCK)[:, None], axis=1), N_EXPERTS - 1)
    return dest, n_blocks * EXPERT_BLOCK, blk_e.astype(jnp.int32), n_used.reshape(1).astype(jnp.int32)


def kernel(x, c, w_mod, b_mod, norm1_g, w_in, conv_w, conv_b, lru_wx, lru_bx, lru_wa, lru_ba, lru_lambda, w_attn_o, w_lru_o, w_out, norm2_g, w_grp, b_grp, w_exp, b_exp, w1, w3, w2, norm_f_g):
    b, s, d = x.shape
    assert d == D_MODEL and s == SPAN * DILATIONS[-1] and w_mod.shape[0] == 1
    mod3 = _modulation(c, w_mod[0], b_mod[0]).reshape(b, 1, 6 * d)
    qkv0, qkv1, qkv2, xc, gy, gl = _projection(x, mod3, norm1_g[0].reshape(1, d), _prep_w_in(w_in[0]),
                                               conv_w[0], conv_b[0])
    attn = _attention((qkv0, qkv1, qkv2), b, s)
    lru = _lru_branch(xc, gy, lru_wx[0], lru_bx[0], lru_wa[0], lru_ba[0], lru_lambda[0])

    n_router = N_GROUPS + N_EXPERTS
    wr = jnp.pad(jnp.concatenate([w_grp[0], w_exp[0]], axis=1), ((0, 0), (0, LANE - n_router))).astype(BF16)
    br = jnp.pad(jnp.concatenate([b_grp[0], b_exp[0]]), (0, LANE - n_router)).reshape(1, LANE)
    wa, wl, wo = w_attn_o[0].astype(BF16), w_lru_o[0].astype(BF16), w_out[0].astype(BF16)
    as_rows = lambda a: a.reshape(-1, ROW_SUBLANES, LANE)
    as_tiles = lambda a: a.reshape(-1, LANE)

    out = None
    nb = b // MOE_BATCH_RANGES
    for b0 in range(0, b, nb):
        x1, h2, route, counts = _mix_route(attn, lru, gl, x, mod3, wa, wl, wo, norm2_g[0].reshape(1, d),
                                           wr, br, b0, nb)
        dest, n_slots, blk_e, n_used = _slot_plan(route, counts, nb * s)
        xp = as_tiles(_sc_scatter_rows(as_rows(h2), dest, n_slots))
        yp = _experts(xp, blk_e, n_used, w1[0], w3[0], w2[0])
        yg = as_tiles(_sc_gather_rows(as_rows(yp), dest))
        out = _combine(yg, route, x1, mod3, norm_f_g.reshape(1, d), b0, out)
    return out
```

```python
import functools
import math

import jax
import jax.numpy as jnp
from jax import lax
from jax.experimental import pallas as pl
from jax.experimental.pallas import tpu as pltpu
from jax.experimental.pallas import tpu_sc as plsc

F32 = jnp.float32
BF16 = jnp.bfloat16

D_MODEL = 1024
HEAD_DIM = 64
N_SLOTS = 8
SPAN = 128
DILATIONS = (1, 4, 16)
GROUP_COLS = 3 * N_SLOTS * HEAD_DIM
ATTN_WIDTH = len(DILATIONS) * N_SLOTS * HEAD_DIM
ATTN_OUT = N_SLOTS * HEAD_DIM
LRU_WIDTH = D_MODEL
LRU_BLOCK_DIM = 64
CONV_WIDTH = 4
CONV_TAIL = 8
LRU_C = 8.0
N_GROUPS = 4
EXPERTS_PER_GROUP = 8
N_EXPERTS = N_GROUPS * EXPERTS_PER_GROUP
D_EXPERT = D_MODEL // 2
EPS = 1e-6
LANE = 128
VMEM_LIMIT = 56 * 1024 * 1024


def _cparams(sem, vmem=None):
    return pltpu.CompilerParams(dimension_semantics=sem, vmem_limit_bytes=vmem)


def _resident(shape):
    nd = len(shape)
    return pl.BlockSpec(shape, lambda *_: (0,) * nd, pipeline_mode=pl.Buffered(1))


def _mod_kernel(c_ref, w_ref, b_ref, o_ref):
    c = c_ref[...]
    ca = c * jax.nn.sigmoid(c)
    o_ref[...] = jnp.dot(ca.astype(BF16), w_ref[...].astype(BF16),
                         preferred_element_type=F32) + b_ref[...]


def _modulation(c, w_mod, b_mod):
    b, d = c.shape
    n = w_mod.shape[1]
    tn = n // 4
    return pl.pallas_call(
        _mod_kernel,
        grid=(n // tn,),
        in_specs=[pl.BlockSpec((b, d), lambda j: (0, 0)),
                  pl.BlockSpec((d, tn), lambda j: (0, j)),
                  pl.BlockSpec((1, tn), lambda j: (0, j))],
        out_specs=pl.BlockSpec((b, tn), lambda j: (0, j)),
        out_shape=jax.ShapeDtypeStruct((b, n), F32),
        compiler_params=_cparams(("arbitrary",)),
        name="modulation",
    )(c, w_mod, b_mod.reshape(1, n))


def _rms_mod(x, g, scale, shift):
    ms = jnp.mean(x * x, axis=-1, keepdims=True)
    return x * lax.rsqrt(ms + EPS) * g * (1.0 + scale) + shift


def _gelu_tanh(y):
    return y * (0.5 * (1.0 + jnp.tanh(0.7978845608028654 * (y + 0.044715 * (y * y * y)))))


def _proj_kernel(x_ref, mod_ref, g_ref, w_ref, cw_ref, cb_ref, qkv0_ref, qkv1_ref, qkv2_ref,
                 xc_ref, gy_ref, gl_ref, hs_ref, tail_ref, *, tm):
    d_model = x_ref.shape[-1]
    m = mod_ref[0]
    h = _rms_mod(x_ref[0], g_ref[...], m[:, d_model:2 * d_model], m[:, 0:d_model])

    def mm(hv, lo, hi):
        return jnp.dot(hv, w_ref[:, lo:hi], preferred_element_type=F32)

    hb = h.astype(BF16)
    c0 = len(DILATIONS) * GROUP_COLS
    qkv0_ref[0] = mm(hb, 0, GROUP_COLS).astype(BF16)

    @pl.when(pl.program_id(1) == 0)
    def _():
        tail_ref[...] = jnp.zeros_like(tail_ref)

    xr = mm(hb, c0, c0 + LRU_WIDTH)
    xe = jnp.concatenate([tail_ref[...], xr], axis=0)
    tail_ref[...] = xr[tm - CONV_TAIL:, :]
    cw = cw_ref[...]
    xc = xr * cw[CONV_WIDTH - 1:CONV_WIDTH] + cb_ref[...]
    for k in range(1, CONV_WIDTH):
        xc = xc + xe[CONV_TAIL - k:CONV_TAIL - k + tm, :] * cw[CONV_WIDTH - 1 - k:CONV_WIDTH - k]
    xc_ref[0] = xc.astype(BF16)
    gy_ref[0] = _gelu_tanh(mm(hb, c0 + LRU_WIDTH, c0 + 2 * LRU_WIDTH)).astype(BF16)
    gl_ref[0] = mm(hb, c0 + 2 * LRU_WIDTH, c0 + 2 * LRU_WIDTH + 2 * d_model).astype(BF16)

    n_slab = d_model // LANE
    for j in range(n_slab):
        hs_ref[j] = h[:, j * LANE:(j + 1) * LANE]
    for g, out_ref in ((1, qkv1_ref), (2, qkv2_ref)):
        d = DILATIONS[g]
        rows = tm // d
        hp = jnp.concatenate(
            [jnp.concatenate([hs_ref[j, pl.ds(p, rows, stride=d), :] for j in range(n_slab)], axis=1)
             for p in range(d)], axis=0).astype(BF16)
        res = mm(hp, g * GROUP_COLS, (g + 1) * GROUP_COLS).astype(BF16)
        for p in range(d):
            out_ref[p] = res[p * rows:(p + 1) * rows]


def _projection(x, mod3, g1, w_r, conv_w, conv_b, *, tm=256):
    b, s, d = x.shape
    n = w_r.shape[1]
    assert s % tm == 0 and tm % (16 * DILATIONS[-1]) == 0 and CONV_TAIL >= CONV_WIDTH - 1
    out_shape = [jax.ShapeDtypeStruct((b * dd, s // dd, GROUP_COLS), BF16) for dd in DILATIONS]
    out_shape += [jax.ShapeDtypeStruct((b, s, LRU_WIDTH), BF16),
                  jax.ShapeDtypeStruct((b, s, LRU_WIDTH), BF16),
                  jax.ShapeDtypeStruct((b, s, 2 * d), BF16)]
    out_specs = [pl.BlockSpec((dd, tm // dd, GROUP_COLS), lambda bi, i: (bi, i, 0)) for dd in DILATIONS]
    out_specs += [pl.BlockSpec((1, tm, LRU_WIDTH), lambda bi, i: (bi, i, 0)),
                  pl.BlockSpec((1, tm, LRU_WIDTH), lambda bi, i: (bi, i, 0)),
                  pl.BlockSpec((1, tm, 2 * d), lambda bi, i: (bi, i, 0))]
    return pl.pallas_call(
        functools.partial(_proj_kernel, tm=tm),
        grid=(b, s // tm),
        in_specs=[pl.BlockSpec((1, tm, d), lambda bi, i: (bi, i, 0)),
                  pl.BlockSpec((1, 1, mod3.shape[-1]), lambda bi, i: (bi, 0, 0)),
                  pl.BlockSpec((1, d), lambda bi, i: (0, 0)),
                  _resident((d, n)),
                  pl.BlockSpec((CONV_WIDTH, LRU_WIDTH), lambda bi, i: (0, 0)),
                  pl.BlockSpec((1, LRU_WIDTH), lambda bi, i: (0, 0))],
        out_specs=out_specs,
        out_shape=out_shape,
        scratch_shapes=[pltpu.VMEM((d // LANE, tm, LANE), F32), pltpu.VMEM((CONV_TAIL, LRU_WIDTH), F32)],
        compiler_params=_cparams(("parallel", "arbitrary"), VMEM_LIMIT),
        name="projection",
    )(x, mod3, g1, w_r, conv_w, conv_b.reshape(1, LRU_WIDTH))


def _attn_kernel(q0, k0, v0, q1, k1, v1, q2, k2, v2, o_ref, acc_ref, lse_ref, bias_ref, *, seq):
    hcols = o_ref.shape[-1]
    n_head = hcols // HEAD_DIM
    head_of_lane = lax.broadcasted_iota(jnp.int32, (SPAN, hcols), 1) // HEAD_DIM
    head_mask_b = [jnp.where(head_of_lane == h, 1.0, 0.0).astype(BF16) for h in range(n_head)]

    def by_head(parts):
        out = parts[n_head - 1]
        for h in range(n_head - 2, -1, -1):
            out = jnp.where(head_of_lane == h, parts[h], out)
        return out

    qi = lax.broadcasted_iota(jnp.int32, (n_head * SPAN, 2 * SPAN), 0) % SPAN
    ki = lax.broadcasted_iota(jnp.int32, (n_head * SPAN, 2 * SPAN), 1)
    band = (ki >= qi) & (ki <= qi + SPAN)
    bias_ref[0] = jnp.where(band, 0.0, -jnp.inf)
    bias_ref[1] = jnp.where(band & (ki >= SPAN), 0.0, -jnp.inf)

    for g, (q_ref, k_ref, v_ref) in enumerate(((q0, k0, v0), (q1, k1, v1), (q2, k2, v2))):
        d = DILATIONS[g]
        n_blk = seq // d // SPAN

        def tile(n, carry, q_ref=q_ref, k_ref=k_ref, v_ref=v_ref, d=d, n_blk=n_blk, g=g):
            p = n // n_blk
            blk = n % n_blk
            r0 = pl.multiple_of(blk * SPAN, SPAN)
            rp = pl.multiple_of(jnp.maximum(blk - 1, 0) * SPAN, SPAN)
            q = q_ref[p, pl.ds(r0, SPAN), :]
            kk = jnp.concatenate([k_ref[p, pl.ds(rp, SPAN), :], k_ref[p, pl.ds(r0, SPAN), :]], axis=0)
            vv = jnp.concatenate([v_ref[p, pl.ds(rp, SPAN), :], v_ref[p, pl.ds(r0, SPAN), :]], axis=0)
            qs = jnp.concatenate([q * head_mask_b[h] for h in range(n_head)], axis=0)
            sc = lax.dot_general(qs, kk, (((1,), (1,)), ((), ())), preferred_element_type=F32)
            sc = sc + bias_ref[jnp.where(blk > 0, 0, 1)]
            mx = jnp.max(sc, axis=-1, keepdims=True)
            e = jnp.exp(sc - mx)
            den = jnp.sum(e, axis=-1, keepdims=True)
            pv = jnp.dot(e.astype(BF16), vv, preferred_element_type=F32)
            lse = mx + jnp.log(den)
            rows_of = lambda a: [a[h * SPAN:(h + 1) * SPAN] for h in range(n_head)]
            o = by_head(rows_of(pv)) / by_head(rows_of(den))
            l = by_head(rows_of(lse))
            start = p + d * r0
            for j in range(hcols // LANE):
                rows = pl.ds(start, SPAN, stride=d) if d > 1 else pl.ds(start, SPAN)
                acc_ref[g, j, rows, :] = o[:, j * LANE:(j + 1) * LANE]
                lse_ref[g, j, rows, :] = l[:, j * LANE:(j + 1) * LANE]
            return carry

        lax.fori_loop(0, seq // SPAN, tile, 0, unroll=8)

    chunk = 256

    def combine(c, carry):
        r = pl.multiple_of(c * chunk, chunk)
        for j in range(hcols // LANE):
            ls = [lse_ref[g, j, pl.ds(r, chunk), :] for g in range(len(DILATIONS))]
            mx = jnp.maximum(jnp.maximum(ls[0], ls[1]), ls[2])
            ws = [jnp.exp(v - mx) for v in ls]
            num = ws[0] * acc_ref[0, j, pl.ds(r, chunk), :]
            for g in range(1, len(DILATIONS)):
                num = num + ws[g] * acc_ref[g, j, pl.ds(r, chunk), :]
            o_ref[0, pl.ds(r, chunk), j * LANE:(j + 1) * LANE] = (num / (ws[0] + ws[1] + ws[2])).astype(BF16)
        return carry

    lax.fori_loop(0, seq // chunk, combine, 0)


def _attention(qkvs, b, s):
    hcols = 4 * HEAD_DIM
    n_hg = ATTN_OUT // hcols
    ncb = ATTN_OUT // hcols
    in_specs, args = [], []
    for g, d in enumerate(DILATIONS):
        for part in range(3):
            in_specs.append(pl.BlockSpec((d, s // d, hcols),
                                         lambda bi, hg, part=part: (bi, 0, part * ncb + hg)))
            args.append(qkvs[g])
    return pl.pallas_call(
        functools.partial(_attn_kernel, seq=s),
        grid=(b, n_hg),
        in_specs=in_specs,
        out_specs=pl.BlockSpec((1, s, hcols), lambda bi, hg: (bi, 0, hg)),
        out_shape=jax.ShapeDtypeStruct((b, s, ATTN_OUT), BF16),
        scratch_shapes=[pltpu.VMEM((len(DILATIONS), hcols // LANE, s, LANE), F32),
                        pltpu.VMEM((len(DILATIONS), hcols // LANE, s, LANE), F32),
                        pltpu.VMEM((2, (hcols // HEAD_DIM) * SPAN, 2 * SPAN), F32)],
        compiler_params=_cparams(("parallel", "parallel"), VMEM_LIMIT),
        name="dilated_attention",
    )(*args)


def _lru_kernel(xc_ref, gy_ref, wg_ref, bx_ref, ba_ref, lam_ref, o_ref, a_ref, b_ref, h_ref, *, pitch):
    nb, ts, tc = xc_ref.shape
    nl = tc // LANE

    @pl.when(pl.program_id(1) == 0)
    def _():
        h_ref[...] = jnp.zeros_like(h_ref)

    xb = xc_ref[...].reshape(nb * ts, tc)
    xc = xb.astype(F32)
    gates = jnp.dot(xb, wg_ref[0], preferred_element_type=F32)
    gate_i = jax.nn.sigmoid(gates[:, :tc] + bx_ref[...])
    gate_r = jax.nn.sigmoid(gates[:, tc:] + ba_ref[...])
    neg_lam = -lam_ref[...]
    softplus = jnp.maximum(neg_lam, 0.0) + jnp.log1p(jnp.exp(-jnp.abs(neg_lam)))
    log_a = (-LRU_C) * gate_r * softplus
    a = jnp.exp(log_a)
    bv = jnp.sqrt(jnp.tanh(-log_a) * (1.0 + a * a)) * gate_i * xc
    for bi in range(nb):
        for j in range(nl):
            a_ref[j, pl.ds(bi * pitch, ts), :] = a[bi * ts:(bi + 1) * ts, j * LANE:(j + 1) * LANE]
            b_ref[j, pl.ds(bi * pitch, ts), :] = bv[bi * ts:(bi + 1) * ts, j * LANE:(j + 1) * LANE]

    def step(t, hs):
        out = []
        for j in range(nl):
            rows = pl.ds(t, nb, stride=pitch)
            h = a_ref[j, rows, :] * hs[j] + b_ref[j, rows, :]
            b_ref[j, rows, :] = h
            out.append(h)
        return tuple(out)

    hs = lax.fori_loop(0, ts, step, tuple(h_ref[j] for j in range(nl)), unroll=8)
    for j in range(nl):
        h_ref[j] = hs[j]
    for bi in range(nb):
        h = jnp.concatenate([b_ref[j, pl.ds(bi * pitch, ts), :] for j in range(nl)], axis=1)
        o_ref[bi] = (h * gy_ref[bi].astype(F32)).astype(BF16)


def _lru_gate_weights(wx, wa, tc):
    nb, bd, _ = wx.shape
    per = tc // bd
    eye = jnp.eye(per, dtype=wx.dtype)

    def bdiag(w):
        w = w.reshape(nb // per, per, bd, bd)
        return jnp.einsum('cpio,pq->cpiqo', w, eye).reshape(nb // per, tc, tc)

    return jnp.concatenate([bdiag(wx), bdiag(wa)], axis=-1).astype(BF16)


def _lru_branch(xc, gy, wx, bx, wa, ba, lam, *, tc=256, ts=128):
    b, s, c = xc.shape
    assert s % ts == 0 and c % tc == 0
    wg = _lru_gate_weights(wx, wa, tc)
    row = lambda v: v.reshape(1, c)
    tile = pl.BlockSpec((b, ts, tc), lambda ci, ti: (0, ti, ci))
    vec = pl.BlockSpec((1, tc), lambda ci, ti: (0, ci))
    pitch = ts + 8
    return pl.pallas_call(
        functools.partial(_lru_kernel, pitch=pitch),
        grid=(c // tc, s // ts),
        in_specs=[tile, tile,
                  pl.BlockSpec((1, tc, 2 * tc), lambda ci, ti: (ci, 0, 0)),
                  vec, vec, vec],
        out_specs=tile,
        out_shape=jax.ShapeDtypeStruct((b, s, c), BF16),
        scratch_shapes=[pltpu.VMEM((tc // LANE, b * pitch, LANE), F32),
                        pltpu.VMEM((tc // LANE, b * pitch, LANE), F32),
                        pltpu.VMEM((tc // LANE, b, LANE), F32)],
        compiler_params=_cparams(("parallel", "arbitrary"), VMEM_LIMIT),
        name="rg_lru",
    )(xc, gy, wg, row(bx), row(ba), row(lam))


def _prep_w_in(w_in):
    a = ATTN_WIDTH
    gw = N_SLOTS * HEAD_DIM
    q = w_in[:, :a] * (HEAD_DIM ** -0.5)
    k = w_in[:, a:2 * a]
    v = w_in[:, 2 * a:3 * a]
    parts = []
    for g in range(len(DILATIONS)):
        sl = slice(g * gw, (g + 1) * gw)
        parts += [q[:, sl], k[:, sl], v[:, sl]]
    parts.append(w_in[:, 3 * a:])
    return jnp.concatenate(parts, axis=1).astype(BF16)


ROW_SUBLANES = D_MODEL // 2 // LANE


def _store_tile_rows(ref, v, row0=0):
    n, half = v.shape[0], v.shape[1] // 2
    lo = pltpu.bitcast(v[:, :half].astype(BF16).astype(F32), jnp.uint32)
    hi = pltpu.bitcast(v[:, half:].astype(BF16).astype(F32), jnp.uint32)
    words = (hi & jnp.uint32(0xFFFF0000)) | (lo >> 16)
    for j in range(ROW_SUBLANES):
        ref[pl.ds(row0 * ROW_SUBLANES + j, n, stride=ROW_SUBLANES), :] = words[:, j * LANE:(j + 1) * LANE]


def _load_tile_rows(ref):
    n = ref.shape[0] // ROW_SUBLANES
    words = [ref[pl.ds(j, n, stride=ROW_SUBLANES), :] for j in range(ROW_SUBLANES)]
    lo = [pltpu.bitcast(w << 16, F32) for w in words]
    hi = [pltpu.bitcast(w & jnp.uint32(0xFFFF0000), F32) for w in words]
    return jnp.concatenate(lo + hi, axis=-1)


SC_CORES, SC_SUBCORES = 2, 16
SC_CHUNK = 128


def _sc_gather_rows(table, idx):
    n = idx.shape[0]
    per_worker = n // (SC_CORES * SC_SUBCORES)
    n_chunks = per_worker // SC_CHUNK
    assert n_chunks * SC_CHUNK * SC_CORES * SC_SUBCORES == n
    mesh = plsc.VectorSubcoreMesh(core_axis_name="c", subcore_axis_name="s")

    def body(table_hbm, idx_hbm, out_hbm, idx_v, rows_v, sem):
        base = (lax.axis_index("s") * SC_CORES + lax.axis_index("c")) * per_worker

        @pl.loop(0, n_chunks)
        def _(i):
            off = pl.multiple_of(base + i * SC_CHUNK, SC_CHUNK)
            pltpu.sync_copy(idx_hbm.at[pl.ds(off, SC_CHUNK)], idx_v)
            pltpu.async_copy(table_hbm.at[idx_v], rows_v, sem).wait()
            pltpu.sync_copy(rows_v, out_hbm.at[pl.ds(off, SC_CHUNK)])

    return pl.kernel(
        body, mesh=mesh,
        out_type=jax.ShapeDtypeStruct((n,) + table.shape[1:], table.dtype),
        scratch_types=[pltpu.VMEM((SC_CHUNK,), jnp.int32),
                       pltpu.VMEM((SC_CHUNK,) + table.shape[1:], table.dtype),
                       pltpu.SemaphoreType.DMA],
        name="sc_gather_rows",
    )(table, idx)


def _sc_scatter_rows(rows, idx, n_out):
    n_rows = rows.shape[0]
    n_choice = idx.shape[0] // n_rows
    per_worker = n_rows // (SC_CORES * SC_SUBCORES)
    n_chunks = per_worker // SC_CHUNK
    assert n_chunks * SC_CHUNK * SC_CORES * SC_SUBCORES == n_rows and n_choice * n_rows == idx.shape[0]
    mesh = plsc.VectorSubcoreMesh(core_axis_name="c", subcore_axis_name="s")

    def body(rows_hbm, idx_hbm, out_hbm, idx_v, rows_v):
        base = (lax.axis_index("s") * SC_CORES + lax.axis_index("c")) * per_worker

        @pl.loop(0, n_chunks)
        def _(i):
            off = pl.multiple_of(base + i * SC_CHUNK, SC_CHUNK)
            pltpu.sync_copy(rows_hbm.at[pl.ds(off, SC_CHUNK)], rows_v)
            for k in range(n_choice):
                pltpu.sync_copy(idx_hbm.at[pl.ds(k * n_rows + off, SC_CHUNK)], idx_v)
                pltpu.sync_copy(rows_v, out_hbm.at[idx_v])

    return pl.kernel(
        body, mesh=mesh,
        out_type=jax.ShapeDtypeStruct((n_out,) + rows.shape[1:], rows.dtype),
        scratch_types=[pltpu.VMEM((SC_CHUNK,), jnp.int32),
                       pltpu.VMEM((SC_CHUNK,) + rows.shape[1:], rows.dtype)],
        name="sc_scatter_rows",
    )(rows, idx)


ROUTE_COLS = 8
EXPERT_LANE0 = N_GROUPS
MIX_SUBTILES = 1


def _mix_kernel(attn_ref, lru_ref, gl_ref, x_ref, mod_ref, wa_ref, wl_ref, wo_ref, g2_ref,
                wr_ref, br_ref, x1_ref, h2_ref, route_ref, cnt_ref, cnt_acc):
    d = x_ref.shape[-1]
    tm = x_ref.shape[1]

    @pl.when((pl.program_id(0) == 0) & (pl.program_id(1) == 0))
    def _():
        cnt_acc[...] = jnp.zeros_like(cnt_acc)

    sub = tm // MIX_SUBTILES
    for r0 in range(0, tm, sub):
        _mix_rows(pl.ds(r0, sub), r0, sub, attn_ref, lru_ref, gl_ref, x_ref, mod_ref, wa_ref, wl_ref, wo_ref,
                  g2_ref, wr_ref, br_ref, x1_ref, h2_ref, route_ref, cnt_acc)
    cnt_ref[...] = cnt_acc[...]


def _mix_rows(rows, r0, tm, attn_ref, lru_ref, gl_ref, x_ref, mod_ref, wa_ref, wl_ref, wo_ref, g2_ref,
              wr_ref, br_ref, x1_ref, h2_ref, route_ref, cnt_acc):
    d = x_ref.shape[-1]
    m = mod_ref[0]
    gate1, shift2, scale2 = m[:, 2 * d:3 * d], m[:, 3 * d:4 * d], m[:, 4 * d:5 * d]
    ya = jnp.dot(attn_ref[0, rows, :], wa_ref[...], preferred_element_type=F32)
    yl = jnp.dot(lru_ref[0, rows, :], wl_ref[...], preferred_element_type=F32)
    gates = jax.nn.sigmoid(gl_ref[0, rows, :].astype(F32))
    mixed = gates[:, :d] * ya + gates[:, d:] * yl
    y = jnp.dot(mixed.astype(BF16), wo_ref[...], preferred_element_type=F32)
    x1 = x_ref[0, rows, :] + (1.0 + gate1) * y
    x1_ref[0, rows, :] = x1.astype(BF16)
    h2 = _rms_mod(x1, g2_ref[...], scale2, shift2)
    _store_tile_rows(h2_ref, h2, r0)
    logits = jnp.dot(h2.astype(BF16), wr_ref[...], preferred_element_type=F32) + br_ref[...]

    lane = lax.broadcasted_iota(jnp.int32, logits.shape, 1)
    neg = -jnp.inf
    nl = logits.shape[-1]

    def top(vals):
        mx = jnp.max(vals, axis=-1, keepdims=True)
        idx = jnp.min(jnp.where(vals == mx, lane, nl), axis=-1, keepdims=True)
        return mx, idx

    is_grp = lane < N_GROUPS
    gmax, gidx = top(jnp.where(is_grp, logits, neg))
    grp_gate = 1.0 / jnp.sum(jnp.where(is_grp, jnp.exp(logits - gmax), 0.0), axis=-1, keepdims=True)
    lo = EXPERT_LANE0 + EXPERTS_PER_GROUP * gidx
    el = jnp.where((lane >= lo) & (lane < lo + EXPERTS_PER_GROUP), logits, neg)
    v1, i1 = top(el)
    v2, i2 = top(jnp.where(lane == i1, neg, el))
    e21 = jnp.exp(v2 - v1)
    wt1 = grp_gate / (1.0 + e21)
    wt2 = wt1 * e21

    oh1 = jnp.where(lane == i1, 1.0, 0.0)
    oh2 = jnp.where(lane == i2, 1.0, 0.0)
    ohs = oh1 + oh2
    rr = lax.broadcasted_iota(jnp.int32, (tm, tm), 0)
    cc = lax.broadcasted_iota(jnp.int32, (tm, tm), 1)
    earlier = jnp.where(cc < rr, 1.0, 0.0).astype(BF16)
    before = jnp.dot(earlier, ohs.astype(BF16), preferred_element_type=F32) + cnt_acc[...]
    rank1 = jnp.sum(oh1 * before, axis=-1, keepdims=True)
    rank2 = jnp.sum(oh2 * before, axis=-1, keepdims=True)
    cnt_acc[...] = cnt_acc[...] + jnp.sum(ohs, axis=0, keepdims=True)

    cols = [(i1 - EXPERT_LANE0).astype(F32), (i2 - EXPERT_LANE0).astype(F32), rank1, rank2, wt1, wt2]
    slab = jnp.zeros(logits.shape, F32)
    for j, v in enumerate(cols):
        slab = jnp.where(lane == j, v, slab)
    route_ref[0, rows, :] = slab[:, :ROUTE_COLS]


def _mix_route(attn, lru, gl, x, mod3, wa, wl, wo, g2, wr, br, b0, nb, *, tm=512):
    _, s, d = x.shape
    tok_in = lambda w: pl.BlockSpec((1, tm, w), lambda bi, i: (b0 + bi, i, 0))
    tok = lambda w: pl.BlockSpec((1, tm, w), lambda bi, i: (bi, i, 0))
    return pl.pallas_call(
        _mix_kernel,
        grid=(nb, s // tm),
        in_specs=[tok_in(attn.shape[-1]), tok_in(d), tok_in(2 * d), tok_in(d),
                  pl.BlockSpec((1, 1, mod3.shape[-1]), lambda bi, i: (b0 + bi, 0, 0)),
                  _resident(wa.shape), _resident(wl.shape), _resident(wo.shape),
                  pl.BlockSpec((1, d), lambda bi, i: (0, 0)),
                  _resident(wr.shape),
                  pl.BlockSpec((1, LANE), lambda bi, i: (0, 0))],
        out_specs=[tok(d),
                   pl.BlockSpec((tm * ROW_SUBLANES, LANE), lambda bi, i: (bi * (s // tm) + i, 0)),
                   tok(ROUTE_COLS),
                   pl.BlockSpec((1, LANE), lambda bi, i: (0, 0))],
        out_shape=[jax.ShapeDtypeStruct((nb, s, d), BF16),
                   jax.ShapeDtypeStruct((nb * s * ROW_SUBLANES, LANE), jnp.uint32),
                   jax.ShapeDtypeStruct((nb, s, ROUTE_COLS), F32),
                   jax.ShapeDtypeStruct((1, LANE), F32)],
        scratch_shapes=[pltpu.VMEM((1, LANE), F32)],
        compiler_params=_cparams(("arbitrary", "arbitrary"), VMEM_LIMIT),
        name="mix_route",
    )(attn, lru, gl, x, mod3, wa, wl, wo, g2, wr, br)


TOP_K = 2
EXPERT_BLOCK = 512
MOE_BATCH_RANGES = 2


def _expert_kernel(be_ref, nu_ref, x_ref, w1_ref, w3_ref, w2_ref, y_ref, wb1, wb3, wb2):
    j = pl.program_id(0)

    @pl.when(j < nu_ref[0])
    def _():
        @pl.when((j == 0) | (be_ref[j] != be_ref[jnp.maximum(j - 1, 0)]))
        def _():
            wb1[...] = w1_ref[0].astype(BF16)
            wb3[...] = w3_ref[0].astype(BF16)
            wb2[...] = w2_ref[0].astype(BF16)

        xb = _load_tile_rows(x_ref).astype(BF16)
        a = jnp.dot(xb, wb1[...], preferred_element_type=F32)
        g = jnp.dot(xb, wb3[...], preferred_element_type=F32)
        hm = (a * jax.nn.sigmoid(a) * g).astype(BF16)
        _store_tile_rows(y_ref, jnp.dot(hm, wb2[...], preferred_element_type=F32))

    @pl.when(j >= nu_ref[0])
    def _():
        y_ref[...] = jnp.zeros_like(y_ref)


def _experts(xp, blk_e, n_used, w1, w3, w2):
    ne, d, de = w1.shape
    nb = xp.shape[0] // (EXPERT_BLOCK * ROW_SUBLANES)
    rows = (EXPERT_BLOCK * ROW_SUBLANES, LANE)
    last = lambda j, nu: jnp.minimum(j, nu[0] - 1)
    grid_spec = pltpu.PrefetchScalarGridSpec(
        num_scalar_prefetch=2,
        grid=(nb,),
        in_specs=[pl.BlockSpec(rows, lambda j, be, nu: (last(j, nu), 0)),
                  pl.BlockSpec((1, d, de), lambda j, be, nu: (be[j], 0, 0)),
                  pl.BlockSpec((1, d, de), lambda j, be, nu: (be[j], 0, 0)),
                  pl.BlockSpec((1, de, d), lambda j, be, nu: (be[j], 0, 0))],
        out_specs=pl.BlockSpec(rows, lambda j, be, nu: (j, 0)),
        scratch_shapes=[pltpu.VMEM((d, de), BF16), pltpu.VMEM((d, de), BF16), pltpu.VMEM((de, d), BF16)])
    return pl.pallas_call(
        _expert_kernel,
        grid_spec=grid_spec,
        out_shape=jax.ShapeDtypeStruct(xp.shape, xp.dtype),
        compiler_params=_cparams(("arbitrary",), VMEM_LIMIT),
        name="experts",
    )(blk_e, n_used, xp, w1, w3, w2)


def _combine_kernel(y0_ref, y1_ref, route_ref, x1_ref, mod_ref, gf_ref, *rest):
    o_ref = rest[-1]
    d = x1_ref.shape[2]
    route = route_ref[0]
    moe = _load_tile_rows(y0_ref) * route[:, 4:5] + _load_tile_rows(y1_ref) * route[:, 5:6]
    gate2 = mod_ref[0][:, 5 * d:6 * d]
    xo = x1_ref[0].astype(F32) + (1.0 + gate2) * moe
    ms = jnp.mean(xo * xo, axis=-1, keepdims=True)
    o_ref[0] = xo * lax.rsqrt(ms + EPS) * gf_ref[...]


def _combine(yg, route, x1, mod3, gf, b0, out_prev, *, tm=256):
    nb, s, d = x1.shape
    b_all = mod3.shape[0]
    spt = s // tm
    nt = nb * spt
    rows = (tm * ROW_SUBLANES, LANE)
    in_specs = [pl.BlockSpec(rows, lambda bi, i: (bi * spt + i, 0)),
                pl.BlockSpec(rows, lambda bi, i: (nt + bi * spt + i, 0)),
                pl.BlockSpec((1, tm, ROUTE_COLS), lambda bi, i: (bi, i, 0)),
                pl.BlockSpec((1, tm, d), lambda bi, i: (bi, i, 0)),
                pl.BlockSpec((1, 1, mod3.shape[-1]), lambda bi, i: (b0 + bi, 0, 0)),
                pl.BlockSpec((1, d), lambda bi, i: (0, 0))]
    args = [yg, yg, route, x1, mod3, gf]
    aliases = {}
    if out_prev is not None:
        in_specs.append(pl.BlockSpec(memory_space=pl.ANY))
        aliases = {len(args): 0}
        args.append(out_prev)
    return pl.pallas_call(
        _combine_kernel,
        grid=(nb, spt),
        in_specs=in_specs,
        out_specs=pl.BlockSpec((1, tm, d), lambda bi, i: (b0 + bi, i, 0)),
        out_shape=jax.ShapeDtypeStruct((b_all, s, d), F32),
        input_output_aliases=aliases,
        compiler_params=_cparams(("parallel", "parallel"), VMEM_LIMIT),
        name="combine",
    )(*args)


def _slot_plan(route, counts, n_tok):
    sizes = counts[0, EXPERT_LANE0:EXPERT_LANE0 + N_EXPERTS].astype(jnp.int32)
    padded = (sizes + EXPERT_BLOCK - 1) // EXPERT_BLOCK * EXPERT_BLOCK
    pad_ends = jnp.cumsum(padded)
    pad_starts = pad_ends - padded
    eid = route[..., 0:TOP_K].astype(jnp.int32).reshape(n_tok, TOP_K)
    rank = route[..., TOP_K:2 * TOP_K].astype(jnp.int32).reshape(n_tok, TOP_K)
    dest = (pad_starts[eid] + rank).T.reshape(TOP_K * n_tok)
    n_blocks = (n_tok * TOP_K + N_EXPERTS * (EXPERT_BLOCK - 1) + EXPERT_BLOCK - 1) // EXPERT_BLOCK
    gran = SC_CORES * SC_SUBCORES * SC_CHUNK // math.gcd(SC_CORES * SC_SUBCORES * SC_CHUNK, EXPERT_BLOCK)
    n_blocks = (n_blocks + gran - 1) // gran * gran
    n_used = pad_ends[-1] // EXPERT_BLOCK
    blk = jnp.minimum(jnp.arange(n_blocks), n_used - 1)
    blk_e = jnp.minimum(jnp.sum(pad_ends[None, :] <= (blk * EXPERT_BLOCK)[:, None], axis=1), N_EXPERTS - 1)
    return dest, n_blocks * EXPERT_BLOCK, blk_e.astype(jnp.int32), n_used.reshape(1).astype(jnp.int32)


def kernel(x, c, w_mod, b_mod, norm1_g, w_in, conv_w, conv_b, lru_wx, lru_bx, lru_wa, lru_ba, lru_lambda, w_attn_o, w_lru_o, w_out, norm2_g, w_grp, b_grp, w_exp, b_exp, w1, w3, w2, norm_f_g):
    b, s, d = x.shape
    assert d == D_MODEL and s == SPAN * DILATIONS[-1] and w_mod.shape[0] == 1
    mod3 = _modulation(c, w_mod[0], b_mod[0]).reshape(b, 1, 6 * d)
    qkv0, qkv1, qkv2, xc, gy, gl = _projection(x, mod3, norm1_g[0].reshape(1, d), _prep_w_in(w_in[0]),
                                               conv_w[0], conv_b[0])
    attn = _attention((qkv0, qkv1, qkv2), b, s)
    lru = _lru_branch(xc, gy, lru_wx[0], lru_bx[0], lru_wa[0], lru_ba[0], lru_lambda[0])

    n_router = N_GROUPS + N_EXPERTS
    wr = jnp.pad(jnp.concatenate([w_grp[0], w_exp[0]], axis=1), ((0, 0), (0, LANE - n_router))).astype(BF16)
    br = jnp.pad(jnp.concatenate([b_grp[0], b_exp[0]]), (0, LANE - n_router)).reshape(1, LANE)
    wa, wl, wo = w_attn_o[0].astype(BF16), w_lru_o[0].astype(BF16), w_out[0].astype(BF16)
    as_rows = lambda a: a.reshape(-1, ROW_SUBLANES, LANE)
    as_tiles = lambda a: a.reshape(-1, LANE)

    out = None
    nb = b // MOE_BATCH_RANGES
    for b0 in range(0, b, nb):
        x1, h2, route, counts = _mix_route(attn, lru, gl, x, mod3, wa, wl, wo, norm2_g[0].reshape(1, d),
                                           wr, br, b0, nb)
        dest, n_slots, blk_e, n_used = _slot_plan(route, counts, nb * s)
        xp = as_tiles(_sc_scatter_rows(as_rows(h2), dest, n_slots))
        yp = _experts(xp, blk_e, n_used, w1[0], w3[0], w2[0])
        yg = as_tiles(_sc_gather_rows(as_rows(yp), dest))
        out = _combine(yg, route, x1, mod3, norm_f_g.reshape(1, d), b0, out)
    return out
```

```python
import functools
import math

import jax
import jax.numpy as jnp
from jax import lax
from jax.experimental import pallas as pl
from jax.experimental.pallas import tpu as pltpu
from jax.experimental.pallas import tpu_sc as plsc

F32 = jnp.float32
BF16 = jnp.bfloat16

D_MODEL = 1024
HEAD_DIM = 64
N_SLOTS = 8
SPAN = 128
DILATIONS = (1, 4, 16)
GROUP_COLS = 3 * N_SLOTS * HEAD_DIM
ATTN_WIDTH = len(DILATIONS) * N_SLOTS * HEAD_DIM
ATTN_OUT = N_SLOTS * HEAD_DIM
LRU_WIDTH = D_MODEL
LRU_BLOCK_DIM = 64
CONV_WIDTH = 4
CONV_TAIL = 8
LRU_C = 8.0
N_GROUPS = 4
EXPERTS_PER_GROUP = 8
N_EXPERTS = N_GROUPS * EXPERTS_PER_GROUP
D_EXPERT = D_MODEL // 2
EPS = 1e-6
LANE = 128
VMEM_LIMIT = 56 * 1024 * 1024


def _cparams(sem, vmem=None):
    return pltpu.CompilerParams(dimension_semantics=sem, vmem_limit_bytes=vmem)


def _resident(shape):
    nd = len(shape)
    return pl.BlockSpec(shape, lambda *_: (0,) * nd, pipeline_mode=pl.Buffered(1))


def _mod_kernel(c_ref, w_ref, b_ref, o_ref):
    c = c_ref[...]
    ca = c * jax.nn.sigmoid(c)
    o_ref[...] = jnp.dot(ca.astype(BF16), w_ref[...].astype(BF16),
                         preferred_element_type=F32) + b_ref[...]


def _modulation(c, w_mod, b_mod):
    b, d = c.shape
    n = w_mod.shape[1]
    tn = n // 4
    return pl.pallas_call(
        _mod_kernel,
        grid=(n // tn,),
        in_specs=[pl.BlockSpec((b, d), lambda j: (0, 0)),
                  pl.BlockSpec((d, tn), lambda j: (0, j)),
                  pl.BlockSpec((1, tn), lambda j: (0, j))],
        out_specs=pl.BlockSpec((b, tn), lambda j: (0, j)),
        out_shape=jax.ShapeDtypeStruct((b, n), F32),
        compiler_params=_cparams(("arbitrary",)),
        name="modulation",
    )(c, w_mod, b_mod.reshape(1, n))


def _rms_mod(x, g, scale, shift):
    ms = jnp.mean(x * x, axis=-1, keepdims=True)
    return x * lax.rsqrt(ms + EPS) * g * (1.0 + scale) + shift


def _gelu_tanh(y):
    return y * (0.5 * (1.0 + jnp.tanh(0.7978845608028654 * (y + 0.044715 * (y * y * y)))))


def _proj_kernel(x_ref, mod_ref, g_ref, w_ref, cw_ref, cb_ref, qkv0_ref, qkv1_ref, qkv2_ref,
                 xc_ref, gy_ref, gl_ref, hs_ref, tail_ref, *, tm):
    d_model = x_ref.shape[-1]
    m = mod_ref[0]
    h = _rms_mod(x_ref[0], g_ref[...], m[:, d_model:2 * d_model], m[:, 0:d_model])

    def mm(hv, lo, hi):
        return jnp.dot(hv, w_ref[:, lo:hi], preferred_element_type=F32)

    hb = h.astype(BF16)
    c0 = len(DILATIONS) * GROUP_COLS
    qkv0_ref[0] = mm(hb, 0, GROUP_COLS).astype(BF16)

    @pl.when(pl.program_id(1) == 0)
    def _():
        tail_ref[...] = jnp.zeros_like(tail_ref)

    xr = mm(hb, c0, c0 + LRU_WIDTH)
    xe = jnp.concatenate([tail_ref[...], xr], axis=0)
    tail_ref[...] = xr[tm - CONV_TAIL:, :]
    cw = cw_ref[...]
    xc = xr * cw[CONV_WIDTH - 1:CONV_WIDTH] + cb_ref[...]
    for k in range(1, CONV_WIDTH):
        xc = xc + xe[CONV_TAIL - k:CONV_TAIL - k + tm, :] * cw[CONV_WIDTH - 1 - k:CONV_WIDTH - k]
    xc_ref[0] = xc.astype(BF16)
    gy_ref[0] = _gelu_tanh(mm(hb, c0 + LRU_WIDTH, c0 + 2 * LRU_WIDTH)).astype(BF16)
    gl_ref[0] = mm(hb, c0 + 2 * LRU_WIDTH, c0 + 2 * LRU_WIDTH + 2 * d_model).astype(BF16)

    n_slab = d_model // LANE
    for j in range(n_slab):
        hs_ref[j] = h[:, j * LANE:(j + 1) * LANE]
    for g, out_ref in ((1, qkv1_ref), (2, qkv2_ref)):
        d = DILATIONS[g]
        rows = tm // d
        hp = jnp.concatenate(
            [jnp.concatenate([hs_ref[j, pl.ds(p, rows, stride=d), :] for j in range(n_slab)], axis=1)
             for p in range(d)], axis=0).astype(BF16)
        res = mm(hp, g * GROUP_COLS, (g + 1) * GROUP_COLS).astype(BF16)
        for p in range(d):
            out_ref[p] = res[p * rows:(p + 1) * rows]


def _projection(x, mod3, g1, w_r, conv_w, conv_b, *, tm=256):
    b, s, d = x.shape
    n = w_r.shape[1]
    assert s % tm == 0 and tm % (16 * DILATIONS[-1]) == 0 and CONV_TAIL >= CONV_WIDTH - 1
    out_shape = [jax.ShapeDtypeStruct((b * dd, s // dd, GROUP_COLS), BF16) for dd in DILATIONS]
    out_shape += [jax.ShapeDtypeStruct((b, s, LRU_WIDTH), BF16),
                  jax.ShapeDtypeStruct((b, s, LRU_WIDTH), BF16),
                  jax.ShapeDtypeStruct((b, s, 2 * d), BF16)]
    out_specs = [pl.BlockSpec((dd, tm // dd, GROUP_COLS), lambda bi, i: (bi, i, 0)) for dd in DILATIONS]
    out_specs += [pl.BlockSpec((1, tm, LRU_WIDTH), lambda bi, i: (bi, i, 0)),
                  pl.BlockSpec((1, tm, LRU_WIDTH), lambda bi, i: (bi, i, 0)),
                  pl.BlockSpec((1, tm, 2 * d), lambda bi, i: (bi, i, 0))]
    return pl.pallas_call(
        functools.partial(_proj_kernel, tm=tm),
        grid=(b, s // tm),
        in_specs=[pl.BlockSpec((1, tm, d), lambda bi, i: (bi, i, 0)),
                  pl.BlockSpec((1, 1, mod3.shape[-1]), lambda bi, i: (bi, 0, 0)),
                  pl.BlockSpec((1, d), lambda bi, i: (0, 0)),
                  _resident((d, n)),
                  pl.BlockSpec((CONV_WIDTH, LRU_WIDTH), lambda bi, i: (0, 0)),
                  pl.BlockSpec((1, LRU_WIDTH), lambda bi, i: (0, 0))],
        out_specs=out_specs,
        out_shape=out_shape,
        scratch_shapes=[pltpu.VMEM((d // LANE, tm, LANE), F32), pltpu.VMEM((CONV_TAIL, LRU_WIDTH), F32)],
        compiler_params=_cparams(("parallel", "arbitrary"), VMEM_LIMIT),
        name="projection",
    )(x, mod3, g1, w_r, conv_w, conv_b.reshape(1, LRU_WIDTH))


def _attn_kernel(q0, k0, v0, q1, k1, v1, q2, k2, v2, o_ref, acc_ref, lse_ref, bias_ref, *, seq):
    hcols = o_ref.shape[-1]
    n_head = hcols // HEAD_DIM
    head_of_lane = lax.broadcasted_iota(jnp.int32, (SPAN, hcols), 1) // HEAD_DIM
    head_mask_b = [jnp.where(head_of_lane == h, 1.0, 0.0).astype(BF16) for h in range(n_head)]

    def by_head(parts):
        out = parts[n_head - 1]
        for h in range(n_head - 2, -1, -1):
            out = jnp.where(head_of_lane == h, parts[h], out)
        return out

    qi = lax.broadcasted_iota(jnp.int32, (n_head * SPAN, 2 * SPAN), 0) % SPAN
    ki = lax.broadcasted_iota(jnp.int32, (n_head * SPAN, 2 * SPAN), 1)
    band = (ki >= qi) & (ki <= qi + SPAN)
    bias_ref[0] = jnp.where(band, 0.0, -jnp.inf)
    bias_ref[1] = jnp.where(band & (ki >= SPAN), 0.0, -jnp.inf)

    for g, (q_ref, k_ref, v_ref) in enumerate(((q0, k0, v0), (q1, k1, v1), (q2, k2, v2))):
        d = DILATIONS[g]
        n_blk = seq // d // SPAN

        def tile(n, carry, q_ref=q_ref, k_ref=k_ref, v_ref=v_ref, d=d, n_blk=n_blk, g=g):
            p = n // n_blk
            blk = n % n_blk
            r0 = pl.multiple_of(blk * SPAN, SPAN)
            rp = pl.multiple_of(jnp.maximum(blk - 1, 0) * SPAN, SPAN)
            q = q_ref[p, pl.ds(r0, SPAN), :]
            kk = jnp.concatenate([k_ref[p, pl.ds(rp, SPAN), :], k_ref[p, pl.ds(r0, SPAN), :]], axis=0)
            vv = jnp.concatenate([v_ref[p, pl.ds(rp, SPAN), :], v_ref[p, pl.ds(r0, SPAN), :]], axis=0)
            qs = jnp.concatenate([q * head_mask_b[h] for h in range(n_head)], axis=0)
            sc = lax.dot_general(qs, kk, (((1,), (1,)), ((), ())), preferred_element_type=F32)
            sc = sc + bias_ref[jnp.where(blk > 0, 0, 1)]
            mx = jnp.max(sc, axis=-1, keepdims=True)
            e = jnp.exp(sc - mx)
            den = jnp.sum(e, axis=-1, keepdims=True)
            pv = jnp.dot(e.astype(BF16), vv, preferred_element_type=F32)
            lse = mx + jnp.log(den)
            rows_of = lambda a: [a[h * SPAN:(h + 1) * SPAN] for h in range(n_head)]
            o = by_head(rows_of(pv)) / by_head(rows_of(den))
            l = by_head(rows_of(lse))
            start = p + d * r0
            for j in range(hcols // LANE):
                rows = pl.ds(start, SPAN, stride=d) if d > 1 else pl.ds(start, SPAN)
                acc_ref[g, j, rows, :] = o[:, j * LANE:(j + 1) * LANE]
                lse_ref[g, j, rows, :] = l[:, j * LANE:(j + 1) * LANE]
            return carry

        lax.fori_loop(0, seq // SPAN, tile, 0, unroll=8)

    chunk = 256

    def combine(c, carry):
        r = pl.multiple_of(c * chunk, chunk)
        for j in range(hcols // LANE):
            ls = [lse_ref[g, j, pl.ds(r, chunk), :] for g in range(len(DILATIONS))]
            mx = jnp.maximum(jnp.maximum(ls[0], ls[1]), ls[2])
            ws = [jnp.exp(v - mx) for v in ls]
            num = ws[0] * acc_ref[0, j, pl.ds(r, chunk), :]
            for g in range(1, len(DILATIONS)):
                num = num + ws[g] * acc_ref[g, j, pl.ds(r, chunk), :]
            o_ref[0, pl.ds(r, chunk), j * LANE:(j + 1) * LANE] = (num / (ws[0] + ws[1] + ws[2])).astype(BF16)
        return carry

    lax.fori_loop(0, seq // chunk, combine, 0)


def _attention(qkvs, b, s):
    hcols = 4 * HEAD_DIM
    n_hg = ATTN_OUT // hcols
    ncb = ATTN_OUT // hcols
    in_specs, args = [], []
    for g, d in enumerate(DILATIONS):
        for part in range(3):
            in_specs.append(pl.BlockSpec((d, s // d, hcols),
                                         lambda bi, hg, part=part: (bi, 0, part * ncb + hg)))
            args.append(qkvs[g])
    return pl.pallas_call(
        functools.partial(_attn_kernel, seq=s),
        grid=(b, n_hg),
        in_specs=in_specs,
        out_specs=pl.BlockSpec((1, s, hcols), lambda bi, hg: (bi, 0, hg)),
        out_shape=jax.ShapeDtypeStruct((b, s, ATTN_OUT), BF16),
        scratch_shapes=[pltpu.VMEM((len(DILATIONS), hcols // LANE, s, LANE), F32),
                        pltpu.VMEM((len(DILATIONS), hcols // LANE, s, LANE), F32),
                        pltpu.VMEM((2, (hcols // HEAD_DIM) * SPAN, 2 * SPAN), F32)],
        compiler_params=_cparams(("parallel", "parallel"), VMEM_LIMIT),
        name="dilated_attention",
    )(*args)


def _lru_kernel(xc_ref, gy_ref, wg_ref, bx_ref, ba_ref, lam_ref, o_ref, a_ref, b_ref, h_ref, *, pitch):
    nb, ts, tc = xc_ref.shape
    nl = tc // LANE

    @pl.when(pl.program_id(1) == 0)
    def _():
        h_ref[...] = jnp.zeros_like(h_ref)

    xb = xc_ref[...].reshape(nb * ts, tc)
    xc = xb.astype(F32)
    gates = jnp.dot(xb, wg_ref[0], preferred_element_type=F32)
    gate_i = jax.nn.sigmoid(gates[:, :tc] + bx_ref[...])
    gate_r = jax.nn.sigmoid(gates[:, tc:] + ba_ref[...])
    neg_lam = -lam_ref[...]
    softplus = jnp.maximum(neg_lam, 0.0) + jnp.log1p(jnp.exp(-jnp.abs(neg_lam)))
    log_a = (-LRU_C) * gate_r * softplus
    a = jnp.exp(log_a)
    bv = jnp.sqrt(jnp.tanh(-log_a) * (1.0 + a * a)) * gate_i * xc
    for bi in range(nb):
        for j in range(nl):
            a_ref[j, pl.ds(bi * pitch, ts), :] = a[bi * ts:(bi + 1) * ts, j * LANE:(j + 1) * LANE]
            b_ref[j, pl.ds(bi * pitch, ts), :] = bv[bi * ts:(bi + 1) * ts, j * LANE:(j + 1) * LANE]

    def step(t, hs):
        out = []
        for j in range(nl):
            rows = pl.ds(t, nb, stride=pitch)
            h = a_ref[j, rows, :] * hs[j] + b_ref[j, rows, :]
            b_ref[j, rows, :] = h
            out.append(h)
        return tuple(out)

    hs = lax.fori_loop(0, ts, step, tuple(h_ref[j] for j in range(nl)), unroll=8)
    for j in range(nl):
        h_ref[j] = hs[j]
    for bi in range(nb):
        h = jnp.concatenate([b_ref[j, pl.ds(bi * pitch, ts), :] for j in range(nl)], axis=1)
        o_ref[bi] = (h * gy_ref[bi].astype(F32)).astype(BF16)


def _lru_gate_weights(wx, wa, tc):
    nb, bd, _ = wx.shape
    per = tc // bd
    eye = jnp.eye(per, dtype=wx.dtype)

    def bdiag(w):
        w = w.reshape(nb // per, per, bd, bd)
        return jnp.einsum('cpio,pq->cpiqo', w, eye).reshape(nb // per, tc, tc)

    return jnp.concatenate([bdiag(wx), bdiag(wa)], axis=-1).astype(BF16)


def _lru_branch(xc, gy, wx, bx, wa, ba, lam, *, tc=256, ts=128):
    b, s, c = xc.shape
    assert s % ts == 0 and c % tc == 0
    wg = _lru_gate_weights(wx, wa, tc)
    row = lambda v: v.reshape(1, c)
    tile = pl.BlockSpec((b, ts, tc), lambda ci, ti: (0, ti, ci))
    vec = pl.BlockSpec((1, tc), lambda ci, ti: (0, ci))
    pitch = ts + 8
    return pl.pallas_call(
        functools.partial(_lru_kernel, pitch=pitch),
        grid=(c // tc, s // ts),
        in_specs=[tile, tile,
                  pl.BlockSpec((1, tc, 2 * tc), lambda ci, ti: (ci, 0, 0)),
                  vec, vec, vec],
        out_specs=tile,
        out_shape=jax.ShapeDtypeStruct((b, s, c), BF16),
        scratch_shapes=[pltpu.VMEM((tc // LANE, b * pitch, LANE), F32),
                        pltpu.VMEM((tc // LANE, b * pitch, LANE), F32),
                        pltpu.VMEM((tc // LANE, b, LANE), F32)],
        compiler_params=_cparams(("parallel", "arbitrary"), VMEM_LIMIT),
        name="rg_lru",
    )(xc, gy, wg, row(bx), row(ba), row(lam))


def _prep_w_in(w_in):
    a = ATTN_WIDTH
    gw = N_SLOTS * HEAD_DIM
    q = w_in[:, :a] * (HEAD_DIM ** -0.5)
    k = w_in[:, a:2 * a]
    v = w_in[:, 2 * a:3 * a]
    parts = []
    for g in range(len(DILATIONS)):
        sl = slice(g * gw, (g + 1) * gw)
        parts += [q[:, sl], k[:, sl], v[:, sl]]
    parts.append(w_in[:, 3 * a:])
    return jnp.concatenate(parts, axis=1).astype(BF16)


ROW_SUBLANES = D_MODEL // 2 // LANE


def _store_tile_rows(ref, v, row0=0):
    n, half = v.shape[0], v.shape[1] // 2
    lo = pltpu.bitcast(v[:, :half].astype(BF16).astype(F32), jnp.uint32)
    hi = pltpu.bitcast(v[:, half:].astype(BF16).astype(F32), jnp.uint32)
    words = (hi & jnp.uint32(0xFFFF0000)) | (lo >> 16)
    for j in range(ROW_SUBLANES):
        ref[pl.ds(row0 * ROW_SUBLANES + j, n, stride=ROW_SUBLANES), :] = words[:, j * LANE:(j + 1) * LANE]


def _load_tile_rows(ref):
    n = ref.shape[0] // ROW_SUBLANES
    words = [ref[pl.ds(j, n, stride=ROW_SUBLANES), :] for j in range(ROW_SUBLANES)]
    lo = [pltpu.bitcast(w << 16, F32) for w in words]
    hi = [pltpu.bitcast(w & jnp.uint32(0xFFFF0000), F32) for w in words]
    return jnp.concatenate(lo + hi, axis=-1)


SC_CORES, SC_SUBCORES = 2, 16
SC_CHUNK = 128


def _sc_gather_rows(table, idx):
    n = idx.shape[0]
    per_worker = n // (SC_CORES * SC_SUBCORES)
    n_chunks = per_worker // SC_CHUNK
    assert n_chunks * SC_CHUNK * SC_CORES * SC_SUBCORES == n
    mesh = plsc.VectorSubcoreMesh(core_axis_name="c", subcore_axis_name="s")

    def body(table_hbm, idx_hbm, out_hbm, idx_v, rows_v, sem):
        base = (lax.axis_index("s") * SC_CORES + lax.axis_index("c")) * per_worker

        @pl.loop(0, n_chunks)
        def _(i):
            off = pl.multiple_of(base + i * SC_CHUNK, SC_CHUNK)
            pltpu.sync_copy(idx_hbm.at[pl.ds(off, SC_CHUNK)], idx_v)
            pltpu.async_copy(table_hbm.at[idx_v], rows_v, sem).wait()
            pltpu.sync_copy(rows_v, out_hbm.at[pl.ds(off, SC_CHUNK)])

    return pl.kernel(
        body, mesh=mesh,
        out_type=jax.ShapeDtypeStruct((n,) + table.shape[1:], table.dtype),
        scratch_types=[pltpu.VMEM((SC_CHUNK,), jnp.int32),
                       pltpu.VMEM((SC_CHUNK,) + table.shape[1:], table.dtype),
                       pltpu.SemaphoreType.DMA],
        name="sc_gather_rows",
    )(table, idx)


def _sc_scatter_rows(rows, idx, n_out):
    n_rows = rows.shape[0]
    n_choice = idx.shape[0] // n_rows
    per_worker = n_rows // (SC_CORES * SC_SUBCORES)
    n_chunks = per_worker // SC_CHUNK
    assert n_chunks * SC_CHUNK * SC_CORES * SC_SUBCORES == n_rows and n_choice * n_rows == idx.shape[0]
    mesh = plsc.VectorSubcoreMesh(core_axis_name="c", subcore_axis_name="s")

    def body(rows_hbm, idx_hbm, out_hbm, idx_v, rows_v):
        base = (lax.axis_index("s") * SC_CORES + lax.axis_index("c")) * per_worker

        @pl.loop(0, n_chunks)
        def _(i):
            off = pl.multiple_of(base + i * SC_CHUNK, SC_CHUNK)
            pltpu.sync_copy(rows_hbm.at[pl.ds(off, SC_CHUNK)], rows_v)
            for k in range(n_choice):
                pltpu.sync_copy(idx_hbm.at[pl.ds(k * n_rows + off, SC_CHUNK)], idx_v)
                pltpu.sync_copy(rows_v, out_hbm.at[idx_v])

    return pl.kernel(
        body, mesh=mesh,
        out_type=jax.ShapeDtypeStruct((n_out,) + rows.shape[1:], rows.dtype),
        scratch_types=[pltpu.VMEM((SC_CHUNK,), jnp.int32),
                       pltpu.VMEM((SC_CHUNK,) + rows.shape[1:], rows.dtype)],
        name="sc_scatter_rows",
    )(rows, idx)


ROUTE_ROWS = 8
EXPERT_ROW0 = N_GROUPS
ROUTER_ROWS = 48


def _mix_kernel(attn_ref, lru_ref, gl_ref, x_ref, mod_ref, wa_ref, wl_ref, wo_ref, g2_ref,
                wrt_ref, brt_ref, x1_ref, h2_ref, route_ref, cnt_ref, cnt_acc):
    d = x_ref.shape[-1]
    tm = x_ref.shape[1]

    @pl.when((pl.program_id(0) == 0) & (pl.program_id(1) == 0))
    def _():
        cnt_acc[...] = jnp.zeros_like(cnt_acc)

    m = mod_ref[0]
    gate1, shift2, scale2 = m[:, 2 * d:3 * d], m[:, 3 * d:4 * d], m[:, 4 * d:5 * d]
    ya = jnp.dot(attn_ref[0], wa_ref[...], preferred_element_type=F32)
    yl = jnp.dot(lru_ref[0], wl_ref[...], preferred_element_type=F32)
    gates = jax.nn.sigmoid(gl_ref[0].astype(F32))
    mixed = gates[:, :d] * ya + gates[:, d:] * yl
    y = jnp.dot(mixed.astype(BF16), wo_ref[...], preferred_element_type=F32)
    x1 = x_ref[0] + (1.0 + gate1) * y
    x1_ref[0] = x1.astype(BF16)
    h2 = _rms_mod(x1, g2_ref[...], scale2, shift2)
    _store_tile_rows(h2_ref, h2)
    logits = lax.dot_general(wrt_ref[...], h2.astype(BF16), (((1,), (1,)), ((), ())),
                             preferred_element_type=F32) + brt_ref[...]

    row = lax.broadcasted_iota(jnp.int32, logits.shape, 0)
    neg = -jnp.inf

    def top(vals):
        mx = jnp.max(vals, axis=0, keepdims=True)
        idx = jnp.min(jnp.where(vals == mx, row, ROUTER_ROWS), axis=0, keepdims=True)
        return mx, idx

    is_grp = row < N_GROUPS
    gmax, gidx = top(jnp.where(is_grp, logits, neg))
    grp_gate = 1.0 / jnp.sum(jnp.where(is_grp, jnp.exp(logits - gmax), 0.0), axis=0, keepdims=True)
    lo = EXPERT_ROW0 + EXPERTS_PER_GROUP * gidx
    el = jnp.where((row >= lo) & (row < lo + EXPERTS_PER_GROUP), logits, neg)
    v1, i1 = top(el)
    v2, i2 = top(jnp.where(row == i1, neg, el))
    e21 = jnp.exp(v2 - v1)
    wt1 = grp_gate / (1.0 + e21)
    wt2 = wt1 * e21

    oh1 = jnp.where(row == i1, 1.0, 0.0)
    oh2 = jnp.where(row == i2, 1.0, 0.0)
    ohs = oh1 + oh2
    rr = lax.broadcasted_iota(jnp.int32, (tm, tm), 0)
    cc = lax.broadcasted_iota(jnp.int32, (tm, tm), 1)
    earlier = jnp.where(rr < cc, 1.0, 0.0).astype(BF16)
    before = jnp.dot(ohs.astype(BF16), earlier, preferred_element_type=F32) + cnt_acc[...]
    rank1 = jnp.sum(oh1 * before, axis=0, keepdims=True)
    rank2 = jnp.sum(oh2 * before, axis=0, keepdims=True)
    cnt_acc[...] = cnt_acc[...] + jnp.sum(ohs, axis=1, keepdims=True)
    cnt_ref[...] = cnt_acc[...]

    vals = [(i1 - EXPERT_ROW0).astype(F32), (i2 - EXPERT_ROW0).astype(F32), rank1, rank2, wt1, wt2]
    out_row = lax.broadcasted_iota(jnp.int32, (ROUTE_ROWS, tm), 0)
    slab = jnp.zeros((ROUTE_ROWS, tm), F32)
    for j, v in enumerate(vals):
        slab = jnp.where(out_row == j, v, slab)
    route_ref[...] = slab


def _mix_route(attn, lru, gl, x, mod3, wa, wl, wo, g2, wrt, brt, b0, nb, *, tm=512):
    _, s, d = x.shape
    spt = s // tm
    tok_in = lambda w: pl.BlockSpec((1, tm, w), lambda bi, i: (b0 + bi, i, 0))
    return pl.pallas_call(
        _mix_kernel,
        grid=(nb, spt),
        in_specs=[tok_in(attn.shape[-1]), tok_in(d), tok_in(2 * d), tok_in(d),
                  pl.BlockSpec((1, 1, mod3.shape[-1]), lambda bi, i: (b0 + bi, 0, 0)),
                  _resident(wa.shape), _resident(wl.shape), _resident(wo.shape),
                  pl.BlockSpec((1, d), lambda bi, i: (0, 0)),
                  _resident(wrt.shape),
                  pl.BlockSpec((ROUTER_ROWS, 1), lambda bi, i: (0, 0))],
        out_specs=[pl.BlockSpec((1, tm, d), lambda bi, i: (bi, i, 0)),
                   pl.BlockSpec((tm * ROW_SUBLANES, LANE), lambda bi, i: (bi * spt + i, 0)),
                   pl.BlockSpec((ROUTE_ROWS, tm), lambda bi, i: (0, bi * spt + i)),
                   pl.BlockSpec((ROUTER_ROWS, 1), lambda bi, i: (0, 0))],
        out_shape=[jax.ShapeDtypeStruct((nb, s, d), BF16),
                   jax.ShapeDtypeStruct((nb * s * ROW_SUBLANES, LANE), jnp.uint32),
                   jax.ShapeDtypeStruct((ROUTE_ROWS, nb * s), F32),
                   jax.ShapeDtypeStruct((ROUTER_ROWS, 1), F32)],
        scratch_shapes=[pltpu.VMEM((ROUTER_ROWS, 1), F32)],
        compiler_params=_cparams(("arbitrary", "arbitrary"), VMEM_LIMIT),
        name="mix_route",
    )(attn, lru, gl, x, mod3, wa, wl, wo, g2, wrt, brt)


TOP_K = 2
EXPERT_BLOCK = 512
MOE_BATCH_RANGES = 2


def _expert_kernel(be_ref, nu_ref, x_ref, w1_ref, w3_ref, w2_ref, y_ref, wb1, wb3, wb2):
    j = pl.program_id(0)

    @pl.when(j < nu_ref[0])
    def _():
        @pl.when((j == 0) | (be_ref[j] != be_ref[jnp.maximum(j - 1, 0)]))
        def _():
            wb1[...] = w1_ref[0].astype(BF16)
            wb3[...] = w3_ref[0].astype(BF16)
            wb2[...] = w2_ref[0].astype(BF16)

        xb = _load_tile_rows(x_ref).astype(BF16)
        a = jnp.dot(xb, wb1[...], preferred_element_type=F32)
        g = jnp.dot(xb, wb3[...], preferred_element_type=F32)
        hm = (a * jax.nn.sigmoid(a) * g).astype(BF16)
        _store_tile_rows(y_ref, jnp.dot(hm, wb2[...], preferred_element_type=F32))

    @pl.when(j >= nu_ref[0])
    def _():
        y_ref[...] = jnp.zeros_like(y_ref)


def _experts(xp, blk_e, n_used, w1, w3, w2):
    ne, d, de = w1.shape
    nb = xp.shape[0] // (EXPERT_BLOCK * ROW_SUBLANES)
    rows = (EXPERT_BLOCK * ROW_SUBLANES, LANE)
    last = lambda j, nu: jnp.minimum(j, nu[0] - 1)
    grid_spec = pltpu.PrefetchScalarGridSpec(
        num_scalar_prefetch=2,
        grid=(nb,),
        in_specs=[pl.BlockSpec(rows, lambda j, be, nu: (last(j, nu), 0)),
                  pl.BlockSpec((1, d, de), lambda j, be, nu: (be[j], 0, 0)),
                  pl.BlockSpec((1, d, de), lambda j, be, nu: (be[j], 0, 0)),
                  pl.BlockSpec((1, de, d), lambda j, be, nu: (be[j], 0, 0))],
        out_specs=pl.BlockSpec(rows, lambda j, be, nu: (j, 0)),
        scratch_shapes=[pltpu.VMEM((d, de), BF16), pltpu.VMEM((d, de), BF16), pltpu.VMEM((de, d), BF16)])
    return pl.pallas_call(
        _expert_kernel,
        grid_spec=grid_spec,
        out_shape=jax.ShapeDtypeStruct(xp.shape, xp.dtype),
        compiler_params=_cparams(("arbitrary",), VMEM_LIMIT),
        name="experts",
    )(blk_e, n_used, xp, w1, w3, w2)


def _combine_kernel(y0_ref, y1_ref, route_ref, x1_ref, mod_ref, gf_ref, *rest):
    o_ref = rest[-1]
    tm, d = x1_ref.shape[1], x1_ref.shape[2]
    route = jnp.concatenate([route_ref[...], jnp.zeros((LANE - ROUTE_ROWS, tm), F32)], axis=0).T
    moe = _load_tile_rows(y0_ref) * route[:, 4:5] + _load_tile_rows(y1_ref) * route[:, 5:6]
    gate2 = mod_ref[0][:, 5 * d:6 * d]
    xo = x1_ref[0].astype(F32) + (1.0 + gate2) * moe
    ms = jnp.mean(xo * xo, axis=-1, keepdims=True)
    o_ref[0] = xo * lax.rsqrt(ms + EPS) * gf_ref[...]


def _combine(yg, route, x1, mod3, gf, b0, out_prev, *, tm=256):
    nb, s, d = x1.shape
    b_all = mod3.shape[0]
    spt = s // tm
    nt = nb * spt
    rows = (tm * ROW_SUBLANES, LANE)
    in_specs = [pl.BlockSpec(rows, lambda bi, i: (bi * spt + i, 0)),
                pl.BlockSpec(rows, lambda bi, i: (nt + bi * spt + i, 0)),
                pl.BlockSpec((ROUTE_ROWS, tm), lambda bi, i: (0, bi * spt + i)),
                pl.BlockSpec((1, tm, d), lambda bi, i: (bi, i, 0)),
                pl.BlockSpec((1, 1, mod3.shape[-1]), lambda bi, i: (b0 + bi, 0, 0)),
                pl.BlockSpec((1, d), lambda bi, i: (0, 0))]
    args = [yg, yg, route, x1, mod3, gf]
    aliases = {}
    if out_prev is not None:
        in_specs.append(pl.BlockSpec(memory_space=pl.ANY))
        aliases = {len(args): 0}
        args.append(out_prev)
    return pl.pallas_call(
        _combine_kernel,
        grid=(nb, spt),
        in_specs=in_specs,
        out_specs=pl.BlockSpec((1, tm, d), lambda bi, i: (b0 + bi, i, 0)),
        out_shape=jax.ShapeDtypeStruct((b_all, s, d), F32),
        input_output_aliases=aliases,
        compiler_params=_cparams(("parallel", "parallel"), VMEM_LIMIT),
        name="combine",
    )(*args)


def _slot_plan(route, counts, n_tok):
    sizes = counts[EXPERT_ROW0:EXPERT_ROW0 + N_EXPERTS, 0].astype(jnp.int32)
    padded = (sizes + EXPERT_BLOCK - 1) // EXPERT_BLOCK * EXPERT_BLOCK
    pad_ends = jnp.cumsum(padded)
    pad_starts = pad_ends - padded
    eid = route[0:TOP_K].astype(jnp.int32)
    rank = route[TOP_K:2 * TOP_K].astype(jnp.int32)
    dest = (pad_starts[eid] + rank).reshape(TOP_K * n_tok)
    n_blocks = (n_tok * TOP_K + N_EXPERTS * (EXPERT_BLOCK - 1) + EXPERT_BLOCK - 1) // EXPERT_BLOCK
    gran = SC_CORES * SC_SUBCORES * SC_CHUNK // math.gcd(SC_CORES * SC_SUBCORES * SC_CHUNK, EXPERT_BLOCK)
    n_blocks = (n_blocks + gran - 1) // gran * gran
    n_used = pad_ends[-1] // EXPERT_BLOCK
    blk = jnp.minimum(jnp.arange(n_blocks), n_used - 1)
    blk_e = jnp.minimum(jnp.sum(pad_ends[None, :] <= (blk * EXPERT_BLOCK)[:, None], axis=1), N_EXPERTS - 1)
    return dest, n_blocks * EXPERT_BLOCK, blk_e.astype(jnp.int32), n_used.reshape(1).astype(jnp.int32)


def kernel(x, c, w_mod, b_mod, norm1_g, w_in, conv_w, conv_b, lru_wx, lru_bx, lru_wa, lru_ba, lru_lambda, w_attn_o, w_lru_o, w_out, norm2_g, w_grp, b_grp, w_exp, b_exp, w1, w3, w2, norm_f_g):
    b, s, d = x.shape
    assert d == D_MODEL and s == SPAN * DILATIONS[-1] and w_mod.shape[0] == 1
    mod3 = _modulation(c, w_mod[0], b_mod[0]).reshape(b, 1, 6 * d)
    qkv0, qkv1, qkv2, xc, gy, gl = _projection(x, mod3, norm1_g[0].reshape(1, d), _prep_w_in(w_in[0]),
                                               conv_w[0], conv_b[0])
    attn = _attention((qkv0, qkv1, qkv2), b, s)
    lru = _lru_branch(xc, gy, lru_wx[0], lru_bx[0], lru_wa[0], lru_ba[0], lru_lambda[0])

    n_pad = ROUTER_ROWS - N_GROUPS - N_EXPERTS
    wr = jnp.pad(jnp.concatenate([w_grp[0], w_exp[0]], axis=1).T, ((0, n_pad), (0, 0))).astype(BF16)
    br = jnp.pad(jnp.concatenate([b_grp[0], b_exp[0]]), (0, n_pad)).reshape(ROUTER_ROWS, 1)
    wa, wl, wo = w_attn_o[0].astype(BF16), w_lru_o[0].astype(BF16), w_out[0].astype(BF16)
    as_rows = lambda a: a.reshape(-1, ROW_SUBLANES, LANE)
    as_tiles = lambda a: a.reshape(-1, LANE)

    out = None
    nb = b // MOE_BATCH_RANGES
    for b0 in range(0, b, nb):
        x1, h2, route, counts = _mix_route(attn, lru, gl, x, mod3, wa, wl, wo, norm2_g[0].reshape(1, d),
                                           wr, br, b0, nb)
        dest, n_slots, blk_e, n_used = _slot_plan(route, counts, nb * s)
        xp = as_tiles(_sc_scatter_rows(as_rows(h2), dest, n_slots))
        yp = _experts(xp, blk_e, n_used, w1[0], w3[0], w2[0])
        yg = as_tiles(_sc_gather_rows(as_rows(yp), dest))
        out = _combine(yg, route, x1, mod3, norm_f_g.reshape(1, d), b0, out)
    return out
```

```python
import functools
import math

import jax
import jax.numpy as jnp
from jax import lax
from jax.experimental import pallas as pl
from jax.experimental.pallas import tpu as pltpu
from jax.experimental.pallas import tpu_sc as plsc

F32 = jnp.float32
BF16 = jnp.bfloat16

D_MODEL = 1024
HEAD_DIM = 64
N_SLOTS = 8
SPAN = 128
DILATIONS = (1, 4, 16)
GROUP_COLS = 3 * N_SLOTS * HEAD_DIM
ATTN_WIDTH = len(DILATIONS) * N_SLOTS * HEAD_DIM
ATTN_OUT = N_SLOTS * HEAD_DIM
LRU_WIDTH = D_MODEL
LRU_BLOCK_DIM = 64
CONV_WIDTH = 4
CONV_TAIL = 8
LRU_C = 8.0
N_GROUPS = 4
EXPERTS_PER_GROUP = 8
N_EXPERTS = N_GROUPS * EXPERTS_PER_GROUP
D_EXPERT = D_MODEL // 2
EPS = 1e-6
LANE = 128
VMEM_LIMIT = 56 * 1024 * 1024


def _cparams(sem, vmem=None):
    return pltpu.CompilerParams(dimension_semantics=sem, vmem_limit_bytes=vmem)


def _resident(shape):
    nd = len(shape)
    return pl.BlockSpec(shape, lambda *_: (0,) * nd, pipeline_mode=pl.Buffered(1))


def _mod_kernel(c_ref, w_ref, b_ref, o_ref):
    c = c_ref[...]
    ca = c * jax.nn.sigmoid(c)
    o_ref[...] = jnp.dot(ca.astype(BF16), w_ref[...].astype(BF16),
                         preferred_element_type=F32) + b_ref[...]


def _modulation(c, w_mod, b_mod):
    b, d = c.shape
    n = w_mod.shape[1]
    tn = n // 4
    return pl.pallas_call(
        _mod_kernel,
        grid=(n // tn,),
        in_specs=[pl.BlockSpec((b, d), lambda j: (0, 0)),
                  pl.BlockSpec((d, tn), lambda j: (0, j)),
                  pl.BlockSpec((1, tn), lambda j: (0, j))],
        out_specs=pl.BlockSpec((b, tn), lambda j: (0, j)),
        out_shape=jax.ShapeDtypeStruct((b, n), F32),
        compiler_params=_cparams(("arbitrary",)),
        name="modulation",
    )(c, w_mod, b_mod.reshape(1, n))


def _rms_mod(x, g, scale, shift):
    ms = jnp.mean(x * x, axis=-1, keepdims=True)
    return x * lax.rsqrt(ms + EPS) * g * (1.0 + scale) + shift


def _gelu_tanh(y):
    return y * (0.5 * (1.0 + jnp.tanh(0.7978845608028654 * (y + 0.044715 * (y * y * y)))))


def _proj_kernel(x_ref, mod_ref, g_ref, w_ref, cw_ref, cb_ref, qkv0_ref, qkv1_ref, qkv2_ref,
                 xc_ref, gy_ref, gl_ref, hs_ref, xe_ref, *, tm):
    d_model = x_ref.shape[-1]
    m = mod_ref[0]
    h = _rms_mod(x_ref[0], g_ref[...], m[:, d_model:2 * d_model], m[:, 0:d_model])

    def mm(hv, lo, hi):
        return jnp.dot(hv, w_ref[:, lo:hi], preferred_element_type=F32)

    hb = h.astype(BF16)
    c0 = len(DILATIONS) * GROUP_COLS
    qkv0_ref[0] = mm(hb, 0, GROUP_COLS).astype(BF16)

    @pl.when(pl.program_id(1) == 0)
    def _():
        xe_ref[0:CONV_TAIL, :] = jnp.zeros((CONV_TAIL, LRU_WIDTH), F32)

    @pl.when(pl.program_id(1) > 0)
    def _():
        xe_ref[0:CONV_TAIL, :] = xe_ref[tm:tm + CONV_TAIL, :]

    xr = mm(hb, c0, c0 + LRU_WIDTH)
    xe_ref[CONV_TAIL:, :] = xr
    cw = cw_ref[...]
    xc = xr * cw[CONV_WIDTH - 1:CONV_WIDTH] + cb_ref[...]
    for k in range(1, CONV_WIDTH):
        xc = xc + xe_ref[CONV_TAIL - k:CONV_TAIL - k + tm, :] * cw[CONV_WIDTH - 1 - k:CONV_WIDTH - k]
    xc_ref[0] = xc.astype(BF16)
    gy_ref[0] = _gelu_tanh(mm(hb, c0 + LRU_WIDTH, c0 + 2 * LRU_WIDTH)).astype(BF16)
    gl_ref[0] = mm(hb, c0 + 2 * LRU_WIDTH, c0 + 2 * LRU_WIDTH + 2 * d_model).astype(BF16)

    n_slab = d_model // LANE
    for j in range(n_slab):
        hs_ref[j] = h[:, j * LANE:(j + 1) * LANE]
    for g, out_ref in ((1, qkv1_ref), (2, qkv2_ref)):
        d = DILATIONS[g]
        rows = tm // d
        hp = jnp.concatenate(
            [jnp.concatenate([hs_ref[j, pl.ds(p, rows, stride=d), :] for j in range(n_slab)], axis=1)
             for p in range(d)], axis=0).astype(BF16)
        res = mm(hp, g * GROUP_COLS, (g + 1) * GROUP_COLS).astype(BF16)
        for p in range(d):
            out_ref[p] = res[p * rows:(p + 1) * rows]


def _projection(x, mod3, g1, w_r, conv_w, conv_b, *, tm=256):
    b, s, d = x.shape
    n = w_r.shape[1]
    assert s % tm == 0 and tm % (16 * DILATIONS[-1]) == 0 and CONV_TAIL >= CONV_WIDTH - 1
    out_shape = [jax.ShapeDtypeStruct((b * dd, s // dd, GROUP_COLS), BF16) for dd in DILATIONS]
    out_shape += [jax.ShapeDtypeStruct((b, s, LRU_WIDTH), BF16),
                  jax.ShapeDtypeStruct((b, s, LRU_WIDTH), BF16),
                  jax.ShapeDtypeStruct((b, s, 2 * d), BF16)]
    out_specs = [pl.BlockSpec((dd, tm // dd, GROUP_COLS), lambda bi, i: (bi, i, 0)) for dd in DILATIONS]
    out_specs += [pl.BlockSpec((1, tm, LRU_WIDTH), lambda bi, i: (bi, i, 0)),
                  pl.BlockSpec((1, tm, LRU_WIDTH), lambda bi, i: (bi, i, 0)),
                  pl.BlockSpec((1, tm, 2 * d), lambda bi, i: (bi, i, 0))]
    return pl.pallas_call(
        functools.partial(_proj_kernel, tm=tm),
        grid=(b, s // tm),
        in_specs=[pl.BlockSpec((1, tm, d), lambda bi, i: (bi, i, 0)),
                  pl.BlockSpec((1, 1, mod3.shape[-1]), lambda bi, i: (bi, 0, 0)),
                  pl.BlockSpec((1, d), lambda bi, i: (0, 0)),
                  _resident((d, n)),
                  pl.BlockSpec((CONV_WIDTH, LRU_WIDTH), lambda bi, i: (0, 0)),
                  pl.BlockSpec((1, LRU_WIDTH), lambda bi, i: (0, 0))],
        out_specs=out_specs,
        out_shape=out_shape,
        scratch_shapes=[pltpu.VMEM((d // LANE, tm, LANE), F32), pltpu.VMEM((CONV_TAIL + tm, LRU_WIDTH), F32)],
        compiler_params=_cparams(("parallel", "arbitrary"), VMEM_LIMIT),
        name="projection",
    )(x, mod3, g1, w_r, conv_w, conv_b.reshape(1, LRU_WIDTH))


def _attn_kernel(q0, k0, v0, q1, k1, v1, q2, k2, v2, o_ref, acc_ref, lse_ref, bias_ref, *, seq):
    hcols = o_ref.shape[-1]
    n_head = hcols // HEAD_DIM
    head_of_lane = lax.broadcasted_iota(jnp.int32, (SPAN, hcols), 1) // HEAD_DIM
    head_mask_b = [jnp.where(head_of_lane == h, 1.0, 0.0).astype(BF16) for h in range(n_head)]

    def by_head(parts):
        out = parts[n_head - 1]
        for h in range(n_head - 2, -1, -1):
            out = jnp.where(head_of_lane == h, parts[h], out)
        return out

    qi = lax.broadcasted_iota(jnp.int32, (n_head * SPAN, 2 * SPAN), 0) % SPAN
    ki = lax.broadcasted_iota(jnp.int32, (n_head * SPAN, 2 * SPAN), 1)
    band = (ki >= qi) & (ki <= qi + SPAN)
    bias_ref[0] = jnp.where(band, 0.0, -jnp.inf)
    bias_ref[1] = jnp.where(band & (ki >= SPAN), 0.0, -jnp.inf)

    for g, (q_ref, k_ref, v_ref) in enumerate(((q0, k0, v0), (q1, k1, v1), (q2, k2, v2))):
        d = DILATIONS[g]
        n_blk = seq // d // SPAN

        def tile(n, carry, q_ref=q_ref, k_ref=k_ref, v_ref=v_ref, d=d, n_blk=n_blk, g=g):
            p = n // n_blk
            blk = n % n_blk
            r0 = pl.multiple_of(blk * SPAN, SPAN)
            rp = pl.multiple_of(jnp.maximum(blk - 1, 0) * SPAN, SPAN)
            q = q_ref[p, pl.ds(r0, SPAN), :]
            kk = jnp.concatenate([k_ref[p, pl.ds(rp, SPAN), :], k_ref[p, pl.ds(r0, SPAN), :]], axis=0)
            vv = jnp.concatenate([v_ref[p, pl.ds(rp, SPAN), :], v_ref[p, pl.ds(r0, SPAN), :]], axis=0)
            qs = jnp.concatenate([q * head_mask_b[h] for h in range(n_head)], axis=0)
            sc = lax.dot_general(qs, kk, (((1,), (1,)), ((), ())), preferred_element_type=F32)
            sc = sc + bias_ref[jnp.where(blk > 0, 0, 1)]
            mx = jnp.max(sc, axis=-1, keepdims=True)
            e = jnp.exp(sc - mx)
            den = jnp.sum(e, axis=-1, keepdims=True)
            pv = jnp.dot(e.astype(BF16), vv, preferred_element_type=F32)
            lse = mx + jnp.log(den)
            rows_of = lambda a: [a[h * SPAN:(h + 1) * SPAN] for h in range(n_head)]
            o = by_head(rows_of(pv)) / by_head(rows_of(den))
            l = by_head(rows_of(lse))
            start = p + d * r0
            for j in range(hcols // LANE):
                rows = pl.ds(start, SPAN, stride=d) if d > 1 else pl.ds(start, SPAN)
                acc_ref[g, j, rows, :] = o[:, j * LANE:(j + 1) * LANE]
                lse_ref[g, j, rows, :] = l[:, j * LANE:(j + 1) * LANE]
            return carry

        lax.fori_loop(0, seq // SPAN, tile, 0, unroll=8)

    chunk = 256

    def combine(c, carry):
        r = pl.multiple_of(c * chunk, chunk)
        for j in range(hcols // LANE):
            ls = [lse_ref[g, j, pl.ds(r, chunk), :] for g in range(len(DILATIONS))]
            mx = jnp.maximum(jnp.maximum(ls[0], ls[1]), ls[2])
            ws = [jnp.exp(v - mx) for v in ls]
            num = ws[0] * acc_ref[0, j, pl.ds(r, chunk), :]
            for g in range(1, len(DILATIONS)):
                num = num + ws[g] * acc_ref[g, j, pl.ds(r, chunk), :]
            o_ref[0, pl.ds(r, chunk), j * LANE:(j + 1) * LANE] = (num / (ws[0] + ws[1] + ws[2])).astype(BF16)
        return carry

    lax.fori_loop(0, seq // chunk, combine, 0)


def _attention(qkvs, b, s):
    hcols = 4 * HEAD_DIM
    n_hg = ATTN_OUT // hcols
    ncb = ATTN_OUT // hcols
    in_specs, args = [], []
    for g, d in enumerate(DILATIONS):
        for part in range(3):
            in_specs.append(pl.BlockSpec((d, s // d, hcols),
                                         lambda bi, hg, part=part: (bi, 0, part * ncb + hg)))
            args.append(qkvs[g])
    return pl.pallas_call(
        functools.partial(_attn_kernel, seq=s),
        grid=(b, n_hg),
        in_specs=in_specs,
        out_specs=pl.BlockSpec((1, s, hcols), lambda bi, hg: (bi, 0, hg)),
        out_shape=jax.ShapeDtypeStruct((b, s, ATTN_OUT), BF16),
        scratch_shapes=[pltpu.VMEM((len(DILATIONS), hcols // LANE, s, LANE), F32),
                        pltpu.VMEM((len(DILATIONS), hcols // LANE, s, LANE), F32),
                        pltpu.VMEM((2, (hcols // HEAD_DIM) * SPAN, 2 * SPAN), F32)],
        compiler_params=_cparams(("parallel", "parallel"), VMEM_LIMIT),
        name="dilated_attention",
    )(*args)


def _lru_kernel(xc_ref, gy_ref, wg_ref, bx_ref, ba_ref, lam_ref, o_ref, a_ref, b_ref, h_ref, *, pitch):
    nb, ts, tc = xc_ref.shape
    nl = tc // LANE

    @pl.when(pl.program_id(1) == 0)
    def _():
        h_ref[...] = jnp.zeros_like(h_ref)

    xb = xc_ref[...].reshape(nb * ts, tc)
    xc = xb.astype(F32)
    gates = jnp.dot(xb, wg_ref[0], preferred_element_type=F32)
    gate_i = jax.nn.sigmoid(gates[:, :tc] + bx_ref[...])
    gate_r = jax.nn.sigmoid(gates[:, tc:] + ba_ref[...])
    neg_lam = -lam_ref[...]
    softplus = jnp.maximum(neg_lam, 0.0) + jnp.log1p(jnp.exp(-jnp.abs(neg_lam)))
    log_a = (-LRU_C) * gate_r * softplus
    a = jnp.exp(log_a)
    bv = jnp.sqrt(jnp.tanh(-log_a) * (1.0 + a * a)) * gate_i * xc
    for bi in range(nb):
        for j in range(nl):
            a_ref[j, pl.ds(bi * pitch, ts), :] = a[bi * ts:(bi + 1) * ts, j * LANE:(j + 1) * LANE]
            b_ref[j, pl.ds(bi * pitch, ts), :] = bv[bi * ts:(bi + 1) * ts, j * LANE:(j + 1) * LANE]

    def step(t, hs):
        out = []
        for j in range(nl):
            rows = pl.ds(t, nb, stride=pitch)
            h = a_ref[j, rows, :] * hs[j] + b_ref[j, rows, :]
            b_ref[j, rows, :] = h
            out.append(h)
        return tuple(out)

    hs = lax.fori_loop(0, ts, step, tuple(h_ref[j] for j in range(nl)), unroll=8)
    for j in range(nl):
        h_ref[j] = hs[j]
    for bi in range(nb):
        h = jnp.concatenate([b_ref[j, pl.ds(bi * pitch, ts), :] for j in range(nl)], axis=1)
        o_ref[bi] = (h * gy_ref[bi].astype(F32)).astype(BF16)


def _lru_gate_weights(wx, wa, tc):
    nb, bd, _ = wx.shape
    per = tc // bd
    eye = jnp.eye(per, dtype=wx.dtype)

    def bdiag(w):
        w = w.reshape(nb // per, per, bd, bd)
        return jnp.einsum('cpio,pq->cpiqo', w, eye).reshape(nb // per, tc, tc)

    return jnp.concatenate([bdiag(wx), bdiag(wa)], axis=-1).astype(BF16)


def _lru_branch(xc, gy, wx, bx, wa, ba, lam, *, tc=256, ts=128):
    b, s, c = xc.shape
    assert s % ts == 0 and c % tc == 0
    wg = _lru_gate_weights(wx, wa, tc)
    row = lambda v: v.reshape(1, c)
    tile = pl.BlockSpec((b, ts, tc), lambda ci, ti: (0, ti, ci))
    vec = pl.BlockSpec((1, tc), lambda ci, ti: (0, ci))
    pitch = ts + 8
    return pl.pallas_call(
        functools.partial(_lru_kernel, pitch=pitch),
        grid=(c // tc, s // ts),
        in_specs=[tile, tile,
                  pl.BlockSpec((1, tc, 2 * tc), lambda ci, ti: (ci, 0, 0)),
                  vec, vec, vec],
        out_specs=tile,
        out_shape=jax.ShapeDtypeStruct((b, s, c), BF16),
        scratch_shapes=[pltpu.VMEM((tc // LANE, b * pitch, LANE), F32),
                        pltpu.VMEM((tc // LANE, b * pitch, LANE), F32),
                        pltpu.VMEM((tc // LANE, b, LANE), F32)],
        compiler_params=_cparams(("parallel", "arbitrary"), VMEM_LIMIT),
        name="rg_lru",
    )(xc, gy, wg, row(bx), row(ba), row(lam))


def _prep_w_in(w_in):
    a = ATTN_WIDTH
    gw = N_SLOTS * HEAD_DIM
    q = w_in[:, :a] * (HEAD_DIM ** -0.5)
    k = w_in[:, a:2 * a]
    v = w_in[:, 2 * a:3 * a]
    parts = []
    for g in range(len(DILATIONS)):
        sl = slice(g * gw, (g + 1) * gw)
        parts += [q[:, sl], k[:, sl], v[:, sl]]
    parts.append(w_in[:, 3 * a:])
    return jnp.concatenate(parts, axis=1).astype(BF16)


ROW_SUBLANES = D_MODEL // 2 // LANE


def _store_tile_rows(ref, v, row0=0):
    n, half = v.shape[0], v.shape[1] // 2
    lo = pltpu.bitcast(v[:, :half].astype(BF16).astype(F32), jnp.uint32)
    hi = pltpu.bitcast(v[:, half:].astype(BF16).astype(F32), jnp.uint32)
    words = (hi & jnp.uint32(0xFFFF0000)) | (lo >> 16)
    for j in range(ROW_SUBLANES):
        ref[pl.ds(row0 * ROW_SUBLANES + j, n, stride=ROW_SUBLANES), :] = words[:, j * LANE:(j + 1) * LANE]


def _load_tile_rows(ref):
    n = ref.shape[0] // ROW_SUBLANES
    words = [ref[pl.ds(j, n, stride=ROW_SUBLANES), :] for j in range(ROW_SUBLANES)]
    lo = [pltpu.bitcast(w << 16, F32) for w in words]
    hi = [pltpu.bitcast(w & jnp.uint32(0xFFFF0000), F32) for w in words]
    return jnp.concatenate(lo + hi, axis=-1)


SC_CORES, SC_SUBCORES = 2, 16
SC_CHUNK = 128


def _sc_gather_rows(table, idx):
    n = idx.shape[0]
    per_worker = n // (SC_CORES * SC_SUBCORES)
    n_chunks = per_worker // SC_CHUNK
    assert n_chunks * SC_CHUNK * SC_CORES * SC_SUBCORES == n
    mesh = plsc.VectorSubcoreMesh(core_axis_name="c", subcore_axis_name="s")

    def body(table_hbm, idx_hbm, out_hbm, idx_v, rows_v, sem):
        base = (lax.axis_index("s") * SC_CORES + lax.axis_index("c")) * per_worker

        @pl.loop(0, n_chunks)
        def _(i):
            off = pl.multiple_of(base + i * SC_CHUNK, SC_CHUNK)
            pltpu.sync_copy(idx_hbm.at[pl.ds(off, SC_CHUNK)], idx_v)
            pltpu.async_copy(table_hbm.at[idx_v], rows_v, sem).wait()
            pltpu.sync_copy(rows_v, out_hbm.at[pl.ds(off, SC_CHUNK)])

    return pl.kernel(
        body, mesh=mesh,
        out_type=jax.ShapeDtypeStruct((n,) + table.shape[1:], table.dtype),
        scratch_types=[pltpu.VMEM((SC_CHUNK,), jnp.int32),
                       pltpu.VMEM((SC_CHUNK,) + table.shape[1:], table.dtype),
                       pltpu.SemaphoreType.DMA],
        name="sc_gather_rows",
    )(table, idx)


def _sc_scatter_rows(rows, idx, n_out):
    n_rows = rows.shape[0]
    n_choice = idx.shape[0] // n_rows
    per_worker = n_rows // (SC_CORES * SC_SUBCORES)
    n_chunks = per_worker // SC_CHUNK
    assert n_chunks * SC_CHUNK * SC_CORES * SC_SUBCORES == n_rows and n_choice * n_rows == idx.shape[0]
    mesh = plsc.VectorSubcoreMesh(core_axis_name="c", subcore_axis_name="s")

    def body(rows_hbm, idx_hbm, out_hbm, idx_v, rows_v):
        base = (lax.axis_index("s") * SC_CORES + lax.axis_index("c")) * per_worker

        @pl.loop(0, n_chunks)
        def _(i):
            off = pl.multiple_of(base + i * SC_CHUNK, SC_CHUNK)
            pltpu.sync_copy(rows_hbm.at[pl.ds(off, SC_CHUNK)], rows_v)
            for k in range(n_choice):
                pltpu.sync_copy(idx_hbm.at[pl.ds(k * n_rows + off, SC_CHUNK)], idx_v)
                pltpu.sync_copy(rows_v, out_hbm.at[idx_v])

    return pl.kernel(
        body, mesh=mesh,
        out_type=jax.ShapeDtypeStruct((n_out,) + rows.shape[1:], rows.dtype),
        scratch_types=[pltpu.VMEM((SC_CHUNK,), jnp.int32),
                       pltpu.VMEM((SC_CHUNK,) + rows.shape[1:], rows.dtype)],
        name="sc_scatter_rows",
    )(rows, idx)


ROUTE_ROWS = 8
EXPERT_ROW0 = N_GROUPS
ROUTER_ROWS = 48


def _mix_kernel(attn_ref, lru_ref, gl_ref, x_ref, mod_ref, wa_ref, wl_ref, wo_ref, g2_ref,
                wrt_ref, brt_ref, x1_ref, h2_ref, route_ref, cnt_ref, cnt_acc):
    d = x_ref.shape[-1]
    tm = x_ref.shape[1]

    @pl.when((pl.program_id(0) == 0) & (pl.program_id(1) == 0))
    def _():
        cnt_acc[...] = jnp.zeros_like(cnt_acc)

    m = mod_ref[0]
    gate1, shift2, scale2 = m[:, 2 * d:3 * d], m[:, 3 * d:4 * d], m[:, 4 * d:5 * d]
    ya = jnp.dot(attn_ref[0], wa_ref[...], preferred_element_type=F32)
    yl = jnp.dot(lru_ref[0], wl_ref[...], preferred_element_type=F32)
    gates = jax.nn.sigmoid(gl_ref[0].astype(F32))
    mixed = gates[:, :d] * ya + gates[:, d:] * yl
    y = jnp.dot(mixed.astype(BF16), wo_ref[...], preferred_element_type=F32)
    x1 = x_ref[0] + (1.0 + gate1) * y
    x1_ref[0] = x1.astype(BF16)
    h2 = _rms_mod(x1, g2_ref[...], scale2, shift2)
    _store_tile_rows(h2_ref, h2)
    logits = lax.dot_general(wrt_ref[...], h2.astype(BF16), (((1,), (1,)), ((), ())),
                             preferred_element_type=F32) + brt_ref[...]

    row = lax.broadcasted_iota(jnp.int32, logits.shape, 0)
    neg = -jnp.inf

    def top(vals):
        mx = jnp.max(vals, axis=0, keepdims=True)
        idx = jnp.min(jnp.where(vals == mx, row, ROUTER_ROWS), axis=0, keepdims=True)
        return mx, idx

    is_grp = row < N_GROUPS
    gmax, gidx = top(jnp.where(is_grp, logits, neg))
    grp_gate = 1.0 / jnp.sum(jnp.where(is_grp, jnp.exp(logits - gmax), 0.0), axis=0, keepdims=True)
    lo = EXPERT_ROW0 + EXPERTS_PER_GROUP * gidx
    el = jnp.where((row >= lo) & (row < lo + EXPERTS_PER_GROUP), logits, neg)
    v1, i1 = top(el)
    v2, i2 = top(jnp.where(row == i1, neg, el))
    e21 = jnp.exp(v2 - v1)
    wt1 = grp_gate / (1.0 + e21)
    wt2 = wt1 * e21

    oh1 = jnp.where(row == i1, 1.0, 0.0)
    oh2 = jnp.where(row == i2, 1.0, 0.0)
    ohs = oh1 + oh2
    rr = lax.broadcasted_iota(jnp.int32, (tm, tm), 0)
    cc = lax.broadcasted_iota(jnp.int32, (tm, tm), 1)
    earlier = jnp.where(rr < cc, 1.0, 0.0).astype(BF16)
    before = jnp.dot(ohs.astype(BF16), earlier, preferred_element_type=F32) + cnt_acc[...]
    rank1 = jnp.sum(oh1 * before, axis=0, keepdims=True)
    rank2 = jnp.sum(oh2 * before, axis=0, keepdims=True)
    cnt_acc[...] = cnt_acc[...] + jnp.sum(ohs, axis=1, keepdims=True)
    cnt_ref[...] = cnt_acc[...]

    vals = [(i1 - EXPERT_ROW0).astype(F32), (i2 - EXPERT_ROW0).astype(F32), rank1, rank2, wt1, wt2]
    out_row = lax.broadcasted_iota(jnp.int32, (ROUTE_ROWS, tm), 0)
    slab = jnp.zeros((ROUTE_ROWS, tm), F32)
    for j, v in enumerate(vals):
        slab = jnp.where(out_row == j, v, slab)
    route_ref[...] = slab


def _mix_route(attn, lru, gl, x, mod3, wa, wl, wo, g2, wrt, brt, b0, nb, *, tm=512):
    _, s, d = x.shape
    spt = s // tm
    tok_in = lambda w: pl.BlockSpec((1, tm, w), lambda bi, i: (b0 + bi, i, 0))
    return pl.pallas_call(
        _mix_kernel,
        grid=(nb, spt),
        in_specs=[tok_in(attn.shape[-1]), tok_in(d), tok_in(2 * d), tok_in(d),
                  pl.BlockSpec((1, 1, mod3.shape[-1]), lambda bi, i: (b0 + bi, 0, 0)),
                  _resident(wa.shape), _resident(wl.shape), _resident(wo.shape),
                  pl.BlockSpec((1, d), lambda bi, i: (0, 0)),
                  _resident(wrt.shape),
                  pl.BlockSpec((ROUTER_ROWS, 1), lambda bi, i: (0, 0))],
        out_specs=[pl.BlockSpec((1, tm, d), lambda bi, i: (bi, i, 0)),
                   pl.BlockSpec((tm * ROW_SUBLANES, LANE), lambda bi, i: (bi * spt + i, 0)),
                   pl.BlockSpec((ROUTE_ROWS, tm), lambda bi, i: (0, bi * spt + i)),
                   pl.BlockSpec((ROUTER_ROWS, 1), lambda bi, i: (0, 0))],
        out_shape=[jax.ShapeDtypeStruct((nb, s, d), BF16),
                   jax.ShapeDtypeStruct((nb * s * ROW_SUBLANES, LANE), jnp.uint32),
                   jax.ShapeDtypeStruct((ROUTE_ROWS, nb * s), F32),
                   jax.ShapeDtypeStruct((ROUTER_ROWS, 1), F32)],
        scratch_shapes=[pltpu.VMEM((ROUTER_ROWS, 1), F32)],
        compiler_params=_cparams(("arbitrary", "arbitrary"), VMEM_LIMIT),
        name="mix_route",
    )(attn, lru, gl, x, mod3, wa, wl, wo, g2, wrt, brt)


TOP_K = 2
EXPERT_BLOCK = 512
MOE_BATCH_RANGES = 2


def _expert_kernel(be_ref, nu_ref, x_ref, w1_ref, w3_ref, w2_ref, y_ref, wb1, wb3, wb2):
    j = pl.program_id(0)

    @pl.when(j < nu_ref[0])
    def _():
        @pl.when((j == 0) | (be_ref[j] != be_ref[jnp.maximum(j - 1, 0)]))
        def _():
            wb1[...] = w1_ref[0].astype(BF16)
            wb3[...] = w3_ref[0].astype(BF16)
            wb2[...] = w2_ref[0].astype(BF16)

        xb = _load_tile_rows(x_ref).astype(BF16)
        a = jnp.dot(xb, wb1[...], preferred_element_type=F32)
        g = jnp.dot(xb, wb3[...], preferred_element_type=F32)
        hm = (a * jax.nn.sigmoid(a) * g).astype(BF16)
        _store_tile_rows(y_ref, jnp.dot(hm, wb2[...], preferred_element_type=F32))

    @pl.when(j >= nu_ref[0])
    def _():
        y_ref[...] = jnp.zeros_like(y_ref)


def _experts(xp, blk_e, n_used, w1, w3, w2):
    ne, d, de = w1.shape
    nb = xp.shape[0] // (EXPERT_BLOCK * ROW_SUBLANES)
    rows = (EXPERT_BLOCK * ROW_SUBLANES, LANE)
    last = lambda j, nu: jnp.minimum(j, nu[0] - 1)
    grid_spec = pltpu.PrefetchScalarGridSpec(
        num_scalar_prefetch=2,
        grid=(nb,),
        in_specs=[pl.BlockSpec(rows, lambda j, be, nu: (last(j, nu), 0)),
                  pl.BlockSpec((1, d, de), lambda j, be, nu: (be[j], 0, 0)),
                  pl.BlockSpec((1, d, de), lambda j, be, nu: (be[j], 0, 0)),
                  pl.BlockSpec((1, de, d), lambda j, be, nu: (be[j], 0, 0))],
        out_specs=pl.BlockSpec(rows, lambda j, be, nu: (j, 0)),
        scratch_shapes=[pltpu.VMEM((d, de), BF16), pltpu.VMEM((d, de), BF16), pltpu.VMEM((de, d), BF16)])
    return pl.pallas_call(
        _expert_kernel,
        grid_spec=grid_spec,
        out_shape=jax.ShapeDtypeStruct(xp.shape, xp.dtype),
        compiler_params=_cparams(("arbitrary",), VMEM_LIMIT),
        name="experts",
    )(blk_e, n_used, xp, w1, w3, w2)


def _combine_kernel(y0_ref, y1_ref, route_ref, x1_ref, mod_ref, gf_ref, *rest):
    o_ref = rest[-1]
    tm, d = x1_ref.shape[1], x1_ref.shape[2]
    route = jnp.concatenate([route_ref[...], jnp.zeros((LANE - ROUTE_ROWS, tm), F32)], axis=0).T
    moe = _load_tile_rows(y0_ref) * route[:, 4:5] + _load_tile_rows(y1_ref) * route[:, 5:6]
    gate2 = mod_ref[0][:, 5 * d:6 * d]
    xo = x1_ref[0].astype(F32) + (1.0 + gate2) * moe
    ms = jnp.mean(xo * xo, axis=-1, keepdims=True)
    o_ref[0] = xo * lax.rsqrt(ms + EPS) * gf_ref[...]


def _combine(yg, route, x1, mod3, gf, b0, out_prev, *, tm=256):
    nb, s, d = x1.shape
    b_all = mod3.shape[0]
    spt = s // tm
    nt = nb * spt
    rows = (tm * ROW_SUBLANES, LANE)
    in_specs = [pl.BlockSpec(rows, lambda bi, i: (bi * spt + i, 0)),
                pl.BlockSpec(rows, lambda bi, i: (nt + bi * spt + i, 0)),
                pl.BlockSpec((ROUTE_ROWS, tm), lambda bi, i: (0, bi * spt + i)),
                pl.BlockSpec((1, tm, d), lambda bi, i: (bi, i, 0)),
                pl.BlockSpec((1, 1, mod3.shape[-1]), lambda bi, i: (b0 + bi, 0, 0)),
                pl.BlockSpec((1, d), lambda bi, i: (0, 0))]
    args = [yg, yg, route, x1, mod3, gf]
    aliases = {}
    if out_prev is not None:
        in_specs.append(pl.BlockSpec(memory_space=pl.ANY))
        aliases = {len(args): 0}
        args.append(out_prev)
    return pl.pallas_call(
        _combine_kernel,
        grid=(nb, spt),
        in_specs=in_specs,
        out_specs=pl.BlockSpec((1, tm, d), lambda bi, i: (b0 + bi, i, 0)),
        out_shape=jax.ShapeDtypeStruct((b_all, s, d), F32),
        input_output_aliases=aliases,
        compiler_params=_cparams(("parallel", "parallel"), VMEM_LIMIT),
        name="combine",
    )(*args)


def _slot_plan(route, counts, n_tok):
    sizes = counts[EXPERT_ROW0:EXPERT_ROW0 + N_EXPERTS, 0].astype(jnp.int32)
    padded = (sizes + EXPERT_BLOCK - 1) // EXPERT_BLOCK * EXPERT_BLOCK
    pad_ends = jnp.cumsum(padded)
    pad_starts = pad_ends - padded
    eid = route[0:TOP_K].astype(jnp.int32)
    rank = route[TOP_K:2 * TOP_K].astype(jnp.int32)
    start = jnp.sum(jnp.where(eid[..., None] == jnp.arange(N_EXPERTS), pad_starts, 0), axis=-1)
    dest = (start + rank).reshape(TOP_K * n_tok)
    n_blocks = (n_tok * TOP_K + N_EXPERTS * (EXPERT_BLOCK - 1) + EXPERT_BLOCK - 1) // EXPERT_BLOCK
    gran = SC_CORES * SC_SUBCORES * SC_CHUNK // math.gcd(SC_CORES * SC_SUBCORES * SC_CHUNK, EXPERT_BLOCK)
    n_blocks = (n_blocks + gran - 1) // gran * gran
    n_used = pad_ends[-1] // EXPERT_BLOCK
    blk = jnp.minimum(jnp.arange(n_blocks), n_used - 1)
    blk_e = jnp.minimum(jnp.sum(pad_ends[None, :] <= (blk * EXPERT_BLOCK)[:, None], axis=1), N_EXPERTS - 1)
    return dest, n_blocks * EXPERT_BLOCK, blk_e.astype(jnp.int32), n_used.reshape(1).astype(jnp.int32)


def kernel(x, c, w_mod, b_mod, norm1_g, w_in, conv_w, conv_b, lru_wx, lru_bx, lru_wa, lru_ba, lru_lambda, w_attn_o, w_lru_o, w_out, norm2_g, w_grp, b_grp, w_exp, b_exp, w1, w3, w2, norm_f_g):
    b, s, d = x.shape
    assert d == D_MODEL and s == SPAN * DILATIONS[-1] and w_mod.shape[0] == 1
    mod3 = _modulation(c, w_mod[0], b_mod[0]).reshape(b, 1, 6 * d)
    qkv0, qkv1, qkv2, xc, gy, gl = _projection(x, mod3, norm1_g[0].reshape(1, d), _prep_w_in(w_in[0]),
                                               conv_w[0], conv_b[0])
    attn = _attention((qkv0, qkv1, qkv2), b, s)
    lru = _lru_branch(xc, gy, lru_wx[0], lru_bx[0], lru_wa[0], lru_ba[0], lru_lambda[0])

    n_pad = ROUTER_ROWS - N_GROUPS - N_EXPERTS
    wr = jnp.pad(jnp.concatenate([w_grp[0], w_exp[0]], axis=1).T, ((0, n_pad), (0, 0))).astype(BF16)
    br = jnp.pad(jnp.concatenate([b_grp[0], b_exp[0]]), (0, n_pad)).reshape(ROUTER_ROWS, 1)
    wa, wl, wo = w_attn_o[0].astype(BF16), w_lru_o[0].astype(BF16), w_out[0].astype(BF16)
    as_rows = lambda a: a.reshape(-1, ROW_SUBLANES, LANE)
    as_tiles = lambda a: a.reshape(-1, LANE)

    out = None
    nb = b // MOE_BATCH_RANGES
    for b0 in range(0, b, nb):
        x1, h2, route, counts = _mix_route(attn, lru, gl, x, mod3, wa, wl, wo, norm2_g[0].reshape(1, d),
                                           wr, br, b0, nb)
        dest, n_slots, blk_e, n_used = _slot_plan(route, counts, nb * s)
        xp = as_tiles(_sc_scatter_rows(as_rows(h2), dest, n_slots))
        yp = _experts(xp, blk_e, n_used, w1[0], w3[0], w2[0])
        yg = as_tiles(_sc_gather_rows(as_rows(yp), dest))
        out = _combine(yg, route, x1, mod3, norm_f_g.reshape(1, d), b0, out)
    return out
```

```python
import functools
import math

import jax
import jax.numpy as jnp
from jax import lax
from jax.experimental import pallas as pl
from jax.experimental.pallas import tpu as pltpu
from jax.experimental.pallas import tpu_sc as plsc

F32 = jnp.float32
BF16 = jnp.bfloat16

D_MODEL = 1024
HEAD_DIM = 64
N_SLOTS = 8
SPAN = 128
DILATIONS = (1, 4, 16)
GROUP_COLS = 3 * N_SLOTS * HEAD_DIM
ATTN_WIDTH = len(DILATIONS) * N_SLOTS * HEAD_DIM
ATTN_OUT = N_SLOTS * HEAD_DIM
LRU_WIDTH = D_MODEL
LRU_BLOCK_DIM = 64
CONV_WIDTH = 4
CONV_TAIL = 8
LRU_C = 8.0
N_GROUPS = 4
EXPERTS_PER_GROUP = 8
N_EXPERTS = N_GROUPS * EXPERTS_PER_GROUP
D_EXPERT = D_MODEL // 2
EPS = 1e-6
LANE = 128
VMEM_LIMIT = 56 * 1024 * 1024


def _cparams(sem, vmem=None):
    return pltpu.CompilerParams(dimension_semantics=sem, vmem_limit_bytes=vmem)


def _resident(shape):
    nd = len(shape)
    return pl.BlockSpec(shape, lambda *_: (0,) * nd, pipeline_mode=pl.Buffered(1))


def _mod_kernel(c_ref, w_ref, b_ref, o_ref):
    c = c_ref[...]
    ca = c * jax.nn.sigmoid(c)
    o_ref[...] = jnp.dot(ca.astype(BF16), w_ref[...].astype(BF16),
                         preferred_element_type=F32) + b_ref[...]


def _modulation(c, w_mod, b_mod):
    b, d = c.shape
    n = w_mod.shape[1]
    tn = n // 4
    return pl.pallas_call(
        _mod_kernel,
        grid=(n // tn,),
        in_specs=[pl.BlockSpec((b, d), lambda j: (0, 0)),
                  pl.BlockSpec((d, tn), lambda j: (0, j)),
                  pl.BlockSpec((1, tn), lambda j: (0, j))],
        out_specs=pl.BlockSpec((b, tn), lambda j: (0, j)),
        out_shape=jax.ShapeDtypeStruct((b, n), F32),
        compiler_params=_cparams(("arbitrary",)),
        name="modulation",
    )(c, w_mod, b_mod.reshape(1, n))


def _rms_mod(x, g, scale, shift):
    ms = jnp.mean(x * x, axis=-1, keepdims=True)
    return x * lax.rsqrt(ms + EPS) * g * (1.0 + scale) + shift


def _gelu_tanh(y):
    return y * (0.5 * (1.0 + jnp.tanh(0.7978845608028654 * (y + 0.044715 * (y * y * y)))))


def _proj_kernel(x_ref, mod_ref, g_ref, w_ref, cw_ref, cb_ref, qkv0_ref, qkv1_ref, qkv2_ref,
                 xc_ref, gy_ref, gl_ref, hs_ref, xe_ref, *, tm):
    @pl.when(pl.program_id(1) == 0)
    def _():
        xe_ref[0:CONV_TAIL, :] = jnp.zeros((CONV_TAIL, LRU_WIDTH), F32)

    @pl.when(pl.program_id(1) > 0)
    def _():
        xe_ref[0:CONV_TAIL, :] = xe_ref[tm:tm + CONV_TAIL, :]

    d_model = x_ref.shape[-1]
    m = mod_ref[0]
    h = _rms_mod(x_ref[0], g_ref[...], m[:, d_model:2 * d_model], m[:, 0:d_model])

    def mm(hv, lo, hi):
        return jnp.dot(hv, w_ref[:, lo:hi], preferred_element_type=F32)

    hb = h.astype(BF16)
    c0 = len(DILATIONS) * GROUP_COLS
    qkv0_ref[0] = mm(hb, 0, GROUP_COLS).astype(BF16)
    xr = mm(hb, c0, c0 + LRU_WIDTH)
    xe_ref[CONV_TAIL:, :] = xr
    cw = cw_ref[...]
    xc = xr * cw[CONV_WIDTH - 1:CONV_WIDTH] + cb_ref[...]
    for k in range(1, CONV_WIDTH):
        xc = xc + xe_ref[CONV_TAIL - k:CONV_TAIL - k + tm, :] * cw[CONV_WIDTH - 1 - k:CONV_WIDTH - k]
    xc_ref[0] = xc.astype(BF16)
    gy_ref[0] = _gelu_tanh(mm(hb, c0 + LRU_WIDTH, c0 + 2 * LRU_WIDTH)).astype(BF16)
    gl_ref[0] = mm(hb, c0 + 2 * LRU_WIDTH, c0 + 2 * LRU_WIDTH + 2 * d_model).astype(BF16)

    n_slab = d_model // LANE
    for j in range(n_slab):
        hs_ref[j] = h[:, j * LANE:(j + 1) * LANE]
    for g, out_ref in ((1, qkv1_ref), (2, qkv2_ref)):
        d = DILATIONS[g]
        rows = tm // d
        hp = jnp.concatenate(
            [jnp.concatenate([hs_ref[j, pl.ds(p, rows, stride=d), :] for j in range(n_slab)], axis=1)
             for p in range(d)], axis=0).astype(BF16)
        res = mm(hp, g * GROUP_COLS, (g + 1) * GROUP_COLS).astype(BF16)
        for p in range(d):
            out_ref[p] = res[p * rows:(p + 1) * rows]


def _projection(x, mod3, g1, w_r, conv_w, conv_b, *, tm=256):
    b, s, d = x.shape
    n = w_r.shape[1]
    assert s % tm == 0 and tm % (16 * DILATIONS[-1]) == 0 and CONV_TAIL >= CONV_WIDTH - 1
    out_shape = [jax.ShapeDtypeStruct((b * dd, s // dd, GROUP_COLS), BF16) for dd in DILATIONS]
    out_shape += [jax.ShapeDtypeStruct((b, s, LRU_WIDTH), BF16),
                  jax.ShapeDtypeStruct((b, s, LRU_WIDTH), BF16),
                  jax.ShapeDtypeStruct((b, s, 2 * d), BF16)]
    out_specs = [pl.BlockSpec((dd, tm // dd, GROUP_COLS), lambda bi, i: (bi, i, 0)) for dd in DILATIONS]
    out_specs += [pl.BlockSpec((1, tm, LRU_WIDTH), lambda bi, i: (bi, i, 0)),
                  pl.BlockSpec((1, tm, LRU_WIDTH), lambda bi, i: (bi, i, 0)),
                  pl.BlockSpec((1, tm, 2 * d), lambda bi, i: (bi, i, 0))]
    return pl.pallas_call(
        functools.partial(_proj_kernel, tm=tm),
        grid=(b, s // tm),
        in_specs=[pl.BlockSpec((1, tm, d), lambda bi, i: (bi, i, 0)),
                  pl.BlockSpec((1, 1, mod3.shape[-1]), lambda bi, i: (bi, 0, 0)),
                  pl.BlockSpec((1, d), lambda bi, i: (0, 0)),
                  _resident((d, n)),
                  pl.BlockSpec((CONV_WIDTH, LRU_WIDTH), lambda bi, i: (0, 0)),
                  pl.BlockSpec((1, LRU_WIDTH), lambda bi, i: (0, 0))],
        out_specs=out_specs,
        out_shape=out_shape,
        scratch_shapes=[pltpu.VMEM((d // LANE, tm, LANE), F32), pltpu.VMEM((CONV_TAIL + tm, LRU_WIDTH), F32)],
        compiler_params=_cparams(("parallel", "arbitrary"), VMEM_LIMIT),
        name="projection",
    )(x, mod3, g1, w_r, conv_w, conv_b.reshape(1, LRU_WIDTH))


def _attn_kernel(q0, k0, v0, q1, k1, v1, q2, k2, v2, o_ref, acc_ref, lse_ref, bias_ref, *, seq):
    hcols = o_ref.shape[-1]
    n_head = hcols // HEAD_DIM
    head_of_lane = lax.broadcasted_iota(jnp.int32, (SPAN, hcols), 1) // HEAD_DIM
    head_mask_b = [jnp.where(head_of_lane == h, 1.0, 0.0).astype(BF16) for h in range(n_head)]

    def by_head(parts):
        out = parts[n_head - 1]
        for h in range(n_head - 2, -1, -1):
            out = jnp.where(head_of_lane == h, parts[h], out)
        return out

    qi = lax.broadcasted_iota(jnp.int32, (n_head * SPAN, 2 * SPAN), 0) % SPAN
    ki = lax.broadcasted_iota(jnp.int32, (n_head * SPAN, 2 * SPAN), 1)
    band = (ki >= qi) & (ki <= qi + SPAN)
    bias_ref[0] = jnp.where(band, 0.0, -jnp.inf)
    bias_ref[1] = jnp.where(band & (ki >= SPAN), 0.0, -jnp.inf)

    for g, (q_ref, k_ref, v_ref) in enumerate(((q0, k0, v0), (q1, k1, v1), (q2, k2, v2))):
        d = DILATIONS[g]
        n_blk = seq // d // SPAN

        def tile(n, carry, q_ref=q_ref, k_ref=k_ref, v_ref=v_ref, d=d, n_blk=n_blk, g=g):
            p = n // n_blk
            blk = n % n_blk
            r0 = pl.multiple_of(blk * SPAN, SPAN)
            rp = pl.multiple_of(jnp.maximum(blk - 1, 0) * SPAN, SPAN)
            q = q_ref[p, pl.ds(r0, SPAN), :]
            kk = jnp.concatenate([k_ref[p, pl.ds(rp, SPAN), :], k_ref[p, pl.ds(r0, SPAN), :]], axis=0)
            vv = jnp.concatenate([v_ref[p, pl.ds(rp, SPAN), :], v_ref[p, pl.ds(r0, SPAN), :]], axis=0)
            qs = jnp.concatenate([q * head_mask_b[h] for h in range(n_head)], axis=0)
            sc = lax.dot_general(qs, kk, (((1,), (1,)), ((), ())), preferred_element_type=F32)
            sc = sc + bias_ref[jnp.where(blk > 0, 0, 1)]
            mx = jnp.max(sc, axis=-1, keepdims=True)
            e = jnp.exp(sc - mx)
            den = jnp.sum(e, axis=-1, keepdims=True)
            pv = jnp.dot(e.astype(BF16), vv, preferred_element_type=F32)
            lse = mx + jnp.log(den)
            rows_of = lambda a: [a[h * SPAN:(h + 1) * SPAN] for h in range(n_head)]
            o = by_head(rows_of(pv)) / by_head(rows_of(den))
            l = by_head(rows_of(lse))
            start = p + d * r0
            for j in range(hcols // LANE):
                rows = pl.ds(start, SPAN, stride=d) if d > 1 else pl.ds(start, SPAN)
                acc_ref[g, j, rows, :] = o[:, j * LANE:(j + 1) * LANE]
                lse_ref[g, j, rows, :] = l[:, j * LANE:(j + 1) * LANE]
            return carry

        lax.fori_loop(0, seq // SPAN, tile, 0, unroll=8)

    chunk = 256

    def combine(c, carry):
        r = pl.multiple_of(c * chunk, chunk)
        for j in range(hcols // LANE):
            ls = [lse_ref[g, j, pl.ds(r, chunk), :] for g in range(len(DILATIONS))]
            mx = jnp.maximum(jnp.maximum(ls[0], ls[1]), ls[2])
            ws = [jnp.exp(v - mx) for v in ls]
            num = ws[0] * acc_ref[0, j, pl.ds(r, chunk), :]
            for g in range(1, len(DILATIONS)):
                num = num + ws[g] * acc_ref[g, j, pl.ds(r, chunk), :]
            o_ref[0, pl.ds(r, chunk), j * LANE:(j + 1) * LANE] = (num / (ws[0] + ws[1] + ws[2])).astype(BF16)
        return carry

    lax.fori_loop(0, seq // chunk, combine, 0)


def _attention(qkvs, b, s):
    hcols = 4 * HEAD_DIM
    n_hg = ATTN_OUT // hcols
    ncb = ATTN_OUT // hcols
    in_specs, args = [], []
    for g, d in enumerate(DILATIONS):
        for part in range(3):
            in_specs.append(pl.BlockSpec((d, s // d, hcols),
                                         lambda bi, hg, part=part: (bi, 0, part * ncb + hg)))
            args.append(qkvs[g])
    return pl.pallas_call(
        functools.partial(_attn_kernel, seq=s),
        grid=(b, n_hg),
        in_specs=in_specs,
        out_specs=pl.BlockSpec((1, s, hcols), lambda bi, hg: (bi, 0, hg)),
        out_shape=jax.ShapeDtypeStruct((b, s, ATTN_OUT), BF16),
        scratch_shapes=[pltpu.VMEM((len(DILATIONS), hcols // LANE, s, LANE), F32),
                        pltpu.VMEM((len(DILATIONS), hcols // LANE, s, LANE), F32),
                        pltpu.VMEM((2, (hcols // HEAD_DIM) * SPAN, 2 * SPAN), F32)],
        compiler_params=_cparams(("parallel", "parallel"), VMEM_LIMIT),
        name="dilated_attention",
    )(*args)


def _lru_kernel(xc_ref, gy_ref, wg_ref, bx_ref, ba_ref, lam_ref, o_ref, a_ref, b_ref, h_ref, *, pitch):
    nb, ts, tc = xc_ref.shape
    nl = tc // LANE

    @pl.when(pl.program_id(1) == 0)
    def _():
        h_ref[...] = jnp.zeros_like(h_ref)

    xb = xc_ref[...].reshape(nb * ts, tc)
    xc = xb.astype(F32)
    gates = jnp.dot(xb, wg_ref[0], preferred_element_type=F32)
    gate_i = jax.nn.sigmoid(gates[:, :tc] + bx_ref[...])
    gate_r = jax.nn.sigmoid(gates[:, tc:] + ba_ref[...])
    neg_lam = -lam_ref[...]
    softplus = jnp.maximum(neg_lam, 0.0) + jnp.log1p(jnp.exp(-jnp.abs(neg_lam)))
    log_a = (-LRU_C) * gate_r * softplus
    a = jnp.exp(log_a)
    bv = jnp.sqrt(jnp.tanh(-log_a) * (1.0 + a * a)) * gate_i * xc
    for bi in range(nb):
        for j in range(nl):
            a_ref[j, pl.ds(bi * pitch, ts), :] = a[bi * ts:(bi + 1) * ts, j * LANE:(j + 1) * LANE]
            b_ref[j, pl.ds(bi * pitch, ts), :] = bv[bi * ts:(bi + 1) * ts, j * LANE:(j + 1) * LANE]

    def step(t, hs):
        out = []
        for j in range(nl):
            rows = pl.ds(t, nb, stride=pitch)
            h = a_ref[j, rows, :] * hs[j] + b_ref[j, rows, :]
            b_ref[j, rows, :] = h
            out.append(h)
        return tuple(out)

    hs = lax.fori_loop(0, ts, step, tuple(h_ref[j] for j in range(nl)), unroll=8)
    for j in range(nl):
        h_ref[j] = hs[j]
    for bi in range(nb):
        h = jnp.concatenate([b_ref[j, pl.ds(bi * pitch, ts), :] for j in range(nl)], axis=1)
        o_ref[bi] = (h * gy_ref[bi].astype(F32)).astype(BF16)


def _lru_gate_weights(wx, wa, tc):
    nb, bd, _ = wx.shape
    per = tc // bd
    eye = jnp.eye(per, dtype=wx.dtype)

    def bdiag(w):
        w = w.reshape(nb // per, per, bd, bd)
        return jnp.einsum('cpio,pq->cpiqo', w, eye).reshape(nb // per, tc, tc)

    return jnp.concatenate([bdiag(wx), bdiag(wa)], axis=-1).astype(BF16)


def _lru_branch(xc, gy, wx, bx, wa, ba, lam, *, tc=256, ts=128):
    b, s, c = xc.shape
    assert s % ts == 0 and c % tc == 0
    wg = _lru_gate_weights(wx, wa, tc)
    row = lambda v: v.reshape(1, c)
    tile = pl.BlockSpec((b, ts, tc), lambda ci, ti: (0, ti, ci))
    vec = pl.BlockSpec((1, tc), lambda ci, ti: (0, ci))
    pitch = ts + 8
    return pl.pallas_call(
        functools.partial(_lru_kernel, pitch=pitch),
        grid=(c // tc, s // ts),
        in_specs=[tile, tile,
                  pl.BlockSpec((1, tc, 2 * tc), lambda ci, ti: (ci, 0, 0)),
                  vec, vec, vec],
        out_specs=tile,
        out_shape=jax.ShapeDtypeStruct((b, s, c), BF16),
        scratch_shapes=[pltpu.VMEM((tc // LANE, b * pitch, LANE), F32),
                        pltpu.VMEM((tc // LANE, b * pitch, LANE), F32),
                        pltpu.VMEM((tc // LANE, b, LANE), F32)],
        compiler_params=_cparams(("parallel", "arbitrary"), VMEM_LIMIT),
        name="rg_lru",
    )(xc, gy, wg, row(bx), row(ba), row(lam))


def _prep_w_in(w_in):
    a = ATTN_WIDTH
    gw = N_SLOTS * HEAD_DIM
    q = w_in[:, :a] * (HEAD_DIM ** -0.5)
    k = w_in[:, a:2 * a]
    v = w_in[:, 2 * a:3 * a]
    parts = []
    for g in range(len(DILATIONS)):
        sl = slice(g * gw, (g + 1) * gw)
        parts += [q[:, sl], k[:, sl], v[:, sl]]
    parts.append(w_in[:, 3 * a:])
    return jnp.concatenate(parts, axis=1).astype(BF16)


ROW_SUBLANES = D_MODEL // 2 // LANE


def _store_tile_rows(ref, v, row0=0):
    n, half = v.shape[0], v.shape[1] // 2
    lo = pltpu.bitcast(v[:, :half].astype(BF16).astype(F32), jnp.uint32)
    hi = pltpu.bitcast(v[:, half:].astype(BF16).astype(F32), jnp.uint32)
    words = (hi & jnp.uint32(0xFFFF0000)) | (lo >> 16)
    for j in range(ROW_SUBLANES):
        ref[pl.ds(row0 * ROW_SUBLANES + j, n, stride=ROW_SUBLANES), :] = words[:, j * LANE:(j + 1) * LANE]


def _load_tile_rows(ref, n=None):
    n = ref.shape[0] // ROW_SUBLANES if n is None else n
    words = [ref[pl.ds(j, n, stride=ROW_SUBLANES), :] for j in range(ROW_SUBLANES)]
    lo = [pltpu.bitcast(w << 16, F32) for w in words]
    hi = [pltpu.bitcast(w & jnp.uint32(0xFFFF0000), F32) for w in words]
    return jnp.concatenate(lo + hi, axis=-1)


SC_CORES, SC_SUBCORES = 2, 16
SC_CHUNK = 128


def _sc_gather_rows(table, idx):
    n = idx.shape[0]
    per_worker = n // (SC_CORES * SC_SUBCORES)
    n_chunks = per_worker // SC_CHUNK
    assert n_chunks * SC_CHUNK * SC_CORES * SC_SUBCORES == n
    mesh = plsc.VectorSubcoreMesh(core_axis_name="c", subcore_axis_name="s")

    def body(table_hbm, idx_hbm, out_hbm, idx_v, rows_v, sem):
        base = (lax.axis_index("s") * SC_CORES + lax.axis_index("c")) * per_worker

        @pl.loop(0, n_chunks)
        def _(i):
            off = pl.multiple_of(base + i * SC_CHUNK, SC_CHUNK)
            pltpu.sync_copy(idx_hbm.at[pl.ds(off, SC_CHUNK)], idx_v)
            pltpu.async_copy(table_hbm.at[idx_v], rows_v, sem).wait()
            pltpu.sync_copy(rows_v, out_hbm.at[pl.ds(off, SC_CHUNK)])

    return pl.kernel(
        body, mesh=mesh,
        out_type=jax.ShapeDtypeStruct((n,) + table.shape[1:], table.dtype),
        scratch_types=[pltpu.VMEM((SC_CHUNK,), jnp.int32),
                       pltpu.VMEM((SC_CHUNK,) + table.shape[1:], table.dtype),
                       pltpu.SemaphoreType.DMA],
        name="sc_gather_rows",
    )(table, idx)


def _sc_scatter_rows(rows, idx, n_out):
    n_rows = rows.shape[0]
    n_choice = idx.shape[0] // n_rows
    per_worker = n_rows // (SC_CORES * SC_SUBCORES)
    n_chunks = per_worker // SC_CHUNK
    assert n_chunks * SC_CHUNK * SC_CORES * SC_SUBCORES == n_rows and n_choice * n_rows == idx.shape[0]
    mesh = plsc.VectorSubcoreMesh(core_axis_name="c", subcore_axis_name="s")

    def body(rows_hbm, idx_hbm, out_hbm, idx_v, rows_v):
        base = (lax.axis_index("s") * SC_CORES + lax.axis_index("c")) * per_worker

        @pl.loop(0, n_chunks)
        def _(i):
            off = pl.multiple_of(base + i * SC_CHUNK, SC_CHUNK)
            pltpu.sync_copy(rows_hbm.at[pl.ds(off, SC_CHUNK)], rows_v)
            for k in range(n_choice):
                pltpu.sync_copy(idx_hbm.at[pl.ds(k * n_rows + off, SC_CHUNK)], idx_v)
                pltpu.sync_copy(rows_v, out_hbm.at[idx_v])

    return pl.kernel(
        body, mesh=mesh,
        out_type=jax.ShapeDtypeStruct((n_out,) + rows.shape[1:], rows.dtype),
        scratch_types=[pltpu.VMEM((SC_CHUNK,), jnp.int32),
                       pltpu.VMEM((SC_CHUNK,) + rows.shape[1:], rows.dtype)],
        name="sc_scatter_rows",
    )(rows, idx)


ROUTE_ROWS = 8
EXPERT_ROW0 = N_GROUPS
ROUTER_ROWS = 48


def _mix_kernel(attn_ref, lru_ref, gl_ref, x_ref, mod_ref, wa_ref, wl_ref, wo_ref, g2_ref,
                wrt_ref, brt_ref, x1_ref, h2_ref, route_ref, cnt_ref, cnt_acc):
    d = x_ref.shape[-1]
    tm = x_ref.shape[1]

    @pl.when((pl.program_id(0) == 0) & (pl.program_id(1) == 0))
    def _():
        cnt_acc[...] = jnp.zeros_like(cnt_acc)

    m = mod_ref[0]
    gate1, shift2, scale2 = m[:, 2 * d:3 * d], m[:, 3 * d:4 * d], m[:, 4 * d:5 * d]
    ya = jnp.dot(attn_ref[0], wa_ref[...], preferred_element_type=F32)
    yl = jnp.dot(lru_ref[0], wl_ref[...], preferred_element_type=F32)
    gates = jax.nn.sigmoid(gl_ref[0].astype(F32))
    mixed = gates[:, :d] * ya + gates[:, d:] * yl
    y = jnp.dot(mixed.astype(BF16), wo_ref[...], preferred_element_type=F32)
    x1 = x_ref[0] + (1.0 + gate1) * y
    x1_ref[0] = x1.astype(BF16)
    h2 = _rms_mod(x1, g2_ref[...], scale2, shift2)
    _store_tile_rows(h2_ref, h2)
    logits = lax.dot_general(wrt_ref[...], h2.astype(BF16), (((1,), (1,)), ((), ())),
                             preferred_element_type=F32) + brt_ref[...]

    row = lax.broadcasted_iota(jnp.int32, logits.shape, 0)
    neg = -jnp.inf

    def top(vals):
        mx = jnp.max(vals, axis=0, keepdims=True)
        idx = jnp.min(jnp.where(vals == mx, row, ROUTER_ROWS), axis=0, keepdims=True)
        return mx, idx

    is_grp = row < N_GROUPS
    gmax, gidx = top(jnp.where(is_grp, logits, neg))
    grp_gate = 1.0 / jnp.sum(jnp.where(is_grp, jnp.exp(logits - gmax), 0.0), axis=0, keepdims=True)
    lo = EXPERT_ROW0 + EXPERTS_PER_GROUP * gidx
    el = jnp.where((row >= lo) & (row < lo + EXPERTS_PER_GROUP), logits, neg)
    v1, i1 = top(el)
    v2, i2 = top(jnp.where(row == i1, neg, el))
    e21 = jnp.exp(v2 - v1)
    wt1 = grp_gate / (1.0 + e21)
    wt2 = wt1 * e21

    oh1 = jnp.where(row == i1, 1.0, 0.0)
    oh2 = jnp.where(row == i2, 1.0, 0.0)
    ohs = oh1 + oh2
    rr = lax.broadcasted_iota(jnp.int32, (tm, tm), 0)
    cc = lax.broadcasted_iota(jnp.int32, (tm, tm), 1)
    earlier = jnp.where(rr < cc, 1.0, 0.0).astype(BF16)
    before = jnp.dot(ohs.astype(BF16), earlier, preferred_element_type=F32) + cnt_acc[...]
    rank1 = jnp.sum(oh1 * before, axis=0, keepdims=True)
    rank2 = jnp.sum(oh2 * before, axis=0, keepdims=True)
    cnt_acc[...] = cnt_acc[...] + jnp.sum(ohs, axis=1, keepdims=True)
    cnt_ref[...] = cnt_acc[...]

    vals = [(i1 - EXPERT_ROW0).astype(F32), (i2 - EXPERT_ROW0).astype(F32), rank1, rank2, wt1, wt2]
    out_row = lax.broadcasted_iota(jnp.int32, (ROUTE_ROWS, tm), 0)
    slab = jnp.zeros((ROUTE_ROWS, tm), F32)
    for j, v in enumerate(vals):
        slab = jnp.where(out_row == j, v, slab)
    route_ref[...] = slab


def _mix_route(attn, lru, gl, x, mod3, wa, wl, wo, g2, wrt, brt, b0, nb, *, tm=512):
    _, s, d = x.shape
    spt = s // tm
    tok_in = lambda w: pl.BlockSpec((1, tm, w), lambda bi, i: (b0 + bi, i, 0))
    return pl.pallas_call(
        _mix_kernel,
        grid=(nb, spt),
        in_specs=[tok_in(attn.shape[-1]), tok_in(d), tok_in(2 * d), tok_in(d),
                  pl.BlockSpec((1, 1, mod3.shape[-1]), lambda bi, i: (b0 + bi, 0, 0)),
                  _resident(wa.shape), _resident(wl.shape), _resident(wo.shape),
                  pl.BlockSpec((1, d), lambda bi, i: (0, 0)),
                  _resident(wrt.shape),
                  pl.BlockSpec((ROUTER_ROWS, 1), lambda bi, i: (0, 0))],
        out_specs=[pl.BlockSpec((1, tm, d), lambda bi, i: (bi, i, 0)),
                   pl.BlockSpec((tm * ROW_SUBLANES, LANE), lambda bi, i: (bi * spt + i, 0)),
                   pl.BlockSpec((ROUTE_ROWS, tm), lambda bi, i: (0, bi * spt + i)),
                   pl.BlockSpec((ROUTER_ROWS, 1), lambda bi, i: (0, 0))],
        out_shape=[jax.ShapeDtypeStruct((nb, s, d), BF16),
                   jax.ShapeDtypeStruct((nb * s * ROW_SUBLANES, LANE), jnp.uint32),
                   jax.ShapeDtypeStruct((ROUTE_ROWS, nb * s), F32),
                   jax.ShapeDtypeStruct((ROUTER_ROWS, 1), F32)],
        scratch_shapes=[pltpu.VMEM((ROUTER_ROWS, 1), F32)],
        compiler_params=_cparams(("arbitrary", "arbitrary"), VMEM_LIMIT),
        name="mix_route",
    )(attn, lru, gl, x, mod3, wa, wl, wo, g2, wrt, brt)


TOP_K = 2
EXPERT_BLOCK = 512
MOE_BATCH_RANGES = 2


def _expert_kernel(be_ref, bv_ref, nu_ref, x_ref, w1_ref, w3_ref, w2_ref, y_ref, wb1, wb3, wb2):
    j = pl.program_id(0)
    half = EXPERT_BLOCK // 2

    def ffn(n):
        xb = _load_tile_rows(x_ref, n).astype(BF16)
        a = jnp.dot(xb, wb1[...], preferred_element_type=F32)
        g = jnp.dot(xb, wb3[...], preferred_element_type=F32)
        hm = (a * jax.nn.sigmoid(a) * g).astype(BF16)
        _store_tile_rows(y_ref, jnp.dot(hm, wb2[...], preferred_element_type=F32))

    @pl.when(j < nu_ref[0])
    def _():
        @pl.when((j == 0) | (be_ref[j] != be_ref[jnp.maximum(j - 1, 0)]))
        def _():
            wb1[...] = w1_ref[0].astype(BF16)
            wb3[...] = w3_ref[0].astype(BF16)
            wb2[...] = w2_ref[0].astype(BF16)

        @pl.when(bv_ref[j] > half)
        def _():
            ffn(EXPERT_BLOCK)

        @pl.when(bv_ref[j] <= half)
        def _():
            ffn(half)
            y_ref[half * ROW_SUBLANES:, :] = jnp.zeros((half * ROW_SUBLANES, LANE), y_ref.dtype)

    @pl.when(j >= nu_ref[0])
    def _():
        y_ref[...] = jnp.zeros_like(y_ref)


def _experts(xp, blk_e, blk_valid, n_used, w1, w3, w2):
    ne, d, de = w1.shape
    nb = xp.shape[0] // (EXPERT_BLOCK * ROW_SUBLANES)
    rows = (EXPERT_BLOCK * ROW_SUBLANES, LANE)
    last = lambda j, nu: jnp.minimum(j, nu[0] - 1)
    grid_spec = pltpu.PrefetchScalarGridSpec(
        num_scalar_prefetch=3,
        grid=(nb,),
        in_specs=[pl.BlockSpec(rows, lambda j, be, bv, nu: (last(j, nu), 0)),
                  pl.BlockSpec((1, d, de), lambda j, be, bv, nu: (be[j], 0, 0)),
                  pl.BlockSpec((1, d, de), lambda j, be, bv, nu: (be[j], 0, 0)),
                  pl.BlockSpec((1, de, d), lambda j, be, bv, nu: (be[j], 0, 0))],
        out_specs=pl.BlockSpec(rows, lambda j, be, bv, nu: (j, 0)),
        scratch_shapes=[pltpu.VMEM((d, de), BF16), pltpu.VMEM((d, de), BF16), pltpu.VMEM((de, d), BF16)])
    return pl.pallas_call(
        _expert_kernel,
        grid_spec=grid_spec,
        out_shape=jax.ShapeDtypeStruct(xp.shape, xp.dtype),
        compiler_params=_cparams(("arbitrary",), VMEM_LIMIT),
        name="experts",
    )(blk_e, blk_valid, n_used, xp, w1, w3, w2)


def _combine_kernel(y0_ref, y1_ref, route_ref, x1_ref, mod_ref, gf_ref, *rest):
    o_ref = rest[-1]
    tm, d = x1_ref.shape[1], x1_ref.shape[2]
    route = jnp.concatenate([route_ref[...], jnp.zeros((LANE - ROUTE_ROWS, tm), F32)], axis=0).T
    moe = _load_tile_rows(y0_ref) * route[:, 4:5] + _load_tile_rows(y1_ref) * route[:, 5:6]
    gate2 = mod_ref[0][:, 5 * d:6 * d]
    xo = x1_ref[0].astype(F32) + (1.0 + gate2) * moe
    ms = jnp.mean(xo * xo, axis=-1, keepdims=True)
    o_ref[0] = xo * lax.rsqrt(ms + EPS) * gf_ref[...]


def _combine(yg, route, x1, mod3, gf, b0, out_prev, *, tm=256):
    nb, s, d = x1.shape
    b_all = mod3.shape[0]
    spt = s // tm
    nt = nb * spt
    rows = (tm * ROW_SUBLANES, LANE)
    in_specs = [pl.BlockSpec(rows, lambda bi, i: (bi * spt + i, 0)),
                pl.BlockSpec(rows, lambda bi, i: (nt + bi * spt + i, 0)),
                pl.BlockSpec((ROUTE_ROWS, tm), lambda bi, i: (0, bi * spt + i)),
                pl.BlockSpec((1, tm, d), lambda bi, i: (bi, i, 0)),
                pl.BlockSpec((1, 1, mod3.shape[-1]), lambda bi, i: (b0 + bi, 0, 0)),
                pl.BlockSpec((1, d), lambda bi, i: (0, 0))]
    args = [yg, yg, route, x1, mod3, gf]
    aliases = {}
    if out_prev is not None:
        in_specs.append(pl.BlockSpec(memory_space=pl.ANY))
        aliases = {len(args): 0}
        args.append(out_prev)
    return pl.pallas_call(
        _combine_kernel,
        grid=(nb, spt),
        in_specs=in_specs,
        out_specs=pl.BlockSpec((1, tm, d), lambda bi, i: (b0 + bi, i, 0)),
        out_shape=jax.ShapeDtypeStruct((b_all, s, d), F32),
        input_output_aliases=aliases,
        compiler_params=_cparams(("parallel", "parallel"), VMEM_LIMIT),
        name="combine",
    )(*args)


def _slot_plan(route, counts, n_tok):
    sizes = counts[EXPERT_ROW0:EXPERT_ROW0 + N_EXPERTS, 0].astype(jnp.int32)
    padded = (sizes + EXPERT_BLOCK - 1) // EXPERT_BLOCK * EXPERT_BLOCK
    pad_ends = jnp.cumsum(padded)
    pad_starts = pad_ends - padded
    eid = route[0:TOP_K].astype(jnp.int32)
    rank = route[TOP_K:2 * TOP_K].astype(jnp.int32)
    start = jnp.sum(jnp.where(eid[..., None] == jnp.arange(N_EXPERTS), pad_starts, 0), axis=-1)
    dest = (start + rank).reshape(TOP_K * n_tok)
    n_blocks = (n_tok * TOP_K + N_EXPERTS * (EXPERT_BLOCK - 1) + EXPERT_BLOCK - 1) // EXPERT_BLOCK
    gran = SC_CORES * SC_SUBCORES * SC_CHUNK // math.gcd(SC_CORES * SC_SUBCORES * SC_CHUNK, EXPERT_BLOCK)
    n_blocks = (n_blocks + gran - 1) // gran * gran
    n_used = pad_ends[-1] // EXPERT_BLOCK
    blk = jnp.minimum(jnp.arange(n_blocks), n_used - 1)
    blk_e = jnp.minimum(jnp.sum(pad_ends[None, :] <= (blk * EXPERT_BLOCK)[:, None], axis=1), N_EXPERTS - 1)
    blk_valid = jnp.clip(sizes[blk_e] - (blk * EXPERT_BLOCK - pad_starts[blk_e]), 0, EXPERT_BLOCK)
    return (dest, n_blocks * EXPERT_BLOCK, blk_e.astype(jnp.int32), blk_valid.astype(jnp.int32),
            n_used.reshape(1).astype(jnp.int32))


def kernel(x, c, w_mod, b_mod, norm1_g, w_in, conv_w, conv_b, lru_wx, lru_bx, lru_wa, lru_ba, lru_lambda, w_attn_o, w_lru_o, w_out, norm2_g, w_grp, b_grp, w_exp, b_exp, w1, w3, w2, norm_f_g):
    b, s, d = x.shape
    assert d == D_MODEL and s == SPAN * DILATIONS[-1] and w_mod.shape[0] == 1
    mod3 = _modulation(c, w_mod[0], b_mod[0]).reshape(b, 1, 6 * d)
    qkv0, qkv1, qkv2, xc, gy, gl = _projection(x, mod3, norm1_g[0].reshape(1, d), _prep_w_in(w_in[0]),
                                               conv_w[0], conv_b[0])
    attn = _attention((qkv0, qkv1, qkv2), b, s)
    lru = _lru_branch(xc, gy, lru_wx[0], lru_bx[0], lru_wa[0], lru_ba[0], lru_lambda[0])

    n_pad = ROUTER_ROWS - N_GROUPS - N_EXPERTS
    wr = jnp.pad(jnp.concatenate([w_grp[0], w_exp[0]], axis=1).T, ((0, n_pad), (0, 0))).astype(BF16)
    br = jnp.pad(jnp.concatenate([b_grp[0], b_exp[0]]), (0, n_pad)).reshape(ROUTER_ROWS, 1)
    wa, wl, wo = w_attn_o[0].astype(BF16), w_lru_o[0].astype(BF16), w_out[0].astype(BF16)
    as_rows = lambda a: a.reshape(-1, ROW_SUBLANES, LANE)
    as_tiles = lambda a: a.reshape(-1, LANE)

    out = None
    nb = b // MOE_BATCH_RANGES
    for b0 in range(0, b, nb):
        x1, h2, route, counts = _mix_route(attn, lru, gl, x, mod3, wa, wl, wo, norm2_g[0].reshape(1, d),
                                           wr, br, b0, nb)
        dest, n_slots, blk_e, blk_valid, n_used = _slot_plan(route, counts, nb * s)
        xp = as_tiles(_sc_scatter_rows(as_rows(h2), dest, n_slots))
        yp = _experts(xp, blk_e, blk_valid, n_used, w1[0], w3[0], w2[0])
        yg = as_tiles(_sc_gather_rows(as_rows(yp), dest))
        out = _combine(yg, route, x1, mod3, norm_f_g.reshape(1, d), b0, out)
    return out
```

```python
import functools
import math

import jax
import jax.numpy as jnp
from jax import lax
from jax.experimental import pallas as pl
from jax.experimental.pallas import tpu as pltpu
from jax.experimental.pallas import tpu_sc as plsc

F32 = jnp.float32
BF16 = jnp.bfloat16

D_MODEL = 1024
HEAD_DIM = 64
N_SLOTS = 8
SPAN = 128
DILATIONS = (1, 4, 16)
GROUP_COLS = 3 * N_SLOTS * HEAD_DIM
ATTN_WIDTH = len(DILATIONS) * N_SLOTS * HEAD_DIM
ATTN_OUT = N_SLOTS * HEAD_DIM
LRU_WIDTH = D_MODEL
LRU_BLOCK_DIM = 64
CONV_WIDTH = 4
CONV_TAIL = 8
LRU_C = 8.0
N_GROUPS = 4
EXPERTS_PER_GROUP = 8
N_EXPERTS = N_GROUPS * EXPERTS_PER_GROUP
D_EXPERT = D_MODEL // 2
EPS = 1e-6
LANE = 128
VMEM_LIMIT = 56 * 1024 * 1024


def _cparams(sem, vmem=None):
    return pltpu.CompilerParams(dimension_semantics=sem, vmem_limit_bytes=vmem)


def _resident(shape):
    nd = len(shape)
    return pl.BlockSpec(shape, lambda *_: (0,) * nd, pipeline_mode=pl.Buffered(1))


def _mod_kernel(c_ref, w_ref, b_ref, o_ref):
    c = c_ref[...]
    ca = c * jax.nn.sigmoid(c)
    o_ref[...] = jnp.dot(ca.astype(BF16), w_ref[...].astype(BF16),
                         preferred_element_type=F32) + b_ref[...]


def _modulation(c, w_mod, b_mod):
    b, d = c.shape
    n = w_mod.shape[1]
    tn = n // 4
    return pl.pallas_call(
        _mod_kernel,
        grid=(n // tn,),
        in_specs=[pl.BlockSpec((b, d), lambda j: (0, 0)),
                  pl.BlockSpec((d, tn), lambda j: (0, j)),
                  pl.BlockSpec((1, tn), lambda j: (0, j))],
        out_specs=pl.BlockSpec((b, tn), lambda j: (0, j)),
        out_shape=jax.ShapeDtypeStruct((b, n), F32),
        compiler_params=_cparams(("arbitrary",)),
        name="modulation",
    )(c, w_mod, b_mod.reshape(1, n))


def _rms_mod(x, g, scale, shift):
    ms = jnp.mean(x * x, axis=-1, keepdims=True)
    return x * lax.rsqrt(ms + EPS) * g * (1.0 + scale) + shift


def _gelu_tanh(y):
    return y * (0.5 * (1.0 + jnp.tanh(0.7978845608028654 * (y + 0.044715 * (y * y * y)))))


def _proj_kernel(x_ref, mod_ref, g_ref, w_ref, cw_ref, cb_ref, qkv0_ref, qkv1_ref, qkv2_ref,
                 xc_ref, gy_ref, hs_ref, xe_ref, *, tm):
    @pl.when(pl.program_id(1) == 0)
    def _():
        xe_ref[0:CONV_TAIL, :] = jnp.zeros((CONV_TAIL, LRU_WIDTH), F32)

    @pl.when(pl.program_id(1) > 0)
    def _():
        xe_ref[0:CONV_TAIL, :] = xe_ref[tm:tm + CONV_TAIL, :]

    d_model = x_ref.shape[-1]
    m = mod_ref[0]
    h = _rms_mod(x_ref[0], g_ref[...], m[:, d_model:2 * d_model], m[:, 0:d_model])

    def mm(hv, lo, hi):
        return jnp.dot(hv, w_ref[:, lo:hi], preferred_element_type=F32)

    hb = h.astype(BF16)
    c0 = len(DILATIONS) * GROUP_COLS
    qkv0_ref[0] = mm(hb, 0, GROUP_COLS).astype(BF16)
    xr = mm(hb, c0, c0 + LRU_WIDTH)
    xe_ref[CONV_TAIL:, :] = xr
    cw = cw_ref[...]
    xc = xr * cw[CONV_WIDTH - 1:CONV_WIDTH] + cb_ref[...]
    for k in range(1, CONV_WIDTH):
        xc = xc + xe_ref[CONV_TAIL - k:CONV_TAIL - k + tm, :] * cw[CONV_WIDTH - 1 - k:CONV_WIDTH - k]
    xc_ref[0] = xc.astype(BF16)
    gy_ref[0] = _gelu_tanh(mm(hb, c0 + LRU_WIDTH, c0 + 2 * LRU_WIDTH)).astype(BF16)

    n_slab = d_model // LANE
    for j in range(n_slab):
        hs_ref[j] = h[:, j * LANE:(j + 1) * LANE]
    for g, out_ref in ((1, qkv1_ref), (2, qkv2_ref)):
        d = DILATIONS[g]
        rows = tm // d
        hp = jnp.concatenate(
            [jnp.concatenate([hs_ref[j, pl.ds(p, rows, stride=d), :] for j in range(n_slab)], axis=1)
             for p in range(d)], axis=0).astype(BF16)
        res = mm(hp, g * GROUP_COLS, (g + 1) * GROUP_COLS).astype(BF16)
        for p in range(d):
            out_ref[p] = res[p * rows:(p + 1) * rows]


def _projection(x, mod3, g1, w_r, conv_w, conv_b, *, tm=256):
    b, s, d = x.shape
    n = w_r.shape[1]
    assert s % tm == 0 and tm % (16 * DILATIONS[-1]) == 0 and CONV_TAIL >= CONV_WIDTH - 1
    out_shape = [jax.ShapeDtypeStruct((b * dd, s // dd, GROUP_COLS), BF16) for dd in DILATIONS]
    out_shape += [jax.ShapeDtypeStruct((b, s, LRU_WIDTH), BF16),
                  jax.ShapeDtypeStruct((b, s, LRU_WIDTH), BF16)]
    out_specs = [pl.BlockSpec((dd, tm // dd, GROUP_COLS), lambda bi, i: (bi, i, 0)) for dd in DILATIONS]
    out_specs += [pl.BlockSpec((1, tm, LRU_WIDTH), lambda bi, i: (bi, i, 0)),
                  pl.BlockSpec((1, tm, LRU_WIDTH), lambda bi, i: (bi, i, 0))]
    return pl.pallas_call(
        functools.partial(_proj_kernel, tm=tm),
        grid=(b, s // tm),
        in_specs=[pl.BlockSpec((1, tm, d), lambda bi, i: (bi, i, 0)),
                  pl.BlockSpec((1, 1, mod3.shape[-1]), lambda bi, i: (bi, 0, 0)),
                  pl.BlockSpec((1, d), lambda bi, i: (0, 0)),
                  _resident((d, n)),
                  pl.BlockSpec((CONV_WIDTH, LRU_WIDTH), lambda bi, i: (0, 0)),
                  pl.BlockSpec((1, LRU_WIDTH), lambda bi, i: (0, 0))],
        out_specs=out_specs,
        out_shape=out_shape,
        scratch_shapes=[pltpu.VMEM((d // LANE, tm, LANE), F32), pltpu.VMEM((CONV_TAIL + tm, LRU_WIDTH), F32)],
        compiler_params=_cparams(("parallel", "arbitrary"), VMEM_LIMIT),
        name="projection",
    )(x, mod3, g1, w_r, conv_w, conv_b.reshape(1, LRU_WIDTH))


def _attn_kernel(q0, k0, v0, q1, k1, v1, q2, k2, v2, o_ref, acc_ref, lse_ref, bias_ref, *, seq):
    hcols = o_ref.shape[-1]
    n_head = hcols // HEAD_DIM
    head_of_lane = lax.broadcasted_iota(jnp.int32, (SPAN, hcols), 1) // HEAD_DIM
    head_mask_b = [jnp.where(head_of_lane == h, 1.0, 0.0).astype(BF16) for h in range(n_head)]

    def by_head(parts):
        out = parts[n_head - 1]
        for h in range(n_head - 2, -1, -1):
            out = jnp.where(head_of_lane == h, parts[h], out)
        return out

    qi = lax.broadcasted_iota(jnp.int32, (n_head * SPAN, 2 * SPAN), 0) % SPAN
    ki = lax.broadcasted_iota(jnp.int32, (n_head * SPAN, 2 * SPAN), 1)
    band = (ki >= qi) & (ki <= qi + SPAN)
    bias_ref[0] = jnp.where(band, 0.0, -jnp.inf)
    bias_ref[1] = jnp.where(band & (ki >= SPAN), 0.0, -jnp.inf)

    for g, (q_ref, k_ref, v_ref) in enumerate(((q0, k0, v0), (q1, k1, v1), (q2, k2, v2))):
        d = DILATIONS[g]
        n_blk = seq // d // SPAN

        def tile(n, carry, q_ref=q_ref, k_ref=k_ref, v_ref=v_ref, d=d, n_blk=n_blk, g=g):
            p = n // n_blk
            blk = n % n_blk
            r0 = pl.multiple_of(blk * SPAN, SPAN)
            rp = pl.multiple_of(jnp.maximum(blk - 1, 0) * SPAN, SPAN)
            q = q_ref[p, pl.ds(r0, SPAN), :]
            kk = jnp.concatenate([k_ref[p, pl.ds(rp, SPAN), :], k_ref[p, pl.ds(r0, SPAN), :]], axis=0)
            vv = jnp.concatenate([v_ref[p, pl.ds(rp, SPAN), :], v_ref[p, pl.ds(r0, SPAN), :]], axis=0)
            qs = jnp.concatenate([q * head_mask_b[h] for h in range(n_head)], axis=0)
            sc = lax.dot_general(qs, kk, (((1,), (1,)), ((), ())), preferred_element_type=F32)
            sc = sc + bias_ref[jnp.where(blk > 0, 0, 1)]
            mx = jnp.max(sc, axis=-1, keepdims=True)
            e = jnp.exp(sc - mx)
            den = jnp.sum(e, axis=-1, keepdims=True)
            pv = jnp.dot(e.astype(BF16), vv, preferred_element_type=F32)
            lse = mx + jnp.log(den)
            rows_of = lambda a: [a[h * SPAN:(h + 1) * SPAN] for h in range(n_head)]
            o = by_head(rows_of(pv)) / by_head(rows_of(den))
            l = by_head(rows_of(lse))
            start = p + d * r0
            for j in range(hcols // LANE):
                rows = pl.ds(start, SPAN, stride=d) if d > 1 else pl.ds(start, SPAN)
                acc_ref[g, j, rows, :] = o[:, j * LANE:(j + 1) * LANE]
                lse_ref[g, j, rows, :] = l[:, j * LANE:(j + 1) * LANE]
            return carry

        lax.fori_loop(0, seq // SPAN, tile, 0, unroll=8)

    chunk = 256

    def combine(c, carry):
        r = pl.multiple_of(c * chunk, chunk)
        for j in range(hcols // LANE):
            ls = [lse_ref[g, j, pl.ds(r, chunk), :] for g in range(len(DILATIONS))]
            mx = jnp.maximum(jnp.maximum(ls[0], ls[1]), ls[2])
            ws = [jnp.exp(v - mx) for v in ls]
            num = ws[0] * acc_ref[0, j, pl.ds(r, chunk), :]
            for g in range(1, len(DILATIONS)):
                num = num + ws[g] * acc_ref[g, j, pl.ds(r, chunk), :]
            o_ref[0, pl.ds(r, chunk), j * LANE:(j + 1) * LANE] = (num / (ws[0] + ws[1] + ws[2])).astype(BF16)
        return carry

    lax.fori_loop(0, seq // chunk, combine, 0)


def _attention(qkvs, b, s):
    hcols = 4 * HEAD_DIM
    n_hg = ATTN_OUT // hcols
    ncb = ATTN_OUT // hcols
    in_specs, args = [], []
    for g, d in enumerate(DILATIONS):
        for part in range(3):
            in_specs.append(pl.BlockSpec((d, s // d, hcols),
                                         lambda bi, hg, part=part: (bi, 0, part * ncb + hg)))
            args.append(qkvs[g])
    return pl.pallas_call(
        functools.partial(_attn_kernel, seq=s),
        grid=(b, n_hg),
        in_specs=in_specs,
        out_specs=pl.BlockSpec((1, s, hcols), lambda bi, hg: (bi, 0, hg)),
        out_shape=jax.ShapeDtypeStruct((b, s, ATTN_OUT), BF16),
        scratch_shapes=[pltpu.VMEM((len(DILATIONS), hcols // LANE, s, LANE), F32),
                        pltpu.VMEM((len(DILATIONS), hcols // LANE, s, LANE), F32),
                        pltpu.VMEM((2, (hcols // HEAD_DIM) * SPAN, 2 * SPAN), F32)],
        compiler_params=_cparams(("parallel", "parallel"), VMEM_LIMIT),
        name="dilated_attention",
    )(*args)


def _lru_kernel(xc_ref, gy_ref, wg_ref, bx_ref, ba_ref, lam_ref, o_ref, a_ref, b_ref, h_ref, *, pitch):
    nb, ts, tc = xc_ref.shape
    nl = tc // LANE

    @pl.when(pl.program_id(1) == 0)
    def _():
        h_ref[...] = jnp.zeros_like(h_ref)

    xb = xc_ref[...].reshape(nb * ts, tc)
    xc = xb.astype(F32)
    gates = jnp.dot(xb, wg_ref[0], preferred_element_type=F32)
    gate_i = jax.nn.sigmoid(gates[:, :tc] + bx_ref[...])
    gate_r = jax.nn.sigmoid(gates[:, tc:] + ba_ref[...])
    neg_lam = -lam_ref[...]
    softplus = jnp.maximum(neg_lam, 0.0) + jnp.log1p(jnp.exp(-jnp.abs(neg_lam)))
    log_a = (-LRU_C) * gate_r * softplus
    a = jnp.exp(log_a)
    bv = jnp.sqrt(jnp.tanh(-log_a) * (1.0 + a * a)) * gate_i * xc
    for bi in range(nb):
        for j in range(nl):
            a_ref[j, pl.ds(bi * pitch, ts), :] = a[bi * ts:(bi + 1) * ts, j * LANE:(j + 1) * LANE]
            b_ref[j, pl.ds(bi * pitch, ts), :] = bv[bi * ts:(bi + 1) * ts, j * LANE:(j + 1) * LANE]

    def step(t, hs):
        out = []
        for j in range(nl):
            rows = pl.ds(t, nb, stride=pitch)
            h = a_ref[j, rows, :] * hs[j] + b_ref[j, rows, :]
            b_ref[j, rows, :] = h
            out.append(h)
        return tuple(out)

    hs = lax.fori_loop(0, ts, step, tuple(h_ref[j] for j in range(nl)), unroll=8)
    for j in range(nl):
        h_ref[j] = hs[j]
    for bi in range(nb):
        h = jnp.concatenate([b_ref[j, pl.ds(bi * pitch, ts), :] for j in range(nl)], axis=1)
        o_ref[bi] = (h * gy_ref[bi].astype(F32)).astype(BF16)


def _lru_gate_weights(wx, wa, tc):
    nb, bd, _ = wx.shape
    per = tc // bd
    eye = jnp.eye(per, dtype=wx.dtype)

    def bdiag(w):
        w = w.reshape(nb // per, per, bd, bd)
        return jnp.einsum('cpio,pq->cpiqo', w, eye).reshape(nb // per, tc, tc)

    return jnp.concatenate([bdiag(wx), bdiag(wa)], axis=-1).astype(BF16)


def _lru_branch(xc, gy, wx, bx, wa, ba, lam, *, tc=256, ts=128):
    b, s, c = xc.shape
    assert s % ts == 0 and c % tc == 0
    wg = _lru_gate_weights(wx, wa, tc)
    row = lambda v: v.reshape(1, c)
    tile = pl.BlockSpec((b, ts, tc), lambda ci, ti: (0, ti, ci))
    vec = pl.BlockSpec((1, tc), lambda ci, ti: (0, ci))
    pitch = ts + 8
    return pl.pallas_call(
        functools.partial(_lru_kernel, pitch=pitch),
        grid=(c // tc, s // ts),
        in_specs=[tile, tile,
                  pl.BlockSpec((1, tc, 2 * tc), lambda ci, ti: (ci, 0, 0)),
                  vec, vec, vec],
        out_specs=tile,
        out_shape=jax.ShapeDtypeStruct((b, s, c), BF16),
        scratch_shapes=[pltpu.VMEM((tc // LANE, b * pitch, LANE), F32),
                        pltpu.VMEM((tc // LANE, b * pitch, LANE), F32),
                        pltpu.VMEM((tc // LANE, b, LANE), F32)],
        compiler_params=_cparams(("parallel", "arbitrary"), VMEM_LIMIT),
        name="rg_lru",
    )(xc, gy, wg, row(bx), row(ba), row(lam))


def _prep_w_in(w_in):
    a = ATTN_WIDTH
    gw = N_SLOTS * HEAD_DIM
    q = w_in[:, :a] * (HEAD_DIM ** -0.5)
    k = w_in[:, a:2 * a]
    v = w_in[:, 2 * a:3 * a]
    parts = []
    for g in range(len(DILATIONS)):
        sl = slice(g * gw, (g + 1) * gw)
        parts += [q[:, sl], k[:, sl], v[:, sl]]
    parts.append(w_in[:, 3 * a:3 * a + 2 * LRU_WIDTH])
    return jnp.concatenate(parts, axis=1).astype(BF16), w_in[:, 3 * a + 2 * LRU_WIDTH:].astype(BF16)


ROW_SUBLANES = D_MODEL // 2 // LANE


def _store_tile_rows(ref, v, row0=0):
    n, half = v.shape[0], v.shape[1] // 2
    lo = pltpu.bitcast(v[:, :half].astype(BF16).astype(F32), jnp.uint32)
    hi = pltpu.bitcast(v[:, half:].astype(BF16).astype(F32), jnp.uint32)
    words = (hi & jnp.uint32(0xFFFF0000)) | (lo >> 16)
    for j in range(ROW_SUBLANES):
        ref[pl.ds(row0 * ROW_SUBLANES + j, n, stride=ROW_SUBLANES), :] = words[:, j * LANE:(j + 1) * LANE]


def _load_tile_rows(ref, n=None):
    n = ref.shape[0] // ROW_SUBLANES if n is None else n
    words = [ref[pl.ds(j, n, stride=ROW_SUBLANES), :] for j in range(ROW_SUBLANES)]
    lo = [pltpu.bitcast(w << 16, F32) for w in words]
    hi = [pltpu.bitcast(w & jnp.uint32(0xFFFF0000), F32) for w in words]
    return jnp.concatenate(lo + hi, axis=-1)


SC_CORES, SC_SUBCORES = 2, 16
SC_CHUNK = 128


def _sc_gather_rows(table, idx):
    n = idx.shape[0]
    per_worker = n // (SC_CORES * SC_SUBCORES)
    n_chunks = per_worker // SC_CHUNK
    assert n_chunks * SC_CHUNK * SC_CORES * SC_SUBCORES == n
    mesh = plsc.VectorSubcoreMesh(core_axis_name="c", subcore_axis_name="s")

    def body(table_hbm, idx_hbm, out_hbm, idx_v, rows_v, sem):
        base = (lax.axis_index("s") * SC_CORES + lax.axis_index("c")) * per_worker

        @pl.loop(0, n_chunks)
        def _(i):
            off = pl.multiple_of(base + i * SC_CHUNK, SC_CHUNK)
            pltpu.sync_copy(idx_hbm.at[pl.ds(off, SC_CHUNK)], idx_v)
            pltpu.async_copy(table_hbm.at[idx_v], rows_v, sem).wait()
            pltpu.sync_copy(rows_v, out_hbm.at[pl.ds(off, SC_CHUNK)])

    return pl.kernel(
        body, mesh=mesh,
        out_type=jax.ShapeDtypeStruct((n,) + table.shape[1:], table.dtype),
        scratch_types=[pltpu.VMEM((SC_CHUNK,), jnp.int32),
                       pltpu.VMEM((SC_CHUNK,) + table.shape[1:], table.dtype),
                       pltpu.SemaphoreType.DMA],
        name="sc_gather_rows",
    )(table, idx)


def _sc_scatter_rows(rows, idx, n_out):
    n_rows = rows.shape[0]
    n_choice = idx.shape[0] // n_rows
    per_worker = n_rows // (SC_CORES * SC_SUBCORES)
    n_chunks = per_worker // SC_CHUNK
    assert n_chunks * SC_CHUNK * SC_CORES * SC_SUBCORES == n_rows and n_choice * n_rows == idx.shape[0]
    mesh = plsc.VectorSubcoreMesh(core_axis_name="c", subcore_axis_name="s")

    def body(rows_hbm, idx_hbm, out_hbm, idx_v, rows_v):
        base = (lax.axis_index("s") * SC_CORES + lax.axis_index("c")) * per_worker

        @pl.loop(0, n_chunks)
        def _(i):
            off = pl.multiple_of(base + i * SC_CHUNK, SC_CHUNK)
            pltpu.sync_copy(rows_hbm.at[pl.ds(off, SC_CHUNK)], rows_v)
            for k in range(n_choice):
                pltpu.sync_copy(idx_hbm.at[pl.ds(k * n_rows + off, SC_CHUNK)], idx_v)
                pltpu.sync_copy(rows_v, out_hbm.at[idx_v])

    return pl.kernel(
        body, mesh=mesh,
        out_type=jax.ShapeDtypeStruct((n_out,) + rows.shape[1:], rows.dtype),
        scratch_types=[pltpu.VMEM((SC_CHUNK,), jnp.int32),
                       pltpu.VMEM((SC_CHUNK,) + rows.shape[1:], rows.dtype)],
        name="sc_scatter_rows",
    )(rows, idx)


ROUTE_ROWS = 8
EXPERT_ROW0 = N_GROUPS
ROUTER_ROWS = 48


def _mix_kernel(attn_ref, lru_ref, x_ref, mod_ref, g1_ref, wg_ref, wa_ref, wl_ref, wo_ref, g2_ref,
                wrt_ref, brt_ref, x1_ref, h2_ref, route_ref, cnt_ref, cnt_acc):
    d = x_ref.shape[-1]
    tm = x_ref.shape[1]

    @pl.when((pl.program_id(0) == 0) & (pl.program_id(1) == 0))
    def _():
        cnt_acc[...] = jnp.zeros_like(cnt_acc)

    m = mod_ref[0]
    gate1, shift2, scale2 = m[:, 2 * d:3 * d], m[:, 3 * d:4 * d], m[:, 4 * d:5 * d]
    x = x_ref[0]
    h1 = _rms_mod(x, g1_ref[...], m[:, d:2 * d], m[:, 0:d]).astype(BF16)
    gates = jax.nn.sigmoid(jnp.dot(h1, wg_ref[...], preferred_element_type=F32))
    ya = jnp.dot(attn_ref[0], wa_ref[...], preferred_element_type=F32)
    yl = jnp.dot(lru_ref[0], wl_ref[...], preferred_element_type=F32)
    mixed = gates[:, :d] * ya + gates[:, d:] * yl
    y = jnp.dot(mixed.astype(BF16), wo_ref[...], preferred_element_type=F32)
    x1 = x + (1.0 + gate1) * y
    x1_ref[0] = x1.astype(BF16)
    h2 = _rms_mod(x1, g2_ref[...], scale2, shift2)
    _store_tile_rows(h2_ref, h2)
    logits = lax.dot_general(wrt_ref[...], h2.astype(BF16), (((1,), (1,)), ((), ())),
                             preferred_element_type=F32) + brt_ref[...]

    row = lax.broadcasted_iota(jnp.int32, logits.shape, 0)
    neg = -jnp.inf

    def top(vals):
        mx = jnp.max(vals, axis=0, keepdims=True)
        idx = jnp.min(jnp.where(vals == mx, row, ROUTER_ROWS), axis=0, keepdims=True)
        return mx, idx

    is_grp = row < N_GROUPS
    gmax, gidx = top(jnp.where(is_grp, logits, neg))
    grp_gate = 1.0 / jnp.sum(jnp.where(is_grp, jnp.exp(logits - gmax), 0.0), axis=0, keepdims=True)
    lo = EXPERT_ROW0 + EXPERTS_PER_GROUP * gidx
    el = jnp.where((row >= lo) & (row < lo + EXPERTS_PER_GROUP), logits, neg)
    v1, i1 = top(el)
    v2, i2 = top(jnp.where(row == i1, neg, el))
    e21 = jnp.exp(v2 - v1)
    wt1 = grp_gate / (1.0 + e21)
    wt2 = wt1 * e21

    oh1 = jnp.where(row == i1, 1.0, 0.0)
    oh2 = jnp.where(row == i2, 1.0, 0.0)
    ohs = oh1 + oh2
    rr = lax.broadcasted_iota(jnp.int32, (tm, tm), 0)
    cc = lax.broadcasted_iota(jnp.int32, (tm, tm), 1)
    earlier = jnp.where(rr < cc, 1.0, 0.0).astype(BF16)
    before = jnp.dot(ohs.astype(BF16), earlier, preferred_element_type=F32) + cnt_acc[...]
    rank1 = jnp.sum(oh1 * before, axis=0, keepdims=True)
    rank2 = jnp.sum(oh2 * before, axis=0, keepdims=True)
    cnt_acc[...] = cnt_acc[...] + jnp.sum(ohs, axis=1, keepdims=True)
    cnt_ref[...] = cnt_acc[...]

    vals = [(i1 - EXPERT_ROW0).astype(F32), (i2 - EXPERT_ROW0).astype(F32), rank1, rank2, wt1, wt2]
    out_row = lax.broadcasted_iota(jnp.int32, (ROUTE_ROWS, tm), 0)
    slab = jnp.zeros((ROUTE_ROWS, tm), F32)
    for j, v in enumerate(vals):
        slab = jnp.where(out_row == j, v, slab)
    route_ref[...] = slab


def _mix_route(attn, lru, x, mod3, g1, wg, wa, wl, wo, g2, wrt, brt, b0, nb, *, tm=512):
    _, s, d = x.shape
    spt = s // tm
    tok_in = lambda w: pl.BlockSpec((1, tm, w), lambda bi, i: (b0 + bi, i, 0))
    return pl.pallas_call(
        _mix_kernel,
        grid=(nb, spt),
        in_specs=[tok_in(attn.shape[-1]), tok_in(d), tok_in(d),
                  pl.BlockSpec((1, 1, mod3.shape[-1]), lambda bi, i: (b0 + bi, 0, 0)),
                  pl.BlockSpec((1, d), lambda bi, i: (0, 0)),
                  _resident(wg.shape), _resident(wa.shape), _resident(wl.shape), _resident(wo.shape),
                  pl.BlockSpec((1, d), lambda bi, i: (0, 0)),
                  _resident(wrt.shape),
                  pl.BlockSpec((ROUTER_ROWS, 1), lambda bi, i: (0, 0))],
        out_specs=[pl.BlockSpec((1, tm, d), lambda bi, i: (bi, i, 0)),
                   pl.BlockSpec((tm * ROW_SUBLANES, LANE), lambda bi, i: (bi * spt + i, 0)),
                   pl.BlockSpec((ROUTE_ROWS, tm), lambda bi, i: (0, bi * spt + i)),
                   pl.BlockSpec((ROUTER_ROWS, 1), lambda bi, i: (0, 0))],
        out_shape=[jax.ShapeDtypeStruct((nb, s, d), BF16),
                   jax.ShapeDtypeStruct((nb * s * ROW_SUBLANES, LANE), jnp.uint32),
                   jax.ShapeDtypeStruct((ROUTE_ROWS, nb * s), F32),
                   jax.ShapeDtypeStruct((ROUTER_ROWS, 1), F32)],
        scratch_shapes=[pltpu.VMEM((ROUTER_ROWS, 1), F32)],
        compiler_params=_cparams(("arbitrary", "arbitrary"), VMEM_LIMIT),
        name="mix_route",
    )(attn, lru, x, mod3, g1, wg, wa, wl, wo, g2, wrt, brt)


TOP_K = 2
EXPERT_BLOCK = 512
MOE_BATCH_RANGES = 2


def _expert_kernel(be_ref, bv_ref, nu_ref, x_ref, w1_ref, w3_ref, w2_ref, y_ref, wb1, wb3, wb2):
    j = pl.program_id(0)
    half = EXPERT_BLOCK // 2

    def ffn(n):
        xb = _load_tile_rows(x_ref, n).astype(BF16)
        a = jnp.dot(xb, wb1[...], preferred_element_type=F32)
        g = jnp.dot(xb, wb3[...], preferred_element_type=F32)
        hm = (a * jax.nn.sigmoid(a) * g).astype(BF16)
        _store_tile_rows(y_ref, jnp.dot(hm, wb2[...], preferred_element_type=F32))

    @pl.when(j < nu_ref[0])
    def _():
        @pl.when((j == 0) | (be_ref[j] != be_ref[jnp.maximum(j - 1, 0)]))
        def _():
            wb1[...] = w1_ref[0].astype(BF16)
            wb3[...] = w3_ref[0].astype(BF16)
            wb2[...] = w2_ref[0].astype(BF16)

        @pl.when(bv_ref[j] > half)
        def _():
            ffn(EXPERT_BLOCK)

        @pl.when(bv_ref[j] <= half)
        def _():
            ffn(half)
            y_ref[half * ROW_SUBLANES:, :] = jnp.zeros((half * ROW_SUBLANES, LANE), y_ref.dtype)

    @pl.when(j >= nu_ref[0])
    def _():
        y_ref[...] = jnp.zeros_like(y_ref)


def _experts(xp, blk_e, blk_valid, n_used, w1, w3, w2):
    ne, d, de = w1.shape
    nb = xp.shape[0] // (EXPERT_BLOCK * ROW_SUBLANES)
    rows = (EXPERT_BLOCK * ROW_SUBLANES, LANE)
    last = lambda j, nu: jnp.minimum(j, nu[0] - 1)
    grid_spec = pltpu.PrefetchScalarGridSpec(
        num_scalar_prefetch=3,
        grid=(nb,),
        in_specs=[pl.BlockSpec(rows, lambda j, be, bv, nu: (last(j, nu), 0)),
                  pl.BlockSpec((1, d, de), lambda j, be, bv, nu: (be[j], 0, 0)),
                  pl.BlockSpec((1, d, de), lambda j, be, bv, nu: (be[j], 0, 0)),
                  pl.BlockSpec((1, de, d), lambda j, be, bv, nu: (be[j], 0, 0))],
        out_specs=pl.BlockSpec(rows, lambda j, be, bv, nu: (j, 0)),
        scratch_shapes=[pltpu.VMEM((d, de), BF16), pltpu.VMEM((d, de), BF16), pltpu.VMEM((de, d), BF16)])
    return pl.pallas_call(
        _expert_kernel,
        grid_spec=grid_spec,
        out_shape=jax.ShapeDtypeStruct(xp.shape, xp.dtype),
        compiler_params=_cparams(("arbitrary",), VMEM_LIMIT),
        name="experts",
    )(blk_e, blk_valid, n_used, xp, w1, w3, w2)


def _combine_kernel(y0_ref, y1_ref, route_ref, x1_ref, mod_ref, gf_ref, *rest):
    o_ref = rest[-1]
    tm, d = x1_ref.shape[1], x1_ref.shape[2]
    route = jnp.concatenate([route_ref[...], jnp.zeros((LANE - ROUTE_ROWS, tm), F32)], axis=0).T
    moe = _load_tile_rows(y0_ref) * route[:, 4:5] + _load_tile_rows(y1_ref) * route[:, 5:6]
    gate2 = mod_ref[0][:, 5 * d:6 * d]
    xo = x1_ref[0].astype(F32) + (1.0 + gate2) * moe
    ms = jnp.mean(xo * xo, axis=-1, keepdims=True)
    o_ref[0] = xo * lax.rsqrt(ms + EPS) * gf_ref[...]


def _combine(yg, route, x1, mod3, gf, b0, out_prev, *, tm=256):
    nb, s, d = x1.shape
    b_all = mod3.shape[0]
    spt = s // tm
    nt = nb * spt
    rows = (tm * ROW_SUBLANES, LANE)
    in_specs = [pl.BlockSpec(rows, lambda bi, i: (bi * spt + i, 0)),
                pl.BlockSpec(rows, lambda bi, i: (nt + bi * spt + i, 0)),
                pl.BlockSpec((ROUTE_ROWS, tm), lambda bi, i: (0, bi * spt + i)),
                pl.BlockSpec((1, tm, d), lambda bi, i: (bi, i, 0)),
                pl.BlockSpec((1, 1, mod3.shape[-1]), lambda bi, i: (b0 + bi, 0, 0)),
                pl.BlockSpec((1, d), lambda bi, i: (0, 0))]
    args = [yg, yg, route, x1, mod3, gf]
    aliases = {}
    if out_prev is not None:
        in_specs.append(pl.BlockSpec(memory_space=pl.ANY))
        aliases = {len(args): 0}
        args.append(out_prev)
    return pl.pallas_call(
        _combine_kernel,
        grid=(nb, spt),
        in_specs=in_specs,
        out_specs=pl.BlockSpec((1, tm, d), lambda bi, i: (b0 + bi, i, 0)),
        out_shape=jax.ShapeDtypeStruct((b_all, s, d), F32),
        input_output_aliases=aliases,
        compiler_params=_cparams(("parallel", "parallel"), VMEM_LIMIT),
        name="combine",
    )(*args)


def _slot_plan(route, counts, n_tok):
    sizes = counts[EXPERT_ROW0:EXPERT_ROW0 + N_EXPERTS, 0].astype(jnp.int32)
    padded = (sizes + EXPERT_BLOCK - 1) // EXPERT_BLOCK * EXPERT_BLOCK
    pad_ends = jnp.cumsum(padded)
    pad_starts = pad_ends - padded
    eid = route[0:TOP_K].astype(jnp.int32)
    rank = route[TOP_K:2 * TOP_K].astype(jnp.int32)
    start = jnp.sum(jnp.where(eid[..., None] == jnp.arange(N_EXPERTS), pad_starts, 0), axis=-1)
    dest = (start + rank).reshape(TOP_K * n_tok)
    n_blocks = (n_tok * TOP_K + N_EXPERTS * (EXPERT_BLOCK - 1) + EXPERT_BLOCK - 1) // EXPERT_BLOCK
    gran = SC_CORES * SC_SUBCORES * SC_CHUNK // math.gcd(SC_CORES * SC_SUBCORES * SC_CHUNK, EXPERT_BLOCK)
    n_blocks = (n_blocks + gran - 1) // gran * gran
    n_used = pad_ends[-1] // EXPERT_BLOCK
    blk = jnp.minimum(jnp.arange(n_blocks), n_used - 1)
    blk_e = jnp.minimum(jnp.sum(pad_ends[None, :] <= (blk * EXPERT_BLOCK)[:, None], axis=1), N_EXPERTS - 1)
    blk_valid = jnp.clip(sizes[blk_e] - (blk * EXPERT_BLOCK - pad_starts[blk_e]), 0, EXPERT_BLOCK)
    return (dest, n_blocks * EXPERT_BLOCK, blk_e.astype(jnp.int32), blk_valid.astype(jnp.int32),
            n_used.reshape(1).astype(jnp.int32))


def kernel(x, c, w_mod, b_mod, norm1_g, w_in, conv_w, conv_b, lru_wx, lru_bx, lru_wa, lru_ba, lru_lambda, w_attn_o, w_lru_o, w_out, norm2_g, w_grp, b_grp, w_exp, b_exp, w1, w3, w2, norm_f_g):
    b, s, d = x.shape
    assert d == D_MODEL and s == SPAN * DILATIONS[-1] and w_mod.shape[0] == 1
    mod3 = _modulation(c, w_mod[0], b_mod[0]).reshape(b, 1, 6 * d)
    w_proj, w_gate = _prep_w_in(w_in[0])
    g1 = norm1_g[0].reshape(1, d)
    qkv0, qkv1, qkv2, xc, gy = _projection(x, mod3, g1, w_proj, conv_w[0], conv_b[0])
    attn = _attention((qkv0, qkv1, qkv2), b, s)
    lru = _lru_branch(xc, gy, lru_wx[0], lru_bx[0], lru_wa[0], lru_ba[0], lru_lambda[0])

    n_pad = ROUTER_ROWS - N_GROUPS - N_EXPERTS
    wr = jnp.pad(jnp.concatenate([w_grp[0], w_exp[0]], axis=1).T, ((0, n_pad), (0, 0))).astype(BF16)
    br = jnp.pad(jnp.concatenate([b_grp[0], b_exp[0]]), (0, n_pad)).reshape(ROUTER_ROWS, 1)
    wa, wl, wo = w_attn_o[0].astype(BF16), w_lru_o[0].astype(BF16), w_out[0].astype(BF16)
    as_rows = lambda a: a.reshape(-1, ROW_SUBLANES, LANE)
    as_tiles = lambda a: a.reshape(-1, LANE)

    out = None
    nb = b // MOE_BATCH_RANGES
    for b0 in range(0, b, nb):
        x1, h2, route, counts = _mix_route(attn, lru, x, mod3, g1, w_gate, wa, wl, wo,
                                           norm2_g[0].reshape(1, d), wr, br, b0, nb)
        dest, n_slots, blk_e, blk_valid, n_used = _slot_plan(route, counts, nb * s)
        xp = as_tiles(_sc_scatter_rows(as_rows(h2), dest, n_slots))
        yp = _experts(xp, blk_e, blk_valid, n_used, w1[0], w3[0], w2[0])
        yg = as_tiles(_sc_gather_rows(as_rows(yp), dest))
        out = _combine(yg, route, x1, mod3, norm_f_g.reshape(1, d), b0, out)
    return out
```

```python
import functools
import math

import jax
import jax.numpy as jnp
from jax import lax
from jax.experimental import pallas as pl
from jax.experimental.pallas import tpu as pltpu
from jax.experimental.pallas import tpu_sc as plsc

F32 = jnp.float32
BF16 = jnp.bfloat16

D_MODEL = 1024
HEAD_DIM = 64
N_SLOTS = 8
SPAN = 128
DILATIONS = (1, 4, 16)
GROUP_COLS = 3 * N_SLOTS * HEAD_DIM
ATTN_WIDTH = len(DILATIONS) * N_SLOTS * HEAD_DIM
ATTN_OUT = N_SLOTS * HEAD_DIM
LRU_WIDTH = D_MODEL
LRU_BLOCK_DIM = 64
CONV_WIDTH = 4
CONV_TAIL = 8
LRU_C = 8.0
N_GROUPS = 4
EXPERTS_PER_GROUP = 8
N_EXPERTS = N_GROUPS * EXPERTS_PER_GROUP
D_EXPERT = D_MODEL // 2
EPS = 1e-6
LOG2_E = 1.4426950408889634
LANE = 128
VMEM_LIMIT = 56 * 1024 * 1024


def _cparams(sem, vmem=None):
    return pltpu.CompilerParams(dimension_semantics=sem, vmem_limit_bytes=vmem)


def _resident(shape):
    nd = len(shape)
    return pl.BlockSpec(shape, lambda *_: (0,) * nd, pipeline_mode=pl.Buffered(1))


def _mod_kernel(c_ref, w_ref, b_ref, o_ref):
    c = c_ref[...]
    ca = c * jax.nn.sigmoid(c)
    o_ref[...] = jnp.dot(ca.astype(BF16), w_ref[...].astype(BF16),
                         preferred_element_type=F32) + b_ref[...]


def _modulation(c, w_mod, b_mod):
    b, d = c.shape
    n = w_mod.shape[1]
    tn = n // 4
    return pl.pallas_call(
        _mod_kernel,
        grid=(n // tn,),
        in_specs=[pl.BlockSpec((b, d), lambda j: (0, 0)),
                  pl.BlockSpec((d, tn), lambda j: (0, j)),
                  pl.BlockSpec((1, tn), lambda j: (0, j))],
        out_specs=pl.BlockSpec((b, tn), lambda j: (0, j)),
        out_shape=jax.ShapeDtypeStruct((b, n), F32),
        compiler_params=_cparams(("arbitrary",)),
        name="modulation",
    )(c, w_mod, b_mod.reshape(1, n))


def _rms_mod(x, g, scale, shift):
    ms = jnp.mean(x * x, axis=-1, keepdims=True)
    return x * lax.rsqrt(ms + EPS) * g * (1.0 + scale) + shift


def _gelu_tanh(y):
    return y * (0.5 * (1.0 + jnp.tanh(0.7978845608028654 * (y + 0.044715 * (y * y * y)))))


def _proj_kernel(x_ref, mod_ref, g_ref, w_ref, cw_ref, cb_ref, qkv0_ref, qkv1_ref, qkv2_ref,
                 xc_ref, gy_ref, hs_ref, xe_ref, *, tm):
    @pl.when(pl.program_id(1) == 0)
    def _():
        xe_ref[0:CONV_TAIL, :] = jnp.zeros((CONV_TAIL, LRU_WIDTH), F32)

    @pl.when(pl.program_id(1) > 0)
    def _():
        xe_ref[0:CONV_TAIL, :] = xe_ref[tm:tm + CONV_TAIL, :]

    d_model = x_ref.shape[-1]
    m = mod_ref[0]
    h = _rms_mod(x_ref[0], g_ref[...], m[:, d_model:2 * d_model], m[:, 0:d_model])

    def mm(hv, lo, hi):
        return jnp.dot(hv, w_ref[:, lo:hi], preferred_element_type=F32)

    hb = h.astype(BF16)
    c0 = len(DILATIONS) * GROUP_COLS
    qkv0_ref[0] = mm(hb, 0, GROUP_COLS).astype(BF16)
    xr = mm(hb, c0, c0 + LRU_WIDTH)
    xe_ref[CONV_TAIL:, :] = xr
    cw = cw_ref[...]
    xc = xr * cw[CONV_WIDTH - 1:CONV_WIDTH] + cb_ref[...]
    for k in range(1, CONV_WIDTH):
        xc = xc + xe_ref[CONV_TAIL - k:CONV_TAIL - k + tm, :] * cw[CONV_WIDTH - 1 - k:CONV_WIDTH - k]
    xc_ref[0] = xc.astype(BF16)
    gy_ref[0] = _gelu_tanh(mm(hb, c0 + LRU_WIDTH, c0 + 2 * LRU_WIDTH)).astype(BF16)

    n_slab = d_model // LANE
    for j in range(n_slab):
        hs_ref[j] = h[:, j * LANE:(j + 1) * LANE]
    for g, out_ref in ((1, qkv1_ref), (2, qkv2_ref)):
        d = DILATIONS[g]
        rows = tm // d
        hp = jnp.concatenate(
            [jnp.concatenate([hs_ref[j, pl.ds(p, rows, stride=d), :] for j in range(n_slab)], axis=1)
             for p in range(d)], axis=0).astype(BF16)
        res = mm(hp, g * GROUP_COLS, (g + 1) * GROUP_COLS).astype(BF16)
        for p in range(d):
            out_ref[p] = res[p * rows:(p + 1) * rows]


def _projection(x, mod3, g1, w_r, conv_w, conv_b, *, tm=256):
    b, s, d = x.shape
    n = w_r.shape[1]
    assert s % tm == 0 and tm % (16 * DILATIONS[-1]) == 0 and CONV_TAIL >= CONV_WIDTH - 1
    out_shape = [jax.ShapeDtypeStruct((b * dd, s // dd, GROUP_COLS), BF16) for dd in DILATIONS]
    out_shape += [jax.ShapeDtypeStruct((b, s, LRU_WIDTH), BF16),
                  jax.ShapeDtypeStruct((b, s, LRU_WIDTH), BF16)]
    out_specs = [pl.BlockSpec((dd, tm // dd, GROUP_COLS), lambda bi, i: (bi, i, 0)) for dd in DILATIONS]
    out_specs += [pl.BlockSpec((1, tm, LRU_WIDTH), lambda bi, i: (bi, i, 0)),
                  pl.BlockSpec((1, tm, LRU_WIDTH), lambda bi, i: (bi, i, 0))]
    return pl.pallas_call(
        functools.partial(_proj_kernel, tm=tm),
        grid=(b, s // tm),
        in_specs=[pl.BlockSpec((1, tm, d), lambda bi, i: (bi, i, 0)),
                  pl.BlockSpec((1, 1, mod3.shape[-1]), lambda bi, i: (bi, 0, 0)),
                  pl.BlockSpec((1, d), lambda bi, i: (0, 0)),
                  _resident((d, n)),
                  pl.BlockSpec((CONV_WIDTH, LRU_WIDTH), lambda bi, i: (0, 0)),
                  pl.BlockSpec((1, LRU_WIDTH), lambda bi, i: (0, 0))],
        out_specs=out_specs,
        out_shape=out_shape,
        scratch_shapes=[pltpu.VMEM((d // LANE, tm, LANE), F32), pltpu.VMEM((CONV_TAIL + tm, LRU_WIDTH), F32)],
        compiler_params=_cparams(("parallel", "arbitrary"), VMEM_LIMIT),
        name="projection",
    )(x, mod3, g1, w_r, conv_w, conv_b.reshape(1, LRU_WIDTH))


def _attn_kernel(q0, k0, v0, q1, k1, v1, q2, k2, v2, o_ref, acc_ref, lse_ref, bias_ref, *, seq):
    hcols = o_ref.shape[-1]
    n_head = hcols // HEAD_DIM
    head_of_lane = lax.broadcasted_iota(jnp.int32, (SPAN, hcols), 1) // HEAD_DIM
    head_mask_b = [jnp.where(head_of_lane == h, 1.0, 0.0).astype(BF16) for h in range(n_head)]

    def by_head(parts):
        out = parts[n_head - 1]
        for h in range(n_head - 2, -1, -1):
            out = jnp.where(head_of_lane == h, parts[h], out)
        return out

    qi = lax.broadcasted_iota(jnp.int32, (n_head * SPAN, 2 * SPAN), 0) % SPAN
    ki = lax.broadcasted_iota(jnp.int32, (n_head * SPAN, 2 * SPAN), 1)
    band = (ki >= qi) & (ki <= qi + SPAN)
    bias_ref[0] = jnp.where(band, 0.0, -jnp.inf)
    bias_ref[1] = jnp.where(band & (ki >= SPAN), 0.0, -jnp.inf)

    for g, (q_ref, k_ref, v_ref) in enumerate(((q0, k0, v0), (q1, k1, v1), (q2, k2, v2))):
        d = DILATIONS[g]
        n_blk = seq // d // SPAN

        def tile(n, carry, q_ref=q_ref, k_ref=k_ref, v_ref=v_ref, d=d, n_blk=n_blk, g=g):
            p = n // n_blk
            blk = n % n_blk
            r0 = pl.multiple_of(blk * SPAN, SPAN)
            rp = pl.multiple_of(jnp.maximum(blk - 1, 0) * SPAN, SPAN)
            q = q_ref[p, pl.ds(r0, SPAN), :]
            kk = jnp.concatenate([k_ref[p, pl.ds(rp, SPAN), :], k_ref[p, pl.ds(r0, SPAN), :]], axis=0)
            vv = jnp.concatenate([v_ref[p, pl.ds(rp, SPAN), :], v_ref[p, pl.ds(r0, SPAN), :]], axis=0)
            qs = jnp.concatenate([q * head_mask_b[h] for h in range(n_head)], axis=0)
            sc = lax.dot_general(qs, kk, (((1,), (1,)), ((), ())), preferred_element_type=F32)
            sc = sc + bias_ref[jnp.where(blk > 0, 0, 1)]
            mx = jnp.max(sc, axis=-1, keepdims=True)
            e = jnp.exp2(sc - mx)
            den = jnp.sum(e, axis=-1, keepdims=True)
            pv = jnp.dot(e.astype(BF16), vv, preferred_element_type=F32)
            lse = mx + jnp.log(den) * LOG2_E
            rows_of = lambda a: [a[h * SPAN:(h + 1) * SPAN] for h in range(n_head)]
            o = by_head(rows_of(pv)) / by_head(rows_of(den))
            l = by_head(rows_of(lse))
            start = p + d * r0
            for j in range(hcols // LANE):
                rows = pl.ds(start, SPAN, stride=d) if d > 1 else pl.ds(start, SPAN)
                acc_ref[g, j, rows, :] = o[:, j * LANE:(j + 1) * LANE]
                lse_ref[g, j, rows, :] = l[:, j * LANE:(j + 1) * LANE]
            return carry

        lax.fori_loop(0, seq // SPAN, tile, 0, unroll=8)

    chunk = 256

    def combine(c, carry):
        r = pl.multiple_of(c * chunk, chunk)
        for j in range(hcols // LANE):
            ls = [lse_ref[g, j, pl.ds(r, chunk), :] for g in range(len(DILATIONS))]
            mx = jnp.maximum(jnp.maximum(ls[0], ls[1]), ls[2])
            ws = [jnp.exp2(v - mx) for v in ls]
            num = ws[0] * acc_ref[0, j, pl.ds(r, chunk), :]
            for g in range(1, len(DILATIONS)):
                num = num + ws[g] * acc_ref[g, j, pl.ds(r, chunk), :]
            o_ref[0, pl.ds(r, chunk), j * LANE:(j + 1) * LANE] = (num / (ws[0] + ws[1] + ws[2])).astype(BF16)
        return carry

    lax.fori_loop(0, seq // chunk, combine, 0)


def _attention(qkvs, b, s):
    hcols = 4 * HEAD_DIM
    n_hg = ATTN_OUT // hcols
    ncb = ATTN_OUT // hcols
    in_specs, args = [], []
    for g, d in enumerate(DILATIONS):
        for part in range(3):
            in_specs.append(pl.BlockSpec((d, s // d, hcols),
                                         lambda bi, hg, part=part: (bi, 0, part * ncb + hg)))
            args.append(qkvs[g])
    return pl.pallas_call(
        functools.partial(_attn_kernel, seq=s),
        grid=(b, n_hg),
        in_specs=in_specs,
        out_specs=pl.BlockSpec((1, s, hcols), lambda bi, hg: (bi, 0, hg)),
        out_shape=jax.ShapeDtypeStruct((b, s, ATTN_OUT), BF16),
        scratch_shapes=[pltpu.VMEM((len(DILATIONS), hcols // LANE, s, LANE), F32),
                        pltpu.VMEM((len(DILATIONS), hcols // LANE, s, LANE), F32),
                        pltpu.VMEM((2, (hcols // HEAD_DIM) * SPAN, 2 * SPAN), F32)],
        compiler_params=_cparams(("parallel", "parallel"), VMEM_LIMIT),
        name="dilated_attention",
    )(*args)


def _lru_kernel(xc_ref, gy_ref, wg_ref, bx_ref, ba_ref, lam_ref, o_ref, a_ref, b_ref, h_ref, *, pitch):
    nb, ts, tc = xc_ref.shape
    nl = tc // LANE

    @pl.when(pl.program_id(1) == 0)
    def _():
        h_ref[...] = jnp.zeros_like(h_ref)

    xb = xc_ref[...].reshape(nb * ts, tc)
    xc = xb.astype(F32)
    gates = jnp.dot(xb, wg_ref[0], preferred_element_type=F32)
    gate_i = jax.nn.sigmoid(gates[:, :tc] + bx_ref[...])
    gate_r = jax.nn.sigmoid(gates[:, tc:] + ba_ref[...])
    neg_lam = -lam_ref[...]
    softplus = jnp.maximum(neg_lam, 0.0) + jnp.log1p(jnp.exp(-jnp.abs(neg_lam)))
    log_a = (-LRU_C) * gate_r * softplus
    a = jnp.exp(log_a)
    one_m_a2 = jnp.tanh(-log_a) * (1.0 + a * a)
    mult = jnp.where(one_m_a2 > 0.0, one_m_a2 * lax.rsqrt(one_m_a2), 0.0)
    bv = mult * gate_i * xc
    for bi in range(nb):
        for j in range(nl):
            a_ref[j, pl.ds(bi * pitch, ts), :] = a[bi * ts:(bi + 1) * ts, j * LANE:(j + 1) * LANE]
            b_ref[j, pl.ds(bi * pitch, ts), :] = bv[bi * ts:(bi + 1) * ts, j * LANE:(j + 1) * LANE]

    def step(t, hs):
        out = []
        for j in range(nl):
            rows = pl.ds(t, nb, stride=pitch)
            h = a_ref[j, rows, :] * hs[j] + b_ref[j, rows, :]
            b_ref[j, rows, :] = h
            out.append(h)
        return tuple(out)

    hs = lax.fori_loop(0, ts, step, tuple(h_ref[j] for j in range(nl)), unroll=8)
    for j in range(nl):
        h_ref[j] = hs[j]
    for bi in range(nb):
        h = jnp.concatenate([b_ref[j, pl.ds(bi * pitch, ts), :] for j in range(nl)], axis=1)
        o_ref[bi] = (h * gy_ref[bi].astype(F32)).astype(BF16)


def _lru_gate_weights(wx, wa, tc):
    nb, bd, _ = wx.shape
    per = tc // bd
    eye = jnp.eye(per, dtype=wx.dtype)

    def bdiag(w):
        w = w.reshape(nb // per, per, bd, bd)
        return jnp.einsum('cpio,pq->cpiqo', w, eye).reshape(nb // per, tc, tc)

    return jnp.concatenate([bdiag(wx), bdiag(wa)], axis=-1).astype(BF16)


def _lru_branch(xc, gy, wx, bx, wa, ba, lam, *, tc=256, ts=128):
    b, s, c = xc.shape
    assert s % ts == 0 and c % tc == 0
    wg = _lru_gate_weights(wx, wa, tc)
    row = lambda v: v.reshape(1, c)
    tile = pl.BlockSpec((b, ts, tc), lambda ci, ti: (0, ti, ci))
    vec = pl.BlockSpec((1, tc), lambda ci, ti: (0, ci))
    pitch = ts + 8
    return pl.pallas_call(
        functools.partial(_lru_kernel, pitch=pitch),
        grid=(c // tc, s // ts),
        in_specs=[tile, tile,
                  pl.BlockSpec((1, tc, 2 * tc), lambda ci, ti: (ci, 0, 0)),
                  vec, vec, vec],
        out_specs=tile,
        out_shape=jax.ShapeDtypeStruct((b, s, c), BF16),
        scratch_shapes=[pltpu.VMEM((tc // LANE, b * pitch, LANE), F32),
                        pltpu.VMEM((tc // LANE, b * pitch, LANE), F32),
                        pltpu.VMEM((tc // LANE, b, LANE), F32)],
        compiler_params=_cparams(("parallel", "arbitrary"), VMEM_LIMIT),
        name="rg_lru",
    )(xc, gy, wg, row(bx), row(ba), row(lam))


def _prep_w_in(w_in):
    a = ATTN_WIDTH
    gw = N_SLOTS * HEAD_DIM
    q = w_in[:, :a] * (HEAD_DIM ** -0.5 * LOG2_E)
    k = w_in[:, a:2 * a]
    v = w_in[:, 2 * a:3 * a]
    parts = []
    for g in range(len(DILATIONS)):
        sl = slice(g * gw, (g + 1) * gw)
        parts += [q[:, sl], k[:, sl], v[:, sl]]
    parts.append(w_in[:, 3 * a:3 * a + 2 * LRU_WIDTH])
    return jnp.concatenate(parts, axis=1).astype(BF16), w_in[:, 3 * a + 2 * LRU_WIDTH:].astype(BF16)


ROW_SUBLANES = D_MODEL // 2 // LANE


def _store_tile_rows(ref, v, row0=0):
    n, half = v.shape[0], v.shape[1] // 2
    lo = pltpu.bitcast(v[:, :half].astype(BF16).astype(F32), jnp.uint32)
    hi = pltpu.bitcast(v[:, half:].astype(BF16).astype(F32), jnp.uint32)
    words = (hi & jnp.uint32(0xFFFF0000)) | (lo >> 16)
    for j in range(ROW_SUBLANES):
        ref[pl.ds(row0 * ROW_SUBLANES + j, n, stride=ROW_SUBLANES), :] = words[:, j * LANE:(j + 1) * LANE]


def _load_tile_rows(ref, n=None):
    n = ref.shape[0] // ROW_SUBLANES if n is None else n
    words = [ref[pl.ds(j, n, stride=ROW_SUBLANES), :] for j in range(ROW_SUBLANES)]
    lo = [pltpu.bitcast(w << 16, F32) for w in words]
    hi = [pltpu.bitcast(w & jnp.uint32(0xFFFF0000), F32) for w in words]
    return jnp.concatenate(lo + hi, axis=-1)


SC_CORES, SC_SUBCORES = 2, 16
SC_CHUNK = 128


def _sc_gather_rows(table, idx):
    n = idx.shape[0]
    per_worker = n // (SC_CORES * SC_SUBCORES)
    n_chunks = per_worker // SC_CHUNK
    assert n_chunks * SC_CHUNK * SC_CORES * SC_SUBCORES == n
    mesh = plsc.VectorSubcoreMesh(core_axis_name="c", subcore_axis_name="s")

    def body(table_hbm, idx_hbm, out_hbm, idx_v, rows_v, sem):
        base = (lax.axis_index("s") * SC_CORES + lax.axis_index("c")) * per_worker

        @pl.loop(0, n_chunks)
        def _(i):
            off = pl.multiple_of(base + i * SC_CHUNK, SC_CHUNK)
            pltpu.sync_copy(idx_hbm.at[pl.ds(off, SC_CHUNK)], idx_v)
            pltpu.async_copy(table_hbm.at[idx_v], rows_v, sem).wait()
            pltpu.sync_copy(rows_v, out_hbm.at[pl.ds(off, SC_CHUNK)])

    return pl.kernel(
        body, mesh=mesh,
        out_type=jax.ShapeDtypeStruct((n,) + table.shape[1:], table.dtype),
        scratch_types=[pltpu.VMEM((SC_CHUNK,), jnp.int32),
                       pltpu.VMEM((SC_CHUNK,) + table.shape[1:], table.dtype),
                       pltpu.SemaphoreType.DMA],
        name="sc_gather_rows",
    )(table, idx)


def _sc_scatter_rows(rows, idx, n_out):
    n_rows = rows.shape[0]
    n_choice = idx.shape[0] // n_rows
    per_worker = n_rows // (SC_CORES * SC_SUBCORES)
    n_chunks = per_worker // SC_CHUNK
    assert n_chunks * SC_CHUNK * SC_CORES * SC_SUBCORES == n_rows and n_choice * n_rows == idx.shape[0]
    mesh = plsc.VectorSubcoreMesh(core_axis_name="c", subcore_axis_name="s")

    def body(rows_hbm, idx_hbm, out_hbm, idx_v, rows_v):
        base = (lax.axis_index("s") * SC_CORES + lax.axis_index("c")) * per_worker

        @pl.loop(0, n_chunks)
        def _(i):
            off = pl.multiple_of(base + i * SC_CHUNK, SC_CHUNK)
            pltpu.sync_copy(rows_hbm.at[pl.ds(off, SC_CHUNK)], rows_v)
            for k in range(n_choice):
                pltpu.sync_copy(idx_hbm.at[pl.ds(k * n_rows + off, SC_CHUNK)], idx_v)
                pltpu.sync_copy(rows_v, out_hbm.at[idx_v])

    return pl.kernel(
        body, mesh=mesh,
        out_type=jax.ShapeDtypeStruct((n_out,) + rows.shape[1:], rows.dtype),
        scratch_types=[pltpu.VMEM((SC_CHUNK,), jnp.int32),
                       pltpu.VMEM((SC_CHUNK,) + rows.shape[1:], rows.dtype)],
        name="sc_scatter_rows",
    )(rows, idx)


ROUTE_ROWS = 8
EXPERT_ROW0 = N_GROUPS
ROUTER_ROWS = 48


def _mix_kernel(attn_ref, lru_ref, x_ref, mod_ref, g1_ref, wg_ref, wa_ref, wl_ref, wo_ref, g2_ref,
                wrt_ref, brt_ref, x1_ref, h2_ref, route_ref, cnt_ref, cnt_acc):
    d = x_ref.shape[-1]
    tm = x_ref.shape[1]

    @pl.when((pl.program_id(0) == 0) & (pl.program_id(1) == 0))
    def _():
        cnt_acc[...] = jnp.zeros_like(cnt_acc)

    m = mod_ref[0]
    gate1, shift2, scale2 = m[:, 2 * d:3 * d], m[:, 3 * d:4 * d], m[:, 4 * d:5 * d]
    x = x_ref[0]
    h1 = _rms_mod(x, g1_ref[...], m[:, d:2 * d], m[:, 0:d]).astype(BF16)
    gates = jax.nn.sigmoid(jnp.dot(h1, wg_ref[...], preferred_element_type=F32))
    ya = jnp.dot(attn_ref[0], wa_ref[...], preferred_element_type=F32)
    yl = jnp.dot(lru_ref[0], wl_ref[...], preferred_element_type=F32)
    mixed = gates[:, :d] * ya + gates[:, d:] * yl
    y = jnp.dot(mixed.astype(BF16), wo_ref[...], preferred_element_type=F32)
    x1 = x + (1.0 + gate1) * y
    x1_ref[0] = x1.astype(BF16)
    h2 = _rms_mod(x1, g2_ref[...], scale2, shift2)
    _store_tile_rows(h2_ref, h2)
    logits = lax.dot_general(wrt_ref[...], h2.astype(BF16), (((1,), (1,)), ((), ())),
                             preferred_element_type=F32) + brt_ref[...]

    row = lax.broadcasted_iota(jnp.int32, logits.shape, 0)
    neg = -jnp.inf

    def top(vals):
        mx = jnp.max(vals, axis=0, keepdims=True)
        idx = jnp.min(jnp.where(vals == mx, row, ROUTER_ROWS), axis=0, keepdims=True)
        return mx, idx

    is_grp = row < N_GROUPS
    gmax, gidx = top(jnp.where(is_grp, logits, neg))
    grp_gate = 1.0 / jnp.sum(jnp.where(is_grp, jnp.exp(logits - gmax), 0.0), axis=0, keepdims=True)
    lo = EXPERT_ROW0 + EXPERTS_PER_GROUP * gidx
    el = jnp.where((row >= lo) & (row < lo + EXPERTS_PER_GROUP), logits, neg)
    v1, i1 = top(el)
    v2, i2 = top(jnp.where(row == i1, neg, el))
    e21 = jnp.exp(v2 - v1)
    wt1 = grp_gate / (1.0 + e21)
    wt2 = wt1 * e21

    oh1 = jnp.where(row == i1, 1.0, 0.0)
    oh2 = jnp.where(row == i2, 1.0, 0.0)
    ohs = oh1 + oh2
    rr = lax.broadcasted_iota(jnp.int32, (tm, tm), 0)
    cc = lax.broadcasted_iota(jnp.int32, (tm, tm), 1)
    earlier = jnp.where(rr < cc, 1.0, 0.0).astype(BF16)
    before = jnp.dot(ohs.astype(BF16), earlier, preferred_element_type=F32) + cnt_acc[...]
    rank1 = jnp.sum(oh1 * before, axis=0, keepdims=True)
    rank2 = jnp.sum(oh2 * before, axis=0, keepdims=True)
    cnt_acc[...] = cnt_acc[...] + jnp.sum(ohs, axis=1, keepdims=True)
    cnt_ref[...] = cnt_acc[...]

    vals = [(i1 - EXPERT_ROW0).astype(F32), (i2 - EXPERT_ROW0).astype(F32), rank1, rank2, wt1, wt2]
    out_row = lax.broadcasted_iota(jnp.int32, (ROUTE_ROWS, tm), 0)
    slab = jnp.zeros((ROUTE_ROWS, tm), F32)
    for j, v in enumerate(vals):
        slab = jnp.where(out_row == j, v, slab)
    route_ref[...] = slab


def _mix_route(attn, lru, x, mod3, g1, wg, wa, wl, wo, g2, wrt, brt, b0, nb, *, tm=512):
    _, s, d = x.shape
    spt = s // tm
    tok_in = lambda w: pl.BlockSpec((1, tm, w), lambda bi, i: (b0 + bi, i, 0))
    return pl.pallas_call(
        _mix_kernel,
        grid=(nb, spt),
        in_specs=[tok_in(attn.shape[-1]), tok_in(d), tok_in(d),
                  pl.BlockSpec((1, 1, mod3.shape[-1]), lambda bi, i: (b0 + bi, 0, 0)),
                  pl.BlockSpec((1, d), lambda bi, i: (0, 0)),
                  _resident(wg.shape), _resident(wa.shape), _resident(wl.shape), _resident(wo.shape),
                  pl.BlockSpec((1, d), lambda bi, i: (0, 0)),
                  _resident(wrt.shape),
                  pl.BlockSpec((ROUTER_ROWS, 1), lambda bi, i: (0, 0))],
        out_specs=[pl.BlockSpec((1, tm, d), lambda bi, i: (bi, i, 0)),
                   pl.BlockSpec((tm * ROW_SUBLANES, LANE), lambda bi, i: (bi * spt + i, 0)),
                   pl.BlockSpec((ROUTE_ROWS, tm), lambda bi, i: (0, bi * spt + i)),
                   pl.BlockSpec((ROUTER_ROWS, 1), lambda bi, i: (0, 0))],
        out_shape=[jax.ShapeDtypeStruct((nb, s, d), BF16),
                   jax.ShapeDtypeStruct((nb * s * ROW_SUBLANES, LANE), jnp.uint32),
                   jax.ShapeDtypeStruct((ROUTE_ROWS, nb * s), F32),
                   jax.ShapeDtypeStruct((ROUTER_ROWS, 1), F32)],
        scratch_shapes=[pltpu.VMEM((ROUTER_ROWS, 1), F32)],
        compiler_params=_cparams(("arbitrary", "arbitrary"), VMEM_LIMIT),
        name="mix_route",
    )(attn, lru, x, mod3, g1, wg, wa, wl, wo, g2, wrt, brt)


TOP_K = 2
EXPERT_BLOCK = 512
MOE_BATCH_RANGES = 2


def _expert_kernel(be_ref, bv_ref, nu_ref, x_ref, w1_ref, w3_ref, w2_ref, y_ref, wb1, wb3, wb2):
    j = pl.program_id(0)
    half = EXPERT_BLOCK // 2

    def ffn(n):
        xb = _load_tile_rows(x_ref, n).astype(BF16)
        a = jnp.dot(xb, wb1[...], preferred_element_type=F32)
        g = jnp.dot(xb, wb3[...], preferred_element_type=F32)
        hm = (a * jax.nn.sigmoid(a) * g).astype(BF16)
        _store_tile_rows(y_ref, jnp.dot(hm, wb2[...], preferred_element_type=F32))

    @pl.when(j < nu_ref[0])
    def _():
        @pl.when((j == 0) | (be_ref[j] != be_ref[jnp.maximum(j - 1, 0)]))
        def _():
            wb1[...] = w1_ref[0].astype(BF16)
            wb3[...] = w3_ref[0].astype(BF16)
            wb2[...] = w2_ref[0].astype(BF16)

        @pl.when(bv_ref[j] > half)
        def _():
            ffn(EXPERT_BLOCK)

        @pl.when(bv_ref[j] <= half)
        def _():
            ffn(half)
            y_ref[half * ROW_SUBLANES:, :] = jnp.zeros((half * ROW_SUBLANES, LANE), y_ref.dtype)

    @pl.when(j >= nu_ref[0])
    def _():
        y_ref[...] = jnp.zeros_like(y_ref)


def _experts(xp, blk_e, blk_valid, n_used, w1, w3, w2):
    ne, d, de = w1.shape
    nb = xp.shape[0] // (EXPERT_BLOCK * ROW_SUBLANES)
    rows = (EXPERT_BLOCK * ROW_SUBLANES, LANE)
    last = lambda j, nu: jnp.minimum(j, nu[0] - 1)
    grid_spec = pltpu.PrefetchScalarGridSpec(
        num_scalar_prefetch=3,
        grid=(nb,),
        in_specs=[pl.BlockSpec(rows, lambda j, be, bv, nu: (last(j, nu), 0)),
                  pl.BlockSpec((1, d, de), lambda j, be, bv, nu: (be[j], 0, 0)),
                  pl.BlockSpec((1, d, de), lambda j, be, bv, nu: (be[j], 0, 0)),
                  pl.BlockSpec((1, de, d), lambda j, be, bv, nu: (be[j], 0, 0))],
        out_specs=pl.BlockSpec(rows, lambda j, be, bv, nu: (j, 0)),
        scratch_shapes=[pltpu.VMEM((d, de), BF16), pltpu.VMEM((d, de), BF16), pltpu.VMEM((de, d), BF16)])
    return pl.pallas_call(
        _expert_kernel,
        grid_spec=grid_spec,
        out_shape=jax.ShapeDtypeStruct(xp.shape, xp.dtype),
        compiler_params=_cparams(("arbitrary",), VMEM_LIMIT),
        name="experts",
    )(blk_e, blk_valid, n_used, xp, w1, w3, w2)


def _combine_kernel(y0_ref, y1_ref, route_ref, x1_ref, mod_ref, gf_ref, *rest):
    o_ref = rest[-1]
    tm, d = x1_ref.shape[1], x1_ref.shape[2]
    route = jnp.concatenate([route_ref[...], jnp.zeros((LANE - ROUTE_ROWS, tm), F32)], axis=0).T
    moe = _load_tile_rows(y0_ref) * route[:, 4:5] + _load_tile_rows(y1_ref) * route[:, 5:6]
    gate2 = mod_ref[0][:, 5 * d:6 * d]
    xo = x1_ref[0].astype(F32) + (1.0 + gate2) * moe
    ms = jnp.mean(xo * xo, axis=-1, keepdims=True)
    o_ref[0] = xo * lax.rsqrt(ms + EPS) * gf_ref[...]


def _combine(yg, route, x1, mod3, gf, b0, out_prev, *, tm=256):
    nb, s, d = x1.shape
    b_all = mod3.shape[0]
    spt = s // tm
    nt = nb * spt
    rows = (tm * ROW_SUBLANES, LANE)
    in_specs = [pl.BlockSpec(rows, lambda bi, i: (bi * spt + i, 0)),
                pl.BlockSpec(rows, lambda bi, i: (nt + bi * spt + i, 0)),
                pl.BlockSpec((ROUTE_ROWS, tm), lambda bi, i: (0, bi * spt + i)),
                pl.BlockSpec((1, tm, d), lambda bi, i: (bi, i, 0)),
                pl.BlockSpec((1, 1, mod3.shape[-1]), lambda bi, i: (b0 + bi, 0, 0)),
                pl.BlockSpec((1, d), lambda bi, i: (0, 0))]
    args = [yg, yg, route, x1, mod3, gf]
    aliases = {}
    if out_prev is not None:
        in_specs.append(pl.BlockSpec(memory_space=pl.ANY))
        aliases = {len(args): 0}
        args.append(out_prev)
    return pl.pallas_call(
        _combine_kernel,
        grid=(nb, spt),
        in_specs=in_specs,
        out_specs=pl.BlockSpec((1, tm, d), lambda bi, i: (b0 + bi, i, 0)),
        out_shape=jax.ShapeDtypeStruct((b_all, s, d), F32),
        input_output_aliases=aliases,
        compiler_params=_cparams(("parallel", "parallel"), VMEM_LIMIT),
        name="combine",
    )(*args)


def _slot_plan(route, counts, n_tok):
    sizes = counts[EXPERT_ROW0:EXPERT_ROW0 + N_EXPERTS, 0].astype(jnp.int32)
    padded = (sizes + EXPERT_BLOCK - 1) // EXPERT_BLOCK * EXPERT_BLOCK
    pad_ends = jnp.cumsum(padded)
    pad_starts = pad_ends - padded
    eid = route[0:TOP_K].astype(jnp.int32)
    rank = route[TOP_K:2 * TOP_K].astype(jnp.int32)
    start = jnp.sum(jnp.where(eid[..., None] == jnp.arange(N_EXPERTS), pad_starts, 0), axis=-1)
    dest = (start + rank).reshape(TOP_K * n_tok)
    n_blocks = (n_tok * TOP_K + N_EXPERTS * (EXPERT_BLOCK - 1) + EXPERT_BLOCK - 1) // EXPERT_BLOCK
    gran = SC_CORES * SC_SUBCORES * SC_CHUNK // math.gcd(SC_CORES * SC_SUBCORES * SC_CHUNK, EXPERT_BLOCK)
    n_blocks = (n_blocks + gran - 1) // gran * gran
    n_used = pad_ends[-1] // EXPERT_BLOCK
    blk = jnp.minimum(jnp.arange(n_blocks), n_used - 1)
    blk_e = jnp.minimum(jnp.sum(pad_ends[None, :] <= (blk * EXPERT_BLOCK)[:, None], axis=1), N_EXPERTS - 1)
    blk_valid = jnp.clip(sizes[blk_e] - (blk * EXPERT_BLOCK - pad_starts[blk_e]), 0, EXPERT_BLOCK)
    return (dest, n_blocks * EXPERT_BLOCK, blk_e.astype(jnp.int32), blk_valid.astype(jnp.int32),
            n_used.reshape(1).astype(jnp.int32))


def kernel(x, c, w_mod, b_mod, norm1_g, w_in, conv_w, conv_b, lru_wx, lru_bx, lru_wa, lru_ba, lru_lambda, w_attn_o, w_lru_o, w_out, norm2_g, w_grp, b_grp, w_exp, b_exp, w1, w3, w2, norm_f_g):
    b, s, d = x.shape
    assert d == D_MODEL and s == SPAN * DILATIONS[-1] and w_mod.shape[0] == 1
    mod3 = _modulation(c, w_mod[0], b_mod[0]).reshape(b, 1, 6 * d)
    w_proj, w_gate = _prep_w_in(w_in[0])
    g1 = norm1_g[0].reshape(1, d)
    qkv0, qkv1, qkv2, xc, gy = _projection(x, mod3, g1, w_proj, conv_w[0], conv_b[0])
    attn = _attention((qkv0, qkv1, qkv2), b, s)
    lru = _lru_branch(xc, gy, lru_wx[0], lru_bx[0], lru_wa[0], lru_ba[0], lru_lambda[0])

    n_pad = ROUTER_ROWS - N_GROUPS - N_EXPERTS
    wr = jnp.pad(jnp.concatenate([w_grp[0], w_exp[0]], axis=1).T, ((0, n_pad), (0, 0))).astype(BF16)
    br = jnp.pad(jnp.concatenate([b_grp[0], b_exp[0]]), (0, n_pad)).reshape(ROUTER_ROWS, 1)
    wa, wl, wo = w_attn_o[0].astype(BF16), w_lru_o[0].astype(BF16), w_out[0].astype(BF16)
    as_rows = lambda a: a.reshape(-1, ROW_SUBLANES, LANE)
    as_tiles = lambda a: a.reshape(-1, LANE)

    out = None
    nb = b // MOE_BATCH_RANGES
    for b0 in range(0, b, nb):
        x1, h2, route, counts = _mix_route(attn, lru, x, mod3, g1, w_gate, wa, wl, wo,
                                           norm2_g[0].reshape(1, d), wr, br, b0, nb)
        dest, n_slots, blk_e, blk_valid, n_used = _slot_plan(route, counts, nb * s)
        xp = as_tiles(_sc_scatter_rows(as_rows(h2), dest, n_slots))
        yp = _experts(xp, blk_e, blk_valid, n_used, w1[0], w3[0], w2[0])
        yg = as_tiles(_sc_gather_rows(as_rows(yp), dest))
        out = _combine(yg, route, x1, mod3, norm_f_g.reshape(1, d), b0, out)
    return out
```

```python
import functools
import math

import jax
import jax.numpy as jnp
from jax import lax
from jax.experimental import pallas as pl
from jax.experimental.pallas import tpu as pltpu
from jax.experimental.pallas import tpu_sc as plsc

F32 = jnp.float32
BF16 = jnp.bfloat16

D_MODEL = 1024
HEAD_DIM = 64
N_SLOTS = 8
SPAN = 128
DILATIONS = (1, 4, 16)
GROUP_COLS = 3 * N_SLOTS * HEAD_DIM
ATTN_WIDTH = len(DILATIONS) * N_SLOTS * HEAD_DIM
ATTN_OUT = N_SLOTS * HEAD_DIM
LRU_WIDTH = D_MODEL
LRU_BLOCK_DIM = 64
CONV_WIDTH = 4
CONV_TAIL = 8
LRU_C = 8.0
N_GROUPS = 4
EXPERTS_PER_GROUP = 8
N_EXPERTS = N_GROUPS * EXPERTS_PER_GROUP
D_EXPERT = D_MODEL // 2
EPS = 1e-6
LOG2_E = 1.4426950408889634
LANE = 128
VMEM_LIMIT = 56 * 1024 * 1024


def _cparams(sem, vmem=None):
    return pltpu.CompilerParams(dimension_semantics=sem, vmem_limit_bytes=vmem)


def _resident(shape):
    nd = len(shape)
    return pl.BlockSpec(shape, lambda *_: (0,) * nd, pipeline_mode=pl.Buffered(1))


def _mod_kernel(c_ref, w_ref, b_ref, o_ref):
    c = c_ref[...]
    ca = c * jax.nn.sigmoid(c)
    o_ref[...] = jnp.dot(ca.astype(BF16), w_ref[...].astype(BF16),
                         preferred_element_type=F32) + b_ref[...]


def _modulation(c, w_mod, b_mod):
    b, d = c.shape
    n = w_mod.shape[1]
    tn = n // 4
    return pl.pallas_call(
        _mod_kernel,
        grid=(n // tn,),
        in_specs=[pl.BlockSpec((b, d), lambda j: (0, 0)),
                  pl.BlockSpec((d, tn), lambda j: (0, j)),
                  pl.BlockSpec((1, tn), lambda j: (0, j))],
        out_specs=pl.BlockSpec((b, tn), lambda j: (0, j)),
        out_shape=jax.ShapeDtypeStruct((b, n), F32),
        compiler_params=_cparams(("arbitrary",)),
        name="modulation",
    )(c, w_mod, b_mod.reshape(1, n))


def _rms_mod(x, g, scale, shift):
    ms = jnp.mean(x * x, axis=-1, keepdims=True)
    return x * lax.rsqrt(ms + EPS) * g * (1.0 + scale) + shift


def _gelu_tanh(y):
    return y * (0.5 * (1.0 + jnp.tanh(0.7978845608028654 * (y + 0.044715 * (y * y * y)))))


def _proj_kernel(x_ref, mod_ref, g_ref, w_ref, cw_ref, cb_ref, qkv0_ref, qkv1_ref, qkv2_ref,
                 xc_ref, gy_ref, hs_ref, xe_ref, *, tm):
    @pl.when(pl.program_id(1) == 0)
    def _():
        xe_ref[0:CONV_TAIL, :] = jnp.zeros((CONV_TAIL, LRU_WIDTH), F32)

    @pl.when(pl.program_id(1) > 0)
    def _():
        xe_ref[0:CONV_TAIL, :] = xe_ref[tm:tm + CONV_TAIL, :]

    d_model = x_ref.shape[-1]
    m = mod_ref[0]
    h = _rms_mod(x_ref[0], g_ref[...], m[:, d_model:2 * d_model], m[:, 0:d_model])

    def mm(hv, lo, hi):
        return jnp.dot(hv, w_ref[:, lo:hi], preferred_element_type=F32)

    hb = h.astype(BF16)
    c0 = len(DILATIONS) * GROUP_COLS
    qkv0_ref[0] = mm(hb, 0, GROUP_COLS).astype(BF16)
    xr = mm(hb, c0, c0 + LRU_WIDTH)
    xe_ref[CONV_TAIL:, :] = xr
    cw = cw_ref[...]
    xc = xr * cw[CONV_WIDTH - 1:CONV_WIDTH] + cb_ref[...]
    for k in range(1, CONV_WIDTH):
        xc = xc + xe_ref[CONV_TAIL - k:CONV_TAIL - k + tm, :] * cw[CONV_WIDTH - 1 - k:CONV_WIDTH - k]
    xc_ref[0] = xc.astype(BF16)
    gy_ref[0] = _gelu_tanh(mm(hb, c0 + LRU_WIDTH, c0 + 2 * LRU_WIDTH)).astype(BF16)

    n_slab = d_model // LANE
    for j in range(n_slab):
        hs_ref[j] = h[:, j * LANE:(j + 1) * LANE]
    for g, out_ref in ((1, qkv1_ref), (2, qkv2_ref)):
        d = DILATIONS[g]
        rows = tm // d
        hp = jnp.concatenate(
            [jnp.concatenate([hs_ref[j, pl.ds(p, rows, stride=d), :] for j in range(n_slab)], axis=1)
             for p in range(d)], axis=0).astype(BF16)
        res = mm(hp, g * GROUP_COLS, (g + 1) * GROUP_COLS).astype(BF16)
        for p in range(d):
            out_ref[p] = res[p * rows:(p + 1) * rows]


def _projection(x, mod3, g1, w_r, conv_w, conv_b, *, tm=512):
    b, s, d = x.shape
    n = w_r.shape[1]
    assert s % tm == 0 and tm % (16 * DILATIONS[-1]) == 0 and CONV_TAIL >= CONV_WIDTH - 1
    out_shape = [jax.ShapeDtypeStruct((b * dd, s // dd, GROUP_COLS), BF16) for dd in DILATIONS]
    out_shape += [jax.ShapeDtypeStruct((b, s, LRU_WIDTH), BF16),
                  jax.ShapeDtypeStruct((b, s, LRU_WIDTH), BF16)]
    out_specs = [pl.BlockSpec((dd, tm // dd, GROUP_COLS), lambda bi, i: (bi, i, 0)) for dd in DILATIONS]
    out_specs += [pl.BlockSpec((1, tm, LRU_WIDTH), lambda bi, i: (bi, i, 0)),
                  pl.BlockSpec((1, tm, LRU_WIDTH), lambda bi, i: (bi, i, 0))]
    return pl.pallas_call(
        functools.partial(_proj_kernel, tm=tm),
        grid=(b, s // tm),
        in_specs=[pl.BlockSpec((1, tm, d), lambda bi, i: (bi, i, 0)),
                  pl.BlockSpec((1, 1, mod3.shape[-1]), lambda bi, i: (bi, 0, 0)),
                  pl.BlockSpec((1, d), lambda bi, i: (0, 0)),
                  _resident((d, n)),
                  pl.BlockSpec((CONV_WIDTH, LRU_WIDTH), lambda bi, i: (0, 0)),
                  pl.BlockSpec((1, LRU_WIDTH), lambda bi, i: (0, 0))],
        out_specs=out_specs,
        out_shape=out_shape,
        scratch_shapes=[pltpu.VMEM((d // LANE, tm, LANE), F32), pltpu.VMEM((CONV_TAIL + tm, LRU_WIDTH), F32)],
        compiler_params=_cparams(("parallel", "arbitrary"), VMEM_LIMIT),
        name="projection",
    )(x, mod3, g1, w_r, conv_w, conv_b.reshape(1, LRU_WIDTH))


def _attn_kernel(q0, k0, v0, q1, k1, v1, q2, k2, v2, o_ref, acc_ref, lse_ref, bias_ref, *, seq):
    hcols = o_ref.shape[-1]
    n_head = hcols // HEAD_DIM
    head_of_lane = lax.broadcasted_iota(jnp.int32, (SPAN, hcols), 1) // HEAD_DIM
    head_mask_b = [jnp.where(head_of_lane == h, 1.0, 0.0).astype(BF16) for h in range(n_head)]

    def by_head(parts):
        out = parts[n_head - 1]
        for h in range(n_head - 2, -1, -1):
            out = jnp.where(head_of_lane == h, parts[h], out)
        return out

    qi = lax.broadcasted_iota(jnp.int32, (n_head * SPAN, 2 * SPAN), 0) % SPAN
    ki = lax.broadcasted_iota(jnp.int32, (n_head * SPAN, 2 * SPAN), 1)
    band = (ki >= qi) & (ki <= qi + SPAN)
    bias_ref[0] = jnp.where(band, 0.0, -jnp.inf)
    bias_ref[1] = jnp.where(band & (ki >= SPAN), 0.0, -jnp.inf)

    for g, (q_ref, k_ref, v_ref) in enumerate(((q0, k0, v0), (q1, k1, v1), (q2, k2, v2))):
        d = DILATIONS[g]
        n_blk = seq // d // SPAN

        def tile(n, carry, q_ref=q_ref, k_ref=k_ref, v_ref=v_ref, d=d, n_blk=n_blk, g=g):
            p = n // n_blk
            blk = n % n_blk
            r0 = pl.multiple_of(blk * SPAN, SPAN)
            rp = pl.multiple_of(jnp.maximum(blk - 1, 0) * SPAN, SPAN)
            q = q_ref[p, pl.ds(r0, SPAN), :]
            kk = jnp.concatenate([k_ref[p, pl.ds(rp, SPAN), :], k_ref[p, pl.ds(r0, SPAN), :]], axis=0)
            vv = jnp.concatenate([v_ref[p, pl.ds(rp, SPAN), :], v_ref[p, pl.ds(r0, SPAN), :]], axis=0)
            qs = jnp.concatenate([q * head_mask_b[h] for h in range(n_head)], axis=0)
            sc = lax.dot_general(qs, kk, (((1,), (1,)), ((), ())), preferred_element_type=F32)
            sc = sc + bias_ref[jnp.where(blk > 0, 0, 1)]
            mx = jnp.max(sc, axis=-1, keepdims=True)
            e = jnp.exp2(sc - mx)
            den = jnp.sum(e, axis=-1, keepdims=True)
            pv = jnp.dot(e.astype(BF16), vv, preferred_element_type=F32)
            lse = mx + jnp.log(den) * LOG2_E
            rows_of = lambda a: [a[h * SPAN:(h + 1) * SPAN] for h in range(n_head)]
            o = by_head(rows_of(pv)) / by_head(rows_of(den))
            l = by_head(rows_of(lse))
            start = p + d * r0
            for j in range(hcols // LANE):
                rows = pl.ds(start, SPAN, stride=d) if d > 1 else pl.ds(start, SPAN)
                acc_ref[g, j, rows, :] = o[:, j * LANE:(j + 1) * LANE]
                lse_ref[g, j, rows, :] = l[:, j * LANE:(j + 1) * LANE]
            return carry

        lax.fori_loop(0, seq // SPAN, tile, 0, unroll=16)

    chunk = 256

    def combine(c, carry):
        r = pl.multiple_of(c * chunk, chunk)
        for j in range(hcols // LANE):
            ls = [lse_ref[g, j, pl.ds(r, chunk), :] for g in range(len(DILATIONS))]
            mx = jnp.maximum(jnp.maximum(ls[0], ls[1]), ls[2])
            ws = [jnp.exp2(v - mx) for v in ls]
            num = ws[0] * acc_ref[0, j, pl.ds(r, chunk), :]
            for g in range(1, len(DILATIONS)):
                num = num + ws[g] * acc_ref[g, j, pl.ds(r, chunk), :]
            o_ref[0, pl.ds(r, chunk), j * LANE:(j + 1) * LANE] = (num / (ws[0] + ws[1] + ws[2])).astype(BF16)
        return carry

    lax.fori_loop(0, seq // chunk, combine, 0)


def _attention(qkvs, b, s):
    hcols = 4 * HEAD_DIM
    n_hg = ATTN_OUT // hcols
    ncb = ATTN_OUT // hcols
    in_specs, args = [], []
    for g, d in enumerate(DILATIONS):
        for part in range(3):
            in_specs.append(pl.BlockSpec((d, s // d, hcols),
                                         lambda bi, hg, part=part: (bi, 0, part * ncb + hg)))
            args.append(qkvs[g])
    return pl.pallas_call(
        functools.partial(_attn_kernel, seq=s),
        grid=(b, n_hg),
        in_specs=in_specs,
        out_specs=pl.BlockSpec((1, s, hcols), lambda bi, hg: (bi, 0, hg)),
        out_shape=jax.ShapeDtypeStruct((b, s, ATTN_OUT), BF16),
        scratch_shapes=[pltpu.VMEM((len(DILATIONS), hcols // LANE, s, LANE), F32),
                        pltpu.VMEM((len(DILATIONS), hcols // LANE, s, LANE), F32),
                        pltpu.VMEM((2, (hcols // HEAD_DIM) * SPAN, 2 * SPAN), F32)],
        compiler_params=_cparams(("parallel", "parallel"), VMEM_LIMIT),
        name="dilated_attention",
    )(*args)


def _lru_kernel(xc_ref, gy_ref, wg_ref, bx_ref, ba_ref, lam_ref, o_ref, a_ref, b_ref, h_ref, *, pitch):
    nb, ts, tc = xc_ref.shape
    nl = tc // LANE

    @pl.when(pl.program_id(1) == 0)
    def _():
        h_ref[...] = jnp.zeros_like(h_ref)

    xb = xc_ref[...].reshape(nb * ts, tc)
    xc = xb.astype(F32)
    gates = jnp.dot(xb, wg_ref[0], preferred_element_type=F32)
    gate_i = jax.nn.sigmoid(gates[:, :tc] + bx_ref[...])
    gate_r = jax.nn.sigmoid(gates[:, tc:] + ba_ref[...])
    neg_lam = -lam_ref[...]
    softplus = jnp.maximum(neg_lam, 0.0) + jnp.log1p(jnp.exp(-jnp.abs(neg_lam)))
    log_a = (-LRU_C) * gate_r * softplus
    a = jnp.exp(log_a)
    one_m_a2 = jnp.tanh(-log_a) * (1.0 + a * a)
    mult = jnp.where(one_m_a2 > 0.0, one_m_a2 * lax.rsqrt(one_m_a2), 0.0)
    bv = mult * gate_i * xc
    for bi in range(nb):
        for j in range(nl):
            a_ref[j, pl.ds(bi * pitch, ts), :] = a[bi * ts:(bi + 1) * ts, j * LANE:(j + 1) * LANE]
            b_ref[j, pl.ds(bi * pitch, ts), :] = bv[bi * ts:(bi + 1) * ts, j * LANE:(j + 1) * LANE]

    def step(t, hs):
        out = []
        for j in range(nl):
            rows = pl.ds(t, nb, stride=pitch)
            h = a_ref[j, rows, :] * hs[j] + b_ref[j, rows, :]
            b_ref[j, rows, :] = h
            out.append(h)
        return tuple(out)

    hs = lax.fori_loop(0, ts, step, tuple(h_ref[j] for j in range(nl)), unroll=8)
    for j in range(nl):
        h_ref[j] = hs[j]
    for bi in range(nb):
        h = jnp.concatenate([b_ref[j, pl.ds(bi * pitch, ts), :] for j in range(nl)], axis=1)
        o_ref[bi] = (h * gy_ref[bi].astype(F32)).astype(BF16)


def _lru_gate_weights(wx, wa, tc):
    nb, bd, _ = wx.shape
    per = tc // bd
    eye = jnp.eye(per, dtype=wx.dtype)

    def bdiag(w):
        w = w.reshape(nb // per, per, bd, bd)
        return jnp.einsum('cpio,pq->cpiqo', w, eye).reshape(nb // per, tc, tc)

    return jnp.concatenate([bdiag(wx), bdiag(wa)], axis=-1).astype(BF16)


def _lru_branch(xc, gy, wx, bx, wa, ba, lam, *, tc=256, ts=128):
    b, s, c = xc.shape
    assert s % ts == 0 and c % tc == 0
    wg = _lru_gate_weights(wx, wa, tc)
    row = lambda v: v.reshape(1, c)
    tile = pl.BlockSpec((b, ts, tc), lambda ci, ti: (0, ti, ci))
    vec = pl.BlockSpec((1, tc), lambda ci, ti: (0, ci))
    pitch = ts + 8
    return pl.pallas_call(
        functools.partial(_lru_kernel, pitch=pitch),
        grid=(c // tc, s // ts),
        in_specs=[tile, tile,
                  pl.BlockSpec((1, tc, 2 * tc), lambda ci, ti: (ci, 0, 0)),
                  vec, vec, vec],
        out_specs=tile,
        out_shape=jax.ShapeDtypeStruct((b, s, c), BF16),
        scratch_shapes=[pltpu.VMEM((tc // LANE, b * pitch, LANE), F32),
                        pltpu.VMEM((tc // LANE, b * pitch, LANE), F32),
                        pltpu.VMEM((tc // LANE, b, LANE), F32)],
        compiler_params=_cparams(("parallel", "arbitrary"), VMEM_LIMIT),
        name="rg_lru",
    )(xc, gy, wg, row(bx), row(ba), row(lam))


def _prep_w_in(w_in):
    a = ATTN_WIDTH
    gw = N_SLOTS * HEAD_DIM
    q = w_in[:, :a] * (HEAD_DIM ** -0.5 * LOG2_E)
    k = w_in[:, a:2 * a]
    v = w_in[:, 2 * a:3 * a]
    parts = []
    for g in range(len(DILATIONS)):
        sl = slice(g * gw, (g + 1) * gw)
        parts += [q[:, sl], k[:, sl], v[:, sl]]
    parts.append(w_in[:, 3 * a:3 * a + 2 * LRU_WIDTH])
    return jnp.concatenate(parts, axis=1).astype(BF16), w_in[:, 3 * a + 2 * LRU_WIDTH:].astype(BF16)


ROW_SUBLANES = D_MODEL // 2 // LANE


def _store_tile_rows(ref, v, row0=0):
    n, half = v.shape[0], v.shape[1] // 2
    lo = pltpu.bitcast(v[:, :half].astype(BF16).astype(F32), jnp.uint32)
    hi = pltpu.bitcast(v[:, half:].astype(BF16).astype(F32), jnp.uint32)
    words = (hi & jnp.uint32(0xFFFF0000)) | (lo >> 16)
    for j in range(ROW_SUBLANES):
        ref[pl.ds(row0 * ROW_SUBLANES + j, n, stride=ROW_SUBLANES), :] = words[:, j * LANE:(j + 1) * LANE]


def _load_tile_rows(ref, n=None):
    n = ref.shape[0] // ROW_SUBLANES if n is None else n
    words = [ref[pl.ds(j, n, stride=ROW_SUBLANES), :] for j in range(ROW_SUBLANES)]
    lo = [pltpu.bitcast(w << 16, F32) for w in words]
    hi = [pltpu.bitcast(w & jnp.uint32(0xFFFF0000), F32) for w in words]
    return jnp.concatenate(lo + hi, axis=-1)


SC_CORES, SC_SUBCORES = 2, 16
SC_CHUNK = 128


def _sc_gather_rows(table, idx):
    n = idx.shape[0]
    per_worker = n // (SC_CORES * SC_SUBCORES)
    n_chunks = per_worker // SC_CHUNK
    assert n_chunks * SC_CHUNK * SC_CORES * SC_SUBCORES == n
    mesh = plsc.VectorSubcoreMesh(core_axis_name="c", subcore_axis_name="s")

    def body(table_hbm, idx_hbm, out_hbm, idx_v, rows_v, sem):
        base = (lax.axis_index("s") * SC_CORES + lax.axis_index("c")) * per_worker

        @pl.loop(0, n_chunks)
        def _(i):
            off = pl.multiple_of(base + i * SC_CHUNK, SC_CHUNK)
            pltpu.sync_copy(idx_hbm.at[pl.ds(off, SC_CHUNK)], idx_v)
            pltpu.async_copy(table_hbm.at[idx_v], rows_v, sem).wait()
            pltpu.sync_copy(rows_v, out_hbm.at[pl.ds(off, SC_CHUNK)])

    return pl.kernel(
        body, mesh=mesh,
        out_type=jax.ShapeDtypeStruct((n,) + table.shape[1:], table.dtype),
        scratch_types=[pltpu.VMEM((SC_CHUNK,), jnp.int32),
                       pltpu.VMEM((SC_CHUNK,) + table.shape[1:], table.dtype),
                       pltpu.SemaphoreType.DMA],
        name="sc_gather_rows",
    )(table, idx)


def _sc_scatter_rows(rows, idx, n_out):
    n_rows = rows.shape[0]
    n_choice = idx.shape[0] // n_rows
    per_worker = n_rows // (SC_CORES * SC_SUBCORES)
    n_chunks = per_worker // SC_CHUNK
    assert n_chunks * SC_CHUNK * SC_CORES * SC_SUBCORES == n_rows and n_choice * n_rows == idx.shape[0]
    mesh = plsc.VectorSubcoreMesh(core_axis_name="c", subcore_axis_name="s")

    def body(rows_hbm, idx_hbm, out_hbm, idx_v, rows_v):
        base = (lax.axis_index("s") * SC_CORES + lax.axis_index("c")) * per_worker

        @pl.loop(0, n_chunks)
        def _(i):
            off = pl.multiple_of(base + i * SC_CHUNK, SC_CHUNK)
            pltpu.sync_copy(rows_hbm.at[pl.ds(off, SC_CHUNK)], rows_v)
            for k in range(n_choice):
                pltpu.sync_copy(idx_hbm.at[pl.ds(k * n_rows + off, SC_CHUNK)], idx_v)
                pltpu.sync_copy(rows_v, out_hbm.at[idx_v])

    return pl.kernel(
        body, mesh=mesh,
        out_type=jax.ShapeDtypeStruct((n_out,) + rows.shape[1:], rows.dtype),
        scratch_types=[pltpu.VMEM((SC_CHUNK,), jnp.int32),
                       pltpu.VMEM((SC_CHUNK,) + rows.shape[1:], rows.dtype)],
        name="sc_scatter_rows",
    )(rows, idx)


ROUTE_ROWS = 8
EXPERT_ROW0 = N_GROUPS
ROUTER_ROWS = 48


def _mix_kernel(attn_ref, lru_ref, x_ref, mod_ref, g1_ref, wg_ref, wa_ref, wl_ref, wo_ref, g2_ref,
                wrt_ref, brt_ref, x1_ref, h2_ref, route_ref, cnt_ref, cnt_acc):
    d = x_ref.shape[-1]
    tm = x_ref.shape[1]

    @pl.when((pl.program_id(0) == 0) & (pl.program_id(1) == 0))
    def _():
        cnt_acc[...] = jnp.zeros_like(cnt_acc)

    m = mod_ref[0]
    gate1, shift2, scale2 = m[:, 2 * d:3 * d], m[:, 3 * d:4 * d], m[:, 4 * d:5 * d]
    x = x_ref[0]
    h1 = _rms_mod(x, g1_ref[...], m[:, d:2 * d], m[:, 0:d]).astype(BF16)
    gates = jax.nn.sigmoid(jnp.dot(h1, wg_ref[...], preferred_element_type=F32))
    ya = jnp.dot(attn_ref[0], wa_ref[...], preferred_element_type=F32)
    yl = jnp.dot(lru_ref[0], wl_ref[...], preferred_element_type=F32)
    mixed = gates[:, :d] * ya + gates[:, d:] * yl
    y = jnp.dot(mixed.astype(BF16), wo_ref[...], preferred_element_type=F32)
    x1 = x + (1.0 + gate1) * y
    x1_ref[0] = x1.astype(BF16)
    h2 = _rms_mod(x1, g2_ref[...], scale2, shift2)
    _store_tile_rows(h2_ref, h2)
    logits = lax.dot_general(wrt_ref[...], h2.astype(BF16), (((1,), (1,)), ((), ())),
                             preferred_element_type=F32) + brt_ref[...]

    row = lax.broadcasted_iota(jnp.int32, logits.shape, 0)
    neg = -jnp.inf

    def top(vals):
        mx = jnp.max(vals, axis=0, keepdims=True)
        idx = jnp.min(jnp.where(vals == mx, row, ROUTER_ROWS), axis=0, keepdims=True)
        return mx, idx

    is_grp = row < N_GROUPS
    gmax, gidx = top(jnp.where(is_grp, logits, neg))
    grp_gate = 1.0 / jnp.sum(jnp.where(is_grp, jnp.exp(logits - gmax), 0.0), axis=0, keepdims=True)
    lo = EXPERT_ROW0 + EXPERTS_PER_GROUP * gidx
    el = jnp.where((row >= lo) & (row < lo + EXPERTS_PER_GROUP), logits, neg)
    v1, i1 = top(el)
    v2, i2 = top(jnp.where(row == i1, neg, el))
    e21 = jnp.exp(v2 - v1)
    wt1 = grp_gate / (1.0 + e21)
    wt2 = wt1 * e21

    oh1 = jnp.where(row == i1, 1.0, 0.0)
    oh2 = jnp.where(row == i2, 1.0, 0.0)
    ohs = oh1 + oh2
    rr = lax.broadcasted_iota(jnp.int32, (tm, tm), 0)
    cc = lax.broadcasted_iota(jnp.int32, (tm, tm), 1)
    earlier = jnp.where(rr < cc, 1.0, 0.0).astype(BF16)
    before = jnp.dot(ohs.astype(BF16), earlier, preferred_element_type=F32) + cnt_acc[...]
    rank1 = jnp.sum(oh1 * before, axis=0, keepdims=True)
    rank2 = jnp.sum(oh2 * before, axis=0, keepdims=True)
    cnt_acc[...] = cnt_acc[...] + jnp.sum(ohs, axis=1, keepdims=True)
    cnt_ref[...] = cnt_acc[...]

    vals = [(i1 - EXPERT_ROW0).astype(F32), (i2 - EXPERT_ROW0).astype(F32), rank1, rank2, wt1, wt2]
    out_row = lax.broadcasted_iota(jnp.int32, (ROUTE_ROWS, tm), 0)
    slab = jnp.zeros((ROUTE_ROWS, tm), F32)
    for j, v in enumerate(vals):
        slab = jnp.where(out_row == j, v, slab)
    route_ref[...] = slab


def _mix_route(attn, lru, x, mod3, g1, wg, wa, wl, wo, g2, wrt, brt, b0, nb, *, tm=512):
    _, s, d = x.shape
    spt = s // tm
    tok_in = lambda w: pl.BlockSpec((1, tm, w), lambda bi, i: (b0 + bi, i, 0))
    return pl.pallas_call(
        _mix_kernel,
        grid=(nb, spt),
        in_specs=[tok_in(attn.shape[-1]), tok_in(d), tok_in(d),
                  pl.BlockSpec((1, 1, mod3.shape[-1]), lambda bi, i: (b0 + bi, 0, 0)),
                  pl.BlockSpec((1, d), lambda bi, i: (0, 0)),
                  _resident(wg.shape), _resident(wa.shape), _resident(wl.shape), _resident(wo.shape),
                  pl.BlockSpec((1, d), lambda bi, i: (0, 0)),
                  _resident(wrt.shape),
                  pl.BlockSpec((ROUTER_ROWS, 1), lambda bi, i: (0, 0))],
        out_specs=[pl.BlockSpec((1, tm, d), lambda bi, i: (bi, i, 0)),
                   pl.BlockSpec((tm * ROW_SUBLANES, LANE), lambda bi, i: (bi * spt + i, 0)),
                   pl.BlockSpec((ROUTE_ROWS, tm), lambda bi, i: (0, bi * spt + i)),
                   pl.BlockSpec((ROUTER_ROWS, 1), lambda bi, i: (0, 0))],
        out_shape=[jax.ShapeDtypeStruct((nb, s, d), BF16),
                   jax.ShapeDtypeStruct((nb * s * ROW_SUBLANES, LANE), jnp.uint32),
                   jax.ShapeDtypeStruct((ROUTE_ROWS, nb * s), F32),
                   jax.ShapeDtypeStruct((ROUTER_ROWS, 1), F32)],
        scratch_shapes=[pltpu.VMEM((ROUTER_ROWS, 1), F32)],
        compiler_params=_cparams(("arbitrary", "arbitrary"), VMEM_LIMIT),
        name="mix_route",
    )(attn, lru, x, mod3, g1, wg, wa, wl, wo, g2, wrt, brt)


TOP_K = 2
EXPERT_BLOCK = 512
MOE_BATCH_RANGES = 2


def _expert_kernel(be_ref, bv_ref, nu_ref, x_ref, w1_ref, w3_ref, w2_ref, y_ref, wb1, wb3, wb2):
    j = pl.program_id(0)
    half = EXPERT_BLOCK // 2

    def ffn(n):
        xb = _load_tile_rows(x_ref, n).astype(BF16)
        a = jnp.dot(xb, wb1[...], preferred_element_type=F32)
        g = jnp.dot(xb, wb3[...], preferred_element_type=F32)
        hm = (a * jax.nn.sigmoid(a) * g).astype(BF16)
        _store_tile_rows(y_ref, jnp.dot(hm, wb2[...], preferred_element_type=F32))

    @pl.when(j < nu_ref[0])
    def _():
        @pl.when((j == 0) | (be_ref[j] != be_ref[jnp.maximum(j - 1, 0)]))
        def _():
            wb1[...] = w1_ref[0].astype(BF16)
            wb3[...] = w3_ref[0].astype(BF16)
            wb2[...] = w2_ref[0].astype(BF16)

        @pl.when(bv_ref[j] > half)
        def _():
            ffn(EXPERT_BLOCK)

        @pl.when(bv_ref[j] <= half)
        def _():
            ffn(half)
            y_ref[half * ROW_SUBLANES:, :] = jnp.zeros((half * ROW_SUBLANES, LANE), y_ref.dtype)

    @pl.when(j >= nu_ref[0])
    def _():
        y_ref[...] = jnp.zeros_like(y_ref)


def _experts(xp, blk_e, blk_valid, n_used, w1, w3, w2):
    ne, d, de = w1.shape
    nb = xp.shape[0] // (EXPERT_BLOCK * ROW_SUBLANES)
    rows = (EXPERT_BLOCK * ROW_SUBLANES, LANE)
    last = lambda j, nu: jnp.minimum(j, nu[0] - 1)
    grid_spec = pltpu.PrefetchScalarGridSpec(
        num_scalar_prefetch=3,
        grid=(nb,),
        in_specs=[pl.BlockSpec(rows, lambda j, be, bv, nu: (last(j, nu), 0)),
                  pl.BlockSpec((1, d, de), lambda j, be, bv, nu: (be[j], 0, 0)),
                  pl.BlockSpec((1, d, de), lambda j, be, bv, nu: (be[j], 0, 0)),
                  pl.BlockSpec((1, de, d), lambda j, be, bv, nu: (be[j], 0, 0))],
        out_specs=pl.BlockSpec(rows, lambda j, be, bv, nu: (j, 0)),
        scratch_shapes=[pltpu.VMEM((d, de), BF16), pltpu.VMEM((d, de), BF16), pltpu.VMEM((de, d), BF16)])
    return pl.pallas_call(
        _expert_kernel,
        grid_spec=grid_spec,
        out_shape=jax.ShapeDtypeStruct(xp.shape, xp.dtype),
        compiler_params=_cparams(("arbitrary",), VMEM_LIMIT),
        name="experts",
    )(blk_e, blk_valid, n_used, xp, w1, w3, w2)


def _combine_kernel(y0_ref, y1_ref, route_ref, x1_ref, mod_ref, gf_ref, *rest):
    o_ref = rest[-1]
    tm, d = x1_ref.shape[1], x1_ref.shape[2]
    route = jnp.concatenate([route_ref[...], jnp.zeros((LANE - ROUTE_ROWS, tm), F32)], axis=0).T
    moe = _load_tile_rows(y0_ref) * route[:, 4:5] + _load_tile_rows(y1_ref) * route[:, 5:6]
    gate2 = mod_ref[0][:, 5 * d:6 * d]
    xo = x1_ref[0].astype(F32) + (1.0 + gate2) * moe
    ms = jnp.mean(xo * xo, axis=-1, keepdims=True)
    o_ref[0] = xo * lax.rsqrt(ms + EPS) * gf_ref[...]


def _combine(yg, route, x1, mod3, gf, b0, out_prev, *, tm=256):
    nb, s, d = x1.shape
    b_all = mod3.shape[0]
    spt = s // tm
    nt = nb * spt
    rows = (tm * ROW_SUBLANES, LANE)
    in_specs = [pl.BlockSpec(rows, lambda bi, i: (bi * spt + i, 0)),
                pl.BlockSpec(rows, lambda bi, i: (nt + bi * spt + i, 0)),
                pl.BlockSpec((ROUTE_ROWS, tm), lambda bi, i: (0, bi * spt + i)),
                pl.BlockSpec((1, tm, d), lambda bi, i: (bi, i, 0)),
                pl.BlockSpec((1, 1, mod3.shape[-1]), lambda bi, i: (b0 + bi, 0, 0)),
                pl.BlockSpec((1, d), lambda bi, i: (0, 0))]
    args = [yg, yg, route, x1, mod3, gf]
    aliases = {}
    if out_prev is not None:
        in_specs.append(pl.BlockSpec(memory_space=pl.ANY))
        aliases = {len(args): 0}
        args.append(out_prev)
    return pl.pallas_call(
        _combine_kernel,
        grid=(nb, spt),
        in_specs=in_specs,
        out_specs=pl.BlockSpec((1, tm, d), lambda bi, i: (b0 + bi, i, 0)),
        out_shape=jax.ShapeDtypeStruct((b_all, s, d), F32),
        input_output_aliases=aliases,
        compiler_params=_cparams(("parallel", "parallel"), VMEM_LIMIT),
        name="combine",
    )(*args)


def _slot_plan(route, counts, n_tok):
    sizes = counts[EXPERT_ROW0:EXPERT_ROW0 + N_EXPERTS, 0].astype(jnp.int32)
    padded = (sizes + EXPERT_BLOCK - 1) // EXPERT_BLOCK * EXPERT_BLOCK
    pad_ends = jnp.cumsum(padded)
    pad_starts = pad_ends - padded
    eid = route[0:TOP_K].astype(jnp.int32)
    rank = route[TOP_K:2 * TOP_K].astype(jnp.int32)
    start = jnp.sum(jnp.where(eid[..., None] == jnp.arange(N_EXPERTS), pad_starts, 0), axis=-1)
    dest = (start + rank).reshape(TOP_K * n_tok)
    n_blocks = (n_tok * TOP_K + N_EXPERTS * (EXPERT_BLOCK - 1) + EXPERT_BLOCK - 1) // EXPERT_BLOCK
    gran = SC_CORES * SC_SUBCORES * SC_CHUNK // math.gcd(SC_CORES * SC_SUBCORES * SC_CHUNK, EXPERT_BLOCK)
    n_blocks = (n_blocks + gran - 1) // gran * gran
    n_used = pad_ends[-1] // EXPERT_BLOCK
    blk = jnp.minimum(jnp.arange(n_blocks), n_used - 1)
    blk_e = jnp.minimum(jnp.sum(pad_ends[None, :] <= (blk * EXPERT_BLOCK)[:, None], axis=1), N_EXPERTS - 1)
    blk_valid = jnp.clip(sizes[blk_e] - (blk * EXPERT_BLOCK - pad_starts[blk_e]), 0, EXPERT_BLOCK)
    return (dest, n_blocks * EXPERT_BLOCK, blk_e.astype(jnp.int32), blk_valid.astype(jnp.int32),
            n_used.reshape(1).astype(jnp.int32))


def kernel(x, c, w_mod, b_mod, norm1_g, w_in, conv_w, conv_b, lru_wx, lru_bx, lru_wa, lru_ba, lru_lambda, w_attn_o, w_lru_o, w_out, norm2_g, w_grp, b_grp, w_exp, b_exp, w1, w3, w2, norm_f_g):
    b, s, d = x.shape
    assert d == D_MODEL and s == SPAN * DILATIONS[-1] and w_mod.shape[0] == 1
    mod3 = _modulation(c, w_mod[0], b_mod[0]).reshape(b, 1, 6 * d)
    w_proj, w_gate = _prep_w_in(w_in[0])
    g1 = norm1_g[0].reshape(1, d)
    qkv0, qkv1, qkv2, xc, gy = _projection(x, mod3, g1, w_proj, conv_w[0], conv_b[0])
    attn = _attention((qkv0, qkv1, qkv2), b, s)
    lru = _lru_branch(xc, gy, lru_wx[0], lru_bx[0], lru_wa[0], lru_ba[0], lru_lambda[0])

    n_pad = ROUTER_ROWS - N_GROUPS - N_EXPERTS
    wr = jnp.pad(jnp.concatenate([w_grp[0], w_exp[0]], axis=1).T, ((0, n_pad), (0, 0))).astype(BF16)
    br = jnp.pad(jnp.concatenate([b_grp[0], b_exp[0]]), (0, n_pad)).reshape(ROUTER_ROWS, 1)
    wa, wl, wo = w_attn_o[0].astype(BF16), w_lru_o[0].astype(BF16), w_out[0].astype(BF16)
    as_rows = lambda a: a.reshape(-1, ROW_SUBLANES, LANE)
    as_tiles = lambda a: a.reshape(-1, LANE)

    out = None
    nb = b // MOE_BATCH_RANGES
    for b0 in range(0, b, nb):
        x1, h2, route, counts = _mix_route(attn, lru, x, mod3, g1, w_gate, wa, wl, wo,
                                           norm2_g[0].reshape(1, d), wr, br, b0, nb)
        dest, n_slots, blk_e, blk_valid, n_used = _slot_plan(route, counts, nb * s)
        xp = as_tiles(_sc_scatter_rows(as_rows(h2), dest, n_slots))
        yp = _experts(xp, blk_e, blk_valid, n_used, w1[0], w3[0], w2[0])
        yg = as_tiles(_sc_gather_rows(as_rows(yp), dest))
        out = _combine(yg, route, x1, mod3, norm_f_g.reshape(1, d), b0, out)
    return out
```

```python
import functools
import math

import jax
import jax.numpy as jnp
from jax import lax
from jax.experimental import pallas as pl
from jax.experimental.pallas import tpu as pltpu
from jax.experimental.pallas import tpu_sc as plsc

F32 = jnp.float32
BF16 = jnp.bfloat16

D_MODEL = 1024
HEAD_DIM = 64
N_SLOTS = 8
SPAN = 128
DILATIONS = (1, 4, 16)
GROUP_COLS = 3 * N_SLOTS * HEAD_DIM
ATTN_WIDTH = len(DILATIONS) * N_SLOTS * HEAD_DIM
ATTN_OUT = N_SLOTS * HEAD_DIM
LRU_WIDTH = D_MODEL
LRU_BLOCK_DIM = 64
CONV_WIDTH = 4
CONV_TAIL = 8
LRU_C = 8.0
N_GROUPS = 4
EXPERTS_PER_GROUP = 8
N_EXPERTS = N_GROUPS * EXPERTS_PER_GROUP
D_EXPERT = D_MODEL // 2
EPS = 1e-6
LOG2_E = 1.4426950408889634
LANE = 128
VMEM_LIMIT = 56 * 1024 * 1024


def _cparams(sem, vmem=None):
    return pltpu.CompilerParams(dimension_semantics=sem, vmem_limit_bytes=vmem)


def _resident(shape):
    nd = len(shape)
    return pl.BlockSpec(shape, lambda *_: (0,) * nd, pipeline_mode=pl.Buffered(1))


def _mod_kernel(c_ref, w_ref, b_ref, o_ref):
    c = c_ref[...]
    ca = c * jax.nn.sigmoid(c)
    o_ref[...] = jnp.dot(ca.astype(BF16), w_ref[...].astype(BF16),
                         preferred_element_type=F32) + b_ref[...]


def _modulation(c, w_mod, b_mod):
    b, d = c.shape
    n = w_mod.shape[1]
    tn = n // 4
    return pl.pallas_call(
        _mod_kernel,
        grid=(n // tn,),
        in_specs=[pl.BlockSpec((b, d), lambda j: (0, 0)),
                  pl.BlockSpec((d, tn), lambda j: (0, j)),
                  pl.BlockSpec((1, tn), lambda j: (0, j))],
        out_specs=pl.BlockSpec((b, tn), lambda j: (0, j)),
        out_shape=jax.ShapeDtypeStruct((b, n), F32),
        compiler_params=_cparams(("arbitrary",)),
        name="modulation",
    )(c, w_mod, b_mod.reshape(1, n))


def _rms_mod(x, g, scale, shift):
    ms = jnp.mean(x * x, axis=-1, keepdims=True)
    return x * lax.rsqrt(ms + EPS) * g * (1.0 + scale) + shift


def _gelu_tanh(y):
    return y * (0.5 * (1.0 + jnp.tanh(0.7978845608028654 * (y + 0.044715 * (y * y * y)))))


def _proj_kernel(x_ref, mod_ref, g_ref, w_ref, cw_ref, cb_ref, qkv0_ref, qkv1_ref, qkv2_ref,
                 xc_ref, gy_ref, hs_ref, xe_ref, *, tm):
    @pl.when(pl.program_id(1) == 0)
    def _():
        xe_ref[0:CONV_TAIL, :] = jnp.zeros((CONV_TAIL, LRU_WIDTH), F32)

    @pl.when(pl.program_id(1) > 0)
    def _():
        xe_ref[0:CONV_TAIL, :] = xe_ref[tm:tm + CONV_TAIL, :]

    d_model = x_ref.shape[-1]
    m = mod_ref[0]
    h = _rms_mod(x_ref[0], g_ref[...], m[:, d_model:2 * d_model], m[:, 0:d_model])

    def mm(hv, lo, hi):
        return jnp.dot(hv, w_ref[:, lo:hi], preferred_element_type=F32)

    hb = h.astype(BF16)
    c0 = len(DILATIONS) * GROUP_COLS
    qkv0_ref[0] = mm(hb, 0, GROUP_COLS).astype(BF16)
    xr = mm(hb, c0, c0 + LRU_WIDTH)
    xe_ref[CONV_TAIL:, :] = xr
    cw = cw_ref[...]
    xc = xr * cw[CONV_WIDTH - 1:CONV_WIDTH] + cb_ref[...]
    for k in range(1, CONV_WIDTH):
        xc = xc + xe_ref[CONV_TAIL - k:CONV_TAIL - k + tm, :] * cw[CONV_WIDTH - 1 - k:CONV_WIDTH - k]
    xc_ref[0] = xc.astype(BF16)
    gy_ref[0] = _gelu_tanh(mm(hb, c0 + LRU_WIDTH, c0 + 2 * LRU_WIDTH)).astype(BF16)

    n_slab = d_model // LANE
    for j in range(n_slab):
        hs_ref[j] = h[:, j * LANE:(j + 1) * LANE]
    for g, out_ref in ((1, qkv1_ref), (2, qkv2_ref)):
        d = DILATIONS[g]
        rows = tm // d
        hp = jnp.concatenate(
            [jnp.concatenate([hs_ref[j, pl.ds(p, rows, stride=d), :] for j in range(n_slab)], axis=1)
             for p in range(d)], axis=0).astype(BF16)
        res = mm(hp, g * GROUP_COLS, (g + 1) * GROUP_COLS).astype(BF16)
        for p in range(d):
            out_ref[p] = res[p * rows:(p + 1) * rows]


def _projection(x, mod3, g1, w_r, conv_w, conv_b, *, tm=512):
    b, s, d = x.shape
    n = w_r.shape[1]
    assert s % tm == 0 and tm % (16 * DILATIONS[-1]) == 0 and CONV_TAIL >= CONV_WIDTH - 1
    out_shape = [jax.ShapeDtypeStruct((b * dd, s // dd, GROUP_COLS), BF16) for dd in DILATIONS]
    out_shape += [jax.ShapeDtypeStruct((b, s, LRU_WIDTH), BF16),
                  jax.ShapeDtypeStruct((b, s, LRU_WIDTH), BF16)]
    out_specs = [pl.BlockSpec((dd, tm // dd, GROUP_COLS), lambda bi, i: (bi, i, 0)) for dd in DILATIONS]
    out_specs += [pl.BlockSpec((1, tm, LRU_WIDTH), lambda bi, i: (bi, i, 0)),
                  pl.BlockSpec((1, tm, LRU_WIDTH), lambda bi, i: (bi, i, 0))]
    return pl.pallas_call(
        functools.partial(_proj_kernel, tm=tm),
        grid=(b, s // tm),
        in_specs=[pl.BlockSpec((1, tm, d), lambda bi, i: (bi, i, 0)),
                  pl.BlockSpec((1, 1, mod3.shape[-1]), lambda bi, i: (bi, 0, 0)),
                  pl.BlockSpec((1, d), lambda bi, i: (0, 0)),
                  _resident((d, n)),
                  pl.BlockSpec((CONV_WIDTH, LRU_WIDTH), lambda bi, i: (0, 0)),
                  pl.BlockSpec((1, LRU_WIDTH), lambda bi, i: (0, 0))],
        out_specs=out_specs,
        out_shape=out_shape,
        scratch_shapes=[pltpu.VMEM((d // LANE, tm, LANE), F32), pltpu.VMEM((CONV_TAIL + tm, LRU_WIDTH), F32)],
        compiler_params=_cparams(("parallel", "arbitrary"), VMEM_LIMIT),
        name="projection",
    )(x, mod3, g1, w_r, conv_w, conv_b.reshape(1, LRU_WIDTH))


def _attn_kernel(q0, k0, v0, q1, k1, v1, q2, k2, v2, o_ref, acc_ref, lse_ref, bias_ref, *, seq):
    hcols = o_ref.shape[-1]
    n_head = hcols // HEAD_DIM
    head_of_lane = lax.broadcasted_iota(jnp.int32, (SPAN, hcols), 1) // HEAD_DIM
    head_mask_b = [jnp.where(head_of_lane == h, 1.0, 0.0).astype(BF16) for h in range(n_head)]

    def by_head(parts):
        out = parts[n_head - 1]
        for h in range(n_head - 2, -1, -1):
            out = jnp.where(head_of_lane == h, parts[h], out)
        return out

    qi = lax.broadcasted_iota(jnp.int32, (n_head * SPAN, 2 * SPAN), 0) % SPAN
    ki = lax.broadcasted_iota(jnp.int32, (n_head * SPAN, 2 * SPAN), 1)
    band = (ki >= qi) & (ki <= qi + SPAN)
    bias_ref[0] = jnp.where(band, 0.0, -jnp.inf)
    bias_ref[1] = jnp.where(band & (ki >= SPAN), 0.0, -jnp.inf)

    for g, (q_ref, k_ref, v_ref) in enumerate(((q0, k0, v0), (q1, k1, v1), (q2, k2, v2))):
        d = DILATIONS[g]
        n_blk = seq // d // SPAN

        def tile(n, carry, q_ref=q_ref, k_ref=k_ref, v_ref=v_ref, d=d, n_blk=n_blk, g=g):
            p = n // n_blk
            blk = n % n_blk
            r0 = pl.multiple_of(blk * SPAN, SPAN)
            rp = pl.multiple_of(jnp.maximum(blk - 1, 0) * SPAN, SPAN)
            q = q_ref[p, pl.ds(r0, SPAN), :]
            kk = jnp.concatenate([k_ref[p, pl.ds(rp, SPAN), :], k_ref[p, pl.ds(r0, SPAN), :]], axis=0)
            vv = jnp.concatenate([v_ref[p, pl.ds(rp, SPAN), :], v_ref[p, pl.ds(r0, SPAN), :]], axis=0)
            qs = jnp.concatenate([q * head_mask_b[h] for h in range(n_head)], axis=0)
            sc = lax.dot_general(qs, kk, (((1,), (1,)), ((), ())), preferred_element_type=F32)
            sc = sc + bias_ref[jnp.where(blk > 0, 0, 1)]
            mx = jnp.max(sc, axis=-1, keepdims=True)
            e = jnp.exp2(sc - mx)
            den = jnp.sum(e, axis=-1, keepdims=True)
            pv = jnp.dot(e.astype(BF16), vv, preferred_element_type=F32)
            lse = mx + jnp.log(den) * LOG2_E
            rows_of = lambda a: [a[h * SPAN:(h + 1) * SPAN] for h in range(n_head)]
            o = by_head(rows_of(pv)) / by_head(rows_of(den))
            l = by_head(rows_of(lse))
            start = p + d * r0
            for j in range(hcols // LANE):
                rows = pl.ds(start, SPAN, stride=d) if d > 1 else pl.ds(start, SPAN)
                acc_ref[g, j, rows, :] = o[:, j * LANE:(j + 1) * LANE]
                lse_ref[g, j, rows, :] = l[:, j * LANE:(j + 1) * LANE]
            return carry

        lax.fori_loop(0, seq // SPAN, tile, 0, unroll=16)

    chunk = 256

    def combine(c, carry):
        r = pl.multiple_of(c * chunk, chunk)
        for j in range(hcols // LANE):
            ls = [lse_ref[g, j, pl.ds(r, chunk), :] for g in range(len(DILATIONS))]
            mx = jnp.maximum(jnp.maximum(ls[0], ls[1]), ls[2])
            ws = [jnp.exp2(v - mx) for v in ls]
            num = ws[0] * acc_ref[0, j, pl.ds(r, chunk), :]
            for g in range(1, len(DILATIONS)):
                num = num + ws[g] * acc_ref[g, j, pl.ds(r, chunk), :]
            o_ref[0, pl.ds(r, chunk), j * LANE:(j + 1) * LANE] = (num / (ws[0] + ws[1] + ws[2])).astype(BF16)
        return carry

    lax.fori_loop(0, seq // chunk, combine, 0)


def _attention(qkvs, b, s):
    hcols = 2 * HEAD_DIM
    n_hg = ATTN_OUT // hcols
    ncb = ATTN_OUT // hcols
    in_specs, args = [], []
    for g, d in enumerate(DILATIONS):
        for part in range(3):
            in_specs.append(pl.BlockSpec((d, s // d, hcols),
                                         lambda bi, hg, part=part: (bi, 0, part * ncb + hg)))
            args.append(qkvs[g])
    return pl.pallas_call(
        functools.partial(_attn_kernel, seq=s),
        grid=(b, n_hg),
        in_specs=in_specs,
        out_specs=pl.BlockSpec((1, s, hcols), lambda bi, hg: (bi, 0, hg)),
        out_shape=jax.ShapeDtypeStruct((b, s, ATTN_OUT), BF16),
        scratch_shapes=[pltpu.VMEM((len(DILATIONS), hcols // LANE, s, LANE), F32),
                        pltpu.VMEM((len(DILATIONS), hcols // LANE, s, LANE), F32),
                        pltpu.VMEM((2, (hcols // HEAD_DIM) * SPAN, 2 * SPAN), F32)],
        compiler_params=_cparams(("parallel", "parallel"), VMEM_LIMIT),
        name="dilated_attention",
    )(*args)


def _lru_kernel(xc_ref, gy_ref, wg_ref, bx_ref, ba_ref, lam_ref, o_ref, a_ref, b_ref, h_ref, *, pitch):
    nb, ts, tc = xc_ref.shape
    nl = tc // LANE

    @pl.when(pl.program_id(1) == 0)
    def _():
        h_ref[...] = jnp.zeros_like(h_ref)

    xb = xc_ref[...].reshape(nb * ts, tc)
    xc = xb.astype(F32)
    gates = jnp.dot(xb, wg_ref[0], preferred_element_type=F32)
    gate_i = jax.nn.sigmoid(gates[:, :tc] + bx_ref[...])
    gate_r = jax.nn.sigmoid(gates[:, tc:] + ba_ref[...])
    neg_lam = -lam_ref[...]
    softplus = jnp.maximum(neg_lam, 0.0) + jnp.log1p(jnp.exp(-jnp.abs(neg_lam)))
    log_a = (-LRU_C) * gate_r * softplus
    a = jnp.exp(log_a)
    one_m_a2 = jnp.tanh(-log_a) * (1.0 + a * a)
    mult = jnp.where(one_m_a2 > 0.0, one_m_a2 * lax.rsqrt(one_m_a2), 0.0)
    bv = mult * gate_i * xc
    for bi in range(nb):
        for j in range(nl):
            a_ref[j, pl.ds(bi * pitch, ts), :] = a[bi * ts:(bi + 1) * ts, j * LANE:(j + 1) * LANE]
            b_ref[j, pl.ds(bi * pitch, ts), :] = bv[bi * ts:(bi + 1) * ts, j * LANE:(j + 1) * LANE]

    def step(t, hs):
        out = []
        for j in range(nl):
            rows = pl.ds(t, nb, stride=pitch)
            h = a_ref[j, rows, :] * hs[j] + b_ref[j, rows, :]
            b_ref[j, rows, :] = h
            out.append(h)
        return tuple(out)

    hs = lax.fori_loop(0, ts, step, tuple(h_ref[j] for j in range(nl)), unroll=8)
    for j in range(nl):
        h_ref[j] = hs[j]
    for bi in range(nb):
        h = jnp.concatenate([b_ref[j, pl.ds(bi * pitch, ts), :] for j in range(nl)], axis=1)
        o_ref[bi] = (h * gy_ref[bi].astype(F32)).astype(BF16)


def _lru_gate_weights(wx, wa, tc):
    nb, bd, _ = wx.shape
    per = tc // bd
    eye = jnp.eye(per, dtype=wx.dtype)

    def bdiag(w):
        w = w.reshape(nb // per, per, bd, bd)
        return jnp.einsum('cpio,pq->cpiqo', w, eye).reshape(nb // per, tc, tc)

    return jnp.concatenate([bdiag(wx), bdiag(wa)], axis=-1).astype(BF16)


def _lru_branch(xc, gy, wx, bx, wa, ba, lam, *, tc=256, ts=128):
    b, s, c = xc.shape
    assert s % ts == 0 and c % tc == 0
    wg = _lru_gate_weights(wx, wa, tc)
    row = lambda v: v.reshape(1, c)
    tile = pl.BlockSpec((b, ts, tc), lambda ci, ti: (0, ti, ci))
    vec = pl.BlockSpec((1, tc), lambda ci, ti: (0, ci))
    pitch = ts + 8
    return pl.pallas_call(
        functools.partial(_lru_kernel, pitch=pitch),
        grid=(c // tc, s // ts),
        in_specs=[tile, tile,
                  pl.BlockSpec((1, tc, 2 * tc), lambda ci, ti: (ci, 0, 0)),
                  vec, vec, vec],
        out_specs=tile,
        out_shape=jax.ShapeDtypeStruct((b, s, c), BF16),
        scratch_shapes=[pltpu.VMEM((tc // LANE, b * pitch, LANE), F32),
                        pltpu.VMEM((tc // LANE, b * pitch, LANE), F32),
                        pltpu.VMEM((tc // LANE, b, LANE), F32)],
        compiler_params=_cparams(("parallel", "arbitrary"), VMEM_LIMIT),
        name="rg_lru",
    )(xc, gy, wg, row(bx), row(ba), row(lam))


def _prep_w_in(w_in):
    a = ATTN_WIDTH
    gw = N_SLOTS * HEAD_DIM
    q = w_in[:, :a] * (HEAD_DIM ** -0.5 * LOG2_E)
    k = w_in[:, a:2 * a]
    v = w_in[:, 2 * a:3 * a]
    parts = []
    for g in range(len(DILATIONS)):
        sl = slice(g * gw, (g + 1) * gw)
        parts += [q[:, sl], k[:, sl], v[:, sl]]
    parts.append(w_in[:, 3 * a:3 * a + 2 * LRU_WIDTH])
    return jnp.concatenate(parts, axis=1).astype(BF16), w_in[:, 3 * a + 2 * LRU_WIDTH:].astype(BF16)


ROW_SUBLANES = D_MODEL // 2 // LANE


def _store_tile_rows(ref, v, row0=0):
    n, half = v.shape[0], v.shape[1] // 2
    lo = pltpu.bitcast(v[:, :half].astype(BF16).astype(F32), jnp.uint32)
    hi = pltpu.bitcast(v[:, half:].astype(BF16).astype(F32), jnp.uint32)
    words = (hi & jnp.uint32(0xFFFF0000)) | (lo >> 16)
    for j in range(ROW_SUBLANES):
        ref[pl.ds(row0 * ROW_SUBLANES + j, n, stride=ROW_SUBLANES), :] = words[:, j * LANE:(j + 1) * LANE]


def _load_tile_rows(ref, n=None):
    n = ref.shape[0] // ROW_SUBLANES if n is None else n
    words = [ref[pl.ds(j, n, stride=ROW_SUBLANES), :] for j in range(ROW_SUBLANES)]
    lo = [pltpu.bitcast(w << 16, F32) for w in words]
    hi = [pltpu.bitcast(w & jnp.uint32(0xFFFF0000), F32) for w in words]
    return jnp.concatenate(lo + hi, axis=-1)


SC_CORES, SC_SUBCORES = 2, 16
SC_CHUNK = 128


def _sc_gather_rows(table, idx):
    n = idx.shape[0]
    per_worker = n // (SC_CORES * SC_SUBCORES)
    n_chunks = per_worker // SC_CHUNK
    assert n_chunks * SC_CHUNK * SC_CORES * SC_SUBCORES == n
    mesh = plsc.VectorSubcoreMesh(core_axis_name="c", subcore_axis_name="s")

    def body(table_hbm, idx_hbm, out_hbm, idx_v, rows_v, sem):
        base = (lax.axis_index("s") * SC_CORES + lax.axis_index("c")) * per_worker

        @pl.loop(0, n_chunks)
        def _(i):
            off = pl.multiple_of(base + i * SC_CHUNK, SC_CHUNK)
            pltpu.sync_copy(idx_hbm.at[pl.ds(off, SC_CHUNK)], idx_v)
            pltpu.async_copy(table_hbm.at[idx_v], rows_v, sem).wait()
            pltpu.sync_copy(rows_v, out_hbm.at[pl.ds(off, SC_CHUNK)])

    return pl.kernel(
        body, mesh=mesh,
        out_type=jax.ShapeDtypeStruct((n,) + table.shape[1:], table.dtype),
        scratch_types=[pltpu.VMEM((SC_CHUNK,), jnp.int32),
                       pltpu.VMEM((SC_CHUNK,) + table.shape[1:], table.dtype),
                       pltpu.SemaphoreType.DMA],
        name="sc_gather_rows",
    )(table, idx)


def _sc_scatter_rows(rows, idx, n_out):
    n_rows = rows.shape[0]
    n_choice = idx.shape[0] // n_rows
    per_worker = n_rows // (SC_CORES * SC_SUBCORES)
    n_chunks = per_worker // SC_CHUNK
    assert n_chunks * SC_CHUNK * SC_CORES * SC_SUBCORES == n_rows and n_choice * n_rows == idx.shape[0]
    mesh = plsc.VectorSubcoreMesh(core_axis_name="c", subcore_axis_name="s")

    def body(rows_hbm, idx_hbm, out_hbm, idx_v, rows_v):
        base = (lax.axis_index("s") * SC_CORES + lax.axis_index("c")) * per_worker

        @pl.loop(0, n_chunks)
        def _(i):
            off = pl.multiple_of(base + i * SC_CHUNK, SC_CHUNK)
            pltpu.sync_copy(rows_hbm.at[pl.ds(off, SC_CHUNK)], rows_v)
            for k in range(n_choice):
                pltpu.sync_copy(idx_hbm.at[pl.ds(k * n_rows + off, SC_CHUNK)], idx_v)
                pltpu.sync_copy(rows_v, out_hbm.at[idx_v])

    return pl.kernel(
        body, mesh=mesh,
        out_type=jax.ShapeDtypeStruct((n_out,) + rows.shape[1:], rows.dtype),
        scratch_types=[pltpu.VMEM((SC_CHUNK,), jnp.int32),
                       pltpu.VMEM((SC_CHUNK,) + rows.shape[1:], rows.dtype)],
        name="sc_scatter_rows",
    )(rows, idx)


ROUTE_ROWS = 8
EXPERT_ROW0 = N_GROUPS
ROUTER_ROWS = 48


def _mix_kernel(attn_ref, lru_ref, x_ref, mod_ref, g1_ref, wg_ref, wa_ref, wl_ref, wo_ref, g2_ref,
                wrt_ref, brt_ref, x1_ref, h2_ref, route_ref, cnt_ref, cnt_acc):
    d = x_ref.shape[-1]
    tm = x_ref.shape[1]

    @pl.when((pl.program_id(0) == 0) & (pl.program_id(1) == 0))
    def _():
        cnt_acc[...] = jnp.zeros_like(cnt_acc)

    m = mod_ref[0]
    gate1, shift2, scale2 = m[:, 2 * d:3 * d], m[:, 3 * d:4 * d], m[:, 4 * d:5 * d]
    x = x_ref[0]
    h1 = _rms_mod(x, g1_ref[...], m[:, d:2 * d], m[:, 0:d]).astype(BF16)
    gates = jax.nn.sigmoid(jnp.dot(h1, wg_ref[...], preferred_element_type=F32))
    ya = jnp.dot(attn_ref[0], wa_ref[...], preferred_element_type=F32)
    yl = jnp.dot(lru_ref[0], wl_ref[...], preferred_element_type=F32)
    mixed = gates[:, :d] * ya + gates[:, d:] * yl
    y = jnp.dot(mixed.astype(BF16), wo_ref[...], preferred_element_type=F32)
    x1 = x + (1.0 + gate1) * y
    x1_ref[0] = x1.astype(BF16)
    h2 = _rms_mod(x1, g2_ref[...], scale2, shift2)
    _store_tile_rows(h2_ref, h2)
    logits = lax.dot_general(wrt_ref[...], h2.astype(BF16), (((1,), (1,)), ((), ())),
                             preferred_element_type=F32) + brt_ref[...]

    row = lax.broadcasted_iota(jnp.int32, logits.shape, 0)
    neg = -jnp.inf

    def top(vals):
        mx = jnp.max(vals, axis=0, keepdims=True)
        idx = jnp.min(jnp.where(vals == mx, row, ROUTER_ROWS), axis=0, keepdims=True)
        return mx, idx

    is_grp = row < N_GROUPS
    gmax, gidx = top(jnp.where(is_grp, logits, neg))
    grp_gate = 1.0 / jnp.sum(jnp.where(is_grp, jnp.exp(logits - gmax), 0.0), axis=0, keepdims=True)
    lo = EXPERT_ROW0 + EXPERTS_PER_GROUP * gidx
    el = jnp.where((row >= lo) & (row < lo + EXPERTS_PER_GROUP), logits, neg)
    v1, i1 = top(el)
    v2, i2 = top(jnp.where(row == i1, neg, el))
    e21 = jnp.exp(v2 - v1)
    wt1 = grp_gate / (1.0 + e21)
    wt2 = wt1 * e21

    oh1 = jnp.where(row == i1, 1.0, 0.0)
    oh2 = jnp.where(row == i2, 1.0, 0.0)
    ohs = oh1 + oh2
    rr = lax.broadcasted_iota(jnp.int32, (tm, tm), 0)
    cc = lax.broadcasted_iota(jnp.int32, (tm, tm), 1)
    earlier = jnp.where(rr < cc, 1.0, 0.0).astype(BF16)
    before = jnp.dot(ohs.astype(BF16), earlier, preferred_element_type=F32) + cnt_acc[...]
    rank1 = jnp.sum(oh1 * before, axis=0, keepdims=True)
    rank2 = jnp.sum(oh2 * before, axis=0, keepdims=True)
    cnt_acc[...] = cnt_acc[...] + jnp.sum(ohs, axis=1, keepdims=True)
    cnt_ref[...] = cnt_acc[...]

    vals = [(i1 - EXPERT_ROW0).astype(F32), (i2 - EXPERT_ROW0).astype(F32), rank1, rank2, wt1, wt2]
    out_row = lax.broadcasted_iota(jnp.int32, (ROUTE_ROWS, tm), 0)
    slab = jnp.zeros((ROUTE_ROWS, tm), F32)
    for j, v in enumerate(vals):
        slab = jnp.where(out_row == j, v, slab)
    route_ref[...] = slab


def _mix_route(attn, lru, x, mod3, g1, wg, wa, wl, wo, g2, wrt, brt, b0, nb, *, tm=512):
    _, s, d = x.shape
    spt = s // tm
    tok_in = lambda w: pl.BlockSpec((1, tm, w), lambda bi, i: (b0 + bi, i, 0))
    return pl.pallas_call(
        _mix_kernel,
        grid=(nb, spt),
        in_specs=[tok_in(attn.shape[-1]), tok_in(d), tok_in(d),
                  pl.BlockSpec((1, 1, mod3.shape[-1]), lambda bi, i: (b0 + bi, 0, 0)),
                  pl.BlockSpec((1, d), lambda bi, i: (0, 0)),
                  _resident(wg.shape), _resident(wa.shape), _resident(wl.shape), _resident(wo.shape),
                  pl.BlockSpec((1, d), lambda bi, i: (0, 0)),
                  _resident(wrt.shape),
                  pl.BlockSpec((ROUTER_ROWS, 1), lambda bi, i: (0, 0))],
        out_specs=[pl.BlockSpec((1, tm, d), lambda bi, i: (bi, i, 0)),
                   pl.BlockSpec((tm * ROW_SUBLANES, LANE), lambda bi, i: (bi * spt + i, 0)),
                   pl.BlockSpec((ROUTE_ROWS, tm), lambda bi, i: (0, bi * spt + i)),
                   pl.BlockSpec((ROUTER_ROWS, 1), lambda bi, i: (0, 0))],
        out_shape=[jax.ShapeDtypeStruct((nb, s, d), BF16),
                   jax.ShapeDtypeStruct((nb * s * ROW_SUBLANES, LANE), jnp.uint32),
                   jax.ShapeDtypeStruct((ROUTE_ROWS, nb * s), F32),
                   jax.ShapeDtypeStruct((ROUTER_ROWS, 1), F32)],
        scratch_shapes=[pltpu.VMEM((ROUTER_ROWS, 1), F32)],
        compiler_params=_cparams(("arbitrary", "arbitrary"), VMEM_LIMIT),
        name="mix_route",
    )(attn, lru, x, mod3, g1, wg, wa, wl, wo, g2, wrt, brt)


TOP_K = 2
EXPERT_BLOCK = 512
MOE_BATCH_RANGES = 2


def _expert_kernel(be_ref, bv_ref, nu_ref, x_ref, w1_ref, w3_ref, w2_ref, y_ref, wb1, wb3, wb2):
    j = pl.program_id(0)
    half = EXPERT_BLOCK // 2

    def ffn(n):
        xb = _load_tile_rows(x_ref, n).astype(BF16)
        a = jnp.dot(xb, wb1[...], preferred_element_type=F32)
        g = jnp.dot(xb, wb3[...], preferred_element_type=F32)
        hm = (a * jax.nn.sigmoid(a) * g).astype(BF16)
        _store_tile_rows(y_ref, jnp.dot(hm, wb2[...], preferred_element_type=F32))

    @pl.when(j < nu_ref[0])
    def _():
        @pl.when((j == 0) | (be_ref[j] != be_ref[jnp.maximum(j - 1, 0)]))
        def _():
            wb1[...] = w1_ref[0].astype(BF16)
            wb3[...] = w3_ref[0].astype(BF16)
            wb2[...] = w2_ref[0].astype(BF16)

        @pl.when(bv_ref[j] > half)
        def _():
            ffn(EXPERT_BLOCK)

        @pl.when(bv_ref[j] <= half)
        def _():
            ffn(half)
            y_ref[half * ROW_SUBLANES:, :] = jnp.zeros((half * ROW_SUBLANES, LANE), y_ref.dtype)

    @pl.when(j >= nu_ref[0])
    def _():
        y_ref[...] = jnp.zeros_like(y_ref)


def _experts(xp, blk_e, blk_valid, n_used, w1, w3, w2):
    ne, d, de = w1.shape
    nb = xp.shape[0] // (EXPERT_BLOCK * ROW_SUBLANES)
    rows = (EXPERT_BLOCK * ROW_SUBLANES, LANE)
    last = lambda j, nu: jnp.minimum(j, nu[0] - 1)
    grid_spec = pltpu.PrefetchScalarGridSpec(
        num_scalar_prefetch=3,
        grid=(nb,),
        in_specs=[pl.BlockSpec(rows, lambda j, be, bv, nu: (last(j, nu), 0)),
                  pl.BlockSpec((1, d, de), lambda j, be, bv, nu: (be[j], 0, 0)),
                  pl.BlockSpec((1, d, de), lambda j, be, bv, nu: (be[j], 0, 0)),
                  pl.BlockSpec((1, de, d), lambda j, be, bv, nu: (be[j], 0, 0))],
        out_specs=pl.BlockSpec(rows, lambda j, be, bv, nu: (j, 0)),
        scratch_shapes=[pltpu.VMEM((d, de), BF16), pltpu.VMEM((d, de), BF16), pltpu.VMEM((de, d), BF16)])
    return pl.pallas_call(
        _expert_kernel,
        grid_spec=grid_spec,
        out_shape=jax.ShapeDtypeStruct(xp.shape, xp.dtype),
        compiler_params=_cparams(("arbitrary",), VMEM_LIMIT),
        name="experts",
    )(blk_e, blk_valid, n_used, xp, w1, w3, w2)


def _combine_kernel(y0_ref, y1_ref, route_ref, x1_ref, mod_ref, gf_ref, *rest):
    o_ref = rest[-1]
    tm, d = x1_ref.shape[1], x1_ref.shape[2]
    route = jnp.concatenate([route_ref[...], jnp.zeros((LANE - ROUTE_ROWS, tm), F32)], axis=0).T
    moe = _load_tile_rows(y0_ref) * route[:, 4:5] + _load_tile_rows(y1_ref) * route[:, 5:6]
    gate2 = mod_ref[0][:, 5 * d:6 * d]
    xo = x1_ref[0].astype(F32) + (1.0 + gate2) * moe
    ms = jnp.mean(xo * xo, axis=-1, keepdims=True)
    o_ref[0] = xo * lax.rsqrt(ms + EPS) * gf_ref[...]


def _combine(yg, route, x1, mod3, gf, b0, out_prev, *, tm=256):
    nb, s, d = x1.shape
    b_all = mod3.shape[0]
    spt = s // tm
    nt = nb * spt
    rows = (tm * ROW_SUBLANES, LANE)
    in_specs = [pl.BlockSpec(rows, lambda bi, i: (bi * spt + i, 0)),
                pl.BlockSpec(rows, lambda bi, i: (nt + bi * spt + i, 0)),
                pl.BlockSpec((ROUTE_ROWS, tm), lambda bi, i: (0, bi * spt + i)),
                pl.BlockSpec((1, tm, d), lambda bi, i: (bi, i, 0)),
                pl.BlockSpec((1, 1, mod3.shape[-1]), lambda bi, i: (b0 + bi, 0, 0)),
                pl.BlockSpec((1, d), lambda bi, i: (0, 0))]
    args = [yg, yg, route, x1, mod3, gf]
    aliases = {}
    if out_prev is not None:
        in_specs.append(pl.BlockSpec(memory_space=pl.ANY))
        aliases = {len(args): 0}
        args.append(out_prev)
    return pl.pallas_call(
        _combine_kernel,
        grid=(nb, spt),
        in_specs=in_specs,
        out_specs=pl.BlockSpec((1, tm, d), lambda bi, i: (b0 + bi, i, 0)),
        out_shape=jax.ShapeDtypeStruct((b_all, s, d), F32),
        input_output_aliases=aliases,
        compiler_params=_cparams(("parallel", "parallel"), VMEM_LIMIT),
        name="combine",
    )(*args)


def _slot_plan(route, counts, n_tok):
    sizes = counts[EXPERT_ROW0:EXPERT_ROW0 + N_EXPERTS, 0].astype(jnp.int32)
    padded = (sizes + EXPERT_BLOCK - 1) // EXPERT_BLOCK * EXPERT_BLOCK
    pad_ends = jnp.cumsum(padded)
    pad_starts = pad_ends - padded
    eid = route[0:TOP_K].astype(jnp.int32)
    rank = route[TOP_K:2 * TOP_K].astype(jnp.int32)
    start = jnp.sum(jnp.where(eid[..., None] == jnp.arange(N_EXPERTS), pad_starts, 0), axis=-1)
    dest = (start + rank).reshape(TOP_K * n_tok)
    n_blocks = (n_tok * TOP_K + N_EXPERTS * (EXPERT_BLOCK - 1) + EXPERT_BLOCK - 1) // EXPERT_BLOCK
    gran = SC_CORES * SC_SUBCORES * SC_CHUNK // math.gcd(SC_CORES * SC_SUBCORES * SC_CHUNK, EXPERT_BLOCK)
    n_blocks = (n_blocks + gran - 1) // gran * gran
    n_used = pad_ends[-1] // EXPERT_BLOCK
    blk = jnp.minimum(jnp.arange(n_blocks), n_used - 1)
    blk_e = jnp.minimum(jnp.sum(pad_ends[None, :] <= (blk * EXPERT_BLOCK)[:, None], axis=1), N_EXPERTS - 1)
    blk_valid = jnp.clip(sizes[blk_e] - (blk * EXPERT_BLOCK - pad_starts[blk_e]), 0, EXPERT_BLOCK)
    return (dest, n_blocks * EXPERT_BLOCK, blk_e.astype(jnp.int32), blk_valid.astype(jnp.int32),
            n_used.reshape(1).astype(jnp.int32))


def kernel(x, c, w_mod, b_mod, norm1_g, w_in, conv_w, conv_b, lru_wx, lru_bx, lru_wa, lru_ba, lru_lambda, w_attn_o, w_lru_o, w_out, norm2_g, w_grp, b_grp, w_exp, b_exp, w1, w3, w2, norm_f_g):
    b, s, d = x.shape
    assert d == D_MODEL and s == SPAN * DILATIONS[-1] and w_mod.shape[0] == 1
    mod3 = _modulation(c, w_mod[0], b_mod[0]).reshape(b, 1, 6 * d)
    w_proj, w_gate = _prep_w_in(w_in[0])
    g1 = norm1_g[0].reshape(1, d)
    qkv0, qkv1, qkv2, xc, gy = _projection(x, mod3, g1, w_proj, conv_w[0], conv_b[0])
    attn = _attention((qkv0, qkv1, qkv2), b, s)
    lru = _lru_branch(xc, gy, lru_wx[0], lru_bx[0], lru_wa[0], lru_ba[0], lru_lambda[0])

    n_pad = ROUTER_ROWS - N_GROUPS - N_EXPERTS
    wr = jnp.pad(jnp.concatenate([w_grp[0], w_exp[0]], axis=1).T, ((0, n_pad), (0, 0))).astype(BF16)
    br = jnp.pad(jnp.concatenate([b_grp[0], b_exp[0]]), (0, n_pad)).reshape(ROUTER_ROWS, 1)
    wa, wl, wo = w_attn_o[0].astype(BF16), w_lru_o[0].astype(BF16), w_out[0].astype(BF16)
    as_rows = lambda a: a.reshape(-1, ROW_SUBLANES, LANE)
    as_tiles = lambda a: a.reshape(-1, LANE)

    out = None
    nb = b // MOE_BATCH_RANGES
    for b0 in range(0, b, nb):
        x1, h2, route, counts = _mix_route(attn, lru, x, mod3, g1, w_gate, wa, wl, wo,
                                           norm2_g[0].reshape(1, d), wr, br, b0, nb)
        dest, n_slots, blk_e, blk_valid, n_used = _slot_plan(route, counts, nb * s)
        xp = as_tiles(_sc_scatter_rows(as_rows(h2), dest, n_slots))
        yp = _experts(xp, blk_e, blk_valid, n_used, w1[0], w3[0], w2[0])
        yg = as_tiles(_sc_gather_rows(as_rows(yp), dest))
        out = _combine(yg, route, x1, mod3, norm_f_g.reshape(1, d), b0, out)
    return out
```

```python
import functools
import math

import jax
import jax.numpy as jnp
from jax import lax
from jax.experimental import pallas as pl
from jax.experimental.pallas import tpu as pltpu
from jax.experimental.pallas import tpu_sc as plsc

F32 = jnp.float32
BF16 = jnp.bfloat16

D_MODEL = 1024
HEAD_DIM = 64
N_SLOTS = 8
SPAN = 128
DILATIONS = (1, 4, 16)
GROUP_COLS = 3 * N_SLOTS * HEAD_DIM
ATTN_WIDTH = len(DILATIONS) * N_SLOTS * HEAD_DIM
ATTN_OUT = N_SLOTS * HEAD_DIM
LRU_WIDTH = D_MODEL
LRU_BLOCK_DIM = 64
CONV_WIDTH = 4
CONV_TAIL = 8
LRU_C = 8.0
N_GROUPS = 4
EXPERTS_PER_GROUP = 8
N_EXPERTS = N_GROUPS * EXPERTS_PER_GROUP
D_EXPERT = D_MODEL // 2
EPS = 1e-6
LOG2_E = 1.4426950408889634
LANE = 128
VMEM_LIMIT = 56 * 1024 * 1024


def _cparams(sem, vmem=None):
    return pltpu.CompilerParams(dimension_semantics=sem, vmem_limit_bytes=vmem)


def _resident(shape):
    nd = len(shape)
    return pl.BlockSpec(shape, lambda *_: (0,) * nd, pipeline_mode=pl.Buffered(1))


def _mod_kernel(c_ref, w_ref, b_ref, o_ref):
    c = c_ref[...]
    ca = c * jax.nn.sigmoid(c)
    o_ref[...] = jnp.dot(ca.astype(BF16), w_ref[...].astype(BF16),
                         preferred_element_type=F32) + b_ref[...]


def _modulation(c, w_mod, b_mod):
    b, d = c.shape
    n = w_mod.shape[1]
    tn = n // 4
    return pl.pallas_call(
        _mod_kernel,
        grid=(n // tn,),
        in_specs=[pl.BlockSpec((b, d), lambda j: (0, 0)),
                  pl.BlockSpec((d, tn), lambda j: (0, j)),
                  pl.BlockSpec((1, tn), lambda j: (0, j))],
        out_specs=pl.BlockSpec((b, tn), lambda j: (0, j)),
        out_shape=jax.ShapeDtypeStruct((b, n), F32),
        compiler_params=_cparams(("arbitrary",)),
        name="modulation",
    )(c, w_mod, b_mod.reshape(1, n))


def _rms_mod(x, g, scale, shift):
    ms = jnp.mean(x * x, axis=-1, keepdims=True)
    return x * lax.rsqrt(ms + EPS) * g * (1.0 + scale) + shift


def _gelu_tanh(y):
    return y * (0.5 * (1.0 + jnp.tanh(0.7978845608028654 * (y + 0.044715 * (y * y * y)))))


def _proj_kernel(x_ref, mod_ref, g_ref, w_ref, cw_ref, cb_ref, qkv0_ref, qkv1_ref, qkv2_ref,
                 xc_ref, gy_ref, hs_ref, xe_ref, *, tm):
    @pl.when(pl.program_id(1) == 0)
    def _():
        xe_ref[0:CONV_TAIL, :] = jnp.zeros((CONV_TAIL, LRU_WIDTH), F32)

    @pl.when(pl.program_id(1) > 0)
    def _():
        xe_ref[0:CONV_TAIL, :] = xe_ref[tm:tm + CONV_TAIL, :]

    d_model = x_ref.shape[-1]
    m = mod_ref[0]
    h = _rms_mod(x_ref[0], g_ref[...], m[:, d_model:2 * d_model], m[:, 0:d_model])

    def mm(hv, lo, hi):
        return jnp.dot(hv, w_ref[:, lo:hi], preferred_element_type=F32)

    hb = h.astype(BF16)
    c0 = len(DILATIONS) * GROUP_COLS
    qkv0_ref[0] = mm(hb, 0, GROUP_COLS).astype(BF16)
    xr = mm(hb, c0, c0 + LRU_WIDTH)
    xe_ref[CONV_TAIL:, :] = xr
    cw = cw_ref[...]
    xc = xr * cw[CONV_WIDTH - 1:CONV_WIDTH] + cb_ref[...]
    for k in range(1, CONV_WIDTH):
        xc = xc + xe_ref[CONV_TAIL - k:CONV_TAIL - k + tm, :] * cw[CONV_WIDTH - 1 - k:CONV_WIDTH - k]
    xc_ref[0] = xc.astype(BF16)
    gy_ref[0] = _gelu_tanh(mm(hb, c0 + LRU_WIDTH, c0 + 2 * LRU_WIDTH)).astype(BF16)

    n_slab = d_model // LANE
    for j in range(n_slab):
        hs_ref[j] = h[:, j * LANE:(j + 1) * LANE]
    for g, out_ref in ((1, qkv1_ref), (2, qkv2_ref)):
        d = DILATIONS[g]
        rows = tm // d
        hp = jnp.concatenate(
            [jnp.concatenate([hs_ref[j, pl.ds(p, rows, stride=d), :] for j in range(n_slab)], axis=1)
             for p in range(d)], axis=0).astype(BF16)
        res = mm(hp, g * GROUP_COLS, (g + 1) * GROUP_COLS).astype(BF16)
        for p in range(d):
            out_ref[p] = res[p * rows:(p + 1) * rows]


def _projection(x, mod3, g1, w_r, conv_w, conv_b, *, tm=512):
    b, s, d = x.shape
    n = w_r.shape[1]
    assert s % tm == 0 and tm % (16 * DILATIONS[-1]) == 0 and CONV_TAIL >= CONV_WIDTH - 1
    out_shape = [jax.ShapeDtypeStruct((b * dd, s // dd, GROUP_COLS), BF16) for dd in DILATIONS]
    out_shape += [jax.ShapeDtypeStruct((b, s, LRU_WIDTH), BF16),
                  jax.ShapeDtypeStruct((b, s, LRU_WIDTH), BF16)]
    out_specs = [pl.BlockSpec((dd, tm // dd, GROUP_COLS), lambda bi, i: (bi, i, 0)) for dd in DILATIONS]
    out_specs += [pl.BlockSpec((1, tm, LRU_WIDTH), lambda bi, i: (bi, i, 0)),
                  pl.BlockSpec((1, tm, LRU_WIDTH), lambda bi, i: (bi, i, 0))]
    return pl.pallas_call(
        functools.partial(_proj_kernel, tm=tm),
        grid=(b, s // tm),
        in_specs=[pl.BlockSpec((1, tm, d), lambda bi, i: (bi, i, 0)),
                  pl.BlockSpec((1, 1, mod3.shape[-1]), lambda bi, i: (bi, 0, 0)),
                  pl.BlockSpec((1, d), lambda bi, i: (0, 0)),
                  _resident((d, n)),
                  pl.BlockSpec((CONV_WIDTH, LRU_WIDTH), lambda bi, i: (0, 0)),
                  pl.BlockSpec((1, LRU_WIDTH), lambda bi, i: (0, 0))],
        out_specs=out_specs,
        out_shape=out_shape,
        scratch_shapes=[pltpu.VMEM((d // LANE, tm, LANE), F32), pltpu.VMEM((CONV_TAIL + tm, LRU_WIDTH), F32)],
        compiler_params=_cparams(("parallel", "arbitrary"), VMEM_LIMIT),
        name="projection",
    )(x, mod3, g1, w_r, conv_w, conv_b.reshape(1, LRU_WIDTH))


def _attn_kernel(q0, k0, v0, q1, k1, v1, q2, k2, v2, o_ref, acc_ref, lse_ref, bias_ref, *, seq):
    hcols = o_ref.shape[-1]
    n_head = hcols // HEAD_DIM
    head_of_lane = lax.broadcasted_iota(jnp.int32, (SPAN, hcols), 1) // HEAD_DIM
    head_mask_b = [jnp.where(head_of_lane == h, 1.0, 0.0).astype(BF16) for h in range(n_head)]

    def by_head(parts):
        out = parts[n_head - 1]
        for h in range(n_head - 2, -1, -1):
            out = jnp.where(head_of_lane == h, parts[h], out)
        return out

    qi = lax.broadcasted_iota(jnp.int32, (n_head * SPAN, 2 * SPAN), 0) % SPAN
    ki = lax.broadcasted_iota(jnp.int32, (n_head * SPAN, 2 * SPAN), 1)
    band = (ki >= qi) & (ki <= qi + SPAN)
    bias_ref[0] = jnp.where(band, 0.0, -jnp.inf)
    bias_ref[1] = jnp.where(band & (ki >= SPAN), 0.0, -jnp.inf)

    for g, (q_ref, k_ref, v_ref) in enumerate(((q0, k0, v0), (q1, k1, v1), (q2, k2, v2))):
        d = DILATIONS[g]
        n_blk = seq // d // SPAN

        def tile(n, carry, q_ref=q_ref, k_ref=k_ref, v_ref=v_ref, d=d, n_blk=n_blk, g=g):
            p = n // n_blk
            blk = n % n_blk
            r0 = pl.multiple_of(blk * SPAN, SPAN)
            rp = pl.multiple_of(jnp.maximum(blk - 1, 0) * SPAN, SPAN)
            q = q_ref[p, pl.ds(r0, SPAN), :]
            kk = jnp.concatenate([k_ref[p, pl.ds(rp, SPAN), :], k_ref[p, pl.ds(r0, SPAN), :]], axis=0)
            vv = jnp.concatenate([v_ref[p, pl.ds(rp, SPAN), :], v_ref[p, pl.ds(r0, SPAN), :]], axis=0)
            qs = jnp.concatenate([q * head_mask_b[h] for h in range(n_head)], axis=0)
            sc = lax.dot_general(qs, kk, (((1,), (1,)), ((), ())), preferred_element_type=F32)
            sc = sc + bias_ref[jnp.where(blk > 0, 0, 1)]
            mx = jnp.max(sc, axis=-1, keepdims=True)
            e = jnp.exp2(sc - mx)
            den = jnp.sum(e, axis=-1, keepdims=True)
            pv = jnp.dot(e.astype(BF16), vv, preferred_element_type=F32)
            lse = mx + jnp.log(den) * LOG2_E
            rows_of = lambda a: [a[h * SPAN:(h + 1) * SPAN] for h in range(n_head)]
            o = by_head(rows_of(pv)) / by_head(rows_of(den))
            l = by_head(rows_of(lse))
            start = p + d * r0
            for j in range(hcols // LANE):
                rows = pl.ds(start, SPAN, stride=d) if d > 1 else pl.ds(start, SPAN)
                acc_ref[g, j, rows, :] = o[:, j * LANE:(j + 1) * LANE]
                lse_ref[g, j, rows, :] = l[:, j * LANE:(j + 1) * LANE]
            return carry

        lax.fori_loop(0, seq // SPAN, tile, 0, unroll=16)

    chunk = 256

    def combine(c, carry):
        r = pl.multiple_of(c * chunk, chunk)
        for j in range(hcols // LANE):
            ls = [lse_ref[g, j, pl.ds(r, chunk), :] for g in range(len(DILATIONS))]
            mx = jnp.maximum(jnp.maximum(ls[0], ls[1]), ls[2])
            ws = [jnp.exp2(v - mx) for v in ls]
            num = ws[0] * acc_ref[0, j, pl.ds(r, chunk), :]
            for g in range(1, len(DILATIONS)):
                num = num + ws[g] * acc_ref[g, j, pl.ds(r, chunk), :]
            o_ref[0, pl.ds(r, chunk), j * LANE:(j + 1) * LANE] = (num / (ws[0] + ws[1] + ws[2])).astype(BF16)
        return carry

    lax.fori_loop(0, seq // chunk, combine, 0)


def _attention(qkvs, b, s):
    hcols = 2 * HEAD_DIM
    n_hg = ATTN_OUT // hcols
    ncb = ATTN_OUT // hcols
    in_specs, args = [], []
    for g, d in enumerate(DILATIONS):
        for part in range(3):
            in_specs.append(pl.BlockSpec((d, s // d, hcols),
                                         lambda bi, hg, part=part: (bi, 0, part * ncb + hg)))
            args.append(qkvs[g])
    return pl.pallas_call(
        functools.partial(_attn_kernel, seq=s),
        grid=(b, n_hg),
        in_specs=in_specs,
        out_specs=pl.BlockSpec((1, s, hcols), lambda bi, hg: (bi, 0, hg)),
        out_shape=jax.ShapeDtypeStruct((b, s, ATTN_OUT), BF16),
        scratch_shapes=[pltpu.VMEM((len(DILATIONS), hcols // LANE, s, LANE), F32),
                        pltpu.VMEM((len(DILATIONS), hcols // LANE, s, LANE), F32),
                        pltpu.VMEM((2, (hcols // HEAD_DIM) * SPAN, 2 * SPAN), F32)],
        compiler_params=_cparams(("parallel", "parallel"), VMEM_LIMIT),
        name="dilated_attention",
    )(*args)


def _lru_kernel(xc_ref, gy_ref, wg_ref, bx_ref, ba_ref, lam_ref, o_ref, a_ref, b_ref, h_ref, *, pitch):
    nb, ts, tc = xc_ref.shape
    nl = tc // LANE

    @pl.when(pl.program_id(1) == 0)
    def _():
        h_ref[...] = jnp.zeros_like(h_ref)

    xb = xc_ref[...].reshape(nb * ts, tc)
    xc = xb.astype(F32)
    gates = jnp.dot(xb, wg_ref[0], preferred_element_type=F32)
    gate_i = jax.nn.sigmoid(gates[:, :tc] + bx_ref[...])
    gate_r = jax.nn.sigmoid(gates[:, tc:] + ba_ref[...])
    neg_lam = -lam_ref[...]
    softplus = jnp.maximum(neg_lam, 0.0) + jnp.log1p(jnp.exp(-jnp.abs(neg_lam)))
    log_a = (-LRU_C) * gate_r * softplus
    a = jnp.exp(log_a)
    one_m_a2 = jnp.tanh(-log_a) * (1.0 + a * a)
    mult = jnp.where(one_m_a2 > 0.0, one_m_a2 * lax.rsqrt(one_m_a2), 0.0)
    bv = mult * gate_i * xc
    for bi in range(nb):
        for j in range(nl):
            a_ref[j, pl.ds(bi * pitch, ts), :] = a[bi * ts:(bi + 1) * ts, j * LANE:(j + 1) * LANE]
            b_ref[j, pl.ds(bi * pitch, ts), :] = bv[bi * ts:(bi + 1) * ts, j * LANE:(j + 1) * LANE]

    def step(t, hs):
        out = []
        for j in range(nl):
            rows = pl.ds(t, nb, stride=pitch)
            h = a_ref[j, rows, :] * hs[j] + b_ref[j, rows, :]
            b_ref[j, rows, :] = h
            out.append(h)
        return tuple(out)

    hs = lax.fori_loop(0, ts, step, tuple(h_ref[j] for j in range(nl)), unroll=8)
    for j in range(nl):
        h_ref[j] = hs[j]
    for bi in range(nb):
        h = jnp.concatenate([b_ref[j, pl.ds(bi * pitch, ts), :] for j in range(nl)], axis=1)
        o_ref[bi] = (h * gy_ref[bi].astype(F32)).astype(BF16)


def _lru_gate_weights(wx, wa, tc):
    nb, bd, _ = wx.shape
    per = tc // bd
    eye = jnp.eye(per, dtype=wx.dtype)

    def bdiag(w):
        w = w.reshape(nb // per, per, bd, bd)
        return jnp.einsum('cpio,pq->cpiqo', w, eye).reshape(nb // per, tc, tc)

    return jnp.concatenate([bdiag(wx), bdiag(wa)], axis=-1).astype(BF16)


def _lru_branch(xc, gy, wx, bx, wa, ba, lam, *, tc=256, ts=128):
    b, s, c = xc.shape
    assert s % ts == 0 and c % tc == 0
    wg = _lru_gate_weights(wx, wa, tc)
    row = lambda v: v.reshape(1, c)
    tile = pl.BlockSpec((b, ts, tc), lambda ci, ti: (0, ti, ci))
    vec = pl.BlockSpec((1, tc), lambda ci, ti: (0, ci))
    pitch = ts + 8
    return pl.pallas_call(
        functools.partial(_lru_kernel, pitch=pitch),
        grid=(c // tc, s // ts),
        in_specs=[tile, tile,
                  pl.BlockSpec((1, tc, 2 * tc), lambda ci, ti: (ci, 0, 0)),
                  vec, vec, vec],
        out_specs=tile,
        out_shape=jax.ShapeDtypeStruct((b, s, c), BF16),
        scratch_shapes=[pltpu.VMEM((tc // LANE, b * pitch, LANE), F32),
                        pltpu.VMEM((tc // LANE, b * pitch, LANE), F32),
                        pltpu.VMEM((tc // LANE, b, LANE), F32)],
        compiler_params=_cparams(("parallel", "arbitrary"), VMEM_LIMIT),
        name="rg_lru",
    )(xc, gy, wg, row(bx), row(ba), row(lam))


def _prep_w_in(w_in):
    a = ATTN_WIDTH
    gw = N_SLOTS * HEAD_DIM
    q = w_in[:, :a] * (HEAD_DIM ** -0.5 * LOG2_E)
    k = w_in[:, a:2 * a]
    v = w_in[:, 2 * a:3 * a]
    parts = []
    for g in range(len(DILATIONS)):
        sl = slice(g * gw, (g + 1) * gw)
        parts += [q[:, sl], k[:, sl], v[:, sl]]
    parts.append(w_in[:, 3 * a:3 * a + 2 * LRU_WIDTH])
    return jnp.concatenate(parts, axis=1).astype(BF16), w_in[:, 3 * a + 2 * LRU_WIDTH:].astype(BF16)


ROW_SUBLANES = D_MODEL // 2 // LANE


def _store_tile_rows(ref, v, row0=0):
    n, half = v.shape[0], v.shape[1] // 2
    lo = pltpu.bitcast(v[:, :half].astype(BF16).astype(F32), jnp.uint32)
    hi = pltpu.bitcast(v[:, half:].astype(BF16).astype(F32), jnp.uint32)
    words = (hi & jnp.uint32(0xFFFF0000)) | (lo >> 16)
    for j in range(ROW_SUBLANES):
        ref[pl.ds(row0 * ROW_SUBLANES + j, n, stride=ROW_SUBLANES), :] = words[:, j * LANE:(j + 1) * LANE]


def _load_tile_rows(ref, n=None):
    n = ref.shape[0] // ROW_SUBLANES if n is None else n
    words = [ref[pl.ds(j, n, stride=ROW_SUBLANES), :] for j in range(ROW_SUBLANES)]
    lo = [pltpu.bitcast(w << 16, F32) for w in words]
    hi = [pltpu.bitcast(w & jnp.uint32(0xFFFF0000), F32) for w in words]
    return jnp.concatenate(lo + hi, axis=-1)


SC_CORES, SC_SUBCORES = 2, 16
SC_CHUNK = 128


def _sc_gather_rows(table, idx):
    n = idx.shape[0]
    per_worker = n // (SC_CORES * SC_SUBCORES)
    n_chunks = per_worker // SC_CHUNK
    assert n_chunks * SC_CHUNK * SC_CORES * SC_SUBCORES == n
    mesh = plsc.VectorSubcoreMesh(core_axis_name="c", subcore_axis_name="s")

    def body(table_hbm, idx_hbm, out_hbm, idx_v, rows_v, sem):
        base = (lax.axis_index("s") * SC_CORES + lax.axis_index("c")) * per_worker

        @pl.loop(0, n_chunks)
        def _(i):
            off = pl.multiple_of(base + i * SC_CHUNK, SC_CHUNK)
            pltpu.sync_copy(idx_hbm.at[pl.ds(off, SC_CHUNK)], idx_v)
            pltpu.async_copy(table_hbm.at[idx_v], rows_v, sem).wait()
            pltpu.sync_copy(rows_v, out_hbm.at[pl.ds(off, SC_CHUNK)])

    return pl.kernel(
        body, mesh=mesh,
        out_type=jax.ShapeDtypeStruct((n,) + table.shape[1:], table.dtype),
        scratch_types=[pltpu.VMEM((SC_CHUNK,), jnp.int32),
                       pltpu.VMEM((SC_CHUNK,) + table.shape[1:], table.dtype),
                       pltpu.SemaphoreType.DMA],
        name="sc_gather_rows",
    )(table, idx)


def _sc_scatter_rows(rows, idx, n_out):
    n_rows = rows.shape[0]
    n_choice = idx.shape[0] // n_rows
    per_worker = n_rows // (SC_CORES * SC_SUBCORES)
    n_chunks = per_worker // SC_CHUNK
    assert n_chunks * SC_CHUNK * SC_CORES * SC_SUBCORES == n_rows and n_choice * n_rows == idx.shape[0]
    mesh = plsc.VectorSubcoreMesh(core_axis_name="c", subcore_axis_name="s")

    def body(rows_hbm, idx_hbm, out_hbm, idx_v, rows_v):
        base = (lax.axis_index("s") * SC_CORES + lax.axis_index("c")) * per_worker

        @pl.loop(0, n_chunks)
        def _(i):
            off = pl.multiple_of(base + i * SC_CHUNK, SC_CHUNK)
            pltpu.sync_copy(rows_hbm.at[pl.ds(off, SC_CHUNK)], rows_v)
            for k in range(n_choice):
                pltpu.sync_copy(idx_hbm.at[pl.ds(k * n_rows + off, SC_CHUNK)], idx_v)
                pltpu.sync_copy(rows_v, out_hbm.at[idx_v])

    return pl.kernel(
        body, mesh=mesh,
        out_type=jax.ShapeDtypeStruct((n_out,) + rows.shape[1:], rows.dtype),
        scratch_types=[pltpu.VMEM((SC_CHUNK,), jnp.int32),
                       pltpu.VMEM((SC_CHUNK,) + rows.shape[1:], rows.dtype)],
        name="sc_scatter_rows",
    )(rows, idx)


ROUTE_ROWS = 8
EXPERT_ROW0 = N_GROUPS
ROUTER_ROWS = 48


def _mix_kernel(attn_ref, lru_ref, x_ref, mod_ref, g1_ref, wg_ref, wa_ref, wl_ref, wo_ref, g2_ref,
                wrt_ref, brt_ref, x1_ref, h2_ref, route_ref, cnt_ref, cnt_acc):
    d = x_ref.shape[-1]
    tm = x_ref.shape[1]

    @pl.when((pl.program_id(0) == 0) & (pl.program_id(1) == 0))
    def _():
        cnt_acc[...] = jnp.zeros_like(cnt_acc)

    m = mod_ref[0]
    gate1, shift2, scale2 = m[:, 2 * d:3 * d], m[:, 3 * d:4 * d], m[:, 4 * d:5 * d]
    x = x_ref[0]
    h1 = _rms_mod(x, g1_ref[...], m[:, d:2 * d], m[:, 0:d]).astype(BF16)
    gates = jax.nn.sigmoid(jnp.dot(h1, wg_ref[...], preferred_element_type=F32))
    ya = jnp.dot(attn_ref[0], wa_ref[...], preferred_element_type=F32)
    yl = jnp.dot(lru_ref[0], wl_ref[...], preferred_element_type=F32)
    mixed = gates[:, :d] * ya + gates[:, d:] * yl
    y = jnp.dot(mixed.astype(BF16), wo_ref[...], preferred_element_type=F32)
    x1 = x + (1.0 + gate1) * y
    x1_ref[0] = x1.astype(BF16)
    h2 = _rms_mod(x1, g2_ref[...], scale2, shift2)
    _store_tile_rows(h2_ref, h2)
    logits = lax.dot_general(wrt_ref[...], h2.astype(BF16), (((1,), (1,)), ((), ())),
                             preferred_element_type=F32) + brt_ref[...]

    row = lax.broadcasted_iota(jnp.int32, logits.shape, 0)
    neg = -jnp.inf

    def top(vals):
        mx = jnp.max(vals, axis=0, keepdims=True)
        idx = jnp.min(jnp.where(vals == mx, row, ROUTER_ROWS), axis=0, keepdims=True)
        return mx, idx

    is_grp = row < N_GROUPS
    gmax, gidx = top(jnp.where(is_grp, logits, neg))
    grp_gate = 1.0 / jnp.sum(jnp.where(is_grp, jnp.exp(logits - gmax), 0.0), axis=0, keepdims=True)
    lo = EXPERT_ROW0 + EXPERTS_PER_GROUP * gidx
    el = jnp.where((row >= lo) & (row < lo + EXPERTS_PER_GROUP), logits, neg)
    v1, i1 = top(el)
    v2, i2 = top(jnp.where(row == i1, neg, el))
    e21 = jnp.exp(v2 - v1)
    wt1 = grp_gate / (1.0 + e21)
    wt2 = wt1 * e21

    oh1 = jnp.where(row == i1, 1.0, 0.0)
    oh2 = jnp.where(row == i2, 1.0, 0.0)
    ohs = oh1 + oh2
    rr = lax.broadcasted_iota(jnp.int32, (tm, tm), 0)
    cc = lax.broadcasted_iota(jnp.int32, (tm, tm), 1)
    earlier = jnp.where(rr < cc, 1.0, 0.0).astype(BF16)
    before = jnp.dot(ohs.astype(BF16), earlier, preferred_element_type=F32) + cnt_acc[...]
    rank1 = jnp.sum(oh1 * before, axis=0, keepdims=True)
    rank2 = jnp.sum(oh2 * before, axis=0, keepdims=True)
    cnt_acc[...] = cnt_acc[...] + jnp.sum(ohs, axis=1, keepdims=True)
    cnt_ref[...] = cnt_acc[...]

    vals = [(i1 - EXPERT_ROW0).astype(F32), (i2 - EXPERT_ROW0).astype(F32), rank1, rank2, wt1, wt2]
    out_row = lax.broadcasted_iota(jnp.int32, (ROUTE_ROWS, tm), 0)
    slab = jnp.zeros((ROUTE_ROWS, tm), F32)
    for j, v in enumerate(vals):
        slab = jnp.where(out_row == j, v, slab)
    route_ref[...] = slab


def _mix_route(attn, lru, x, mod3, g1, wg, wa, wl, wo, g2, wrt, brt, b0, nb, *, tm=512):
    _, s, d = x.shape
    spt = s // tm
    tok_in = lambda w: pl.BlockSpec((1, tm, w), lambda bi, i: (b0 + bi, i, 0))
    return pl.pallas_call(
        _mix_kernel,
        grid=(nb, spt),
        in_specs=[tok_in(attn.shape[-1]), tok_in(d), tok_in(d),
                  pl.BlockSpec((1, 1, mod3.shape[-1]), lambda bi, i: (b0 + bi, 0, 0)),
                  pl.BlockSpec((1, d), lambda bi, i: (0, 0)),
                  _resident(wg.shape), _resident(wa.shape), _resident(wl.shape), _resident(wo.shape),
                  pl.BlockSpec((1, d), lambda bi, i: (0, 0)),
                  _resident(wrt.shape),
                  pl.BlockSpec((ROUTER_ROWS, 1), lambda bi, i: (0, 0))],
        out_specs=[pl.BlockSpec((1, tm, d), lambda bi, i: (bi, i, 0)),
                   pl.BlockSpec((tm * ROW_SUBLANES, LANE), lambda bi, i: (bi * spt + i, 0)),
                   pl.BlockSpec((ROUTE_ROWS, tm), lambda bi, i: (0, bi * spt + i)),
                   pl.BlockSpec((ROUTER_ROWS, 1), lambda bi, i: (0, 0))],
        out_shape=[jax.ShapeDtypeStruct((nb, s, d), BF16),
                   jax.ShapeDtypeStruct((nb * s * ROW_SUBLANES, LANE), jnp.uint32),
                   jax.ShapeDtypeStruct((ROUTE_ROWS, nb * s), F32),
                   jax.ShapeDtypeStruct((ROUTER_ROWS, 1), F32)],
        scratch_shapes=[pltpu.VMEM((ROUTER_ROWS, 1), F32)],
        compiler_params=_cparams(("arbitrary", "arbitrary"), VMEM_LIMIT),
        name="mix_route",
    )(attn, lru, x, mod3, g1, wg, wa, wl, wo, g2, wrt, brt)


TOP_K = 2
EXPERT_BLOCK = 512
MOE_BATCH_RANGES = 2


def _expert_kernel(be_ref, bv_ref, first_ref, slot_ref, next_ref, nu_ref, x_ref, w1_hbm, w3_hbm, w2_hbm,
                   y_ref, wf1, wf3, wf2, wb1, wb3, wb2, sem):
    j = pl.program_id(0)
    half = EXPERT_BLOCK // 2

    def fetch(e, s):
        return [pltpu.make_async_copy(w1_hbm.at[e], wf1.at[s], sem.at[s, 0]),
                pltpu.make_async_copy(w3_hbm.at[e], wf3.at[s], sem.at[s, 1]),
                pltpu.make_async_copy(w2_hbm.at[e], wf2.at[s], sem.at[s, 2])]

    def ffn(n):
        xb = _load_tile_rows(x_ref, n).astype(BF16)
        a = jnp.dot(xb, wb1[...], preferred_element_type=F32)
        g = jnp.dot(xb, wb3[...], preferred_element_type=F32)
        hm = (a * jax.nn.sigmoid(a) * g).astype(BF16)
        _store_tile_rows(y_ref, jnp.dot(hm, wb2[...], preferred_element_type=F32))

    @pl.when(j < nu_ref[0])
    def _():
        @pl.when(first_ref[j] == 1)
        def _():
            e, s = be_ref[j], slot_ref[j]

            @pl.when(j == 0)
            def _():
                for c in fetch(e, s):
                    c.start()

            @pl.when(next_ref[j] >= 0)
            def _():
                for c in fetch(next_ref[j], 1 - s):
                    c.start()

            for c in fetch(e, s):
                c.wait()
            wb1[...] = wf1[s].astype(BF16)
            wb3[...] = wf3[s].astype(BF16)
            wb2[...] = wf2[s].astype(BF16)

        @pl.when(bv_ref[j] > half)
        def _():
            ffn(EXPERT_BLOCK)

        @pl.when(bv_ref[j] <= half)
        def _():
            ffn(half)
            y_ref[half * ROW_SUBLANES:, :] = jnp.zeros((half * ROW_SUBLANES, LANE), y_ref.dtype)

    @pl.when(j >= nu_ref[0])
    def _():
        y_ref[...] = jnp.zeros_like(y_ref)


def _experts(xp, plan, w1, w3, w2):
    ne, d, de = w1.shape
    nb = xp.shape[0] // (EXPERT_BLOCK * ROW_SUBLANES)
    rows = (EXPERT_BLOCK * ROW_SUBLANES, LANE)
    grid_spec = pltpu.PrefetchScalarGridSpec(
        num_scalar_prefetch=len(plan),
        grid=(nb,),
        in_specs=[pl.BlockSpec(rows, lambda j, *p: (jnp.minimum(j, p[-1][0] - 1), 0)),
                  pl.BlockSpec(memory_space=pl.ANY),
                  pl.BlockSpec(memory_space=pl.ANY),
                  pl.BlockSpec(memory_space=pl.ANY)],
        out_specs=pl.BlockSpec(rows, lambda j, *p: (j, 0)),
        scratch_shapes=[pltpu.VMEM((2, d, de), F32), pltpu.VMEM((2, d, de), F32), pltpu.VMEM((2, de, d), F32),
                        pltpu.VMEM((d, de), BF16), pltpu.VMEM((d, de), BF16), pltpu.VMEM((de, d), BF16),
                        pltpu.SemaphoreType.DMA((2, 3))])
    return pl.pallas_call(
        _expert_kernel,
        grid_spec=grid_spec,
        out_shape=jax.ShapeDtypeStruct(xp.shape, xp.dtype),
        compiler_params=_cparams(("arbitrary",), VMEM_LIMIT),
        name="experts",
    )(*plan, xp, w1, w3, w2)


def _combine_kernel(y0_ref, y1_ref, route_ref, x1_ref, mod_ref, gf_ref, *rest):
    o_ref = rest[-1]
    tm, d = x1_ref.shape[1], x1_ref.shape[2]
    route = jnp.concatenate([route_ref[...], jnp.zeros((LANE - ROUTE_ROWS, tm), F32)], axis=0).T
    moe = _load_tile_rows(y0_ref) * route[:, 4:5] + _load_tile_rows(y1_ref) * route[:, 5:6]
    gate2 = mod_ref[0][:, 5 * d:6 * d]
    xo = x1_ref[0].astype(F32) + (1.0 + gate2) * moe
    ms = jnp.mean(xo * xo, axis=-1, keepdims=True)
    o_ref[0] = xo * lax.rsqrt(ms + EPS) * gf_ref[...]


def _combine(yg, route, x1, mod3, gf, b0, out_prev, *, tm=256):
    nb, s, d = x1.shape
    b_all = mod3.shape[0]
    spt = s // tm
    nt = nb * spt
    rows = (tm * ROW_SUBLANES, LANE)
    in_specs = [pl.BlockSpec(rows, lambda bi, i: (bi * spt + i, 0)),
                pl.BlockSpec(rows, lambda bi, i: (nt + bi * spt + i, 0)),
                pl.BlockSpec((ROUTE_ROWS, tm), lambda bi, i: (0, bi * spt + i)),
                pl.BlockSpec((1, tm, d), lambda bi, i: (bi, i, 0)),
                pl.BlockSpec((1, 1, mod3.shape[-1]), lambda bi, i: (b0 + bi, 0, 0)),
                pl.BlockSpec((1, d), lambda bi, i: (0, 0))]
    args = [yg, yg, route, x1, mod3, gf]
    aliases = {}
    if out_prev is not None:
        in_specs.append(pl.BlockSpec(memory_space=pl.ANY))
        aliases = {len(args): 0}
        args.append(out_prev)
    return pl.pallas_call(
        _combine_kernel,
        grid=(nb, spt),
        in_specs=in_specs,
        out_specs=pl.BlockSpec((1, tm, d), lambda bi, i: (b0 + bi, i, 0)),
        out_shape=jax.ShapeDtypeStruct((b_all, s, d), F32),
        input_output_aliases=aliases,
        compiler_params=_cparams(("parallel", "parallel"), VMEM_LIMIT),
        name="combine",
    )(*args)


def _slot_plan(route, counts, n_tok):
    sizes = counts[EXPERT_ROW0:EXPERT_ROW0 + N_EXPERTS, 0].astype(jnp.int32)
    padded = (sizes + EXPERT_BLOCK - 1) // EXPERT_BLOCK * EXPERT_BLOCK
    pad_ends = jnp.cumsum(padded)
    pad_starts = pad_ends - padded
    eid = route[0:TOP_K].astype(jnp.int32)
    rank = route[TOP_K:2 * TOP_K].astype(jnp.int32)
    start = jnp.sum(jnp.where(eid[..., None] == jnp.arange(N_EXPERTS), pad_starts, 0), axis=-1)
    dest = (start + rank).reshape(TOP_K * n_tok)
    n_blocks = (n_tok * TOP_K + N_EXPERTS * (EXPERT_BLOCK - 1) + EXPERT_BLOCK - 1) // EXPERT_BLOCK
    gran = SC_CORES * SC_SUBCORES * SC_CHUNK // math.gcd(SC_CORES * SC_SUBCORES * SC_CHUNK, EXPERT_BLOCK)
    n_blocks = (n_blocks + gran - 1) // gran * gran
    n_used = pad_ends[-1] // EXPERT_BLOCK
    blk = jnp.minimum(jnp.arange(n_blocks), n_used - 1)
    blk_e = jnp.minimum(jnp.sum(pad_ends[None, :] <= (blk * EXPERT_BLOCK)[:, None], axis=1), N_EXPERTS - 1)
    blk_valid = jnp.clip(sizes[blk_e] - (blk * EXPERT_BLOCK - pad_starts[blk_e]), 0, EXPERT_BLOCK)
    idx = jnp.arange(n_blocks)
    first = (idx < n_used) & ((idx == 0) | (blk_e != jnp.roll(blk_e, 1)))
    slot = (jnp.cumsum(first) - 1) % 2
    later_first = lax.cummin(jnp.where(first, idx, n_blocks), reverse=True)
    next_first = jnp.concatenate([later_first[1:], jnp.full((1,), n_blocks)])
    next_e = jnp.where(next_first < n_blocks, blk_e[jnp.minimum(next_first, n_blocks - 1)], -1)
    i32 = lambda a: a.astype(jnp.int32)
    plan = (i32(blk_e), i32(blk_valid), i32(first), i32(slot), i32(next_e), i32(n_used.reshape(1)))
    return dest, n_blocks * EXPERT_BLOCK, plan


def kernel(x, c, w_mod, b_mod, norm1_g, w_in, conv_w, conv_b, lru_wx, lru_bx, lru_wa, lru_ba, lru_lambda, w_attn_o, w_lru_o, w_out, norm2_g, w_grp, b_grp, w_exp, b_exp, w1, w3, w2, norm_f_g):
    b, s, d = x.shape
    assert d == D_MODEL and s == SPAN * DILATIONS[-1] and w_mod.shape[0] == 1
    mod3 = _modulation(c, w_mod[0], b_mod[0]).reshape(b, 1, 6 * d)
    w_proj, w_gate = _prep_w_in(w_in[0])
    g1 = norm1_g[0].reshape(1, d)
    qkv0, qkv1, qkv2, xc, gy = _projection(x, mod3, g1, w_proj, conv_w[0], conv_b[0])
    attn = _attention((qkv0, qkv1, qkv2), b, s)
    lru = _lru_branch(xc, gy, lru_wx[0], lru_bx[0], lru_wa[0], lru_ba[0], lru_lambda[0])

    n_pad = ROUTER_ROWS - N_GROUPS - N_EXPERTS
    wr = jnp.pad(jnp.concatenate([w_grp[0], w_exp[0]], axis=1).T, ((0, n_pad), (0, 0))).astype(BF16)
    br = jnp.pad(jnp.concatenate([b_grp[0], b_exp[0]]), (0, n_pad)).reshape(ROUTER_ROWS, 1)
    wa, wl, wo = w_attn_o[0].astype(BF16), w_lru_o[0].astype(BF16), w_out[0].astype(BF16)
    as_rows = lambda a: a.reshape(-1, ROW_SUBLANES, LANE)
    as_tiles = lambda a: a.reshape(-1, LANE)

    out = None
    nb = b // MOE_BATCH_RANGES
    for b0 in range(0, b, nb):
        x1, h2, route, counts = _mix_route(attn, lru, x, mod3, g1, w_gate, wa, wl, wo,
                                           norm2_g[0].reshape(1, d), wr, br, b0, nb)
        dest, n_slots, plan = _slot_plan(route, counts, nb * s)
        xp = as_tiles(_sc_scatter_rows(as_rows(h2), dest, n_slots))
        yp = _experts(xp, plan, w1[0], w3[0], w2[0])
        yg = as_tiles(_sc_gather_rows(as_rows(yp), dest))
        out = _combine(yg, route, x1, mod3, norm_f_g.reshape(1, d), b0, out)
    return out
```

```python
import functools
import math

import jax
import jax.numpy as jnp
from jax import lax
from jax.experimental import pallas as pl
from jax.experimental.pallas import tpu as pltpu
from jax.experimental.pallas import tpu_sc as plsc

F32 = jnp.float32
BF16 = jnp.bfloat16

D_MODEL = 1024
HEAD_DIM = 64
N_SLOTS = 8
SPAN = 128
DILATIONS = (1, 4, 16)
GROUP_COLS = 3 * N_SLOTS * HEAD_DIM
ATTN_WIDTH = len(DILATIONS) * N_SLOTS * HEAD_DIM
ATTN_OUT = N_SLOTS * HEAD_DIM
LRU_WIDTH = D_MODEL
CONV_WIDTH = 4
CONV_TAIL = 8
LRU_C = 8.0
N_GROUPS = 4
EXPERTS_PER_GROUP = 8
N_EXPERTS = N_GROUPS * EXPERTS_PER_GROUP
EPS = 1e-6
LOG2_E = 1.4426950408889634
LANE = 128
VMEM_LIMIT = 56 * 1024 * 1024


def _cparams(sem, vmem=None):
    return pltpu.CompilerParams(dimension_semantics=sem, vmem_limit_bytes=vmem)


def _resident(shape):
    nd = len(shape)
    return pl.BlockSpec(shape, lambda *_: (0,) * nd, pipeline_mode=pl.Buffered(1))


def _mod_kernel(c_ref, w_ref, b_ref, o_ref):
    c = c_ref[...]
    ca = c * jax.nn.sigmoid(c)
    o_ref[...] = jnp.dot(ca.astype(BF16), w_ref[...].astype(BF16),
                         preferred_element_type=F32) + b_ref[...]


def _modulation(c, w_mod, b_mod):
    b, d = c.shape
    n = w_mod.shape[1]
    tn = n // 4
    return pl.pallas_call(
        _mod_kernel,
        grid=(n // tn,),
        in_specs=[pl.BlockSpec((b, d), lambda j: (0, 0)),
                  pl.BlockSpec((d, tn), lambda j: (0, j)),
                  pl.BlockSpec((1, tn), lambda j: (0, j))],
        out_specs=pl.BlockSpec((b, tn), lambda j: (0, j)),
        out_shape=jax.ShapeDtypeStruct((b, n), F32),
        compiler_params=_cparams(("arbitrary",)),
        name="modulation",
    )(c, w_mod, b_mod.reshape(1, n))


def _rms_mod(x, g, scale, shift):
    ms = jnp.mean(x * x, axis=-1, keepdims=True)
    return x * lax.rsqrt(ms + EPS) * g * (1.0 + scale) + shift


def _gelu_tanh(y):
    return y * (0.5 * (1.0 + jnp.tanh(0.7978845608028654 * (y + 0.044715 * (y * y * y)))))


def _proj_kernel(x_ref, mod_ref, g_ref, w_ref, cw_ref, cb_ref, qkv0_ref, qkv1_ref, qkv2_ref,
                 xc_ref, gy_ref, hs_ref, xe_ref, *, tm):
    @pl.when(pl.program_id(1) == 0)
    def _():
        xe_ref[0:CONV_TAIL, :] = jnp.zeros((CONV_TAIL, LRU_WIDTH), F32)

    @pl.when(pl.program_id(1) > 0)
    def _():
        xe_ref[0:CONV_TAIL, :] = xe_ref[tm:tm + CONV_TAIL, :]

    d_model = x_ref.shape[-1]
    m = mod_ref[0]
    h = _rms_mod(x_ref[0], g_ref[...], m[:, d_model:2 * d_model], m[:, 0:d_model])

    def mm(hv, lo, hi):
        return jnp.dot(hv, w_ref[:, lo:hi], preferred_element_type=F32)

    hb = h.astype(BF16)
    c0 = len(DILATIONS) * GROUP_COLS
    qkv0_ref[0] = mm(hb, 0, GROUP_COLS).astype(BF16)
    xr = mm(hb, c0, c0 + LRU_WIDTH)
    xe_ref[CONV_TAIL:, :] = xr
    cw = cw_ref[...]
    xc = xr * cw[CONV_WIDTH - 1:CONV_WIDTH] + cb_ref[...]
    for k in range(1, CONV_WIDTH):
        xc = xc + xe_ref[CONV_TAIL - k:CONV_TAIL - k + tm, :] * cw[CONV_WIDTH - 1 - k:CONV_WIDTH - k]
    xc_ref[0] = xc.astype(BF16)
    gy_ref[0] = _gelu_tanh(mm(hb, c0 + LRU_WIDTH, c0 + 2 * LRU_WIDTH)).astype(BF16)

    n_slab = d_model // LANE
    for j in range(n_slab):
        hs_ref[j] = h[:, j * LANE:(j + 1) * LANE]
    for g, out_ref in ((1, qkv1_ref), (2, qkv2_ref)):
        d = DILATIONS[g]
        rows = tm // d
        hp = jnp.concatenate(
            [jnp.concatenate([hs_ref[j, pl.ds(p, rows, stride=d), :] for j in range(n_slab)], axis=1)
             for p in range(d)], axis=0).astype(BF16)
        res = mm(hp, g * GROUP_COLS, (g + 1) * GROUP_COLS).astype(BF16)
        for p in range(d):
            out_ref[p] = res[p * rows:(p + 1) * rows]


def _projection(x, mod3, g1, w_r, conv_w, conv_b, *, tm=512):
    b, s, d = x.shape
    n = w_r.shape[1]
    assert s % tm == 0 and tm % (16 * DILATIONS[-1]) == 0 and CONV_TAIL >= CONV_WIDTH - 1
    out_shape = [jax.ShapeDtypeStruct((b * dd, s // dd, GROUP_COLS), BF16) for dd in DILATIONS]
    out_shape += [jax.ShapeDtypeStruct((b, s, LRU_WIDTH), BF16),
                  jax.ShapeDtypeStruct((b, s, LRU_WIDTH), BF16)]
    out_specs = [pl.BlockSpec((dd, tm // dd, GROUP_COLS), lambda bi, i: (bi, i, 0)) for dd in DILATIONS]
    out_specs += [pl.BlockSpec((1, tm, LRU_WIDTH), lambda bi, i: (bi, i, 0)),
                  pl.BlockSpec((1, tm, LRU_WIDTH), lambda bi, i: (bi, i, 0))]
    return pl.pallas_call(
        functools.partial(_proj_kernel, tm=tm),
        grid=(b, s // tm),
        in_specs=[pl.BlockSpec((1, tm, d), lambda bi, i: (bi, i, 0)),
                  pl.BlockSpec((1, 1, mod3.shape[-1]), lambda bi, i: (bi, 0, 0)),
                  pl.BlockSpec((1, d), lambda bi, i: (0, 0)),
                  _resident((d, n)),
                  pl.BlockSpec((CONV_WIDTH, LRU_WIDTH), lambda bi, i: (0, 0)),
                  pl.BlockSpec((1, LRU_WIDTH), lambda bi, i: (0, 0))],
        out_specs=out_specs,
        out_shape=out_shape,
        scratch_shapes=[pltpu.VMEM((d // LANE, tm, LANE), F32), pltpu.VMEM((CONV_TAIL + tm, LRU_WIDTH), F32)],
        compiler_params=_cparams(("parallel", "arbitrary"), VMEM_LIMIT),
        name="projection",
    )(x, mod3, g1, w_r, conv_w, conv_b.reshape(1, LRU_WIDTH))


def _attn_kernel(q0, k0, v0, q1, k1, v1, q2, k2, v2, o_ref, acc_ref, lse_ref, bias_ref, *, seq):
    hcols = o_ref.shape[-1]
    n_head = hcols // HEAD_DIM
    head_of_lane = lax.broadcasted_iota(jnp.int32, (SPAN, hcols), 1) // HEAD_DIM
    head_mask_b = [jnp.where(head_of_lane == h, 1.0, 0.0).astype(BF16) for h in range(n_head)]

    def by_head(parts):
        out = parts[n_head - 1]
        for h in range(n_head - 2, -1, -1):
            out = jnp.where(head_of_lane == h, parts[h], out)
        return out

    qi = lax.broadcasted_iota(jnp.int32, (n_head * SPAN, 2 * SPAN), 0) % SPAN
    ki = lax.broadcasted_iota(jnp.int32, (n_head * SPAN, 2 * SPAN), 1)
    band = (ki >= qi) & (ki <= qi + SPAN)
    bias_ref[0] = jnp.where(band, 0.0, -jnp.inf)
    bias_ref[1] = jnp.where(band & (ki >= SPAN), 0.0, -jnp.inf)

    for g, (q_ref, k_ref, v_ref) in enumerate(((q0, k0, v0), (q1, k1, v1), (q2, k2, v2))):
        d = DILATIONS[g]
        n_blk = seq // d // SPAN

        def tile(n, carry, q_ref=q_ref, k_ref=k_ref, v_ref=v_ref, d=d, n_blk=n_blk, g=g):
            p = n // n_blk
            blk = n % n_blk
            r0 = pl.multiple_of(blk * SPAN, SPAN)
            rp = pl.multiple_of(jnp.maximum(blk - 1, 0) * SPAN, SPAN)
            q = q_ref[p, pl.ds(r0, SPAN), :]
            kk = jnp.concatenate([k_ref[p, pl.ds(rp, SPAN), :], k_ref[p, pl.ds(r0, SPAN), :]], axis=0)
            vv = jnp.concatenate([v_ref[p, pl.ds(rp, SPAN), :], v_ref[p, pl.ds(r0, SPAN), :]], axis=0)
            qs = jnp.concatenate([q * head_mask_b[h] for h in range(n_head)], axis=0)
            sc = lax.dot_general(qs, kk, (((1,), (1,)), ((), ())), preferred_element_type=F32)
            sc = sc + bias_ref[jnp.where(blk > 0, 0, 1)]
            mx = jnp.max(sc, axis=-1, keepdims=True)
            e = jnp.exp2(sc - mx)
            den = jnp.sum(e, axis=-1, keepdims=True)
            pv = jnp.dot(e.astype(BF16), vv, preferred_element_type=F32)
            lse = mx + jnp.log(den) * LOG2_E
            rows_of = lambda a: [a[h * SPAN:(h + 1) * SPAN] for h in range(n_head)]
            o = by_head(rows_of(pv)) / by_head(rows_of(den))
            l = by_head(rows_of(lse))
            start = p + d * r0
            for j in range(hcols // LANE):
                rows = pl.ds(start, SPAN, stride=d) if d > 1 else pl.ds(start, SPAN)
                acc_ref[g, j, rows, :] = o[:, j * LANE:(j + 1) * LANE]
                lse_ref[g, j, rows, :] = l[:, j * LANE:(j + 1) * LANE]
            return carry

        lax.fori_loop(0, seq // SPAN, tile, 0, unroll=16)

    chunk = 256

    def combine(c, carry):
        r = pl.multiple_of(c * chunk, chunk)
        for j in range(hcols // LANE):
            ls = [lse_ref[g, j, pl.ds(r, chunk), :] for g in range(len(DILATIONS))]
            mx = jnp.maximum(jnp.maximum(ls[0], ls[1]), ls[2])
            ws = [jnp.exp2(v - mx) for v in ls]
            num = ws[0] * acc_ref[0, j, pl.ds(r, chunk), :]
            for g in range(1, len(DILATIONS)):
                num = num + ws[g] * acc_ref[g, j, pl.ds(r, chunk), :]
            o_ref[0, pl.ds(r, chunk), j * LANE:(j + 1) * LANE] = (num / (ws[0] + ws[1] + ws[2])).astype(BF16)
        return carry

    lax.fori_loop(0, seq // chunk, combine, 0)


def _attention(qkvs, b, s):
    hcols = 2 * HEAD_DIM
    n_hg = ATTN_OUT // hcols
    ncb = ATTN_OUT // hcols
    in_specs, args = [], []
    for g, d in enumerate(DILATIONS):
        for part in range(3):
            in_specs.append(pl.BlockSpec((d, s // d, hcols),
                                         lambda bi, hg, part=part: (bi, 0, part * ncb + hg)))
            args.append(qkvs[g])
    return pl.pallas_call(
        functools.partial(_attn_kernel, seq=s),
        grid=(b, n_hg),
        in_specs=in_specs,
        out_specs=pl.BlockSpec((1, s, hcols), lambda bi, hg: (bi, 0, hg)),
        out_shape=jax.ShapeDtypeStruct((b, s, ATTN_OUT), BF16),
        scratch_shapes=[pltpu.VMEM((len(DILATIONS), hcols // LANE, s, LANE), F32),
                        pltpu.VMEM((len(DILATIONS), hcols // LANE, s, LANE), F32),
                        pltpu.VMEM((2, (hcols // HEAD_DIM) * SPAN, 2 * SPAN), F32)],
        compiler_params=_cparams(("parallel", "parallel"), VMEM_LIMIT),
        name="dilated_attention",
    )(*args)


def _lru_kernel(xc_ref, gy_ref, wg_ref, bx_ref, ba_ref, lam_ref, o_ref, a_ref, b_ref, h_ref, *, pitch):
    nb, ts, tc = xc_ref.shape
    nl = tc // LANE

    @pl.when(pl.program_id(1) == 0)
    def _():
        h_ref[...] = jnp.zeros_like(h_ref)

    xb = xc_ref[...].reshape(nb * ts, tc)
    xc = xb.astype(F32)
    gates = jnp.dot(xb, wg_ref[0], preferred_element_type=F32)
    gate_i = jax.nn.sigmoid(gates[:, :tc] + bx_ref[...])
    gate_r = jax.nn.sigmoid(gates[:, tc:] + ba_ref[...])
    neg_lam = -lam_ref[...]
    softplus = jnp.maximum(neg_lam, 0.0) + jnp.log1p(jnp.exp(-jnp.abs(neg_lam)))
    log_a = (-LRU_C) * gate_r * softplus
    a = jnp.exp(log_a)
    one_m_a2 = jnp.tanh(-log_a) * (1.0 + a * a)
    mult = jnp.where(one_m_a2 > 0.0, one_m_a2 * lax.rsqrt(one_m_a2), 0.0)
    bv = mult * gate_i * xc
    for bi in range(nb):
        for j in range(nl):
            a_ref[j, pl.ds(bi * pitch, ts), :] = a[bi * ts:(bi + 1) * ts, j * LANE:(j + 1) * LANE]
            b_ref[j, pl.ds(bi * pitch, ts), :] = bv[bi * ts:(bi + 1) * ts, j * LANE:(j + 1) * LANE]

    def step(t, hs):
        out = []
        for j in range(nl):
            rows = pl.ds(t, nb, stride=pitch)
            h = a_ref[j, rows, :] * hs[j] + b_ref[j, rows, :]
            b_ref[j, rows, :] = h
            out.append(h)
        return tuple(out)

    hs = lax.fori_loop(0, ts, step, tuple(h_ref[j] for j in range(nl)), unroll=8)
    for j in range(nl):
        h_ref[j] = hs[j]
    for bi in range(nb):
        h = jnp.concatenate([b_ref[j, pl.ds(bi * pitch, ts), :] for j in range(nl)], axis=1)
        o_ref[bi] = (h * gy_ref[bi].astype(F32)).astype(BF16)


def _lru_gate_weights(wx, wa, tc):
    nb, bd, _ = wx.shape
    per = tc // bd
    eye = jnp.eye(per, dtype=wx.dtype)

    def bdiag(w):
        w = w.reshape(nb // per, per, bd, bd)
        return jnp.einsum('cpio,pq->cpiqo', w, eye).reshape(nb // per, tc, tc)

    return jnp.concatenate([bdiag(wx), bdiag(wa)], axis=-1).astype(BF16)


def _lru_branch(xc, gy, wx, bx, wa, ba, lam, *, tc=256, ts=128):
    b, s, c = xc.shape
    assert s % ts == 0 and c % tc == 0
    wg = _lru_gate_weights(wx, wa, tc)
    row = lambda v: v.reshape(1, c)
    tile = pl.BlockSpec((b, ts, tc), lambda ci, ti: (0, ti, ci))
    vec = pl.BlockSpec((1, tc), lambda ci, ti: (0, ci))
    pitch = ts + 8
    return pl.pallas_call(
        functools.partial(_lru_kernel, pitch=pitch),
        grid=(c // tc, s // ts),
        in_specs=[tile, tile,
                  pl.BlockSpec((1, tc, 2 * tc), lambda ci, ti: (ci, 0, 0)),
                  vec, vec, vec],
        out_specs=tile,
        out_shape=jax.ShapeDtypeStruct((b, s, c), BF16),
        scratch_shapes=[pltpu.VMEM((tc // LANE, b * pitch, LANE), F32),
                        pltpu.VMEM((tc // LANE, b * pitch, LANE), F32),
                        pltpu.VMEM((tc // LANE, b, LANE), F32)],
        compiler_params=_cparams(("parallel", "arbitrary"), VMEM_LIMIT),
        name="rg_lru",
    )(xc, gy, wg, row(bx), row(ba), row(lam))


def _prep_w_in(w_in):
    a = ATTN_WIDTH
    gw = N_SLOTS * HEAD_DIM
    q = w_in[:, :a] * (HEAD_DIM ** -0.5 * LOG2_E)
    k = w_in[:, a:2 * a]
    v = w_in[:, 2 * a:3 * a]
    parts = []
    for g in range(len(DILATIONS)):
        sl = slice(g * gw, (g + 1) * gw)
        parts += [q[:, sl], k[:, sl], v[:, sl]]
    parts.append(w_in[:, 3 * a:3 * a + 2 * LRU_WIDTH])
    return jnp.concatenate(parts, axis=1).astype(BF16), w_in[:, 3 * a + 2 * LRU_WIDTH:].astype(BF16)


ROW_SUBLANES = D_MODEL // 2 // LANE


def _store_tile_rows(ref, v, row0=0):
    n, half = v.shape[0], v.shape[1] // 2
    lo = pltpu.bitcast(v[:, :half].astype(BF16).astype(F32), jnp.uint32)
    hi = pltpu.bitcast(v[:, half:].astype(BF16).astype(F32), jnp.uint32)
    words = (hi & jnp.uint32(0xFFFF0000)) | (lo >> 16)
    for j in range(ROW_SUBLANES):
        ref[pl.ds(row0 * ROW_SUBLANES + j, n, stride=ROW_SUBLANES), :] = words[:, j * LANE:(j + 1) * LANE]


def _load_tile_rows(ref, n=None):
    n = ref.shape[0] // ROW_SUBLANES if n is None else n
    words = [ref[pl.ds(j, n, stride=ROW_SUBLANES), :] for j in range(ROW_SUBLANES)]
    lo = [pltpu.bitcast(w << 16, F32) for w in words]
    hi = [pltpu.bitcast(w & jnp.uint32(0xFFFF0000), F32) for w in words]
    return jnp.concatenate(lo + hi, axis=-1)


SC_CORES, SC_SUBCORES = 2, 16
SC_CHUNK = 128


def _sc_gather_rows(table, idx):
    n = idx.shape[0]
    per_worker = n // (SC_CORES * SC_SUBCORES)
    n_chunks = per_worker // SC_CHUNK
    assert n_chunks * SC_CHUNK * SC_CORES * SC_SUBCORES == n
    mesh = plsc.VectorSubcoreMesh(core_axis_name="c", subcore_axis_name="s")

    def body(table_hbm, idx_hbm, out_hbm, idx_v, rows_v, sem):
        base = (lax.axis_index("s") * SC_CORES + lax.axis_index("c")) * per_worker

        @pl.loop(0, n_chunks)
        def _(i):
            off = pl.multiple_of(base + i * SC_CHUNK, SC_CHUNK)
            pltpu.sync_copy(idx_hbm.at[pl.ds(off, SC_CHUNK)], idx_v)
            pltpu.async_copy(table_hbm.at[idx_v], rows_v, sem).wait()
            pltpu.sync_copy(rows_v, out_hbm.at[pl.ds(off, SC_CHUNK)])

    return pl.kernel(
        body, mesh=mesh,
        out_type=jax.ShapeDtypeStruct((n,) + table.shape[1:], table.dtype),
        scratch_types=[pltpu.VMEM((SC_CHUNK,), jnp.int32),
                       pltpu.VMEM((SC_CHUNK,) + table.shape[1:], table.dtype),
                       pltpu.SemaphoreType.DMA],
        name="sc_gather_rows",
    )(table, idx)


def _sc_scatter_rows(rows, idx, n_out):
    n_rows = rows.shape[0]
    n_choice = idx.shape[0] // n_rows
    per_worker = n_rows // (SC_CORES * SC_SUBCORES)
    n_chunks = per_worker // SC_CHUNK
    assert n_chunks * SC_CHUNK * SC_CORES * SC_SUBCORES == n_rows and n_choice * n_rows == idx.shape[0]
    mesh = plsc.VectorSubcoreMesh(core_axis_name="c", subcore_axis_name="s")

    def body(rows_hbm, idx_hbm, out_hbm, idx_v, rows_v):
        base = (lax.axis_index("s") * SC_CORES + lax.axis_index("c")) * per_worker

        @pl.loop(0, n_chunks)
        def _(i):
            off = pl.multiple_of(base + i * SC_CHUNK, SC_CHUNK)
            pltpu.sync_copy(rows_hbm.at[pl.ds(off, SC_CHUNK)], rows_v)
            for k in range(n_choice):
                pltpu.sync_copy(idx_hbm.at[pl.ds(k * n_rows + off, SC_CHUNK)], idx_v)
                pltpu.sync_copy(rows_v, out_hbm.at[idx_v])

    return pl.kernel(
        body, mesh=mesh,
        out_type=jax.ShapeDtypeStruct((n_out,) + rows.shape[1:], rows.dtype),
        scratch_types=[pltpu.VMEM((SC_CHUNK,), jnp.int32),
                       pltpu.VMEM((SC_CHUNK,) + rows.shape[1:], rows.dtype)],
        name="sc_scatter_rows",
    )(rows, idx)


ROUTE_ROWS = 8
EXPERT_ROW0 = N_GROUPS
ROUTER_ROWS = 48


def _mix_kernel(attn_ref, lru_ref, x_ref, mod_ref, g1_ref, wg_ref, wa_ref, wl_ref, wo_ref, g2_ref,
                wrt_ref, brt_ref, x1_ref, h2_ref, route_ref, cnt_ref, cnt_acc):
    d = x_ref.shape[-1]
    tm = x_ref.shape[1]

    @pl.when((pl.program_id(0) == 0) & (pl.program_id(1) == 0))
    def _():
        cnt_acc[...] = jnp.zeros_like(cnt_acc)

    m = mod_ref[0]
    gate1, shift2, scale2 = m[:, 2 * d:3 * d], m[:, 3 * d:4 * d], m[:, 4 * d:5 * d]
    x = x_ref[0]
    h1 = _rms_mod(x, g1_ref[...], m[:, d:2 * d], m[:, 0:d]).astype(BF16)
    gates = jax.nn.sigmoid(jnp.dot(h1, wg_ref[...], preferred_element_type=F32))
    ya = jnp.dot(attn_ref[0], wa_ref[...], preferred_element_type=F32)
    yl = jnp.dot(lru_ref[0], wl_ref[...], preferred_element_type=F32)
    mixed = gates[:, :d] * ya + gates[:, d:] * yl
    y = jnp.dot(mixed.astype(BF16), wo_ref[...], preferred_element_type=F32)
    x1 = x + (1.0 + gate1) * y
    x1_ref[0] = x1.astype(BF16)
    h2 = _rms_mod(x1, g2_ref[...], scale2, shift2)
    _store_tile_rows(h2_ref, h2)
    logits = lax.dot_general(wrt_ref[...], h2.astype(BF16), (((1,), (1,)), ((), ())),
                             preferred_element_type=F32) + brt_ref[...]

    row = lax.broadcasted_iota(jnp.int32, logits.shape, 0)
    neg = -jnp.inf

    def top(vals):
        mx = jnp.max(vals, axis=0, keepdims=True)
        idx = jnp.min(jnp.where(vals == mx, row, ROUTER_ROWS), axis=0, keepdims=True)
        return mx, idx

    is_grp = row < N_GROUPS
    gmax, gidx = top(jnp.where(is_grp, logits, neg))
    grp_gate = 1.0 / jnp.sum(jnp.where(is_grp, jnp.exp(logits - gmax), 0.0), axis=0, keepdims=True)
    lo = EXPERT_ROW0 + EXPERTS_PER_GROUP * gidx
    el = jnp.where((row >= lo) & (row < lo + EXPERTS_PER_GROUP), logits, neg)
    v1, i1 = top(el)
    v2, i2 = top(jnp.where(row == i1, neg, el))
    e21 = jnp.exp(v2 - v1)
    wt1 = grp_gate / (1.0 + e21)
    wt2 = wt1 * e21

    oh1 = jnp.where(row == i1, 1.0, 0.0)
    oh2 = jnp.where(row == i2, 1.0, 0.0)
    ohs = oh1 + oh2
    rr = lax.broadcasted_iota(jnp.int32, (tm, tm), 0)
    cc = lax.broadcasted_iota(jnp.int32, (tm, tm), 1)
    earlier = jnp.where(rr < cc, 1.0, 0.0).astype(BF16)
    before = jnp.dot(ohs.astype(BF16), earlier, preferred_element_type=F32) + cnt_acc[...]
    rank1 = jnp.sum(oh1 * before, axis=0, keepdims=True)
    rank2 = jnp.sum(oh2 * before, axis=0, keepdims=True)
    cnt_acc[...] = cnt_acc[...] + jnp.sum(ohs, axis=1, keepdims=True)
    cnt_ref[...] = cnt_acc[...]

    vals = [(i1 - EXPERT_ROW0).astype(F32), (i2 - EXPERT_ROW0).astype(F32), rank1, rank2, wt1, wt2]
    out_row = lax.broadcasted_iota(jnp.int32, (ROUTE_ROWS, tm), 0)
    slab = jnp.zeros((ROUTE_ROWS, tm), F32)
    for j, v in enumerate(vals):
        slab = jnp.where(out_row == j, v, slab)
    route_ref[...] = slab


def _mix_route(attn, lru, x, mod3, g1, wg, wa, wl, wo, g2, wrt, brt, b0, nb, *, tm=512):
    _, s, d = x.shape
    spt = s // tm
    tok_in = lambda w: pl.BlockSpec((1, tm, w), lambda bi, i: (b0 + bi, i, 0))
    return pl.pallas_call(
        _mix_kernel,
        grid=(nb, spt),
        in_specs=[tok_in(attn.shape[-1]), tok_in(d), tok_in(d),
                  pl.BlockSpec((1, 1, mod3.shape[-1]), lambda bi, i: (b0 + bi, 0, 0)),
                  pl.BlockSpec((1, d), lambda bi, i: (0, 0)),
                  _resident(wg.shape), _resident(wa.shape), _resident(wl.shape), _resident(wo.shape),
                  pl.BlockSpec((1, d), lambda bi, i: (0, 0)),
                  _resident(wrt.shape),
                  pl.BlockSpec((ROUTER_ROWS, 1), lambda bi, i: (0, 0))],
        out_specs=[pl.BlockSpec((1, tm, d), lambda bi, i: (bi, i, 0)),
                   pl.BlockSpec((tm * ROW_SUBLANES, LANE), lambda bi, i: (bi * spt + i, 0)),
                   pl.BlockSpec((ROUTE_ROWS, tm), lambda bi, i: (0, bi * spt + i)),
                   pl.BlockSpec((ROUTER_ROWS, 1), lambda bi, i: (0, 0))],
        out_shape=[jax.ShapeDtypeStruct((nb, s, d), BF16),
                   jax.ShapeDtypeStruct((nb * s * ROW_SUBLANES, LANE), jnp.uint32),
                   jax.ShapeDtypeStruct((ROUTE_ROWS, nb * s), F32),
                   jax.ShapeDtypeStruct((ROUTER_ROWS, 1), F32)],
        scratch_shapes=[pltpu.VMEM((ROUTER_ROWS, 1), F32)],
        compiler_params=_cparams(("arbitrary", "arbitrary"), VMEM_LIMIT),
        name="mix_route",
    )(attn, lru, x, mod3, g1, wg, wa, wl, wo, g2, wrt, brt)


TOP_K = 2
EXPERT_BLOCK = 512
MOE_BATCH_RANGES = 2


def _expert_kernel(be_ref, bv_ref, first_ref, slot_ref, next_ref, nu_ref, x_ref, w1_hbm, w3_hbm, w2_hbm,
                   y_ref, wf1, wf3, wf2, wb1, wb3, wb2, sem):
    j = pl.program_id(0)
    half = EXPERT_BLOCK // 2

    def fetch(e, s):
        return [pltpu.make_async_copy(w1_hbm.at[e], wf1.at[s], sem.at[s, 0]),
                pltpu.make_async_copy(w3_hbm.at[e], wf3.at[s], sem.at[s, 1]),
                pltpu.make_async_copy(w2_hbm.at[e], wf2.at[s], sem.at[s, 2])]

    def ffn(n):
        xb = _load_tile_rows(x_ref, n).astype(BF16)
        a = jnp.dot(xb, wb1[...], preferred_element_type=F32)
        g = jnp.dot(xb, wb3[...], preferred_element_type=F32)
        hm = (a * jax.nn.sigmoid(a) * g).astype(BF16)
        _store_tile_rows(y_ref, jnp.dot(hm, wb2[...], preferred_element_type=F32))

    @pl.when(j < nu_ref[0])
    def _():
        @pl.when(first_ref[j] == 1)
        def _():
            e, s = be_ref[j], slot_ref[j]

            @pl.when(j == 0)
            def _():
                for c in fetch(e, s):
                    c.start()

            @pl.when(next_ref[j] >= 0)
            def _():
                for c in fetch(next_ref[j], 1 - s):
                    c.start()

            for c in fetch(e, s):
                c.wait()
            wb1[...] = wf1[s].astype(BF16)
            wb3[...] = wf3[s].astype(BF16)
            wb2[...] = wf2[s].astype(BF16)

        @pl.when(bv_ref[j] > half)
        def _():
            ffn(EXPERT_BLOCK)

        @pl.when(bv_ref[j] <= half)
        def _():
            ffn(half)
            y_ref[half * ROW_SUBLANES:, :] = jnp.zeros((half * ROW_SUBLANES, LANE), y_ref.dtype)

    @pl.when(j >= nu_ref[0])
    def _():
        y_ref[...] = jnp.zeros_like(y_ref)


def _experts(xp, plan, w1, w3, w2):
    ne, d, de = w1.shape
    nb = xp.shape[0] // (EXPERT_BLOCK * ROW_SUBLANES)
    rows = (EXPERT_BLOCK * ROW_SUBLANES, LANE)
    grid_spec = pltpu.PrefetchScalarGridSpec(
        num_scalar_prefetch=len(plan),
        grid=(nb,),
        in_specs=[pl.BlockSpec(rows, lambda j, *p: (jnp.minimum(j, p[-1][0] - 1), 0)),
                  pl.BlockSpec(memory_space=pl.ANY),
                  pl.BlockSpec(memory_space=pl.ANY),
                  pl.BlockSpec(memory_space=pl.ANY)],
        out_specs=pl.BlockSpec(rows, lambda j, *p: (j, 0)),
        scratch_shapes=[pltpu.VMEM((2, d, de), F32), pltpu.VMEM((2, d, de), F32), pltpu.VMEM((2, de, d), F32),
                        pltpu.VMEM((d, de), BF16), pltpu.VMEM((d, de), BF16), pltpu.VMEM((de, d), BF16),
                        pltpu.SemaphoreType.DMA((2, 3))])
    return pl.pallas_call(
        _expert_kernel,
        grid_spec=grid_spec,
        out_shape=jax.ShapeDtypeStruct(xp.shape, xp.dtype),
        compiler_params=_cparams(("arbitrary",), VMEM_LIMIT),
        name="experts",
    )(*plan, xp, w1, w3, w2)


def _combine_kernel(y0_ref, y1_ref, route_ref, x1_ref, mod_ref, gf_ref, *rest):
    o_ref = rest[-1]
    tm, d = x1_ref.shape[1], x1_ref.shape[2]
    route = jnp.concatenate([route_ref[...], jnp.zeros((LANE - ROUTE_ROWS, tm), F32)], axis=0).T
    moe = _load_tile_rows(y0_ref) * route[:, 4:5] + _load_tile_rows(y1_ref) * route[:, 5:6]
    gate2 = mod_ref[0][:, 5 * d:6 * d]
    xo = x1_ref[0].astype(F32) + (1.0 + gate2) * moe
    ms = jnp.mean(xo * xo, axis=-1, keepdims=True)
    o_ref[0] = xo * lax.rsqrt(ms + EPS) * gf_ref[...]


def _combine(yg, route, x1, mod3, gf, b0, out_prev, *, tm=512):
    nb, s, d = x1.shape
    b_all = mod3.shape[0]
    spt = s // tm
    nt = nb * spt
    rows = (tm * ROW_SUBLANES, LANE)
    in_specs = [pl.BlockSpec(rows, lambda bi, i: (bi * spt + i, 0)),
                pl.BlockSpec(rows, lambda bi, i: (nt + bi * spt + i, 0)),
                pl.BlockSpec((ROUTE_ROWS, tm), lambda bi, i: (0, bi * spt + i)),
                pl.BlockSpec((1, tm, d), lambda bi, i: (bi, i, 0)),
                pl.BlockSpec((1, 1, mod3.shape[-1]), lambda bi, i: (b0 + bi, 0, 0)),
                pl.BlockSpec((1, d), lambda bi, i: (0, 0))]
    args = [yg, yg, route, x1, mod3, gf]
    aliases = {}
    if out_prev is not None:
        in_specs.append(pl.BlockSpec(memory_space=pl.ANY))
        aliases = {len(args): 0}
        args.append(out_prev)
    return pl.pallas_call(
        _combine_kernel,
        grid=(nb, spt),
        in_specs=in_specs,
        out_specs=pl.BlockSpec((1, tm, d), lambda bi, i: (b0 + bi, i, 0)),
        out_shape=jax.ShapeDtypeStruct((b_all, s, d), F32),
        input_output_aliases=aliases,
        compiler_params=_cparams(("parallel", "parallel"), VMEM_LIMIT),
        name="combine",
    )(*args)


def _slot_plan(route, counts, n_tok):
    sizes = counts[EXPERT_ROW0:EXPERT_ROW0 + N_EXPERTS, 0].astype(jnp.int32)
    padded = (sizes + EXPERT_BLOCK - 1) // EXPERT_BLOCK * EXPERT_BLOCK
    pad_ends = jnp.cumsum(padded)
    pad_starts = pad_ends - padded
    eid = route[0:TOP_K].astype(jnp.int32)
    rank = route[TOP_K:2 * TOP_K].astype(jnp.int32)
    start = jnp.sum(jnp.where(eid[..., None] == jnp.arange(N_EXPERTS), pad_starts, 0), axis=-1)
    dest = (start + rank).reshape(TOP_K * n_tok)
    n_blocks = (n_tok * TOP_K + N_EXPERTS * (EXPERT_BLOCK - 1) + EXPERT_BLOCK - 1) // EXPERT_BLOCK
    gran = SC_CORES * SC_SUBCORES * SC_CHUNK // math.gcd(SC_CORES * SC_SUBCORES * SC_CHUNK, EXPERT_BLOCK)
    n_blocks = (n_blocks + gran - 1) // gran * gran
    n_used = pad_ends[-1] // EXPERT_BLOCK
    blk = jnp.minimum(jnp.arange(n_blocks), n_used - 1)
    blk_e = jnp.minimum(jnp.sum(pad_ends[None, :] <= (blk * EXPERT_BLOCK)[:, None], axis=1), N_EXPERTS - 1)
    blk_valid = jnp.clip(sizes[blk_e] - (blk * EXPERT_BLOCK - pad_starts[blk_e]), 0, EXPERT_BLOCK)
    idx = jnp.arange(n_blocks)
    first = (idx < n_used) & ((idx == 0) | (blk_e != jnp.roll(blk_e, 1)))
    slot = (jnp.cumsum(first) - 1) % 2
    later_first = lax.cummin(jnp.where(first, idx, n_blocks), reverse=True)
    next_first = jnp.concatenate([later_first[1:], jnp.full((1,), n_blocks)])
    next_e = jnp.where(next_first < n_blocks, blk_e[jnp.minimum(next_first, n_blocks - 1)], -1)
    i32 = lambda a: a.astype(jnp.int32)
    plan = (i32(blk_e), i32(blk_valid), i32(first), i32(slot), i32(next_e), i32(n_used.reshape(1)))
    return dest, n_blocks * EXPERT_BLOCK, plan


def kernel(x, c, w_mod, b_mod, norm1_g, w_in, conv_w, conv_b, lru_wx, lru_bx, lru_wa, lru_ba, lru_lambda, w_attn_o, w_lru_o, w_out, norm2_g, w_grp, b_grp, w_exp, b_exp, w1, w3, w2, norm_f_g):
    b, s, d = x.shape
    assert d == D_MODEL and s == SPAN * DILATIONS[-1] and w_mod.shape[0] == 1
    mod3 = _modulation(c, w_mod[0], b_mod[0]).reshape(b, 1, 6 * d)
    w_proj, w_gate = _prep_w_in(w_in[0])
    g1 = norm1_g[0].reshape(1, d)
    qkv0, qkv1, qkv2, xc, gy = _projection(x, mod3, g1, w_proj, conv_w[0], conv_b[0])
    attn = _attention((qkv0, qkv1, qkv2), b, s)
    lru = _lru_branch(xc, gy, lru_wx[0], lru_bx[0], lru_wa[0], lru_ba[0], lru_lambda[0])

    n_pad = ROUTER_ROWS - N_GROUPS - N_EXPERTS
    wr = jnp.pad(jnp.concatenate([w_grp[0], w_exp[0]], axis=1).T, ((0, n_pad), (0, 0))).astype(BF16)
    br = jnp.pad(jnp.concatenate([b_grp[0], b_exp[0]]), (0, n_pad)).reshape(ROUTER_ROWS, 1)
    wa, wl, wo = w_attn_o[0].astype(BF16), w_lru_o[0].astype(BF16), w_out[0].astype(BF16)
    as_rows = lambda a: a.reshape(-1, ROW_SUBLANES, LANE)
    as_tiles = lambda a: a.reshape(-1, LANE)

    out = None
    nb = b // MOE_BATCH_RANGES
    for b0 in range(0, b, nb):
        x1, h2, route, counts = _mix_route(attn, lru, x, mod3, g1, w_gate, wa, wl, wo,
                                           norm2_g[0].reshape(1, d), wr, br, b0, nb)
        dest, n_slots, plan = _slot_plan(route, counts, nb * s)
        xp = as_tiles(_sc_scatter_rows(as_rows(h2), dest, n_slots))
        yp = _experts(xp, plan, w1[0], w3[0], w2[0])
        yg = as_tiles(_sc_gather_rows(as_rows(yp), dest))
        out = _combine(yg, route, x1, mod3, norm_f_g.reshape(1, d), b0, out)
    return out
```

```python
import functools
import math

import jax
import jax.numpy as jnp
from jax import lax
from jax.experimental import pallas as pl
from jax.experimental.pallas import tpu as pltpu
from jax.experimental.pallas import tpu_sc as plsc

F32 = jnp.float32
BF16 = jnp.bfloat16

D_MODEL = 1024
HEAD_DIM = 64
N_SLOTS = 8
SPAN = 128
DILATIONS = (1, 4, 16)
GROUP_COLS = 3 * N_SLOTS * HEAD_DIM
ATTN_WIDTH = len(DILATIONS) * N_SLOTS * HEAD_DIM
ATTN_OUT = N_SLOTS * HEAD_DIM
LRU_WIDTH = D_MODEL
CONV_WIDTH = 4
CONV_TAIL = 8
LRU_C = 8.0
N_GROUPS = 4
EXPERTS_PER_GROUP = 8
N_EXPERTS = N_GROUPS * EXPERTS_PER_GROUP
EPS = 1e-6
LOG2_E = 1.4426950408889634
LANE = 128
VMEM_LIMIT = 56 * 1024 * 1024


def _cparams(sem, vmem=None):
    return pltpu.CompilerParams(dimension_semantics=sem, vmem_limit_bytes=vmem)


def _resident(shape):
    nd = len(shape)
    return pl.BlockSpec(shape, lambda *_: (0,) * nd, pipeline_mode=pl.Buffered(1))


def _mod_kernel(c_ref, w_ref, b_ref, o_ref):
    c = c_ref[...]
    ca = c * jax.nn.sigmoid(c)
    o_ref[...] = jnp.dot(ca.astype(BF16), w_ref[...].astype(BF16),
                         preferred_element_type=F32) + b_ref[...]


def _modulation(c, w_mod, b_mod):
    b, d = c.shape
    n = w_mod.shape[1]
    tn = n // 4
    return pl.pallas_call(
        _mod_kernel,
        grid=(n // tn,),
        in_specs=[pl.BlockSpec((b, d), lambda j: (0, 0)),
                  pl.BlockSpec((d, tn), lambda j: (0, j)),
                  pl.BlockSpec((1, tn), lambda j: (0, j))],
        out_specs=pl.BlockSpec((b, tn), lambda j: (0, j)),
        out_shape=jax.ShapeDtypeStruct((b, n), F32),
        compiler_params=_cparams(("arbitrary",)),
        name="modulation",
    )(c, w_mod, b_mod.reshape(1, n))


def _rms_mod(x, g, scale, shift):
    ms = jnp.mean(x * x, axis=-1, keepdims=True)
    return x * lax.rsqrt(ms + EPS) * g * (1.0 + scale) + shift


def _gelu_tanh(y):
    return y * (0.5 * (1.0 + jnp.tanh(0.7978845608028654 * (y + 0.044715 * (y * y * y)))))


def _proj_kernel(x_ref, mod_ref, g_ref, w_ref, cw_ref, cb_ref, qkv0_ref, qkv1_ref, qkv2_ref,
                 xc_ref, gy_ref, hs_ref, xe_ref, *, tm):
    @pl.when(pl.program_id(1) == 0)
    def _():
        xe_ref[0:CONV_TAIL, :] = jnp.zeros((CONV_TAIL, LRU_WIDTH), F32)

    @pl.when(pl.program_id(1) > 0)
    def _():
        xe_ref[0:CONV_TAIL, :] = xe_ref[tm:tm + CONV_TAIL, :]

    d_model = x_ref.shape[-1]
    m = mod_ref[0]
    h = _rms_mod(x_ref[0], g_ref[...], m[:, d_model:2 * d_model], m[:, 0:d_model])

    def mm(hv, lo, hi):
        return jnp.dot(hv, w_ref[:, lo:hi], preferred_element_type=F32)

    hb = h.astype(BF16)
    c0 = len(DILATIONS) * GROUP_COLS
    qkv0_ref[0] = mm(hb, 0, GROUP_COLS).astype(BF16)
    xr = mm(hb, c0, c0 + LRU_WIDTH)
    xe_ref[CONV_TAIL:, :] = xr
    cw = cw_ref[...]
    xc = xr * cw[CONV_WIDTH - 1:CONV_WIDTH] + cb_ref[...]
    for k in range(1, CONV_WIDTH):
        xc = xc + xe_ref[CONV_TAIL - k:CONV_TAIL - k + tm, :] * cw[CONV_WIDTH - 1 - k:CONV_WIDTH - k]
    xc_ref[0] = xc.astype(BF16)
    gy_ref[0] = _gelu_tanh(mm(hb, c0 + LRU_WIDTH, c0 + 2 * LRU_WIDTH)).astype(BF16)

    n_slab = d_model // LANE
    for j in range(n_slab):
        hs_ref[j] = h[:, j * LANE:(j + 1) * LANE]
    for g, out_ref in ((1, qkv1_ref), (2, qkv2_ref)):
        d = DILATIONS[g]
        rows = tm // d
        hp = jnp.concatenate(
            [jnp.concatenate([hs_ref[j, pl.ds(p, rows, stride=d), :] for j in range(n_slab)], axis=1)
             for p in range(d)], axis=0).astype(BF16)
        res = mm(hp, g * GROUP_COLS, (g + 1) * GROUP_COLS).astype(BF16)
        for p in range(d):
            out_ref[p] = res[p * rows:(p + 1) * rows]


def _projection(x, mod3, g1, w_r, conv_w, conv_b, *, tm=512):
    b, s, d = x.shape
    n = w_r.shape[1]
    assert s % tm == 0 and tm % (16 * DILATIONS[-1]) == 0 and CONV_TAIL >= CONV_WIDTH - 1
    out_shape = [jax.ShapeDtypeStruct((b * dd, s // dd, GROUP_COLS), BF16) for dd in DILATIONS]
    out_shape += [jax.ShapeDtypeStruct((b, s, LRU_WIDTH), BF16),
                  jax.ShapeDtypeStruct((b, s, LRU_WIDTH), BF16)]
    out_specs = [pl.BlockSpec((dd, tm // dd, GROUP_COLS), lambda bi, i: (bi, i, 0)) for dd in DILATIONS]
    out_specs += [pl.BlockSpec((1, tm, LRU_WIDTH), lambda bi, i: (bi, i, 0)),
                  pl.BlockSpec((1, tm, LRU_WIDTH), lambda bi, i: (bi, i, 0))]
    return pl.pallas_call(
        functools.partial(_proj_kernel, tm=tm),
        grid=(b, s // tm),
        in_specs=[pl.BlockSpec((1, tm, d), lambda bi, i: (bi, i, 0)),
                  pl.BlockSpec((1, 1, mod3.shape[-1]), lambda bi, i: (bi, 0, 0)),
                  pl.BlockSpec((1, d), lambda bi, i: (0, 0)),
                  _resident((d, n)),
                  pl.BlockSpec((CONV_WIDTH, LRU_WIDTH), lambda bi, i: (0, 0)),
                  pl.BlockSpec((1, LRU_WIDTH), lambda bi, i: (0, 0))],
        out_specs=out_specs,
        out_shape=out_shape,
        scratch_shapes=[pltpu.VMEM((d // LANE, tm, LANE), F32), pltpu.VMEM((CONV_TAIL + tm, LRU_WIDTH), F32)],
        compiler_params=_cparams(("parallel", "arbitrary"), VMEM_LIMIT),
        name="projection",
    )(x, mod3, g1, w_r, conv_w, conv_b.reshape(1, LRU_WIDTH))


def _attn_kernel(q0, k0, v0, q1, k1, v1, q2, k2, v2, o_ref, acc_ref, lse_ref, bias_ref, *, seq):
    hcols = o_ref.shape[-1]
    n_head = hcols // HEAD_DIM
    head_of_lane = lax.broadcasted_iota(jnp.int32, (SPAN, hcols), 1) // HEAD_DIM
    head_mask_b = [jnp.where(head_of_lane == h, 1.0, 0.0).astype(BF16) for h in range(n_head)]

    def by_head(parts):
        out = parts[n_head - 1]
        for h in range(n_head - 2, -1, -1):
            out = jnp.where(head_of_lane == h, parts[h], out)
        return out

    @pl.when((pl.program_id(0) == 0) & (pl.program_id(1) == 0))
    def _():
        qi = lax.broadcasted_iota(jnp.int32, (n_head * SPAN, 2 * SPAN), 0) % SPAN
        ki = lax.broadcasted_iota(jnp.int32, (n_head * SPAN, 2 * SPAN), 1)
        band = (ki >= qi) & (ki <= qi + SPAN)
        bias_ref[0] = jnp.where(band, 0.0, -jnp.inf)
        bias_ref[1] = jnp.where(band & (ki >= SPAN), 0.0, -jnp.inf)

    for g, (q_ref, k_ref, v_ref) in enumerate(((q0, k0, v0), (q1, k1, v1), (q2, k2, v2))):
        d = DILATIONS[g]
        n_blk = seq // d // SPAN

        def tile(n, carry, q_ref=q_ref, k_ref=k_ref, v_ref=v_ref, d=d, n_blk=n_blk, g=g):
            p = n // n_blk
            blk = n % n_blk
            r0 = pl.multiple_of(blk * SPAN, SPAN)
            rp = pl.multiple_of(jnp.maximum(blk - 1, 0) * SPAN, SPAN)
            q = q_ref[p, pl.ds(r0, SPAN), :]
            kk = jnp.concatenate([k_ref[p, pl.ds(rp, SPAN), :], k_ref[p, pl.ds(r0, SPAN), :]], axis=0)
            vv = jnp.concatenate([v_ref[p, pl.ds(rp, SPAN), :], v_ref[p, pl.ds(r0, SPAN), :]], axis=0)
            qs = jnp.concatenate([q * head_mask_b[h] for h in range(n_head)], axis=0)
            sc = lax.dot_general(qs, kk, (((1,), (1,)), ((), ())), preferred_element_type=F32)
            sc = sc + bias_ref[jnp.where(blk > 0, 0, 1)]
            mx = jnp.max(sc, axis=-1, keepdims=True)
            e = jnp.exp2(sc - mx)
            den = jnp.sum(e, axis=-1, keepdims=True)
            pv = jnp.dot(e.astype(BF16), vv, preferred_element_type=F32)
            lse = mx + jnp.log(den) * LOG2_E
            rows_of = lambda a: [a[h * SPAN:(h + 1) * SPAN] for h in range(n_head)]
            o = by_head(rows_of(pv)) / by_head(rows_of(den))
            l = by_head(rows_of(lse))
            start = p + d * r0
            for j in range(hcols // LANE):
                rows = pl.ds(start, SPAN, stride=d) if d > 1 else pl.ds(start, SPAN)
                acc_ref[g, j, rows, :] = o[:, j * LANE:(j + 1) * LANE]
                lse_ref[g, j, rows, :] = l[:, j * LANE:(j + 1) * LANE]
            return carry

        lax.fori_loop(0, seq // SPAN, tile, 0, unroll=16)

    chunk = 256

    def combine(c, carry):
        r = pl.multiple_of(c * chunk, chunk)
        for j in range(hcols // LANE):
            ls = [lse_ref[g, j, pl.ds(r, chunk), :] for g in range(len(DILATIONS))]
            mx = jnp.maximum(jnp.maximum(ls[0], ls[1]), ls[2])
            ws = [jnp.exp2(v - mx) for v in ls]
            num = ws[0] * acc_ref[0, j, pl.ds(r, chunk), :]
            for g in range(1, len(DILATIONS)):
                num = num + ws[g] * acc_ref[g, j, pl.ds(r, chunk), :]
            o_ref[0, pl.ds(r, chunk), j * LANE:(j + 1) * LANE] = (num / (ws[0] + ws[1] + ws[2])).astype(BF16)
        return carry

    lax.fori_loop(0, seq // chunk, combine, 0)


def _attention(qkvs, b, s):
    hcols = 2 * HEAD_DIM
    n_hg = ATTN_OUT // hcols
    ncb = ATTN_OUT // hcols
    in_specs, args = [], []
    for g, d in enumerate(DILATIONS):
        for part in range(3):
            in_specs.append(pl.BlockSpec((d, s // d, hcols),
                                         lambda bi, hg, part=part: (bi, 0, part * ncb + hg)))
            args.append(qkvs[g])
    return pl.pallas_call(
        functools.partial(_attn_kernel, seq=s),
        grid=(b, n_hg),
        in_specs=in_specs,
        out_specs=pl.BlockSpec((1, s, hcols), lambda bi, hg: (bi, 0, hg)),
        out_shape=jax.ShapeDtypeStruct((b, s, ATTN_OUT), BF16),
        scratch_shapes=[pltpu.VMEM((len(DILATIONS), hcols // LANE, s, LANE), F32),
                        pltpu.VMEM((len(DILATIONS), hcols // LANE, s, LANE), F32),
                        pltpu.VMEM((2, (hcols // HEAD_DIM) * SPAN, 2 * SPAN), F32)],
        compiler_params=_cparams(("arbitrary", "arbitrary"), VMEM_LIMIT),
        name="dilated_attention",
    )(*args)


def _lru_kernel(xc_ref, gy_ref, wg_ref, bx_ref, ba_ref, lam_ref, o_ref, a_ref, b_ref, h_ref, *, pitch):
    nb, ts, tc = xc_ref.shape
    nl = tc // LANE

    @pl.when(pl.program_id(1) == 0)
    def _():
        h_ref[...] = jnp.zeros_like(h_ref)

    xb = xc_ref[...].reshape(nb * ts, tc)
    xc = xb.astype(F32)
    gates = jnp.dot(xb, wg_ref[0], preferred_element_type=F32)
    gate_i = jax.nn.sigmoid(gates[:, :tc] + bx_ref[...])
    gate_r = jax.nn.sigmoid(gates[:, tc:] + ba_ref[...])
    neg_lam = -lam_ref[...]
    softplus = jnp.maximum(neg_lam, 0.0) + jnp.log1p(jnp.exp(-jnp.abs(neg_lam)))
    log_a = (-LRU_C) * gate_r * softplus
    a = jnp.exp(log_a)
    one_m_a2 = jnp.tanh(-log_a) * (1.0 + a * a)
    mult = jnp.where(one_m_a2 > 0.0, one_m_a2 * lax.rsqrt(one_m_a2), 0.0)
    bv = mult * gate_i * xc
    for bi in range(nb):
        for j in range(nl):
            a_ref[j, pl.ds(bi * pitch, ts), :] = a[bi * ts:(bi + 1) * ts, j * LANE:(j + 1) * LANE]
            b_ref[j, pl.ds(bi * pitch, ts), :] = bv[bi * ts:(bi + 1) * ts, j * LANE:(j + 1) * LANE]

    def step(t, hs):
        out = []
        for j in range(nl):
            rows = pl.ds(t, nb, stride=pitch)
            h = a_ref[j, rows, :] * hs[j] + b_ref[j, rows, :]
            b_ref[j, rows, :] = h
            out.append(h)
        return tuple(out)

    hs = lax.fori_loop(0, ts, step, tuple(h_ref[j] for j in range(nl)), unroll=8)
    for j in range(nl):
        h_ref[j] = hs[j]
    for bi in range(nb):
        h = jnp.concatenate([b_ref[j, pl.ds(bi * pitch, ts), :] for j in range(nl)], axis=1)
        o_ref[bi] = (h * gy_ref[bi].astype(F32)).astype(BF16)


def _lru_gate_weights(wx, wa, tc):
    nb, bd, _ = wx.shape
    per = tc // bd
    eye = jnp.eye(per, dtype=wx.dtype)

    def bdiag(w):
        w = w.reshape(nb // per, per, bd, bd)
        return jnp.einsum('cpio,pq->cpiqo', w, eye).reshape(nb // per, tc, tc)

    return jnp.concatenate([bdiag(wx), bdiag(wa)], axis=-1).astype(BF16)


def _lru_branch(xc, gy, wx, bx, wa, ba, lam, *, tc=256, ts=256):
    b, s, c = xc.shape
    assert s % ts == 0 and c % tc == 0
    wg = _lru_gate_weights(wx, wa, tc)
    row = lambda v: v.reshape(1, c)
    tile = pl.BlockSpec((b, ts, tc), lambda ci, ti: (0, ti, ci))
    vec = pl.BlockSpec((1, tc), lambda ci, ti: (0, ci))
    pitch = ts + 8
    return pl.pallas_call(
        functools.partial(_lru_kernel, pitch=pitch),
        grid=(c // tc, s // ts),
        in_specs=[tile, tile,
                  pl.BlockSpec((1, tc, 2 * tc), lambda ci, ti: (ci, 0, 0)),
                  vec, vec, vec],
        out_specs=tile,
        out_shape=jax.ShapeDtypeStruct((b, s, c), BF16),
        scratch_shapes=[pltpu.VMEM((tc // LANE, b * pitch, LANE), F32),
                        pltpu.VMEM((tc // LANE, b * pitch, LANE), F32),
                        pltpu.VMEM((tc // LANE, b, LANE), F32)],
        compiler_params=_cparams(("parallel", "arbitrary"), VMEM_LIMIT),
        name="rg_lru",
    )(xc, gy, wg, row(bx), row(ba), row(lam))


def _prep_w_in(w_in):
    a = ATTN_WIDTH
    gw = N_SLOTS * HEAD_DIM
    q = w_in[:, :a] * (HEAD_DIM ** -0.5 * LOG2_E)
    k = w_in[:, a:2 * a]
    v = w_in[:, 2 * a:3 * a]
    parts = []
    for g in range(len(DILATIONS)):
        sl = slice(g * gw, (g + 1) * gw)
        parts += [q[:, sl], k[:, sl], v[:, sl]]
    parts.append(w_in[:, 3 * a:3 * a + 2 * LRU_WIDTH])
    return jnp.concatenate(parts, axis=1).astype(BF16), w_in[:, 3 * a + 2 * LRU_WIDTH:].astype(BF16)


ROW_SUBLANES = D_MODEL // 2 // LANE


def _store_tile_rows(ref, v, row0=0):
    n, half = v.shape[0], v.shape[1] // 2
    lo = pltpu.bitcast(v[:, :half].astype(BF16).astype(F32), jnp.uint32)
    hi = pltpu.bitcast(v[:, half:].astype(BF16).astype(F32), jnp.uint32)
    words = (hi & jnp.uint32(0xFFFF0000)) | (lo >> 16)
    for j in range(ROW_SUBLANES):
        ref[pl.ds(row0 * ROW_SUBLANES + j, n, stride=ROW_SUBLANES), :] = words[:, j * LANE:(j + 1) * LANE]


def _load_tile_rows(ref, n=None):
    n = ref.shape[0] // ROW_SUBLANES if n is None else n
    words = [ref[pl.ds(j, n, stride=ROW_SUBLANES), :] for j in range(ROW_SUBLANES)]
    lo = [pltpu.bitcast(w << 16, F32) for w in words]
    hi = [pltpu.bitcast(w & jnp.uint32(0xFFFF0000), F32) for w in words]
    return jnp.concatenate(lo + hi, axis=-1)


SC_CORES, SC_SUBCORES = 2, 16
SC_CHUNK = 128


def _sc_gather_rows(table, idx):
    n = idx.shape[0]
    per_worker = n // (SC_CORES * SC_SUBCORES)
    n_chunks = per_worker // SC_CHUNK
    assert n_chunks * SC_CHUNK * SC_CORES * SC_SUBCORES == n
    mesh = plsc.VectorSubcoreMesh(core_axis_name="c", subcore_axis_name="s")

    def body(table_hbm, idx_hbm, out_hbm, idx_v, rows_v, sem):
        base = (lax.axis_index("s") * SC_CORES + lax.axis_index("c")) * per_worker

        @pl.loop(0, n_chunks)
        def _(i):
            off = pl.multiple_of(base + i * SC_CHUNK, SC_CHUNK)
            pltpu.sync_copy(idx_hbm.at[pl.ds(off, SC_CHUNK)], idx_v)
            pltpu.async_copy(table_hbm.at[idx_v], rows_v, sem).wait()
            pltpu.sync_copy(rows_v, out_hbm.at[pl.ds(off, SC_CHUNK)])

    return pl.kernel(
        body, mesh=mesh,
        out_type=jax.ShapeDtypeStruct((n,) + table.shape[1:], table.dtype),
        scratch_types=[pltpu.VMEM((SC_CHUNK,), jnp.int32),
                       pltpu.VMEM((SC_CHUNK,) + table.shape[1:], table.dtype),
                       pltpu.SemaphoreType.DMA],
        name="sc_gather_rows",
    )(table, idx)


def _sc_scatter_rows(rows, idx, n_out):
    n_rows = rows.shape[0]
    n_choice = idx.shape[0] // n_rows
    per_worker = n_rows // (SC_CORES * SC_SUBCORES)
    n_chunks = per_worker // SC_CHUNK
    assert n_chunks * SC_CHUNK * SC_CORES * SC_SUBCORES == n_rows and n_choice * n_rows == idx.shape[0]
    mesh = plsc.VectorSubcoreMesh(core_axis_name="c", subcore_axis_name="s")

    def body(rows_hbm, idx_hbm, out_hbm, idx_v, rows_v):
        base = (lax.axis_index("s") * SC_CORES + lax.axis_index("c")) * per_worker

        @pl.loop(0, n_chunks)
        def _(i):
            off = pl.multiple_of(base + i * SC_CHUNK, SC_CHUNK)
            pltpu.sync_copy(rows_hbm.at[pl.ds(off, SC_CHUNK)], rows_v)
            for k in range(n_choice):
                pltpu.sync_copy(idx_hbm.at[pl.ds(k * n_rows + off, SC_CHUNK)], idx_v)
                pltpu.sync_copy(rows_v, out_hbm.at[idx_v])

    return pl.kernel(
        body, mesh=mesh,
        out_type=jax.ShapeDtypeStruct((n_out,) + rows.shape[1:], rows.dtype),
        scratch_types=[pltpu.VMEM((SC_CHUNK,), jnp.int32),
                       pltpu.VMEM((SC_CHUNK,) + rows.shape[1:], rows.dtype)],
        name="sc_scatter_rows",
    )(rows, idx)


ROUTE_ROWS = 8
EXPERT_ROW0 = N_GROUPS
ROUTER_ROWS = 48


def _mix_kernel(attn_ref, lru_ref, x_ref, mod_ref, g1_ref, wg_ref, wa_ref, wl_ref, wo_ref, g2_ref,
                wrt_ref, brt_ref, x1_ref, h2_ref, route_ref, cnt_ref, cnt_acc):
    d = x_ref.shape[-1]
    tm = x_ref.shape[1]

    @pl.when((pl.program_id(0) == 0) & (pl.program_id(1) == 0))
    def _():
        cnt_acc[...] = jnp.zeros_like(cnt_acc)

    m = mod_ref[0]
    gate1, shift2, scale2 = m[:, 2 * d:3 * d], m[:, 3 * d:4 * d], m[:, 4 * d:5 * d]
    x = x_ref[0]
    h1 = _rms_mod(x, g1_ref[...], m[:, d:2 * d], m[:, 0:d]).astype(BF16)
    gates = jax.nn.sigmoid(jnp.dot(h1, wg_ref[...], preferred_element_type=F32))
    ya = jnp.dot(attn_ref[0], wa_ref[...], preferred_element_type=F32)
    yl = jnp.dot(lru_ref[0], wl_ref[...], preferred_element_type=F32)
    mixed = gates[:, :d] * ya + gates[:, d:] * yl
    y = jnp.dot(mixed.astype(BF16), wo_ref[...], preferred_element_type=F32)
    x1 = x + (1.0 + gate1) * y
    x1_ref[0] = x1.astype(BF16)
    h2 = _rms_mod(x1, g2_ref[...], scale2, shift2)
    _store_tile_rows(h2_ref, h2)
    logits = lax.dot_general(wrt_ref[...], h2.astype(BF16), (((1,), (1,)), ((), ())),
                             preferred_element_type=F32) + brt_ref[...]

    row = lax.broadcasted_iota(jnp.int32, logits.shape, 0)
    neg = -jnp.inf

    def top(vals):
        mx = jnp.max(vals, axis=0, keepdims=True)
        idx = jnp.min(jnp.where(vals == mx, row, ROUTER_ROWS), axis=0, keepdims=True)
        return mx, idx

    is_grp = row < N_GROUPS
    gmax, gidx = top(jnp.where(is_grp, logits, neg))
    grp_gate = 1.0 / jnp.sum(jnp.where(is_grp, jnp.exp(logits - gmax), 0.0), axis=0, keepdims=True)
    lo = EXPERT_ROW0 + EXPERTS_PER_GROUP * gidx
    el = jnp.where((row >= lo) & (row < lo + EXPERTS_PER_GROUP), logits, neg)
    v1, i1 = top(el)
    v2, i2 = top(jnp.where(row == i1, neg, el))
    e21 = jnp.exp(v2 - v1)
    wt1 = grp_gate / (1.0 + e21)
    wt2 = wt1 * e21

    oh1 = jnp.where(row == i1, 1.0, 0.0)
    oh2 = jnp.where(row == i2, 1.0, 0.0)
    ohs = oh1 + oh2
    rr = lax.broadcasted_iota(jnp.int32, (tm, tm), 0)
    cc = lax.broadcasted_iota(jnp.int32, (tm, tm), 1)
    earlier = jnp.where(rr < cc, 1.0, 0.0).astype(BF16)
    before = jnp.dot(ohs.astype(BF16), earlier, preferred_element_type=F32) + cnt_acc[...]
    rank1 = jnp.sum(oh1 * before, axis=0, keepdims=True)
    rank2 = jnp.sum(oh2 * before, axis=0, keepdims=True)
    cnt_acc[...] = cnt_acc[...] + jnp.sum(ohs, axis=1, keepdims=True)
    cnt_ref[...] = cnt_acc[...]

    vals = [(i1 - EXPERT_ROW0).astype(F32), (i2 - EXPERT_ROW0).astype(F32), rank1, rank2, wt1, wt2]
    out_row = lax.broadcasted_iota(jnp.int32, (ROUTE_ROWS, tm), 0)
    slab = jnp.zeros((ROUTE_ROWS, tm), F32)
    for j, v in enumerate(vals):
        slab = jnp.where(out_row == j, v, slab)
    route_ref[...] = slab


def _mix_route(attn, lru, x, mod3, g1, wg, wa, wl, wo, g2, wrt, brt, b0, nb, *, tm=512):
    _, s, d = x.shape
    spt = s // tm
    tok_in = lambda w: pl.BlockSpec((1, tm, w), lambda bi, i: (b0 + bi, i, 0))
    return pl.pallas_call(
        _mix_kernel,
        grid=(nb, spt),
        in_specs=[tok_in(attn.shape[-1]), tok_in(d), tok_in(d),
                  pl.BlockSpec((1, 1, mod3.shape[-1]), lambda bi, i: (b0 + bi, 0, 0)),
                  pl.BlockSpec((1, d), lambda bi, i: (0, 0)),
                  _resident(wg.shape), _resident(wa.shape), _resident(wl.shape), _resident(wo.shape),
                  pl.BlockSpec((1, d), lambda bi, i: (0, 0)),
                  _resident(wrt.shape),
                  pl.BlockSpec((ROUTER_ROWS, 1), lambda bi, i: (0, 0))],
        out_specs=[pl.BlockSpec((1, tm, d), lambda bi, i: (bi, i, 0)),
                   pl.BlockSpec((tm * ROW_SUBLANES, LANE), lambda bi, i: (bi * spt + i, 0)),
                   pl.BlockSpec((ROUTE_ROWS, tm), lambda bi, i: (0, bi * spt + i)),
                   pl.BlockSpec((ROUTER_ROWS, 1), lambda bi, i: (0, 0))],
        out_shape=[jax.ShapeDtypeStruct((nb, s, d), BF16),
                   jax.ShapeDtypeStruct((nb * s * ROW_SUBLANES, LANE), jnp.uint32),
                   jax.ShapeDtypeStruct((ROUTE_ROWS, nb * s), F32),
                   jax.ShapeDtypeStruct((ROUTER_ROWS, 1), F32)],
        scratch_shapes=[pltpu.VMEM((ROUTER_ROWS, 1), F32)],
        compiler_params=_cparams(("arbitrary", "arbitrary"), VMEM_LIMIT),
        name="mix_route",
    )(attn, lru, x, mod3, g1, wg, wa, wl, wo, g2, wrt, brt)


TOP_K = 2
EXPERT_BLOCK = 512
MOE_BATCH_RANGES = 2


def _expert_kernel(be_ref, bv_ref, first_ref, slot_ref, next_ref, nu_ref, x_ref, w1_hbm, w3_hbm, w2_hbm,
                   y_ref, wf1, wf3, wf2, wb1, wb3, wb2, sem):
    j = pl.program_id(0)
    half = EXPERT_BLOCK // 2

    def fetch(e, s):
        return [pltpu.make_async_copy(w1_hbm.at[e], wf1.at[s], sem.at[s, 0]),
                pltpu.make_async_copy(w3_hbm.at[e], wf3.at[s], sem.at[s, 1]),
                pltpu.make_async_copy(w2_hbm.at[e], wf2.at[s], sem.at[s, 2])]

    def ffn(n):
        xb = _load_tile_rows(x_ref, n).astype(BF16)
        a = jnp.dot(xb, wb1[...], preferred_element_type=F32)
        g = jnp.dot(xb, wb3[...], preferred_element_type=F32)
        hm = (a * jax.nn.sigmoid(a) * g).astype(BF16)
        _store_tile_rows(y_ref, jnp.dot(hm, wb2[...], preferred_element_type=F32))

    @pl.when(j < nu_ref[0])
    def _():
        @pl.when(first_ref[j] == 1)
        def _():
            e, s = be_ref[j], slot_ref[j]

            @pl.when(j == 0)
            def _():
                for c in fetch(e, s):
                    c.start()

            @pl.when(next_ref[j] >= 0)
            def _():
                for c in fetch(next_ref[j], 1 - s):
                    c.start()

            for c in fetch(e, s):
                c.wait()
            wb1[...] = wf1[s].astype(BF16)
            wb3[...] = wf3[s].astype(BF16)
            wb2[...] = wf2[s].astype(BF16)

        @pl.when(bv_ref[j] > half)
        def _():
            ffn(EXPERT_BLOCK)

        @pl.when(bv_ref[j] <= half)
        def _():
            ffn(half)
            y_ref[half * ROW_SUBLANES:, :] = jnp.zeros((half * ROW_SUBLANES, LANE), y_ref.dtype)

    @pl.when(j >= nu_ref[0])
    def _():
        y_ref[...] = jnp.zeros_like(y_ref)


def _experts(xp, plan, w1, w3, w2):
    ne, d, de = w1.shape
    nb = xp.shape[0] // (EXPERT_BLOCK * ROW_SUBLANES)
    rows = (EXPERT_BLOCK * ROW_SUBLANES, LANE)
    grid_spec = pltpu.PrefetchScalarGridSpec(
        num_scalar_prefetch=len(plan),
        grid=(nb,),
        in_specs=[pl.BlockSpec(rows, lambda j, *p: (jnp.minimum(j, p[-1][0] - 1), 0)),
                  pl.BlockSpec(memory_space=pl.ANY),
                  pl.BlockSpec(memory_space=pl.ANY),
                  pl.BlockSpec(memory_space=pl.ANY)],
        out_specs=pl.BlockSpec(rows, lambda j, *p: (j, 0)),
        scratch_shapes=[pltpu.VMEM((2, d, de), F32), pltpu.VMEM((2, d, de), F32), pltpu.VMEM((2, de, d), F32),
                        pltpu.VMEM((d, de), BF16), pltpu.VMEM((d, de), BF16), pltpu.VMEM((de, d), BF16),
                        pltpu.SemaphoreType.DMA((2, 3))])
    return pl.pallas_call(
        _expert_kernel,
        grid_spec=grid_spec,
        out_shape=jax.ShapeDtypeStruct(xp.shape, xp.dtype),
        compiler_params=_cparams(("arbitrary",), VMEM_LIMIT),
        name="experts",
    )(*plan, xp, w1, w3, w2)


def _combine_kernel(y0_ref, y1_ref, route_ref, x1_ref, mod_ref, gf_ref, *rest):
    o_ref = rest[-1]
    tm, d = x1_ref.shape[1], x1_ref.shape[2]
    route = jnp.concatenate([route_ref[...], jnp.zeros((LANE - ROUTE_ROWS, tm), F32)], axis=0).T
    moe = _load_tile_rows(y0_ref) * route[:, 4:5] + _load_tile_rows(y1_ref) * route[:, 5:6]
    gate2 = mod_ref[0][:, 5 * d:6 * d]
    xo = x1_ref[0].astype(F32) + (1.0 + gate2) * moe
    ms = jnp.mean(xo * xo, axis=-1, keepdims=True)
    o_ref[0] = xo * lax.rsqrt(ms + EPS) * gf_ref[...]


def _combine(yg, route, x1, mod3, gf, b0, out_prev, *, tm=1024):
    nb, s, d = x1.shape
    b_all = mod3.shape[0]
    spt = s // tm
    nt = nb * spt
    rows = (tm * ROW_SUBLANES, LANE)
    in_specs = [pl.BlockSpec(rows, lambda bi, i: (bi * spt + i, 0)),
                pl.BlockSpec(rows, lambda bi, i: (nt + bi * spt + i, 0)),
                pl.BlockSpec((ROUTE_ROWS, tm), lambda bi, i: (0, bi * spt + i)),
                pl.BlockSpec((1, tm, d), lambda bi, i: (bi, i, 0)),
                pl.BlockSpec((1, 1, mod3.shape[-1]), lambda bi, i: (b0 + bi, 0, 0)),
                pl.BlockSpec((1, d), lambda bi, i: (0, 0))]
    args = [yg, yg, route, x1, mod3, gf]
    aliases = {}
    if out_prev is not None:
        in_specs.append(pl.BlockSpec(memory_space=pl.ANY))
        aliases = {len(args): 0}
        args.append(out_prev)
    return pl.pallas_call(
        _combine_kernel,
        grid=(nb, spt),
        in_specs=in_specs,
        out_specs=pl.BlockSpec((1, tm, d), lambda bi, i: (b0 + bi, i, 0)),
        out_shape=jax.ShapeDtypeStruct((b_all, s, d), F32),
        input_output_aliases=aliases,
        compiler_params=_cparams(("parallel", "parallel"), VMEM_LIMIT),
        name="combine",
    )(*args)


def _slot_plan(route, counts, n_tok):
    sizes = counts[EXPERT_ROW0:EXPERT_ROW0 + N_EXPERTS, 0].astype(jnp.int32)
    padded = (sizes + EXPERT_BLOCK - 1) // EXPERT_BLOCK * EXPERT_BLOCK
    pad_ends = jnp.cumsum(padded)
    pad_starts = pad_ends - padded
    eid = route[0:TOP_K].astype(jnp.int32)
    rank = route[TOP_K:2 * TOP_K].astype(jnp.int32)
    start = jnp.sum(jnp.where(eid[..., None] == jnp.arange(N_EXPERTS), pad_starts, 0), axis=-1)
    dest = (start + rank).reshape(TOP_K * n_tok)
    n_blocks = (n_tok * TOP_K + N_EXPERTS * (EXPERT_BLOCK - 1) + EXPERT_BLOCK - 1) // EXPERT_BLOCK
    gran = SC_CORES * SC_SUBCORES * SC_CHUNK // math.gcd(SC_CORES * SC_SUBCORES * SC_CHUNK, EXPERT_BLOCK)
    n_blocks = (n_blocks + gran - 1) // gran * gran
    n_used = pad_ends[-1] // EXPERT_BLOCK
    blk = jnp.minimum(jnp.arange(n_blocks), n_used - 1)
    blk_e = jnp.minimum(jnp.sum(pad_ends[None, :] <= (blk * EXPERT_BLOCK)[:, None], axis=1), N_EXPERTS - 1)
    blk_valid = jnp.clip(sizes[blk_e] - (blk * EXPERT_BLOCK - pad_starts[blk_e]), 0, EXPERT_BLOCK)
    idx = jnp.arange(n_blocks)
    first = (idx < n_used) & ((idx == 0) | (blk_e != jnp.roll(blk_e, 1)))
    slot = (jnp.cumsum(first) - 1) % 2
    later_first = lax.cummin(jnp.where(first, idx, n_blocks), reverse=True)
    next_first = jnp.concatenate([later_first[1:], jnp.full((1,), n_blocks)])
    next_e = jnp.where(next_first < n_blocks, blk_e[jnp.minimum(next_first, n_blocks - 1)], -1)
    i32 = lambda a: a.astype(jnp.int32)
    plan = (i32(blk_e), i32(blk_valid), i32(first), i32(slot), i32(next_e), i32(n_used.reshape(1)))
    return dest, n_blocks * EXPERT_BLOCK, plan


def kernel(x, c, w_mod, b_mod, norm1_g, w_in, conv_w, conv_b, lru_wx, lru_bx, lru_wa, lru_ba, lru_lambda, w_attn_o, w_lru_o, w_out, norm2_g, w_grp, b_grp, w_exp, b_exp, w1, w3, w2, norm_f_g):
    b, s, d = x.shape
    assert d == D_MODEL and s == SPAN * DILATIONS[-1] and w_mod.shape[0] == 1
    mod3 = _modulation(c, w_mod[0], b_mod[0]).reshape(b, 1, 6 * d)
    w_proj, w_gate = _prep_w_in(w_in[0])
    g1 = norm1_g[0].reshape(1, d)
    qkv0, qkv1, qkv2, xc, gy = _projection(x, mod3, g1, w_proj, conv_w[0], conv_b[0])
    attn = _attention((qkv0, qkv1, qkv2), b, s)
    lru = _lru_branch(xc, gy, lru_wx[0], lru_bx[0], lru_wa[0], lru_ba[0], lru_lambda[0])

    n_pad = ROUTER_ROWS - N_GROUPS - N_EXPERTS
    wr = jnp.pad(jnp.concatenate([w_grp[0], w_exp[0]], axis=1).T, ((0, n_pad), (0, 0))).astype(BF16)
    br = jnp.pad(jnp.concatenate([b_grp[0], b_exp[0]]), (0, n_pad)).reshape(ROUTER_ROWS, 1)
    wa, wl, wo = w_attn_o[0].astype(BF16), w_lru_o[0].astype(BF16), w_out[0].astype(BF16)
    as_rows = lambda a: a.reshape(-1, ROW_SUBLANES, LANE)
    as_tiles = lambda a: a.reshape(-1, LANE)

    out = None
    nb = b // MOE_BATCH_RANGES
    for b0 in range(0, b, nb):
        x1, h2, route, counts = _mix_route(attn, lru, x, mod3, g1, w_gate, wa, wl, wo,
                                           norm2_g[0].reshape(1, d), wr, br, b0, nb)
        dest, n_slots, plan = _slot_plan(route, counts, nb * s)
        xp = as_tiles(_sc_scatter_rows(as_rows(h2), dest, n_slots))
        yp = _experts(xp, plan, w1[0], w3[0], w2[0])
        yg = as_tiles(_sc_gather_rows(as_rows(yp), dest))
        out = _combine(yg, route, x1, mod3, norm_f_g.reshape(1, d), b0, out)
    return out
```

```python
import functools
import math

import jax
import jax.numpy as jnp
from jax import lax
from jax.experimental import pallas as pl
from jax.experimental.pallas import tpu as pltpu
from jax.experimental.pallas import tpu_sc as plsc

F32 = jnp.float32
BF16 = jnp.bfloat16

D_MODEL = 1024
HEAD_DIM = 64
N_SLOTS = 8
SPAN = 128
DILATIONS = (1, 4, 16)
GROUP_COLS = 3 * N_SLOTS * HEAD_DIM
ATTN_WIDTH = len(DILATIONS) * N_SLOTS * HEAD_DIM
ATTN_OUT = N_SLOTS * HEAD_DIM
LRU_WIDTH = D_MODEL
CONV_WIDTH = 4
CONV_TAIL = 8
LRU_C = 8.0
N_GROUPS = 4
EXPERTS_PER_GROUP = 8
N_EXPERTS = N_GROUPS * EXPERTS_PER_GROUP
EPS = 1e-6
LOG2_E = 1.4426950408889634
LANE = 128
VMEM_LIMIT = 56 * 1024 * 1024


def _cparams(sem, vmem=None):
    return pltpu.CompilerParams(dimension_semantics=sem, vmem_limit_bytes=vmem)


def _resident(shape):
    nd = len(shape)
    return pl.BlockSpec(shape, lambda *_: (0,) * nd, pipeline_mode=pl.Buffered(1))


def _mod_kernel(c_ref, w_ref, b_ref, o_ref):
    c = c_ref[...]
    ca = c * jax.nn.sigmoid(c)
    o_ref[...] = jnp.dot(ca.astype(BF16), w_ref[...].astype(BF16),
                         preferred_element_type=F32) + b_ref[...]


def _modulation(c, w_mod, b_mod):
    b, d = c.shape
    n = w_mod.shape[1]
    tn = n // 4
    return pl.pallas_call(
        _mod_kernel,
        grid=(n // tn,),
        in_specs=[pl.BlockSpec((b, d), lambda j: (0, 0)),
                  pl.BlockSpec((d, tn), lambda j: (0, j)),
                  pl.BlockSpec((1, tn), lambda j: (0, j))],
        out_specs=pl.BlockSpec((b, tn), lambda j: (0, j)),
        out_shape=jax.ShapeDtypeStruct((b, n), F32),
        compiler_params=_cparams(("arbitrary",)),
        name="modulation",
    )(c, w_mod, b_mod.reshape(1, n))


def _rms_mod(x, g, scale, shift):
    ms = jnp.mean(x * x, axis=-1, keepdims=True)
    return x * lax.rsqrt(ms + EPS) * g * (1.0 + scale) + shift


def _gelu_tanh(y):
    return y * (0.5 * (1.0 + jnp.tanh(0.7978845608028654 * (y + 0.044715 * (y * y * y)))))


def _proj_kernel(x_ref, mod_ref, g_ref, w_ref, cw_ref, cb_ref, qkv0_ref, qkv1_ref, qkv2_ref,
                 xc_ref, gy_ref, hs_ref, xe_ref, *, tm):
    @pl.when(pl.program_id(1) == 0)
    def _():
        xe_ref[0:CONV_TAIL, :] = jnp.zeros((CONV_TAIL, LRU_WIDTH), F32)

    @pl.when(pl.program_id(1) > 0)
    def _():
        xe_ref[0:CONV_TAIL, :] = xe_ref[tm:tm + CONV_TAIL, :]

    d_model = x_ref.shape[-1]
    m = mod_ref[0]
    h = _rms_mod(x_ref[0], g_ref[...], m[:, d_model:2 * d_model], m[:, 0:d_model])

    def mm(hv, lo, hi):
        return jnp.dot(hv, w_ref[:, lo:hi], preferred_element_type=F32)

    hb = h.astype(BF16)
    c0 = len(DILATIONS) * GROUP_COLS
    qkv0_ref[0] = mm(hb, 0, GROUP_COLS).astype(BF16)
    xr = mm(hb, c0, c0 + LRU_WIDTH)
    xe_ref[CONV_TAIL:, :] = xr
    cw = cw_ref[...]
    xc = xr * cw[CONV_WIDTH - 1:CONV_WIDTH] + cb_ref[...]
    for k in range(1, CONV_WIDTH):
        xc = xc + xe_ref[CONV_TAIL - k:CONV_TAIL - k + tm, :] * cw[CONV_WIDTH - 1 - k:CONV_WIDTH - k]
    xc_ref[0] = xc.astype(BF16)
    gy_ref[0] = _gelu_tanh(mm(hb, c0 + LRU_WIDTH, c0 + 2 * LRU_WIDTH)).astype(BF16)

    n_slab = d_model // LANE
    for j in range(n_slab):
        hs_ref[j] = h[:, j * LANE:(j + 1) * LANE]
    for g, out_ref in ((1, qkv1_ref), (2, qkv2_ref)):
        d = DILATIONS[g]
        rows = tm // d
        hp = jnp.concatenate(
            [jnp.concatenate([hs_ref[j, pl.ds(p, rows, stride=d), :] for j in range(n_slab)], axis=1)
             for p in range(d)], axis=0).astype(BF16)
        res = mm(hp, g * GROUP_COLS, (g + 1) * GROUP_COLS).astype(BF16)
        for p in range(d):
            out_ref[p] = res[p * rows:(p + 1) * rows]


def _projection(x, mod3, g1, w_r, conv_w, conv_b, *, tm=512):
    b, s, d = x.shape
    n = w_r.shape[1]
    assert s % tm == 0 and tm % (16 * DILATIONS[-1]) == 0 and CONV_TAIL >= CONV_WIDTH - 1
    out_shape = [jax.ShapeDtypeStruct((b * dd, s // dd, GROUP_COLS), BF16) for dd in DILATIONS]
    out_shape += [jax.ShapeDtypeStruct((b, s, LRU_WIDTH), BF16),
                  jax.ShapeDtypeStruct((b, s, LRU_WIDTH), BF16)]
    out_specs = [pl.BlockSpec((dd, tm // dd, GROUP_COLS), lambda bi, i: (bi, i, 0)) for dd in DILATIONS]
    out_specs += [pl.BlockSpec((1, tm, LRU_WIDTH), lambda bi, i: (bi, i, 0)),
                  pl.BlockSpec((1, tm, LRU_WIDTH), lambda bi, i: (bi, i, 0))]
    return pl.pallas_call(
        functools.partial(_proj_kernel, tm=tm),
        grid=(b, s // tm),
        in_specs=[pl.BlockSpec((1, tm, d), lambda bi, i: (bi, i, 0)),
                  pl.BlockSpec((1, 1, mod3.shape[-1]), lambda bi, i: (bi, 0, 0)),
                  pl.BlockSpec((1, d), lambda bi, i: (0, 0)),
                  _resident((d, n)),
                  pl.BlockSpec((CONV_WIDTH, LRU_WIDTH), lambda bi, i: (0, 0)),
                  pl.BlockSpec((1, LRU_WIDTH), lambda bi, i: (0, 0))],
        out_specs=out_specs,
        out_shape=out_shape,
        scratch_shapes=[pltpu.VMEM((d // LANE, tm, LANE), F32), pltpu.VMEM((CONV_TAIL + tm, LRU_WIDTH), F32)],
        compiler_params=_cparams(("parallel", "arbitrary"), VMEM_LIMIT),
        name="projection",
    )(x, mod3, g1, w_r, conv_w, conv_b.reshape(1, LRU_WIDTH))


def _attn_kernel(q0, k0, v0, q1, k1, v1, q2, k2, v2, o_ref, acc_ref, lse_ref, bias_ref, *, seq):
    hcols = o_ref.shape[-1]
    n_head = hcols // HEAD_DIM
    head_of_lane = lax.broadcasted_iota(jnp.int32, (SPAN, hcols), 1) // HEAD_DIM
    head_mask_b = [jnp.where(head_of_lane == h, 1.0, 0.0).astype(BF16) for h in range(n_head)]

    def by_head(parts):
        out = parts[n_head - 1]
        for h in range(n_head - 2, -1, -1):
            out = jnp.where(head_of_lane == h, parts[h], out)
        return out

    @pl.when((pl.program_id(0) == 0) & (pl.program_id(1) == 0))
    def _():
        qi = lax.broadcasted_iota(jnp.int32, (n_head * SPAN, 2 * SPAN), 0) % SPAN
        ki = lax.broadcasted_iota(jnp.int32, (n_head * SPAN, 2 * SPAN), 1)
        band = (ki >= qi) & (ki <= qi + SPAN)
        bias_ref[0] = jnp.where(band, 0.0, -jnp.inf)
        bias_ref[1] = jnp.where(band & (ki >= SPAN), 0.0, -jnp.inf)

    for g, (q_ref, k_ref, v_ref) in enumerate(((q0, k0, v0), (q1, k1, v1), (q2, k2, v2))):
        d = DILATIONS[g]
        n_blk = seq // d // SPAN

        def tile(n, carry, q_ref=q_ref, k_ref=k_ref, v_ref=v_ref, d=d, n_blk=n_blk, g=g):
            p = n // n_blk
            blk = n % n_blk
            r0 = pl.multiple_of(blk * SPAN, SPAN)
            rp = pl.multiple_of(jnp.maximum(blk - 1, 0) * SPAN, SPAN)
            q = q_ref[p, pl.ds(r0, SPAN), :]
            kk = jnp.concatenate([k_ref[p, pl.ds(rp, SPAN), :], k_ref[p, pl.ds(r0, SPAN), :]], axis=0)
            vv = jnp.concatenate([v_ref[p, pl.ds(rp, SPAN), :], v_ref[p, pl.ds(r0, SPAN), :]], axis=0)
            qs = jnp.concatenate([q * head_mask_b[h] for h in range(n_head)], axis=0)
            sc = lax.dot_general(qs, kk, (((1,), (1,)), ((), ())), preferred_element_type=F32)
            sc = sc + bias_ref[jnp.where(blk > 0, 0, 1)]
            mx = jnp.max(sc, axis=-1, keepdims=True)
            e = jnp.exp2(sc - mx)
            den = jnp.sum(e, axis=-1, keepdims=True)
            pv = jnp.dot(e.astype(BF16), vv, preferred_element_type=F32)
            rows_of = lambda a: [a[h * SPAN:(h + 1) * SPAN] for h in range(n_head)]
            den_l = by_head(rows_of(den))
            o = by_head(rows_of(pv)) / den_l
            l = by_head(rows_of(mx)) + jnp.log(den_l) * LOG2_E
            start = p + d * r0
            for j in range(hcols // LANE):
                rows = pl.ds(start, SPAN, stride=d) if d > 1 else pl.ds(start, SPAN)
                acc_ref[g, j, rows, :] = o[:, j * LANE:(j + 1) * LANE]
                lse_ref[g, j, rows, :] = l[:, j * LANE:(j + 1) * LANE]
            return carry

        lax.fori_loop(0, seq // SPAN, tile, 0, unroll=16)

    chunk = 256

    def combine(c, carry):
        r = pl.multiple_of(c * chunk, chunk)
        for j in range(hcols // LANE):
            ls = [lse_ref[g, j, pl.ds(r, chunk), :] for g in range(len(DILATIONS))]
            mx = jnp.maximum(jnp.maximum(ls[0], ls[1]), ls[2])
            ws = [jnp.exp2(v - mx) for v in ls]
            num = ws[0] * acc_ref[0, j, pl.ds(r, chunk), :]
            for g in range(1, len(DILATIONS)):
                num = num + ws[g] * acc_ref[g, j, pl.ds(r, chunk), :]
            o_ref[0, pl.ds(r, chunk), j * LANE:(j + 1) * LANE] = (num / (ws[0] + ws[1] + ws[2])).astype(BF16)
        return carry

    lax.fori_loop(0, seq // chunk, combine, 0)


def _attention(qkvs, b, s):
    hcols = 2 * HEAD_DIM
    n_hg = ATTN_OUT // hcols
    ncb = ATTN_OUT // hcols
    in_specs, args = [], []
    for g, d in enumerate(DILATIONS):
        for part in range(3):
            in_specs.append(pl.BlockSpec((d, s // d, hcols),
                                         lambda bi, hg, part=part: (bi, 0, part * ncb + hg)))
            args.append(qkvs[g])
    return pl.pallas_call(
        functools.partial(_attn_kernel, seq=s),
        grid=(b, n_hg),
        in_specs=in_specs,
        out_specs=pl.BlockSpec((1, s, hcols), lambda bi, hg: (bi, 0, hg)),
        out_shape=jax.ShapeDtypeStruct((b, s, ATTN_OUT), BF16),
        scratch_shapes=[pltpu.VMEM((len(DILATIONS), hcols // LANE, s, LANE), F32),
                        pltpu.VMEM((len(DILATIONS), hcols // LANE, s, LANE), F32),
                        pltpu.VMEM((2, (hcols // HEAD_DIM) * SPAN, 2 * SPAN), F32)],
        compiler_params=_cparams(("arbitrary", "arbitrary"), VMEM_LIMIT),
        name="dilated_attention",
    )(*args)


def _lru_kernel(xc_ref, gy_ref, wg_ref, bx_ref, ba_ref, lam_ref, o_ref, a_ref, b_ref, h_ref, *, pitch):
    nb, ts, tc = xc_ref.shape
    nl = tc // LANE

    @pl.when(pl.program_id(1) == 0)
    def _():
        h_ref[...] = jnp.zeros_like(h_ref)

    xb = xc_ref[...].reshape(nb * ts, tc)
    xc = xb.astype(F32)
    gates = jnp.dot(xb, wg_ref[0], preferred_element_type=F32)
    gate_i = jax.nn.sigmoid(gates[:, :tc] + bx_ref[...])
    gate_r = jax.nn.sigmoid(gates[:, tc:] + ba_ref[...])
    neg_lam = -lam_ref[...]
    softplus = jnp.maximum(neg_lam, 0.0) + jnp.log1p(jnp.exp(-jnp.abs(neg_lam)))
    log_a = (-LRU_C) * gate_r * softplus
    a = jnp.exp(log_a)
    one_m_a2 = jnp.tanh(-log_a) * (1.0 + a * a)
    mult = jnp.where(one_m_a2 > 0.0, one_m_a2 * lax.rsqrt(one_m_a2), 0.0)
    bv = mult * gate_i * xc
    for bi in range(nb):
        for j in range(nl):
            a_ref[j, pl.ds(bi * pitch, ts), :] = a[bi * ts:(bi + 1) * ts, j * LANE:(j + 1) * LANE]
            b_ref[j, pl.ds(bi * pitch, ts), :] = bv[bi * ts:(bi + 1) * ts, j * LANE:(j + 1) * LANE]

    def step(t, hs):
        out = []
        for j in range(nl):
            rows = pl.ds(t, nb, stride=pitch)
            h = a_ref[j, rows, :] * hs[j] + b_ref[j, rows, :]
            b_ref[j, rows, :] = h
            out.append(h)
        return tuple(out)

    hs = lax.fori_loop(0, ts, step, tuple(h_ref[j] for j in range(nl)), unroll=8)
    for j in range(nl):
        h_ref[j] = hs[j]
    for bi in range(nb):
        h = jnp.concatenate([b_ref[j, pl.ds(bi * pitch, ts), :] for j in range(nl)], axis=1)
        o_ref[bi] = (h * gy_ref[bi].astype(F32)).astype(BF16)


def _lru_gate_weights(wx, wa, tc):
    nb, bd, _ = wx.shape
    per = tc // bd
    eye = jnp.eye(per, dtype=wx.dtype)

    def bdiag(w):
        w = w.reshape(nb // per, per, bd, bd)
        return jnp.einsum('cpio,pq->cpiqo', w, eye).reshape(nb // per, tc, tc)

    return jnp.concatenate([bdiag(wx), bdiag(wa)], axis=-1).astype(BF16)


def _lru_branch(xc, gy, wx, bx, wa, ba, lam, *, tc=256, ts=256):
    b, s, c = xc.shape
    assert s % ts == 0 and c % tc == 0
    wg = _lru_gate_weights(wx, wa, tc)
    row = lambda v: v.reshape(1, c)
    tile = pl.BlockSpec((b, ts, tc), lambda ci, ti: (0, ti, ci))
    vec = pl.BlockSpec((1, tc), lambda ci, ti: (0, ci))
    pitch = ts + 8
    return pl.pallas_call(
        functools.partial(_lru_kernel, pitch=pitch),
        grid=(c // tc, s // ts),
        in_specs=[tile, tile,
                  pl.BlockSpec((1, tc, 2 * tc), lambda ci, ti: (ci, 0, 0)),
                  vec, vec, vec],
        out_specs=tile,
        out_shape=jax.ShapeDtypeStruct((b, s, c), BF16),
        scratch_shapes=[pltpu.VMEM((tc // LANE, b * pitch, LANE), F32),
                        pltpu.VMEM((tc // LANE, b * pitch, LANE), F32),
                        pltpu.VMEM((tc // LANE, b, LANE), F32)],
        compiler_params=_cparams(("parallel", "arbitrary"), VMEM_LIMIT),
        name="rg_lru",
    )(xc, gy, wg, row(bx), row(ba), row(lam))


def _prep_w_in(w_in):
    a = ATTN_WIDTH
    gw = N_SLOTS * HEAD_DIM
    q = w_in[:, :a] * (HEAD_DIM ** -0.5 * LOG2_E)
    k = w_in[:, a:2 * a]
    v = w_in[:, 2 * a:3 * a]
    parts = []
    for g in range(len(DILATIONS)):
        sl = slice(g * gw, (g + 1) * gw)
        parts += [q[:, sl], k[:, sl], v[:, sl]]
    parts.append(w_in[:, 3 * a:3 * a + 2 * LRU_WIDTH])
    return jnp.concatenate(parts, axis=1).astype(BF16), w_in[:, 3 * a + 2 * LRU_WIDTH:].astype(BF16)


ROW_SUBLANES = D_MODEL // 2 // LANE


def _store_tile_rows(ref, v, row0=0):
    n, half = v.shape[0], v.shape[1] // 2
    lo = pltpu.bitcast(v[:, :half].astype(BF16).astype(F32), jnp.uint32)
    hi = pltpu.bitcast(v[:, half:].astype(BF16).astype(F32), jnp.uint32)
    words = (hi & jnp.uint32(0xFFFF0000)) | (lo >> 16)
    for j in range(ROW_SUBLANES):
        ref[pl.ds(row0 * ROW_SUBLANES + j, n, stride=ROW_SUBLANES), :] = words[:, j * LANE:(j + 1) * LANE]


def _load_tile_rows(ref, n=None):
    n = ref.shape[0] // ROW_SUBLANES if n is None else n
    words = [ref[pl.ds(j, n, stride=ROW_SUBLANES), :] for j in range(ROW_SUBLANES)]
    lo = [pltpu.bitcast(w << 16, F32) for w in words]
    hi = [pltpu.bitcast(w & jnp.uint32(0xFFFF0000), F32) for w in words]
    return jnp.concatenate(lo + hi, axis=-1)


SC_CORES, SC_SUBCORES = 2, 16
SC_CHUNK = 128


def _sc_gather_rows(table, idx):
    n = idx.shape[0]
    per_worker = n // (SC_CORES * SC_SUBCORES)
    n_chunks = per_worker // SC_CHUNK
    assert n_chunks * SC_CHUNK * SC_CORES * SC_SUBCORES == n
    mesh = plsc.VectorSubcoreMesh(core_axis_name="c", subcore_axis_name="s")

    def body(table_hbm, idx_hbm, out_hbm, idx_v, rows_v, sem):
        base = (lax.axis_index("s") * SC_CORES + lax.axis_index("c")) * per_worker

        @pl.loop(0, n_chunks)
        def _(i):
            off = pl.multiple_of(base + i * SC_CHUNK, SC_CHUNK)
            pltpu.sync_copy(idx_hbm.at[pl.ds(off, SC_CHUNK)], idx_v)
            pltpu.async_copy(table_hbm.at[idx_v], rows_v, sem).wait()
            pltpu.sync_copy(rows_v, out_hbm.at[pl.ds(off, SC_CHUNK)])

    return pl.kernel(
        body, mesh=mesh,
        out_type=jax.ShapeDtypeStruct((n,) + table.shape[1:], table.dtype),
        scratch_types=[pltpu.VMEM((SC_CHUNK,), jnp.int32),
                       pltpu.VMEM((SC_CHUNK,) + table.shape[1:], table.dtype),
                       pltpu.SemaphoreType.DMA],
        name="sc_gather_rows",
    )(table, idx)


def _sc_scatter_rows(rows, idx, n_out):
    n_rows = rows.shape[0]
    n_choice = idx.shape[0] // n_rows
    per_worker = n_rows // (SC_CORES * SC_SUBCORES)
    n_chunks = per_worker // SC_CHUNK
    assert n_chunks * SC_CHUNK * SC_CORES * SC_SUBCORES == n_rows and n_choice * n_rows == idx.shape[0]
    mesh = plsc.VectorSubcoreMesh(core_axis_name="c", subcore_axis_name="s")

    def body(rows_hbm, idx_hbm, out_hbm, idx_v, rows_v):
        base = (lax.axis_index("s") * SC_CORES + lax.axis_index("c")) * per_worker

        @pl.loop(0, n_chunks)
        def _(i):
            off = pl.multiple_of(base + i * SC_CHUNK, SC_CHUNK)
            pltpu.sync_copy(rows_hbm.at[pl.ds(off, SC_CHUNK)], rows_v)
            for k in range(n_choice):
                pltpu.sync_copy(idx_hbm.at[pl.ds(k * n_rows + off, SC_CHUNK)], idx_v)
                pltpu.sync_copy(rows_v, out_hbm.at[idx_v])

    return pl.kernel(
        body, mesh=mesh,
        out_type=jax.ShapeDtypeStruct((n_out,) + rows.shape[1:], rows.dtype),
        scratch_types=[pltpu.VMEM((SC_CHUNK,), jnp.int32),
                       pltpu.VMEM((SC_CHUNK,) + rows.shape[1:], rows.dtype)],
        name="sc_scatter_rows",
    )(rows, idx)


ROUTE_ROWS = 8
EXPERT_ROW0 = N_GROUPS
ROUTER_ROWS = 48


def _mix_kernel(attn_ref, lru_ref, x_ref, mod_ref, g1_ref, wg_ref, wa_ref, wl_ref, wo_ref, g2_ref,
                wrt_ref, brt_ref, x1_ref, h2_ref, route_ref, cnt_ref, cnt_acc):
    d = x_ref.shape[-1]
    tm = x_ref.shape[1]

    @pl.when((pl.program_id(0) == 0) & (pl.program_id(1) == 0))
    def _():
        cnt_acc[...] = jnp.zeros_like(cnt_acc)

    m = mod_ref[0]
    gate1, shift2, scale2 = m[:, 2 * d:3 * d], m[:, 3 * d:4 * d], m[:, 4 * d:5 * d]
    x = x_ref[0]
    h1 = _rms_mod(x, g1_ref[...], m[:, d:2 * d], m[:, 0:d]).astype(BF16)
    gates = jax.nn.sigmoid(jnp.dot(h1, wg_ref[...], preferred_element_type=F32))
    ya = jnp.dot(attn_ref[0], wa_ref[...], preferred_element_type=F32)
    yl = jnp.dot(lru_ref[0], wl_ref[...], preferred_element_type=F32)
    mixed = gates[:, :d] * ya + gates[:, d:] * yl
    y = jnp.dot(mixed.astype(BF16), wo_ref[...], preferred_element_type=F32)
    x1 = x + (1.0 + gate1) * y
    x1_ref[0] = x1.astype(BF16)
    h2 = _rms_mod(x1, g2_ref[...], scale2, shift2)
    _store_tile_rows(h2_ref, h2)
    logits = lax.dot_general(wrt_ref[...], h2.astype(BF16), (((1,), (1,)), ((), ())),
                             preferred_element_type=F32) + brt_ref[...]

    row = lax.broadcasted_iota(jnp.int32, logits.shape, 0)
    neg = -jnp.inf

    def top(vals):
        mx = jnp.max(vals, axis=0, keepdims=True)
        idx = jnp.min(jnp.where(vals == mx, row, ROUTER_ROWS), axis=0, keepdims=True)
        return mx, idx

    is_grp = row < N_GROUPS
    gmax, gidx = top(jnp.where(is_grp, logits, neg))
    grp_gate = 1.0 / jnp.sum(jnp.where(is_grp, jnp.exp(logits - gmax), 0.0), axis=0, keepdims=True)
    lo = EXPERT_ROW0 + EXPERTS_PER_GROUP * gidx
    el = jnp.where((row >= lo) & (row < lo + EXPERTS_PER_GROUP), logits, neg)
    v1, i1 = top(el)
    v2, i2 = top(jnp.where(row == i1, neg, el))
    e21 = jnp.exp(v2 - v1)
    wt1 = grp_gate / (1.0 + e21)
    wt2 = wt1 * e21

    oh1 = jnp.where(row == i1, 1.0, 0.0)
    oh2 = jnp.where(row == i2, 1.0, 0.0)
    ohs = oh1 + oh2
    rr = lax.broadcasted_iota(jnp.int32, (tm, tm), 0)
    cc = lax.broadcasted_iota(jnp.int32, (tm, tm), 1)
    earlier = jnp.where(rr < cc, 1.0, 0.0).astype(BF16)
    before = jnp.dot(ohs.astype(BF16), earlier, preferred_element_type=F32) + cnt_acc[...]
    rank1 = jnp.sum(oh1 * before, axis=0, keepdims=True)
    rank2 = jnp.sum(oh2 * before, axis=0, keepdims=True)
    cnt_acc[...] = cnt_acc[...] + jnp.sum(ohs, axis=1, keepdims=True)
    cnt_ref[...] = cnt_acc[...]

    vals = [(i1 - EXPERT_ROW0).astype(F32), (i2 - EXPERT_ROW0).astype(F32), rank1, rank2, wt1, wt2]
    out_row = lax.broadcasted_iota(jnp.int32, (ROUTE_ROWS, tm), 0)
    slab = jnp.zeros((ROUTE_ROWS, tm), F32)
    for j, v in enumerate(vals):
        slab = jnp.where(out_row == j, v, slab)
    route_ref[...] = slab


def _mix_route(attn, lru, x, mod3, g1, wg, wa, wl, wo, g2, wrt, brt, b0, nb, *, tm=512):
    _, s, d = x.shape
    spt = s // tm
    tok_in = lambda w: pl.BlockSpec((1, tm, w), lambda bi, i: (b0 + bi, i, 0))
    return pl.pallas_call(
        _mix_kernel,
        grid=(nb, spt),
        in_specs=[tok_in(attn.shape[-1]), tok_in(d), tok_in(d),
                  pl.BlockSpec((1, 1, mod3.shape[-1]), lambda bi, i: (b0 + bi, 0, 0)),
                  pl.BlockSpec((1, d), lambda bi, i: (0, 0)),
                  _resident(wg.shape), _resident(wa.shape), _resident(wl.shape), _resident(wo.shape),
                  pl.BlockSpec((1, d), lambda bi, i: (0, 0)),
                  _resident(wrt.shape),
                  pl.BlockSpec((ROUTER_ROWS, 1), lambda bi, i: (0, 0))],
        out_specs=[pl.BlockSpec((1, tm, d), lambda bi, i: (bi, i, 0)),
                   pl.BlockSpec((tm * ROW_SUBLANES, LANE), lambda bi, i: (bi * spt + i, 0)),
                   pl.BlockSpec((ROUTE_ROWS, tm), lambda bi, i: (0, bi * spt + i)),
                   pl.BlockSpec((ROUTER_ROWS, 1), lambda bi, i: (0, 0))],
        out_shape=[jax.ShapeDtypeStruct((nb, s, d), BF16),
                   jax.ShapeDtypeStruct((nb * s * ROW_SUBLANES, LANE), jnp.uint32),
                   jax.ShapeDtypeStruct((ROUTE_ROWS, nb * s), F32),
                   jax.ShapeDtypeStruct((ROUTER_ROWS, 1), F32)],
        scratch_shapes=[pltpu.VMEM((ROUTER_ROWS, 1), F32)],
        compiler_params=_cparams(("arbitrary", "arbitrary"), VMEM_LIMIT),
        name="mix_route",
    )(attn, lru, x, mod3, g1, wg, wa, wl, wo, g2, wrt, brt)


TOP_K = 2
EXPERT_BLOCK = 512
MOE_BATCH_RANGES = 2


def _expert_kernel(be_ref, bv_ref, first_ref, slot_ref, next_ref, nu_ref, x_ref, w1_hbm, w3_hbm, w2_hbm,
                   y_ref, wf1, wf3, wf2, wb1, wb3, wb2, sem):
    j = pl.program_id(0)
    half = EXPERT_BLOCK // 2

    def fetch(e, s):
        return [pltpu.make_async_copy(w1_hbm.at[e], wf1.at[s], sem.at[s, 0]),
                pltpu.make_async_copy(w3_hbm.at[e], wf3.at[s], sem.at[s, 1]),
                pltpu.make_async_copy(w2_hbm.at[e], wf2.at[s], sem.at[s, 2])]

    def ffn(n):
        xb = _load_tile_rows(x_ref, n).astype(BF16)
        a = jnp.dot(xb, wb1[...], preferred_element_type=F32)
        g = jnp.dot(xb, wb3[...], preferred_element_type=F32)
        hm = (a * jax.nn.sigmoid(a) * g).astype(BF16)
        _store_tile_rows(y_ref, jnp.dot(hm, wb2[...], preferred_element_type=F32))

    @pl.when(j < nu_ref[0])
    def _():
        @pl.when(first_ref[j] == 1)
        def _():
            e, s = be_ref[j], slot_ref[j]

            @pl.when(j == 0)
            def _():
                for c in fetch(e, s):
                    c.start()

            @pl.when(next_ref[j] >= 0)
            def _():
                for c in fetch(next_ref[j], 1 - s):
                    c.start()

            for c in fetch(e, s):
                c.wait()
            wb1[...] = wf1[s].astype(BF16)
            wb3[...] = wf3[s].astype(BF16)
            wb2[...] = wf2[s].astype(BF16)

        @pl.when(bv_ref[j] > half)
        def _():
            ffn(EXPERT_BLOCK)

        @pl.when(bv_ref[j] <= half)
        def _():
            ffn(half)
            y_ref[half * ROW_SUBLANES:, :] = jnp.zeros((half * ROW_SUBLANES, LANE), y_ref.dtype)

    @pl.when(j >= nu_ref[0])
    def _():
        y_ref[...] = jnp.zeros_like(y_ref)


def _experts(xp, plan, w1, w3, w2):
    ne, d, de = w1.shape
    nb = xp.shape[0] // (EXPERT_BLOCK * ROW_SUBLANES)
    rows = (EXPERT_BLOCK * ROW_SUBLANES, LANE)
    grid_spec = pltpu.PrefetchScalarGridSpec(
        num_scalar_prefetch=len(plan),
        grid=(nb,),
        in_specs=[pl.BlockSpec(rows, lambda j, *p: (jnp.minimum(j, p[-1][0] - 1), 0)),
                  pl.BlockSpec(memory_space=pl.ANY),
                  pl.BlockSpec(memory_space=pl.ANY),
                  pl.BlockSpec(memory_space=pl.ANY)],
        out_specs=pl.BlockSpec(rows, lambda j, *p: (j, 0)),
        scratch_shapes=[pltpu.VMEM((2, d, de), F32), pltpu.VMEM((2, d, de), F32), pltpu.VMEM((2, de, d), F32),
                        pltpu.VMEM((d, de), BF16), pltpu.VMEM((d, de), BF16), pltpu.VMEM((de, d), BF16),
                        pltpu.SemaphoreType.DMA((2, 3))])
    return pl.pallas_call(
        _expert_kernel,
        grid_spec=grid_spec,
        out_shape=jax.ShapeDtypeStruct(xp.shape, xp.dtype),
        compiler_params=_cparams(("arbitrary",), VMEM_LIMIT),
        name="experts",
    )(*plan, xp, w1, w3, w2)


def _combine_kernel(y0_ref, y1_ref, route_ref, x1_ref, mod_ref, gf_ref, *rest):
    o_ref = rest[-1]
    tm, d = x1_ref.shape[1], x1_ref.shape[2]
    route = jnp.concatenate([route_ref[...], jnp.zeros((LANE - ROUTE_ROWS, tm), F32)], axis=0).T
    moe = _load_tile_rows(y0_ref) * route[:, 4:5] + _load_tile_rows(y1_ref) * route[:, 5:6]
    gate2 = mod_ref[0][:, 5 * d:6 * d]
    xo = x1_ref[0].astype(F32) + (1.0 + gate2) * moe
    ms = jnp.mean(xo * xo, axis=-1, keepdims=True)
    o_ref[0] = xo * lax.rsqrt(ms + EPS) * gf_ref[...]


def _combine(yg, route, x1, mod3, gf, b0, out_prev, *, tm=1024):
    nb, s, d = x1.shape
    b_all = mod3.shape[0]
    spt = s // tm
    nt = nb * spt
    rows = (tm * ROW_SUBLANES, LANE)
    in_specs = [pl.BlockSpec(rows, lambda bi, i: (bi * spt + i, 0)),
                pl.BlockSpec(rows, lambda bi, i: (nt + bi * spt + i, 0)),
                pl.BlockSpec((ROUTE_ROWS, tm), lambda bi, i: (0, bi * spt + i)),
                pl.BlockSpec((1, tm, d), lambda bi, i: (bi, i, 0)),
                pl.BlockSpec((1, 1, mod3.shape[-1]), lambda bi, i: (b0 + bi, 0, 0)),
                pl.BlockSpec((1, d), lambda bi, i: (0, 0))]
    args = [yg, yg, route, x1, mod3, gf]
    aliases = {}
    if out_prev is not None:
        in_specs.append(pl.BlockSpec(memory_space=pl.ANY))
        aliases = {len(args): 0}
        args.append(out_prev)
    return pl.pallas_call(
        _combine_kernel,
        grid=(nb, spt),
        in_specs=in_specs,
        out_specs=pl.BlockSpec((1, tm, d), lambda bi, i: (b0 + bi, i, 0)),
        out_shape=jax.ShapeDtypeStruct((b_all, s, d), F32),
        input_output_aliases=aliases,
        compiler_params=_cparams(("parallel", "parallel"), VMEM_LIMIT),
        name="combine",
    )(*args)


def _slot_plan(route, counts, n_tok):
    sizes = counts[EXPERT_ROW0:EXPERT_ROW0 + N_EXPERTS, 0].astype(jnp.int32)
    padded = (sizes + EXPERT_BLOCK - 1) // EXPERT_BLOCK * EXPERT_BLOCK
    pad_ends = jnp.cumsum(padded)
    pad_starts = pad_ends - padded
    eid = route[0:TOP_K].astype(jnp.int32)
    rank = route[TOP_K:2 * TOP_K].astype(jnp.int32)
    start = jnp.sum(jnp.where(eid[..., None] == jnp.arange(N_EXPERTS), pad_starts, 0), axis=-1)
    dest = (start + rank).reshape(TOP_K * n_tok)
    n_blocks = (n_tok * TOP_K + N_EXPERTS * (EXPERT_BLOCK - 1) + EXPERT_BLOCK - 1) // EXPERT_BLOCK
    gran = SC_CORES * SC_SUBCORES * SC_CHUNK // math.gcd(SC_CORES * SC_SUBCORES * SC_CHUNK, EXPERT_BLOCK)
    n_blocks = (n_blocks + gran - 1) // gran * gran
    n_used = pad_ends[-1] // EXPERT_BLOCK
    blk = jnp.minimum(jnp.arange(n_blocks), n_used - 1)
    blk_e = jnp.minimum(jnp.sum(pad_ends[None, :] <= (blk * EXPERT_BLOCK)[:, None], axis=1), N_EXPERTS - 1)
    blk_valid = jnp.clip(sizes[blk_e] - (blk * EXPERT_BLOCK - pad_starts[blk_e]), 0, EXPERT_BLOCK)
    idx = jnp.arange(n_blocks)
    first = (idx < n_used) & ((idx == 0) | (blk_e != jnp.roll(blk_e, 1)))
    slot = (jnp.cumsum(first) - 1) % 2
    later_first = lax.cummin(jnp.where(first, idx, n_blocks), reverse=True)
    next_first = jnp.concatenate([later_first[1:], jnp.full((1,), n_blocks)])
    next_e = jnp.where(next_first < n_blocks, blk_e[jnp.minimum(next_first, n_blocks - 1)], -1)
    i32 = lambda a: a.astype(jnp.int32)
    plan = (i32(blk_e), i32(blk_valid), i32(first), i32(slot), i32(next_e), i32(n_used.reshape(1)))
    return dest, n_blocks * EXPERT_BLOCK, plan


def kernel(x, c, w_mod, b_mod, norm1_g, w_in, conv_w, conv_b, lru_wx, lru_bx, lru_wa, lru_ba, lru_lambda, w_attn_o, w_lru_o, w_out, norm2_g, w_grp, b_grp, w_exp, b_exp, w1, w3, w2, norm_f_g):
    b, s, d = x.shape
    assert d == D_MODEL and s == SPAN * DILATIONS[-1] and w_mod.shape[0] == 1
    mod3 = _modulation(c, w_mod[0], b_mod[0]).reshape(b, 1, 6 * d)
    w_proj, w_gate = _prep_w_in(w_in[0])
    g1 = norm1_g[0].reshape(1, d)
    qkv0, qkv1, qkv2, xc, gy = _projection(x, mod3, g1, w_proj, conv_w[0], conv_b[0])
    attn = _attention((qkv0, qkv1, qkv2), b, s)
    lru = _lru_branch(xc, gy, lru_wx[0], lru_bx[0], lru_wa[0], lru_ba[0], lru_lambda[0])

    n_pad = ROUTER_ROWS - N_GROUPS - N_EXPERTS
    wr = jnp.pad(jnp.concatenate([w_grp[0], w_exp[0]], axis=1).T, ((0, n_pad), (0, 0))).astype(BF16)
    br = jnp.pad(jnp.concatenate([b_grp[0], b_exp[0]]), (0, n_pad)).reshape(ROUTER_ROWS, 1)
    wa, wl, wo = w_attn_o[0].astype(BF16), w_lru_o[0].astype(BF16), w_out[0].astype(BF16)
    as_rows = lambda a: a.reshape(-1, ROW_SUBLANES, LANE)
    as_tiles = lambda a: a.reshape(-1, LANE)

    out = None
    nb = b // MOE_BATCH_RANGES
    for b0 in range(0, b, nb):
        x1, h2, route, counts = _mix_route(attn, lru, x, mod3, g1, w_gate, wa, wl, wo,
                                           norm2_g[0].reshape(1, d), wr, br, b0, nb)
        dest, n_slots, plan = _slot_plan(route, counts, nb * s)
        xp = as_tiles(_sc_scatter_rows(as_rows(h2), dest, n_slots))
        yp = _experts(xp, plan, w1[0], w3[0], w2[0])
        yg = as_tiles(_sc_gather_rows(as_rows(yp), dest))
        out = _combine(yg, route, x1, mod3, norm_f_g.reshape(1, d), b0, out)
    return out
```

```python
import functools
import math

import jax
import jax.numpy as jnp
from jax import lax
from jax.experimental import pallas as pl
from jax.experimental.pallas import tpu as pltpu
from jax.experimental.pallas import tpu_sc as plsc

F32 = jnp.float32
BF16 = jnp.bfloat16

D_MODEL = 1024
HEAD_DIM = 64
N_SLOTS = 8
SPAN = 128
DILATIONS = (1, 4, 16)
GROUP_COLS = 3 * N_SLOTS * HEAD_DIM
ATTN_WIDTH = len(DILATIONS) * N_SLOTS * HEAD_DIM
ATTN_OUT = N_SLOTS * HEAD_DIM
LRU_WIDTH = D_MODEL
CONV_WIDTH = 4
CONV_TAIL = 8
LRU_C = 8.0
N_GROUPS = 4
EXPERTS_PER_GROUP = 8
N_EXPERTS = N_GROUPS * EXPERTS_PER_GROUP
EPS = 1e-6
LOG2_E = 1.4426950408889634
LANE = 128
VMEM_LIMIT = 56 * 1024 * 1024


def _cparams(sem, vmem=None):
    return pltpu.CompilerParams(dimension_semantics=sem, vmem_limit_bytes=vmem)


def _resident(shape):
    nd = len(shape)
    return pl.BlockSpec(shape, lambda *_: (0,) * nd, pipeline_mode=pl.Buffered(1))


def _mod_kernel(c_ref, w_ref, b_ref, o_ref):
    c = c_ref[...]
    ca = c * jax.nn.sigmoid(c)
    o_ref[...] = jnp.dot(ca.astype(BF16), w_ref[...].astype(BF16),
                         preferred_element_type=F32) + b_ref[...]


def _modulation(c, w_mod, b_mod):
    b, d = c.shape
    n = w_mod.shape[1]
    tn = n // 4
    return pl.pallas_call(
        _mod_kernel,
        grid=(n // tn,),
        in_specs=[pl.BlockSpec((b, d), lambda j: (0, 0)),
                  pl.BlockSpec((d, tn), lambda j: (0, j)),
                  pl.BlockSpec((1, tn), lambda j: (0, j))],
        out_specs=pl.BlockSpec((b, tn), lambda j: (0, j)),
        out_shape=jax.ShapeDtypeStruct((b, n), F32),
        compiler_params=_cparams(("arbitrary",)),
        name="modulation",
    )(c, w_mod, b_mod.reshape(1, n))


def _rms_mod(x, g, scale, shift):
    ms = jnp.mean(x * x, axis=-1, keepdims=True)
    return x * lax.rsqrt(ms + EPS) * g * (1.0 + scale) + shift


def _gelu_tanh(y):
    return y * (0.5 * (1.0 + jnp.tanh(0.7978845608028654 * (y + 0.044715 * (y * y * y)))))


def _proj_kernel(x_ref, mod_ref, g_ref, w_ref, cw_ref, cb_ref, qkv0_ref, qkv1_ref, qkv2_ref,
                 xc_ref, gy_ref, hs_ref, xe_ref, *, tm):
    @pl.when(pl.program_id(1) == 0)
    def _():
        xe_ref[0:CONV_TAIL, :] = jnp.zeros((CONV_TAIL, LRU_WIDTH), F32)

    @pl.when(pl.program_id(1) > 0)
    def _():
        xe_ref[0:CONV_TAIL, :] = xe_ref[tm:tm + CONV_TAIL, :]

    d_model = x_ref.shape[-1]
    m = mod_ref[0]
    h = _rms_mod(x_ref[0], g_ref[...], m[:, d_model:2 * d_model], m[:, 0:d_model])

    def mm(hv, lo, hi):
        return jnp.dot(hv, w_ref[:, lo:hi], preferred_element_type=F32)

    hb = h.astype(BF16)
    c0 = len(DILATIONS) * GROUP_COLS
    qkv0_ref[0] = mm(hb, 0, GROUP_COLS).astype(BF16)
    xr = mm(hb, c0, c0 + LRU_WIDTH)
    xe_ref[CONV_TAIL:, :] = xr
    cw = cw_ref[...]
    xc = xr * cw[CONV_WIDTH - 1:CONV_WIDTH] + cb_ref[...]
    for k in range(1, CONV_WIDTH):
        xc = xc + xe_ref[CONV_TAIL - k:CONV_TAIL - k + tm, :] * cw[CONV_WIDTH - 1 - k:CONV_WIDTH - k]
    xc_ref[0] = xc.astype(BF16)
    gy_ref[0] = _gelu_tanh(mm(hb, c0 + LRU_WIDTH, c0 + 2 * LRU_WIDTH)).astype(BF16)

    n_slab = d_model // LANE
    for j in range(n_slab):
        hs_ref[j] = h[:, j * LANE:(j + 1) * LANE]
    for g, out_ref in ((1, qkv1_ref), (2, qkv2_ref)):
        d = DILATIONS[g]
        rows = tm // d
        hp = jnp.concatenate(
            [jnp.concatenate([hs_ref[j, pl.ds(p, rows, stride=d), :] for j in range(n_slab)], axis=1)
             for p in range(d)], axis=0).astype(BF16)
        res = mm(hp, g * GROUP_COLS, (g + 1) * GROUP_COLS).astype(BF16)
        for p in range(d):
            out_ref[p] = res[p * rows:(p + 1) * rows]


def _projection(x, mod3, g1, w_r, conv_w, conv_b, *, tm=512):
    b, s, d = x.shape
    n = w_r.shape[1]
    assert s % tm == 0 and tm % (16 * DILATIONS[-1]) == 0 and CONV_TAIL >= CONV_WIDTH - 1
    out_shape = [jax.ShapeDtypeStruct((b * dd, s // dd, GROUP_COLS), BF16) for dd in DILATIONS]
    out_shape += [jax.ShapeDtypeStruct((b, s, LRU_WIDTH), BF16),
                  jax.ShapeDtypeStruct((b, s, LRU_WIDTH), BF16)]
    out_specs = [pl.BlockSpec((dd, tm // dd, GROUP_COLS), lambda bi, i: (bi, i, 0)) for dd in DILATIONS]
    out_specs += [pl.BlockSpec((1, tm, LRU_WIDTH), lambda bi, i: (bi, i, 0)),
                  pl.BlockSpec((1, tm, LRU_WIDTH), lambda bi, i: (bi, i, 0))]
    return pl.pallas_call(
        functools.partial(_proj_kernel, tm=tm),
        grid=(b, s // tm),
        in_specs=[pl.BlockSpec((1, tm, d), lambda bi, i: (bi, i, 0)),
                  pl.BlockSpec((1, 1, mod3.shape[-1]), lambda bi, i: (bi, 0, 0)),
                  pl.BlockSpec((1, d), lambda bi, i: (0, 0)),
                  _resident((d, n)),
                  pl.BlockSpec((CONV_WIDTH, LRU_WIDTH), lambda bi, i: (0, 0)),
                  pl.BlockSpec((1, LRU_WIDTH), lambda bi, i: (0, 0))],
        out_specs=out_specs,
        out_shape=out_shape,
        scratch_shapes=[pltpu.VMEM((d // LANE, tm, LANE), F32), pltpu.VMEM((CONV_TAIL + tm, LRU_WIDTH), F32)],
        compiler_params=_cparams(("parallel", "arbitrary"), VMEM_LIMIT),
        name="projection",
    )(x, mod3, g1, w_r, conv_w, conv_b.reshape(1, LRU_WIDTH))


def _attn_kernel(q0, k0, v0, q1, k1, v1, q2, k2, v2, o_ref, acc_ref, lse_ref, bias_ref, *, seq):
    hcols = o_ref.shape[-1]
    n_head = hcols // HEAD_DIM
    head_of_lane = lax.broadcasted_iota(jnp.int32, (SPAN, hcols), 1) // HEAD_DIM
    head_mask_b = [jnp.where(head_of_lane == h, 1.0, 0.0).astype(BF16) for h in range(n_head)]
    ones_blk = jnp.ones((2 * SPAN, hcols), BF16)

    def by_head(parts):
        out = parts[n_head - 1]
        for h in range(n_head - 2, -1, -1):
            out = jnp.where(head_of_lane == h, parts[h], out)
        return out

    @pl.when((pl.program_id(0) == 0) & (pl.program_id(1) == 0))
    def _():
        qi = lax.broadcasted_iota(jnp.int32, (n_head * SPAN, 2 * SPAN), 0) % SPAN
        ki = lax.broadcasted_iota(jnp.int32, (n_head * SPAN, 2 * SPAN), 1)
        band = (ki >= qi) & (ki <= qi + SPAN)
        bias_ref[0] = jnp.where(band, 0.0, -jnp.inf)
        bias_ref[1] = jnp.where(band & (ki >= SPAN), 0.0, -jnp.inf)

    for g, (q_ref, k_ref, v_ref) in enumerate(((q0, k0, v0), (q1, k1, v1), (q2, k2, v2))):
        d = DILATIONS[g]
        n_blk = seq // d // SPAN

        def tile(n, carry, q_ref=q_ref, k_ref=k_ref, v_ref=v_ref, d=d, n_blk=n_blk, g=g):
            p = n // n_blk
            blk = n % n_blk
            r0 = pl.multiple_of(blk * SPAN, SPAN)
            rp = pl.multiple_of(jnp.maximum(blk - 1, 0) * SPAN, SPAN)
            q = q_ref[p, pl.ds(r0, SPAN), :]
            kk = jnp.concatenate([k_ref[p, pl.ds(rp, SPAN), :], k_ref[p, pl.ds(r0, SPAN), :]], axis=0)
            vv = jnp.concatenate([v_ref[p, pl.ds(rp, SPAN), :], v_ref[p, pl.ds(r0, SPAN), :]], axis=0)
            qs = jnp.concatenate([q * head_mask_b[h] for h in range(n_head)], axis=0)
            sc = lax.dot_general(qs, kk, (((1,), (1,)), ((), ())), preferred_element_type=F32)
            sc = sc + bias_ref[jnp.where(blk > 0, 0, 1)]
            mx = jnp.max(sc, axis=-1, keepdims=True)
            e = jnp.exp2(sc - mx)
            pv = jnp.dot(e.astype(BF16), jnp.concatenate([vv, ones_blk], axis=1), preferred_element_type=F32)
            rows_of = lambda a: [a[h * SPAN:(h + 1) * SPAN] for h in range(n_head)]
            den_l = by_head(rows_of(pv[:, hcols:]))
            o = by_head(rows_of(pv[:, :hcols])) / den_l
            l = by_head(rows_of(mx)) + jnp.log(den_l) * LOG2_E
            start = p + d * r0
            for j in range(hcols // LANE):
                rows = pl.ds(start, SPAN, stride=d) if d > 1 else pl.ds(start, SPAN)
                acc_ref[g, j, rows, :] = o[:, j * LANE:(j + 1) * LANE]
                lse_ref[g, j, rows, :] = l[:, j * LANE:(j + 1) * LANE]
            return carry

        lax.fori_loop(0, seq // SPAN, tile, 0, unroll=16)

    chunk = 256

    def combine(c, carry):
        r = pl.multiple_of(c * chunk, chunk)
        for j in range(hcols // LANE):
            ls = [lse_ref[g, j, pl.ds(r, chunk), :] for g in range(len(DILATIONS))]
            mx = jnp.maximum(jnp.maximum(ls[0], ls[1]), ls[2])
            ws = [jnp.exp2(v - mx) for v in ls]
            num = ws[0] * acc_ref[0, j, pl.ds(r, chunk), :]
            for g in range(1, len(DILATIONS)):
                num = num + ws[g] * acc_ref[g, j, pl.ds(r, chunk), :]
            o_ref[0, pl.ds(r, chunk), j * LANE:(j + 1) * LANE] = (num / (ws[0] + ws[1] + ws[2])).astype(BF16)
        return carry

    lax.fori_loop(0, seq // chunk, combine, 0)


def _attention(qkvs, b, s):
    hcols = 2 * HEAD_DIM
    n_hg = ATTN_OUT // hcols
    ncb = ATTN_OUT // hcols
    in_specs, args = [], []
    for g, d in enumerate(DILATIONS):
        for part in range(3):
            in_specs.append(pl.BlockSpec((d, s // d, hcols),
                                         lambda bi, hg, part=part: (bi, 0, part * ncb + hg)))
            args.append(qkvs[g])
    return pl.pallas_call(
        functools.partial(_attn_kernel, seq=s),
        grid=(b, n_hg),
        in_specs=in_specs,
        out_specs=pl.BlockSpec((1, s, hcols), lambda bi, hg: (bi, 0, hg)),
        out_shape=jax.ShapeDtypeStruct((b, s, ATTN_OUT), BF16),
        scratch_shapes=[pltpu.VMEM((len(DILATIONS), hcols // LANE, s, LANE), F32),
                        pltpu.VMEM((len(DILATIONS), hcols // LANE, s, LANE), F32),
                        pltpu.VMEM((2, (hcols // HEAD_DIM) * SPAN, 2 * SPAN), F32)],
        compiler_params=_cparams(("arbitrary", "arbitrary"), VMEM_LIMIT),
        name="dilated_attention",
    )(*args)


def _lru_kernel(xc_ref, gy_ref, wg_ref, bx_ref, ba_ref, lam_ref, o_ref, a_ref, b_ref, h_ref, *, pitch):
    nb, ts, tc = xc_ref.shape
    nl = tc // LANE

    @pl.when(pl.program_id(1) == 0)
    def _():
        h_ref[...] = jnp.zeros_like(h_ref)

    xb = xc_ref[...].reshape(nb * ts, tc)
    xc = xb.astype(F32)
    gates = jnp.dot(xb, wg_ref[0], preferred_element_type=F32)
    gate_i = jax.nn.sigmoid(gates[:, :tc] + bx_ref[...])
    gate_r = jax.nn.sigmoid(gates[:, tc:] + ba_ref[...])
    neg_lam = -lam_ref[...]
    softplus = jnp.maximum(neg_lam, 0.0) + jnp.log1p(jnp.exp(-jnp.abs(neg_lam)))
    log_a = (-LRU_C) * gate_r * softplus
    a = jnp.exp(log_a)
    one_m_a2 = jnp.tanh(-log_a) * (1.0 + a * a)
    mult = jnp.where(one_m_a2 > 0.0, one_m_a2 * lax.rsqrt(one_m_a2), 0.0)
    bv = mult * gate_i * xc
    for bi in range(nb):
        for j in range(nl):
            a_ref[j, pl.ds(bi * pitch, ts), :] = a[bi * ts:(bi + 1) * ts, j * LANE:(j + 1) * LANE]
            b_ref[j, pl.ds(bi * pitch, ts), :] = bv[bi * ts:(bi + 1) * ts, j * LANE:(j + 1) * LANE]

    def step(t, hs):
        out = []
        for j in range(nl):
            rows = pl.ds(t, nb, stride=pitch)
            h = a_ref[j, rows, :] * hs[j] + b_ref[j, rows, :]
            b_ref[j, rows, :] = h
            out.append(h)
        return tuple(out)

    hs = lax.fori_loop(0, ts, step, tuple(h_ref[j] for j in range(nl)), unroll=8)
    for j in range(nl):
        h_ref[j] = hs[j]
    for bi in range(nb):
        h = jnp.concatenate([b_ref[j, pl.ds(bi * pitch, ts), :] for j in range(nl)], axis=1)
        o_ref[bi] = (h * gy_ref[bi].astype(F32)).astype(BF16)


def _lru_gate_weights(wx, wa, tc):
    nb, bd, _ = wx.shape
    per = tc // bd
    eye = jnp.eye(per, dtype=wx.dtype)

    def bdiag(w):
        w = w.reshape(nb // per, per, bd, bd)
        return jnp.einsum('cpio,pq->cpiqo', w, eye).reshape(nb // per, tc, tc)

    return jnp.concatenate([bdiag(wx), bdiag(wa)], axis=-1).astype(BF16)


def _lru_branch(xc, gy, wx, bx, wa, ba, lam, *, tc=256, ts=256):
    b, s, c = xc.shape
    assert s % ts == 0 and c % tc == 0
    wg = _lru_gate_weights(wx, wa, tc)
    row = lambda v: v.reshape(1, c)
    tile = pl.BlockSpec((b, ts, tc), lambda ci, ti: (0, ti, ci))
    vec = pl.BlockSpec((1, tc), lambda ci, ti: (0, ci))
    pitch = ts + 8
    return pl.pallas_call(
        functools.partial(_lru_kernel, pitch=pitch),
        grid=(c // tc, s // ts),
        in_specs=[tile, tile,
                  pl.BlockSpec((1, tc, 2 * tc), lambda ci, ti: (ci, 0, 0)),
                  vec, vec, vec],
        out_specs=tile,
        out_shape=jax.ShapeDtypeStruct((b, s, c), BF16),
        scratch_shapes=[pltpu.VMEM((tc // LANE, b * pitch, LANE), F32),
                        pltpu.VMEM((tc // LANE, b * pitch, LANE), F32),
                        pltpu.VMEM((tc // LANE, b, LANE), F32)],
        compiler_params=_cparams(("parallel", "arbitrary"), VMEM_LIMIT),
        name="rg_lru",
    )(xc, gy, wg, row(bx), row(ba), row(lam))


def _prep_w_in(w_in):
    a = ATTN_WIDTH
    gw = N_SLOTS * HEAD_DIM
    q = w_in[:, :a] * (HEAD_DIM ** -0.5 * LOG2_E)
    k = w_in[:, a:2 * a]
    v = w_in[:, 2 * a:3 * a]
    parts = []
    for g in range(len(DILATIONS)):
        sl = slice(g * gw, (g + 1) * gw)
        parts += [q[:, sl], k[:, sl], v[:, sl]]
    parts.append(w_in[:, 3 * a:3 * a + 2 * LRU_WIDTH])
    return jnp.concatenate(parts, axis=1).astype(BF16), w_in[:, 3 * a + 2 * LRU_WIDTH:].astype(BF16)


ROW_SUBLANES = D_MODEL // 2 // LANE


def _store_tile_rows(ref, v, row0=0):
    n, half = v.shape[0], v.shape[1] // 2
    lo = pltpu.bitcast(v[:, :half].astype(BF16).astype(F32), jnp.uint32)
    hi = pltpu.bitcast(v[:, half:].astype(BF16).astype(F32), jnp.uint32)
    words = (hi & jnp.uint32(0xFFFF0000)) | (lo >> 16)
    for j in range(ROW_SUBLANES):
        ref[pl.ds(row0 * ROW_SUBLANES + j, n, stride=ROW_SUBLANES), :] = words[:, j * LANE:(j + 1) * LANE]


def _load_tile_rows(ref, n=None):
    n = ref.shape[0] // ROW_SUBLANES if n is None else n
    words = [ref[pl.ds(j, n, stride=ROW_SUBLANES), :] for j in range(ROW_SUBLANES)]
    lo = [pltpu.bitcast(w << 16, F32) for w in words]
    hi = [pltpu.bitcast(w & jnp.uint32(0xFFFF0000), F32) for w in words]
    return jnp.concatenate(lo + hi, axis=-1)


SC_CORES, SC_SUBCORES = 2, 16
SC_CHUNK = 128


def _sc_gather_rows(table, idx):
    n = idx.shape[0]
    per_worker = n // (SC_CORES * SC_SUBCORES)
    n_chunks = per_worker // SC_CHUNK
    assert n_chunks * SC_CHUNK * SC_CORES * SC_SUBCORES == n
    mesh = plsc.VectorSubcoreMesh(core_axis_name="c", subcore_axis_name="s")

    def body(table_hbm, idx_hbm, out_hbm, idx_v, rows_v, sem):
        base = (lax.axis_index("s") * SC_CORES + lax.axis_index("c")) * per_worker

        @pl.loop(0, n_chunks)
        def _(i):
            off = pl.multiple_of(base + i * SC_CHUNK, SC_CHUNK)
            pltpu.sync_copy(idx_hbm.at[pl.ds(off, SC_CHUNK)], idx_v)
            pltpu.async_copy(table_hbm.at[idx_v], rows_v, sem).wait()
            pltpu.sync_copy(rows_v, out_hbm.at[pl.ds(off, SC_CHUNK)])

    return pl.kernel(
        body, mesh=mesh,
        out_type=jax.ShapeDtypeStruct((n,) + table.shape[1:], table.dtype),
        scratch_types=[pltpu.VMEM((SC_CHUNK,), jnp.int32),
                       pltpu.VMEM((SC_CHUNK,) + table.shape[1:], table.dtype),
                       pltpu.SemaphoreType.DMA],
        name="sc_gather_rows",
    )(table, idx)


def _sc_scatter_rows(rows, idx, n_out):
    n_rows = rows.shape[0]
    n_choice = idx.shape[0] // n_rows
    per_worker = n_rows // (SC_CORES * SC_SUBCORES)
    n_chunks = per_worker // SC_CHUNK
    assert n_chunks * SC_CHUNK * SC_CORES * SC_SUBCORES == n_rows and n_choice * n_rows == idx.shape[0]
    mesh = plsc.VectorSubcoreMesh(core_axis_name="c", subcore_axis_name="s")

    def body(rows_hbm, idx_hbm, out_hbm, idx_v, rows_v):
        base = (lax.axis_index("s") * SC_CORES + lax.axis_index("c")) * per_worker

        @pl.loop(0, n_chunks)
        def _(i):
            off = pl.multiple_of(base + i * SC_CHUNK, SC_CHUNK)
            pltpu.sync_copy(rows_hbm.at[pl.ds(off, SC_CHUNK)], rows_v)
            for k in range(n_choice):
                pltpu.sync_copy(idx_hbm.at[pl.ds(k * n_rows + off, SC_CHUNK)], idx_v)
                pltpu.sync_copy(rows_v, out_hbm.at[idx_v])

    return pl.kernel(
        body, mesh=mesh,
        out_type=jax.ShapeDtypeStruct((n_out,) + rows.shape[1:], rows.dtype),
        scratch_types=[pltpu.VMEM((SC_CHUNK,), jnp.int32),
                       pltpu.VMEM((SC_CHUNK,) + rows.shape[1:], rows.dtype)],
        name="sc_scatter_rows",
    )(rows, idx)


ROUTE_ROWS = 8
EXPERT_ROW0 = N_GROUPS
ROUTER_ROWS = 48


def _mix_kernel(attn_ref, lru_ref, x_ref, mod_ref, g1_ref, wg_ref, wa_ref, wl_ref, wo_ref, g2_ref,
                wrt_ref, brt_ref, x1_ref, h2_ref, route_ref, cnt_ref, cnt_acc):
    d = x_ref.shape[-1]
    tm = x_ref.shape[1]

    @pl.when((pl.program_id(0) == 0) & (pl.program_id(1) == 0))
    def _():
        cnt_acc[...] = jnp.zeros_like(cnt_acc)

    m = mod_ref[0]
    gate1, shift2, scale2 = m[:, 2 * d:3 * d], m[:, 3 * d:4 * d], m[:, 4 * d:5 * d]
    x = x_ref[0]
    h1 = _rms_mod(x, g1_ref[...], m[:, d:2 * d], m[:, 0:d]).astype(BF16)
    gates = jax.nn.sigmoid(jnp.dot(h1, wg_ref[...], preferred_element_type=F32))
    ya = jnp.dot(attn_ref[0], wa_ref[...], preferred_element_type=F32)
    yl = jnp.dot(lru_ref[0], wl_ref[...], preferred_element_type=F32)
    mixed = gates[:, :d] * ya + gates[:, d:] * yl
    y = jnp.dot(mixed.astype(BF16), wo_ref[...], preferred_element_type=F32)
    x1 = x + (1.0 + gate1) * y
    x1_ref[0] = x1.astype(BF16)
    h2 = _rms_mod(x1, g2_ref[...], scale2, shift2)
    _store_tile_rows(h2_ref, h2)
    logits = lax.dot_general(wrt_ref[...], h2.astype(BF16), (((1,), (1,)), ((), ())),
                             preferred_element_type=F32) + brt_ref[...]

    row = lax.broadcasted_iota(jnp.int32, logits.shape, 0)
    neg = -jnp.inf

    def top(vals):
        mx = jnp.max(vals, axis=0, keepdims=True)
        idx = jnp.min(jnp.where(vals == mx, row, ROUTER_ROWS), axis=0, keepdims=True)
        return mx, idx

    is_grp = row < N_GROUPS
    gmax, gidx = top(jnp.where(is_grp, logits, neg))
    grp_gate = 1.0 / jnp.sum(jnp.where(is_grp, jnp.exp(logits - gmax), 0.0), axis=0, keepdims=True)
    lo = EXPERT_ROW0 + EXPERTS_PER_GROUP * gidx
    el = jnp.where((row >= lo) & (row < lo + EXPERTS_PER_GROUP), logits, neg)
    v1, i1 = top(el)
    v2, i2 = top(jnp.where(row == i1, neg, el))
    e21 = jnp.exp(v2 - v1)
    wt1 = grp_gate / (1.0 + e21)
    wt2 = wt1 * e21

    oh1 = jnp.where(row == i1, 1.0, 0.0)
    oh2 = jnp.where(row == i2, 1.0, 0.0)
    ohs = oh1 + oh2
    rr = lax.broadcasted_iota(jnp.int32, (tm, tm), 0)
    cc = lax.broadcasted_iota(jnp.int32, (tm, tm), 1)
    earlier = jnp.where(rr < cc, 1.0, 0.0).astype(BF16)
    before = jnp.dot(ohs.astype(BF16), earlier, preferred_element_type=F32) + cnt_acc[...]
    rank1 = jnp.sum(oh1 * before, axis=0, keepdims=True)
    rank2 = jnp.sum(oh2 * before, axis=0, keepdims=True)
    cnt_acc[...] = cnt_acc[...] + jnp.sum(ohs, axis=1, keepdims=True)
    cnt_ref[...] = cnt_acc[...]

    vals = [(i1 - EXPERT_ROW0).astype(F32), (i2 - EXPERT_ROW0).astype(F32), rank1, rank2, wt1, wt2]
    out_row = lax.broadcasted_iota(jnp.int32, (ROUTE_ROWS, tm), 0)
    slab = jnp.zeros((ROUTE_ROWS, tm), F32)
    for j, v in enumerate(vals):
        slab = jnp.where(out_row == j, v, slab)
    route_ref[...] = slab


def _mix_route(attn, lru, x, mod3, g1, wg, wa, wl, wo, g2, wrt, brt, b0, nb, *, tm=512):
    _, s, d = x.shape
    spt = s // tm
    tok_in = lambda w: pl.BlockSpec((1, tm, w), lambda bi, i: (b0 + bi, i, 0))
    return pl.pallas_call(
        _mix_kernel,
        grid=(nb, spt),
        in_specs=[tok_in(attn.shape[-1]), tok_in(d), tok_in(d),
                  pl.BlockSpec((1, 1, mod3.shape[-1]), lambda bi, i: (b0 + bi, 0, 0)),
                  pl.BlockSpec((1, d), lambda bi, i: (0, 0)),
                  _resident(wg.shape), _resident(wa.shape), _resident(wl.shape), _resident(wo.shape),
                  pl.BlockSpec((1, d), lambda bi, i: (0, 0)),
                  _resident(wrt.shape),
                  pl.BlockSpec((ROUTER_ROWS, 1), lambda bi, i: (0, 0))],
        out_specs=[pl.BlockSpec((1, tm, d), lambda bi, i: (bi, i, 0)),
                   pl.BlockSpec((tm * ROW_SUBLANES, LANE), lambda bi, i: (bi * spt + i, 0)),
                   pl.BlockSpec((ROUTE_ROWS, tm), lambda bi, i: (0, bi * spt + i)),
                   pl.BlockSpec((ROUTER_ROWS, 1), lambda bi, i: (0, 0))],
        out_shape=[jax.ShapeDtypeStruct((nb, s, d), BF16),
                   jax.ShapeDtypeStruct((nb * s * ROW_SUBLANES, LANE), jnp.uint32),
                   jax.ShapeDtypeStruct((ROUTE_ROWS, nb * s), F32),
                   jax.ShapeDtypeStruct((ROUTER_ROWS, 1), F32)],
        scratch_shapes=[pltpu.VMEM((ROUTER_ROWS, 1), F32)],
        compiler_params=_cparams(("arbitrary", "arbitrary"), VMEM_LIMIT),
        name="mix_route",
    )(attn, lru, x, mod3, g1, wg, wa, wl, wo, g2, wrt, brt)


TOP_K = 2
EXPERT_BLOCK = 512
MOE_BATCH_RANGES = 2


def _expert_kernel(be_ref, bv_ref, first_ref, slot_ref, next_ref, nu_ref, x_ref, w1_hbm, w3_hbm, w2_hbm,
                   y_ref, wf1, wf3, wf2, wb1, wb3, wb2, sem):
    j = pl.program_id(0)
    half = EXPERT_BLOCK // 2

    def fetch(e, s):
        return [pltpu.make_async_copy(w1_hbm.at[e], wf1.at[s], sem.at[s, 0]),
                pltpu.make_async_copy(w3_hbm.at[e], wf3.at[s], sem.at[s, 1]),
                pltpu.make_async_copy(w2_hbm.at[e], wf2.at[s], sem.at[s, 2])]

    def ffn(n):
        xb = _load_tile_rows(x_ref, n).astype(BF16)
        a = jnp.dot(xb, wb1[...], preferred_element_type=F32)
        g = jnp.dot(xb, wb3[...], preferred_element_type=F32)
        hm = (a * jax.nn.sigmoid(a) * g).astype(BF16)
        _store_tile_rows(y_ref, jnp.dot(hm, wb2[...], preferred_element_type=F32))

    @pl.when(j < nu_ref[0])
    def _():
        @pl.when(first_ref[j] == 1)
        def _():
            e, s = be_ref[j], slot_ref[j]

            @pl.when(j == 0)
            def _():
                for c in fetch(e, s):
                    c.start()

            @pl.when(next_ref[j] >= 0)
            def _():
                for c in fetch(next_ref[j], 1 - s):
                    c.start()

            for c in fetch(e, s):
                c.wait()
            wb1[...] = wf1[s].astype(BF16)
            wb3[...] = wf3[s].astype(BF16)
            wb2[...] = wf2[s].astype(BF16)

        @pl.when(bv_ref[j] > half)
        def _():
            ffn(EXPERT_BLOCK)

        @pl.when(bv_ref[j] <= half)
        def _():
            ffn(half)
            y_ref[half * ROW_SUBLANES:, :] = jnp.zeros((half * ROW_SUBLANES, LANE), y_ref.dtype)

    @pl.when(j >= nu_ref[0])
    def _():
        y_ref[...] = jnp.zeros_like(y_ref)


def _experts(xp, plan, w1, w3, w2):
    ne, d, de = w1.shape
    nb = xp.shape[0] // (EXPERT_BLOCK * ROW_SUBLANES)
    rows = (EXPERT_BLOCK * ROW_SUBLANES, LANE)
    grid_spec = pltpu.PrefetchScalarGridSpec(
        num_scalar_prefetch=len(plan),
        grid=(nb,),
        in_specs=[pl.BlockSpec(rows, lambda j, *p: (jnp.minimum(j, p[-1][0] - 1), 0)),
                  pl.BlockSpec(memory_space=pl.ANY),
                  pl.BlockSpec(memory_space=pl.ANY),
                  pl.BlockSpec(memory_space=pl.ANY)],
        out_specs=pl.BlockSpec(rows, lambda j, *p: (j, 0)),
        scratch_shapes=[pltpu.VMEM((2, d, de), F32), pltpu.VMEM((2, d, de), F32), pltpu.VMEM((2, de, d), F32),
                        pltpu.VMEM((d, de), BF16), pltpu.VMEM((d, de), BF16), pltpu.VMEM((de, d), BF16),
                        pltpu.SemaphoreType.DMA((2, 3))])
    return pl.pallas_call(
        _expert_kernel,
        grid_spec=grid_spec,
        out_shape=jax.ShapeDtypeStruct(xp.shape, xp.dtype),
        compiler_params=_cparams(("arbitrary",), VMEM_LIMIT),
        name="experts",
    )(*plan, xp, w1, w3, w2)


def _combine_kernel(y0_ref, y1_ref, route_ref, x1_ref, mod_ref, gf_ref, *rest):
    o_ref = rest[-1]
    tm, d = x1_ref.shape[1], x1_ref.shape[2]
    route = jnp.concatenate([route_ref[...], jnp.zeros((LANE - ROUTE_ROWS, tm), F32)], axis=0).T
    moe = _load_tile_rows(y0_ref) * route[:, 4:5] + _load_tile_rows(y1_ref) * route[:, 5:6]
    gate2 = mod_ref[0][:, 5 * d:6 * d]
    xo = x1_ref[0].astype(F32) + (1.0 + gate2) * moe
    ms = jnp.mean(xo * xo, axis=-1, keepdims=True)
    o_ref[0] = xo * lax.rsqrt(ms + EPS) * gf_ref[...]


def _combine(yg, route, x1, mod3, gf, b0, out_prev, *, tm=1024):
    nb, s, d = x1.shape
    b_all = mod3.shape[0]
    spt = s // tm
    nt = nb * spt
    rows = (tm * ROW_SUBLANES, LANE)
    in_specs = [pl.BlockSpec(rows, lambda bi, i: (bi * spt + i, 0)),
                pl.BlockSpec(rows, lambda bi, i: (nt + bi * spt + i, 0)),
                pl.BlockSpec((ROUTE_ROWS, tm), lambda bi, i: (0, bi * spt + i)),
                pl.BlockSpec((1, tm, d), lambda bi, i: (bi, i, 0)),
                pl.BlockSpec((1, 1, mod3.shape[-1]), lambda bi, i: (b0 + bi, 0, 0)),
                pl.BlockSpec((1, d), lambda bi, i: (0, 0))]
    args = [yg, yg, route, x1, mod3, gf]
    aliases = {}
    if out_prev is not None:
        in_specs.append(pl.BlockSpec(memory_space=pl.ANY))
        aliases = {len(args): 0}
        args.append(out_prev)
    return pl.pallas_call(
        _combine_kernel,
        grid=(nb, spt),
        in_specs=in_specs,
        out_specs=pl.BlockSpec((1, tm, d), lambda bi, i: (b0 + bi, i, 0)),
        out_shape=jax.ShapeDtypeStruct((b_all, s, d), F32),
        input_output_aliases=aliases,
        compiler_params=_cparams(("parallel", "parallel"), VMEM_LIMIT),
        name="combine",
    )(*args)


def _slot_plan(route, counts, n_tok):
    sizes = counts[EXPERT_ROW0:EXPERT_ROW0 + N_EXPERTS, 0].astype(jnp.int32)
    padded = (sizes + EXPERT_BLOCK - 1) // EXPERT_BLOCK * EXPERT_BLOCK
    pad_ends = jnp.cumsum(padded)
    pad_starts = pad_ends - padded
    eid = route[0:TOP_K].astype(jnp.int32)
    rank = route[TOP_K:2 * TOP_K].astype(jnp.int32)
    start = jnp.sum(jnp.where(eid[..., None] == jnp.arange(N_EXPERTS), pad_starts, 0), axis=-1)
    dest = (start + rank).reshape(TOP_K * n_tok)
    n_blocks = (n_tok * TOP_K + N_EXPERTS * (EXPERT_BLOCK - 1) + EXPERT_BLOCK - 1) // EXPERT_BLOCK
    gran = SC_CORES * SC_SUBCORES * SC_CHUNK // math.gcd(SC_CORES * SC_SUBCORES * SC_CHUNK, EXPERT_BLOCK)
    n_blocks = (n_blocks + gran - 1) // gran * gran
    n_used = pad_ends[-1] // EXPERT_BLOCK
    blk = jnp.minimum(jnp.arange(n_blocks), n_used - 1)
    blk_e = jnp.minimum(jnp.sum(pad_ends[None, :] <= (blk * EXPERT_BLOCK)[:, None], axis=1), N_EXPERTS - 1)
    blk_valid = jnp.clip(sizes[blk_e] - (blk * EXPERT_BLOCK - pad_starts[blk_e]), 0, EXPERT_BLOCK)
    idx = jnp.arange(n_blocks)
    first = (idx < n_used) & ((idx == 0) | (blk_e != jnp.roll(blk_e, 1)))
    slot = (jnp.cumsum(first) - 1) % 2
    later_first = lax.cummin(jnp.where(first, idx, n_blocks), reverse=True)
    next_first = jnp.concatenate([later_first[1:], jnp.full((1,), n_blocks)])
    next_e = jnp.where(next_first < n_blocks, blk_e[jnp.minimum(next_first, n_blocks - 1)], -1)
    i32 = lambda a: a.astype(jnp.int32)
    plan = (i32(blk_e), i32(blk_valid), i32(first), i32(slot), i32(next_e), i32(n_used.reshape(1)))
    return dest, n_blocks * EXPERT_BLOCK, plan


def kernel(x, c, w_mod, b_mod, norm1_g, w_in, conv_w, conv_b, lru_wx, lru_bx, lru_wa, lru_ba, lru_lambda, w_attn_o, w_lru_o, w_out, norm2_g, w_grp, b_grp, w_exp, b_exp, w1, w3, w2, norm_f_g):
    b, s, d = x.shape
    assert d == D_MODEL and s == SPAN * DILATIONS[-1] and w_mod.shape[0] == 1
    mod3 = _modulation(c, w_mod[0], b_mod[0]).reshape(b, 1, 6 * d)
    w_proj, w_gate = _prep_w_in(w_in[0])
    g1 = norm1_g[0].reshape(1, d)
    qkv0, qkv1, qkv2, xc, gy = _projection(x, mod3, g1, w_proj, conv_w[0], conv_b[0])
    attn = _attention((qkv0, qkv1, qkv2), b, s)
    lru = _lru_branch(xc, gy, lru_wx[0], lru_bx[0], lru_wa[0], lru_ba[0], lru_lambda[0])

    n_pad = ROUTER_ROWS - N_GROUPS - N_EXPERTS
    wr = jnp.pad(jnp.concatenate([w_grp[0], w_exp[0]], axis=1).T, ((0, n_pad), (0, 0))).astype(BF16)
    br = jnp.pad(jnp.concatenate([b_grp[0], b_exp[0]]), (0, n_pad)).reshape(ROUTER_ROWS, 1)
    wa, wl, wo = w_attn_o[0].astype(BF16), w_lru_o[0].astype(BF16), w_out[0].astype(BF16)
    as_rows = lambda a: a.reshape(-1, ROW_SUBLANES, LANE)
    as_tiles = lambda a: a.reshape(-1, LANE)

    out = None
    nb = b // MOE_BATCH_RANGES
    for b0 in range(0, b, nb):
        x1, h2, route, counts = _mix_route(attn, lru, x, mod3, g1, w_gate, wa, wl, wo,
                                           norm2_g[0].reshape(1, d), wr, br, b0, nb)
        dest, n_slots, plan = _slot_plan(route, counts, nb * s)
        xp = as_tiles(_sc_scatter_rows(as_rows(h2), dest, n_slots))
        yp = _experts(xp, plan, w1[0], w3[0], w2[0])
        yg = as_tiles(_sc_gather_rows(as_rows(yp), dest))
        out = _combine(yg, route, x1, mod3, norm_f_g.reshape(1, d), b0, out)
    return out
```

```python
import functools
import math

import jax
import jax.numpy as jnp
from jax import lax
from jax.experimental import pallas as pl
from jax.experimental.pallas import tpu as pltpu
from jax.experimental.pallas import tpu_sc as plsc

F32 = jnp.float32
BF16 = jnp.bfloat16

D_MODEL = 1024
HEAD_DIM = 64
N_SLOTS = 8
SPAN = 128
DILATIONS = (1, 4, 16)
GROUP_COLS = 3 * N_SLOTS * HEAD_DIM
ATTN_WIDTH = len(DILATIONS) * N_SLOTS * HEAD_DIM
ATTN_OUT = N_SLOTS * HEAD_DIM
LRU_WIDTH = D_MODEL
CONV_WIDTH = 4
CONV_TAIL = 8
LRU_C = 8.0
N_GROUPS = 4
EXPERTS_PER_GROUP = 8
N_EXPERTS = N_GROUPS * EXPERTS_PER_GROUP
EPS = 1e-6
LOG2_E = 1.4426950408889634
LANE = 128
VMEM_LIMIT = 56 * 1024 * 1024


def _cparams(sem, vmem=None):
    return pltpu.CompilerParams(dimension_semantics=sem, vmem_limit_bytes=vmem)


def _resident(shape):
    nd = len(shape)
    return pl.BlockSpec(shape, lambda *_: (0,) * nd, pipeline_mode=pl.Buffered(1))


def _mod_kernel(c_ref, w_ref, b_ref, o_ref):
    c = c_ref[...]
    ca = c * jax.nn.sigmoid(c)
    o_ref[...] = jnp.dot(ca.astype(BF16), w_ref[...].astype(BF16),
                         preferred_element_type=F32) + b_ref[...]


def _modulation(c, w_mod, b_mod):
    b, d = c.shape
    n = w_mod.shape[1]
    tn = n // 4
    return pl.pallas_call(
        _mod_kernel,
        grid=(n // tn,),
        in_specs=[pl.BlockSpec((b, d), lambda j: (0, 0)),
                  pl.BlockSpec((d, tn), lambda j: (0, j)),
                  pl.BlockSpec((1, tn), lambda j: (0, j))],
        out_specs=pl.BlockSpec((b, tn), lambda j: (0, j)),
        out_shape=jax.ShapeDtypeStruct((b, n), F32),
        compiler_params=_cparams(("arbitrary",)),
        name="modulation",
    )(c, w_mod, b_mod.reshape(1, n))


def _rms_mod(x, g, scale, shift):
    ms = jnp.mean(x * x, axis=-1, keepdims=True)
    return x * lax.rsqrt(ms + EPS) * g * (1.0 + scale) + shift


def _gelu_tanh(y):
    return y * (0.5 * (1.0 + jnp.tanh(0.7978845608028654 * (y + 0.044715 * (y * y * y)))))


def _proj_kernel(x_ref, mod_ref, g_ref, w_ref, cw_ref, cb_ref, qkv0_ref, qkv1_ref, qkv2_ref,
                 xc_ref, gy_ref, hs_ref, xe_ref, *, tm):
    @pl.when(pl.program_id(1) == 0)
    def _():
        xe_ref[0:CONV_TAIL, :] = jnp.zeros((CONV_TAIL, LRU_WIDTH), F32)

    @pl.when(pl.program_id(1) > 0)
    def _():
        xe_ref[0:CONV_TAIL, :] = xe_ref[tm:tm + CONV_TAIL, :]

    d_model = x_ref.shape[-1]
    m = mod_ref[0]
    h = _rms_mod(x_ref[0], g_ref[...], m[:, d_model:2 * d_model], m[:, 0:d_model])

    def mm(hv, lo, hi):
        return jnp.dot(hv, w_ref[:, lo:hi], preferred_element_type=F32)

    hb = h.astype(BF16)
    c0 = len(DILATIONS) * GROUP_COLS
    qkv0_ref[0] = mm(hb, 0, GROUP_COLS).astype(BF16)
    xr = mm(hb, c0, c0 + LRU_WIDTH)
    xe_ref[CONV_TAIL:, :] = xr
    cw = cw_ref[...]
    xc = xr * cw[CONV_WIDTH - 1:CONV_WIDTH] + cb_ref[...]
    for k in range(1, CONV_WIDTH):
        xc = xc + xe_ref[CONV_TAIL - k:CONV_TAIL - k + tm, :] * cw[CONV_WIDTH - 1 - k:CONV_WIDTH - k]
    xc_ref[0] = xc.astype(BF16)
    gy_ref[0] = _gelu_tanh(mm(hb, c0 + LRU_WIDTH, c0 + 2 * LRU_WIDTH)).astype(BF16)

    n_slab = d_model // LANE
    for j in range(n_slab):
        hs_ref[j] = h[:, j * LANE:(j + 1) * LANE]
    for g, out_ref in ((1, qkv1_ref), (2, qkv2_ref)):
        d = DILATIONS[g]
        rows = tm // d
        hp = jnp.concatenate(
            [jnp.concatenate([hs_ref[j, pl.ds(p, rows, stride=d), :] for j in range(n_slab)], axis=1)
             for p in range(d)], axis=0).astype(BF16)
        res = mm(hp, g * GROUP_COLS, (g + 1) * GROUP_COLS).astype(BF16)
        for p in range(d):
            out_ref[p] = res[p * rows:(p + 1) * rows]


def _projection(x, mod3, g1, w_r, conv_w, conv_b, *, tm=512):
    b, s, d = x.shape
    n = w_r.shape[1]
    assert s % tm == 0 and tm % (16 * DILATIONS[-1]) == 0 and CONV_TAIL >= CONV_WIDTH - 1
    out_shape = [jax.ShapeDtypeStruct((b * dd, s // dd, GROUP_COLS), BF16) for dd in DILATIONS]
    out_shape += [jax.ShapeDtypeStruct((b, s, LRU_WIDTH), BF16),
                  jax.ShapeDtypeStruct((b, s, LRU_WIDTH), BF16)]
    out_specs = [pl.BlockSpec((dd, tm // dd, GROUP_COLS), lambda bi, i: (bi, i, 0)) for dd in DILATIONS]
    out_specs += [pl.BlockSpec((1, tm, LRU_WIDTH), lambda bi, i: (bi, i, 0)),
                  pl.BlockSpec((1, tm, LRU_WIDTH), lambda bi, i: (bi, i, 0))]
    return pl.pallas_call(
        functools.partial(_proj_kernel, tm=tm),
        grid=(b, s // tm),
        in_specs=[pl.BlockSpec((1, tm, d), lambda bi, i: (bi, i, 0)),
                  pl.BlockSpec((1, 1, mod3.shape[-1]), lambda bi, i: (bi, 0, 0)),
                  pl.BlockSpec((1, d), lambda bi, i: (0, 0)),
                  _resident((d, n)),
                  pl.BlockSpec((CONV_WIDTH, LRU_WIDTH), lambda bi, i: (0, 0)),
                  pl.BlockSpec((1, LRU_WIDTH), lambda bi, i: (0, 0))],
        out_specs=out_specs,
        out_shape=out_shape,
        scratch_shapes=[pltpu.VMEM((d // LANE, tm, LANE), F32), pltpu.VMEM((CONV_TAIL + tm, LRU_WIDTH), F32)],
        compiler_params=_cparams(("parallel", "arbitrary"), VMEM_LIMIT),
        name="projection",
    )(x, mod3, g1, w_r, conv_w, conv_b.reshape(1, LRU_WIDTH))


def _attn_kernel(q0, k0, v0, q1, k1, v1, q2, k2, v2, o_ref, acc_ref, lse_ref, bias_ref, *, seq):
    hcols = o_ref.shape[-1]
    n_head = hcols // HEAD_DIM
    head_of_lane = lax.broadcasted_iota(jnp.int32, (SPAN, hcols), 1) // HEAD_DIM
    head_mask_b = [jnp.where(head_of_lane == h, 1.0, 0.0).astype(BF16) for h in range(n_head)]
    ones_blk = jnp.ones((2 * SPAN, hcols), BF16)

    def by_head(parts):
        out = parts[n_head - 1]
        for h in range(n_head - 2, -1, -1):
            out = jnp.where(head_of_lane == h, parts[h], out)
        return out

    @pl.when((pl.program_id(0) == 0) & (pl.program_id(1) == 0))
    def _():
        qi = lax.broadcasted_iota(jnp.int32, (n_head * SPAN, 2 * SPAN), 0) % SPAN
        ki = lax.broadcasted_iota(jnp.int32, (n_head * SPAN, 2 * SPAN), 1)
        band = (ki >= qi) & (ki <= qi + SPAN)
        bias_ref[0] = jnp.where(band, 0.0, -jnp.inf)
        bias_ref[1] = jnp.where(band & (ki >= SPAN), 0.0, -jnp.inf)

    for g, (q_ref, k_ref, v_ref) in enumerate(((q0, k0, v0), (q1, k1, v1), (q2, k2, v2))):
        d = DILATIONS[g]
        n_blk = seq // d // SPAN

        def tile(n, carry, q_ref=q_ref, k_ref=k_ref, v_ref=v_ref, d=d, n_blk=n_blk, g=g):
            p = n // n_blk
            blk = n % n_blk
            r0 = pl.multiple_of(blk * SPAN, SPAN)
            rp = pl.multiple_of(jnp.maximum(blk - 1, 0) * SPAN, SPAN)
            q = q_ref[p, pl.ds(r0, SPAN), :]
            kk = jnp.concatenate([k_ref[p, pl.ds(rp, SPAN), :], k_ref[p, pl.ds(r0, SPAN), :]], axis=0)
            vv = jnp.concatenate([v_ref[p, pl.ds(rp, SPAN), :], v_ref[p, pl.ds(r0, SPAN), :]], axis=0)
            qs = jnp.concatenate([q * head_mask_b[h] for h in range(n_head)], axis=0)
            sc = lax.dot_general(qs, kk, (((1,), (1,)), ((), ())), preferred_element_type=F32)
            sc = sc + bias_ref[jnp.where(blk > 0, 0, 1)]
            mx = jnp.max(sc, axis=-1, keepdims=True)
            e = jnp.exp2(sc - mx)
            pv = jnp.dot(e.astype(BF16), jnp.concatenate([vv, ones_blk], axis=1), preferred_element_type=F32)
            rows_of = lambda a: [a[h * SPAN:(h + 1) * SPAN] for h in range(n_head)]
            den_l = by_head(rows_of(pv[:, hcols:]))
            o = by_head(rows_of(pv[:, :hcols])) / den_l
            l = by_head(rows_of(mx)) + jnp.log(den_l) * LOG2_E
            start = p + d * r0
            for j in range(hcols // LANE):
                rows = pl.ds(start, SPAN, stride=d) if d > 1 else pl.ds(start, SPAN)
                acc_ref[g, j, rows, :] = o[:, j * LANE:(j + 1) * LANE]
                lse_ref[g, j, rows, :] = l[:, j * LANE:(j + 1) * LANE]
            return carry

        lax.fori_loop(0, seq // SPAN, tile, 0, unroll=16)

    chunk = 256

    def combine(c, carry):
        r = pl.multiple_of(c * chunk, chunk)
        for j in range(hcols // LANE):
            ls = [lse_ref[g, j, pl.ds(r, chunk), :] for g in range(len(DILATIONS))]
            mx = jnp.maximum(jnp.maximum(ls[0], ls[1]), ls[2])
            ws = [jnp.exp2(v - mx) for v in ls]
            num = ws[0] * acc_ref[0, j, pl.ds(r, chunk), :]
            for g in range(1, len(DILATIONS)):
                num = num + ws[g] * acc_ref[g, j, pl.ds(r, chunk), :]
            o_ref[0, pl.ds(r, chunk), j * LANE:(j + 1) * LANE] = (num / (ws[0] + ws[1] + ws[2])).astype(BF16)
        return carry

    lax.fori_loop(0, seq // chunk, combine, 0)


def _attention(qkvs, b, s):
    hcols = 2 * HEAD_DIM
    n_hg = ATTN_OUT // hcols
    ncb = ATTN_OUT // hcols
    in_specs, args = [], []
    for g, d in enumerate(DILATIONS):
        for part in range(3):
            in_specs.append(pl.BlockSpec((d, s // d, hcols),
                                         lambda bi, hg, part=part: (bi, 0, part * ncb + hg)))
            args.append(qkvs[g])
    return pl.pallas_call(
        functools.partial(_attn_kernel, seq=s),
        grid=(b, n_hg),
        in_specs=in_specs,
        out_specs=pl.BlockSpec((1, s, hcols), lambda bi, hg: (bi, 0, hg)),
        out_shape=jax.ShapeDtypeStruct((b, s, ATTN_OUT), BF16),
        scratch_shapes=[pltpu.VMEM((len(DILATIONS), hcols // LANE, s, LANE), F32),
                        pltpu.VMEM((len(DILATIONS), hcols // LANE, s, LANE), F32),
                        pltpu.VMEM((2, (hcols // HEAD_DIM) * SPAN, 2 * SPAN), F32)],
        compiler_params=_cparams(("arbitrary", "arbitrary"), VMEM_LIMIT),
        name="dilated_attention",
    )(*args)


def _lru_kernel(xc_ref, gy_ref, wg_ref, bx_ref, ba_ref, lam_ref, o_ref, a_ref, b_ref, h_ref, *, pitch):
    nb, ts, tc = xc_ref.shape
    nl = tc // LANE

    @pl.when(pl.program_id(1) == 0)
    def _():
        h_ref[...] = jnp.zeros_like(h_ref)

    xb = xc_ref[...].reshape(nb * ts, tc)
    xc = xb.astype(F32)
    gates = jnp.dot(xb, wg_ref[0], preferred_element_type=F32)
    sigmoid = lambda v: 0.5 * jnp.tanh(0.5 * v) + 0.5
    gate_i = sigmoid(gates[:, :tc] + bx_ref[...])
    gate_r = sigmoid(gates[:, tc:] + ba_ref[...])
    neg_lam = -lam_ref[...]
    softplus = jnp.maximum(neg_lam, 0.0) + jnp.log1p(jnp.exp(-jnp.abs(neg_lam)))
    log_a = (-LRU_C) * gate_r * softplus
    a = jnp.exp(log_a)
    one_m_a2 = jnp.tanh(-log_a) * (1.0 + a * a)
    mult = jnp.where(one_m_a2 > 0.0, one_m_a2 * lax.rsqrt(one_m_a2), 0.0)
    bv = mult * gate_i * xc
    for bi in range(nb):
        for j in range(nl):
            a_ref[j, pl.ds(bi * pitch, ts), :] = a[bi * ts:(bi + 1) * ts, j * LANE:(j + 1) * LANE]
            b_ref[j, pl.ds(bi * pitch, ts), :] = bv[bi * ts:(bi + 1) * ts, j * LANE:(j + 1) * LANE]

    def step(t, hs):
        out = []
        for j in range(nl):
            rows = pl.ds(t, nb, stride=pitch)
            h = a_ref[j, rows, :] * hs[j] + b_ref[j, rows, :]
            b_ref[j, rows, :] = h
            out.append(h)
        return tuple(out)

    hs = lax.fori_loop(0, ts, step, tuple(h_ref[j] for j in range(nl)), unroll=8)
    for j in range(nl):
        h_ref[j] = hs[j]
    for bi in range(nb):
        h = jnp.concatenate([b_ref[j, pl.ds(bi * pitch, ts), :] for j in range(nl)], axis=1)
        o_ref[bi] = (h * gy_ref[bi].astype(F32)).astype(BF16)


def _lru_gate_weights(wx, wa, tc):
    nb, bd, _ = wx.shape
    per = tc // bd
    eye = jnp.eye(per, dtype=wx.dtype)

    def bdiag(w):
        w = w.reshape(nb // per, per, bd, bd)
        return jnp.einsum('cpio,pq->cpiqo', w, eye).reshape(nb // per, tc, tc)

    return jnp.concatenate([bdiag(wx), bdiag(wa)], axis=-1).astype(BF16)


def _lru_branch(xc, gy, wx, bx, wa, ba, lam, *, tc=256, ts=256):
    b, s, c = xc.shape
    assert s % ts == 0 and c % tc == 0
    wg = _lru_gate_weights(wx, wa, tc)
    row = lambda v: v.reshape(1, c)
    tile = pl.BlockSpec((b, ts, tc), lambda ci, ti: (0, ti, ci))
    vec = pl.BlockSpec((1, tc), lambda ci, ti: (0, ci))
    pitch = ts + 8
    return pl.pallas_call(
        functools.partial(_lru_kernel, pitch=pitch),
        grid=(c // tc, s // ts),
        in_specs=[tile, tile,
                  pl.BlockSpec((1, tc, 2 * tc), lambda ci, ti: (ci, 0, 0)),
                  vec, vec, vec],
        out_specs=tile,
        out_shape=jax.ShapeDtypeStruct((b, s, c), BF16),
        scratch_shapes=[pltpu.VMEM((tc // LANE, b * pitch, LANE), F32),
                        pltpu.VMEM((tc // LANE, b * pitch, LANE), F32),
                        pltpu.VMEM((tc // LANE, b, LANE), F32)],
        compiler_params=_cparams(("parallel", "arbitrary"), VMEM_LIMIT),
        name="rg_lru",
    )(xc, gy, wg, row(bx), row(ba), row(lam))


def _prep_w_in(w_in):
    a = ATTN_WIDTH
    gw = N_SLOTS * HEAD_DIM
    q = w_in[:, :a] * (HEAD_DIM ** -0.5 * LOG2_E)
    k = w_in[:, a:2 * a]
    v = w_in[:, 2 * a:3 * a]
    parts = []
    for g in range(len(DILATIONS)):
        sl = slice(g * gw, (g + 1) * gw)
        parts += [q[:, sl], k[:, sl], v[:, sl]]
    parts.append(w_in[:, 3 * a:3 * a + 2 * LRU_WIDTH])
    return jnp.concatenate(parts, axis=1).astype(BF16), w_in[:, 3 * a + 2 * LRU_WIDTH:].astype(BF16)


ROW_SUBLANES = D_MODEL // 2 // LANE


def _store_tile_rows(ref, v, row0=0):
    n, half = v.shape[0], v.shape[1] // 2
    lo = pltpu.bitcast(v[:, :half].astype(BF16).astype(F32), jnp.uint32)
    hi = pltpu.bitcast(v[:, half:].astype(BF16).astype(F32), jnp.uint32)
    words = (hi & jnp.uint32(0xFFFF0000)) | (lo >> 16)
    for j in range(ROW_SUBLANES):
        ref[pl.ds(row0 * ROW_SUBLANES + j, n, stride=ROW_SUBLANES), :] = words[:, j * LANE:(j + 1) * LANE]


def _load_tile_rows(ref, n=None):
    n = ref.shape[0] // ROW_SUBLANES if n is None else n
    words = [ref[pl.ds(j, n, stride=ROW_SUBLANES), :] for j in range(ROW_SUBLANES)]
    lo = [pltpu.bitcast(w << 16, F32) for w in words]
    hi = [pltpu.bitcast(w & jnp.uint32(0xFFFF0000), F32) for w in words]
    return jnp.concatenate(lo + hi, axis=-1)


SC_CORES, SC_SUBCORES = 2, 16
SC_CHUNK = 128


def _sc_gather_rows(table, idx):
    n = idx.shape[0]
    per_worker = n // (SC_CORES * SC_SUBCORES)
    n_chunks = per_worker // SC_CHUNK
    assert n_chunks * SC_CHUNK * SC_CORES * SC_SUBCORES == n
    mesh = plsc.VectorSubcoreMesh(core_axis_name="c", subcore_axis_name="s")

    def body(table_hbm, idx_hbm, out_hbm, idx_v, rows_v, sem):
        base = (lax.axis_index("s") * SC_CORES + lax.axis_index("c")) * per_worker

        @pl.loop(0, n_chunks)
        def _(i):
            off = pl.multiple_of(base + i * SC_CHUNK, SC_CHUNK)
            pltpu.sync_copy(idx_hbm.at[pl.ds(off, SC_CHUNK)], idx_v)
            pltpu.async_copy(table_hbm.at[idx_v], rows_v, sem).wait()
            pltpu.sync_copy(rows_v, out_hbm.at[pl.ds(off, SC_CHUNK)])

    return pl.kernel(
        body, mesh=mesh,
        out_type=jax.ShapeDtypeStruct((n,) + table.shape[1:], table.dtype),
        scratch_types=[pltpu.VMEM((SC_CHUNK,), jnp.int32),
                       pltpu.VMEM((SC_CHUNK,) + table.shape[1:], table.dtype),
                       pltpu.SemaphoreType.DMA],
        name="sc_gather_rows",
    )(table, idx)


def _sc_scatter_rows(rows, idx, n_out):
    n_rows = rows.shape[0]
    n_choice = idx.shape[0] // n_rows
    per_worker = n_rows // (SC_CORES * SC_SUBCORES)
    n_chunks = per_worker // SC_CHUNK
    assert n_chunks * SC_CHUNK * SC_CORES * SC_SUBCORES == n_rows and n_choice * n_rows == idx.shape[0]
    mesh = plsc.VectorSubcoreMesh(core_axis_name="c", subcore_axis_name="s")

    def body(rows_hbm, idx_hbm, out_hbm, idx_v, rows_v):
        base = (lax.axis_index("s") * SC_CORES + lax.axis_index("c")) * per_worker

        @pl.loop(0, n_chunks)
        def _(i):
            off = pl.multiple_of(base + i * SC_CHUNK, SC_CHUNK)
            pltpu.sync_copy(rows_hbm.at[pl.ds(off, SC_CHUNK)], rows_v)
            for k in range(n_choice):
                pltpu.sync_copy(idx_hbm.at[pl.ds(k * n_rows + off, SC_CHUNK)], idx_v)
                pltpu.sync_copy(rows_v, out_hbm.at[idx_v])

    return pl.kernel(
        body, mesh=mesh,
        out_type=jax.ShapeDtypeStruct((n_out,) + rows.shape[1:], rows.dtype),
        scratch_types=[pltpu.VMEM((SC_CHUNK,), jnp.int32),
                       pltpu.VMEM((SC_CHUNK,) + rows.shape[1:], rows.dtype)],
        name="sc_scatter_rows",
    )(rows, idx)


ROUTE_ROWS = 8
EXPERT_ROW0 = N_GROUPS
ROUTER_ROWS = 48


def _mix_kernel(attn_ref, lru_ref, x_ref, mod_ref, g1_ref, wg_ref, wa_ref, wl_ref, wo_ref, g2_ref,
                wrt_ref, brt_ref, x1_ref, h2_ref, route_ref, cnt_ref, cnt_acc):
    d = x_ref.shape[-1]
    tm = x_ref.shape[1]

    @pl.when((pl.program_id(0) == 0) & (pl.program_id(1) == 0))
    def _():
        cnt_acc[...] = jnp.zeros_like(cnt_acc)

    m = mod_ref[0]
    gate1, shift2, scale2 = m[:, 2 * d:3 * d], m[:, 3 * d:4 * d], m[:, 4 * d:5 * d]
    x = x_ref[0]
    h1 = _rms_mod(x, g1_ref[...], m[:, d:2 * d], m[:, 0:d]).astype(BF16)
    gates = jax.nn.sigmoid(jnp.dot(h1, wg_ref[...], preferred_element_type=F32))
    ya = jnp.dot(attn_ref[0], wa_ref[...], preferred_element_type=F32)
    yl = jnp.dot(lru_ref[0], wl_ref[...], preferred_element_type=F32)
    mixed = gates[:, :d] * ya + gates[:, d:] * yl
    y = jnp.dot(mixed.astype(BF16), wo_ref[...], preferred_element_type=F32)
    x1 = x + (1.0 + gate1) * y
    x1_ref[0] = x1.astype(BF16)
    h2 = _rms_mod(x1, g2_ref[...], scale2, shift2)
    _store_tile_rows(h2_ref, h2)
    logits = lax.dot_general(wrt_ref[...], h2.astype(BF16), (((1,), (1,)), ((), ())),
                             preferred_element_type=F32) + brt_ref[...]

    row = lax.broadcasted_iota(jnp.int32, logits.shape, 0)
    neg = -jnp.inf

    def top(vals):
        mx = jnp.max(vals, axis=0, keepdims=True)
        idx = jnp.min(jnp.where(vals == mx, row, ROUTER_ROWS), axis=0, keepdims=True)
        return mx, idx

    is_grp = row < N_GROUPS
    gmax, gidx = top(jnp.where(is_grp, logits, neg))
    grp_gate = 1.0 / jnp.sum(jnp.where(is_grp, jnp.exp(logits - gmax), 0.0), axis=0, keepdims=True)
    lo = EXPERT_ROW0 + EXPERTS_PER_GROUP * gidx
    el = jnp.where((row >= lo) & (row < lo + EXPERTS_PER_GROUP), logits, neg)
    v1, i1 = top(el)
    v2, i2 = top(jnp.where(row == i1, neg, el))
    e21 = jnp.exp(v2 - v1)
    wt1 = grp_gate / (1.0 + e21)
    wt2 = wt1 * e21

    oh1 = jnp.where(row == i1, 1.0, 0.0)
    oh2 = jnp.where(row == i2, 1.0, 0.0)
    ohs = oh1 + oh2
    rr = lax.broadcasted_iota(jnp.int32, (tm, tm), 0)
    cc = lax.broadcasted_iota(jnp.int32, (tm, tm), 1)
    earlier = jnp.where(rr < cc, 1.0, 0.0).astype(BF16)
    before = jnp.dot(ohs.astype(BF16), earlier, preferred_element_type=F32) + cnt_acc[...]
    rank1 = jnp.sum(oh1 * before, axis=0, keepdims=True)
    rank2 = jnp.sum(oh2 * before, axis=0, keepdims=True)
    cnt_acc[...] = cnt_acc[...] + jnp.sum(ohs, axis=1, keepdims=True)
    cnt_ref[...] = cnt_acc[...]

    vals = [(i1 - EXPERT_ROW0).astype(F32), (i2 - EXPERT_ROW0).astype(F32), rank1, rank2, wt1, wt2]
    out_row = lax.broadcasted_iota(jnp.int32, (ROUTE_ROWS, tm), 0)
    slab = jnp.zeros((ROUTE_ROWS, tm), F32)
    for j, v in enumerate(vals):
        slab = jnp.where(out_row == j, v, slab)
    route_ref[...] = slab


def _mix_route(attn, lru, x, mod3, g1, wg, wa, wl, wo, g2, wrt, brt, b0, nb, *, tm=512):
    _, s, d = x.shape
    spt = s // tm
    tok_in = lambda w: pl.BlockSpec((1, tm, w), lambda bi, i: (b0 + bi, i, 0))
    return pl.pallas_call(
        _mix_kernel,
        grid=(nb, spt),
        in_specs=[tok_in(attn.shape[-1]), tok_in(d), tok_in(d),
                  pl.BlockSpec((1, 1, mod3.shape[-1]), lambda bi, i: (b0 + bi, 0, 0)),
                  pl.BlockSpec((1, d), lambda bi, i: (0, 0)),
                  _resident(wg.shape), _resident(wa.shape), _resident(wl.shape), _resident(wo.shape),
                  pl.BlockSpec((1, d), lambda bi, i: (0, 0)),
                  _resident(wrt.shape),
                  pl.BlockSpec((ROUTER_ROWS, 1), lambda bi, i: (0, 0))],
        out_specs=[pl.BlockSpec((1, tm, d), lambda bi, i: (bi, i, 0)),
                   pl.BlockSpec((tm * ROW_SUBLANES, LANE), lambda bi, i: (bi * spt + i, 0)),
                   pl.BlockSpec((ROUTE_ROWS, tm), lambda bi, i: (0, bi * spt + i)),
                   pl.BlockSpec((ROUTER_ROWS, 1), lambda bi, i: (0, 0))],
        out_shape=[jax.ShapeDtypeStruct((nb, s, d), BF16),
                   jax.ShapeDtypeStruct((nb * s * ROW_SUBLANES, LANE), jnp.uint32),
                   jax.ShapeDtypeStruct((ROUTE_ROWS, nb * s), F32),
                   jax.ShapeDtypeStruct((ROUTER_ROWS, 1), F32)],
        scratch_shapes=[pltpu.VMEM((ROUTER_ROWS, 1), F32)],
        compiler_params=_cparams(("arbitrary", "arbitrary"), VMEM_LIMIT),
        name="mix_route",
    )(attn, lru, x, mod3, g1, wg, wa, wl, wo, g2, wrt, brt)


TOP_K = 2
EXPERT_BLOCK = 512
MOE_BATCH_RANGES = 2


def _expert_kernel(be_ref, bv_ref, first_ref, slot_ref, next_ref, nu_ref, x_ref, w1_hbm, w3_hbm, w2_hbm,
                   y_ref, wf1, wf3, wf2, wb1, wb3, wb2, sem):
    j = pl.program_id(0)
    half = EXPERT_BLOCK // 2

    def fetch(e, s):
        return [pltpu.make_async_copy(w1_hbm.at[e], wf1.at[s], sem.at[s, 0]),
                pltpu.make_async_copy(w3_hbm.at[e], wf3.at[s], sem.at[s, 1]),
                pltpu.make_async_copy(w2_hbm.at[e], wf2.at[s], sem.at[s, 2])]

    def ffn(n):
        xb = _load_tile_rows(x_ref, n).astype(BF16)
        a = jnp.dot(xb, wb1[...], preferred_element_type=F32)
        g = jnp.dot(xb, wb3[...], preferred_element_type=F32)
        hm = (a * jax.nn.sigmoid(a) * g).astype(BF16)
        _store_tile_rows(y_ref, jnp.dot(hm, wb2[...], preferred_element_type=F32))

    @pl.when(j < nu_ref[0])
    def _():
        @pl.when(first_ref[j] == 1)
        def _():
            e, s = be_ref[j], slot_ref[j]

            @pl.when(j == 0)
            def _():
                for c in fetch(e, s):
                    c.start()

            @pl.when(next_ref[j] >= 0)
            def _():
                for c in fetch(next_ref[j], 1 - s):
                    c.start()

            for c in fetch(e, s):
                c.wait()
            wb1[...] = wf1[s].astype(BF16)
            wb3[...] = wf3[s].astype(BF16)
            wb2[...] = wf2[s].astype(BF16)

        @pl.when(bv_ref[j] > half)
        def _():
            ffn(EXPERT_BLOCK)

        @pl.when(bv_ref[j] <= half)
        def _():
            ffn(half)
            y_ref[half * ROW_SUBLANES:, :] = jnp.zeros((half * ROW_SUBLANES, LANE), y_ref.dtype)

    @pl.when(j >= nu_ref[0])
    def _():
        y_ref[...] = jnp.zeros_like(y_ref)


def _experts(xp, plan, w1, w3, w2):
    ne, d, de = w1.shape
    nb = xp.shape[0] // (EXPERT_BLOCK * ROW_SUBLANES)
    rows = (EXPERT_BLOCK * ROW_SUBLANES, LANE)
    grid_spec = pltpu.PrefetchScalarGridSpec(
        num_scalar_prefetch=len(plan),
        grid=(nb,),
        in_specs=[pl.BlockSpec(rows, lambda j, *p: (jnp.minimum(j, p[-1][0] - 1), 0)),
                  pl.BlockSpec(memory_space=pl.ANY),
                  pl.BlockSpec(memory_space=pl.ANY),
                  pl.BlockSpec(memory_space=pl.ANY)],
        out_specs=pl.BlockSpec(rows, lambda j, *p: (j, 0)),
        scratch_shapes=[pltpu.VMEM((2, d, de), F32), pltpu.VMEM((2, d, de), F32), pltpu.VMEM((2, de, d), F32),
                        pltpu.VMEM((d, de), BF16), pltpu.VMEM((d, de), BF16), pltpu.VMEM((de, d), BF16),
                        pltpu.SemaphoreType.DMA((2, 3))])
    return pl.pallas_call(
        _expert_kernel,
        grid_spec=grid_spec,
        out_shape=jax.ShapeDtypeStruct(xp.shape, xp.dtype),
        compiler_params=_cparams(("arbitrary",), VMEM_LIMIT),
        name="experts",
    )(*plan, xp, w1, w3, w2)


def _combine_kernel(y0_ref, y1_ref, route_ref, x1_ref, mod_ref, gf_ref, *rest):
    o_ref = rest[-1]
    tm, d = x1_ref.shape[1], x1_ref.shape[2]
    route = jnp.concatenate([route_ref[...], jnp.zeros((LANE - ROUTE_ROWS, tm), F32)], axis=0).T
    moe = _load_tile_rows(y0_ref) * route[:, 4:5] + _load_tile_rows(y1_ref) * route[:, 5:6]
    gate2 = mod_ref[0][:, 5 * d:6 * d]
    xo = x1_ref[0].astype(F32) + (1.0 + gate2) * moe
    ms = jnp.mean(xo * xo, axis=-1, keepdims=True)
    o_ref[0] = xo * lax.rsqrt(ms + EPS) * gf_ref[...]


def _combine(yg, route, x1, mod3, gf, b0, out_prev, *, tm=1024):
    nb, s, d = x1.shape
    b_all = mod3.shape[0]
    spt = s // tm
    nt = nb * spt
    rows = (tm * ROW_SUBLANES, LANE)
    in_specs = [pl.BlockSpec(rows, lambda bi, i: (bi * spt + i, 0)),
                pl.BlockSpec(rows, lambda bi, i: (nt + bi * spt + i, 0)),
                pl.BlockSpec((ROUTE_ROWS, tm), lambda bi, i: (0, bi * spt + i)),
                pl.BlockSpec((1, tm, d), lambda bi, i: (bi, i, 0)),
                pl.BlockSpec((1, 1, mod3.shape[-1]), lambda bi, i: (b0 + bi, 0, 0)),
                pl.BlockSpec((1, d), lambda bi, i: (0, 0))]
    args = [yg, yg, route, x1, mod3, gf]
    aliases = {}
    if out_prev is not None:
        in_specs.append(pl.BlockSpec(memory_space=pl.ANY))
        aliases = {len(args): 0}
        args.append(out_prev)
    return pl.pallas_call(
        _combine_kernel,
        grid=(nb, spt),
        in_specs=in_specs,
        out_specs=pl.BlockSpec((1, tm, d), lambda bi, i: (b0 + bi, i, 0)),
        out_shape=jax.ShapeDtypeStruct((b_all, s, d), F32),
        input_output_aliases=aliases,
        compiler_params=_cparams(("parallel", "parallel"), VMEM_LIMIT),
        name="combine",
    )(*args)


def _slot_plan(route, counts, n_tok):
    sizes = counts[EXPERT_ROW0:EXPERT_ROW0 + N_EXPERTS, 0].astype(jnp.int32)
    padded = (sizes + EXPERT_BLOCK - 1) // EXPERT_BLOCK * EXPERT_BLOCK
    pad_ends = jnp.cumsum(padded)
    pad_starts = pad_ends - padded
    eid = route[0:TOP_K].astype(jnp.int32)
    rank = route[TOP_K:2 * TOP_K].astype(jnp.int32)
    start = jnp.sum(jnp.where(eid[..., None] == jnp.arange(N_EXPERTS), pad_starts, 0), axis=-1)
    dest = (start + rank).reshape(TOP_K * n_tok)
    n_blocks = (n_tok * TOP_K + N_EXPERTS * (EXPERT_BLOCK - 1) + EXPERT_BLOCK - 1) // EXPERT_BLOCK
    gran = SC_CORES * SC_SUBCORES * SC_CHUNK // math.gcd(SC_CORES * SC_SUBCORES * SC_CHUNK, EXPERT_BLOCK)
    n_blocks = (n_blocks + gran - 1) // gran * gran
    n_used = pad_ends[-1] // EXPERT_BLOCK
    blk = jnp.minimum(jnp.arange(n_blocks), n_used - 1)
    blk_e = jnp.minimum(jnp.sum(pad_ends[None, :] <= (blk * EXPERT_BLOCK)[:, None], axis=1), N_EXPERTS - 1)
    blk_valid = jnp.clip(sizes[blk_e] - (blk * EXPERT_BLOCK - pad_starts[blk_e]), 0, EXPERT_BLOCK)
    idx = jnp.arange(n_blocks)
    first = (idx < n_used) & ((idx == 0) | (blk_e != jnp.roll(blk_e, 1)))
    slot = (jnp.cumsum(first) - 1) % 2
    later_first = lax.cummin(jnp.where(first, idx, n_blocks), reverse=True)
    next_first = jnp.concatenate([later_first[1:], jnp.full((1,), n_blocks)])
    next_e = jnp.where(next_first < n_blocks, blk_e[jnp.minimum(next_first, n_blocks - 1)], -1)
    i32 = lambda a: a.astype(jnp.int32)
    plan = (i32(blk_e), i32(blk_valid), i32(first), i32(slot), i32(next_e), i32(n_used.reshape(1)))
    return dest, n_blocks * EXPERT_BLOCK, plan


def kernel(x, c, w_mod, b_mod, norm1_g, w_in, conv_w, conv_b, lru_wx, lru_bx, lru_wa, lru_ba, lru_lambda, w_attn_o, w_lru_o, w_out, norm2_g, w_grp, b_grp, w_exp, b_exp, w1, w3, w2, norm_f_g):
    b, s, d = x.shape
    assert d == D_MODEL and s == SPAN * DILATIONS[-1] and w_mod.shape[0] == 1
    mod3 = _modulation(c, w_mod[0], b_mod[0]).reshape(b, 1, 6 * d)
    w_proj, w_gate = _prep_w_in(w_in[0])
    g1 = norm1_g[0].reshape(1, d)
    qkv0, qkv1, qkv2, xc, gy = _projection(x, mod3, g1, w_proj, conv_w[0], conv_b[0])
    attn = _attention((qkv0, qkv1, qkv2), b, s)
    lru = _lru_branch(xc, gy, lru_wx[0], lru_bx[0], lru_wa[0], lru_ba[0], lru_lambda[0])

    n_pad = ROUTER_ROWS - N_GROUPS - N_EXPERTS
    wr = jnp.pad(jnp.concatenate([w_grp[0], w_exp[0]], axis=1).T, ((0, n_pad), (0, 0))).astype(BF16)
    br = jnp.pad(jnp.concatenate([b_grp[0], b_exp[0]]), (0, n_pad)).reshape(ROUTER_ROWS, 1)
    wa, wl, wo = w_attn_o[0].astype(BF16), w_lru_o[0].astype(BF16), w_out[0].astype(BF16)
    as_rows = lambda a: a.reshape(-1, ROW_SUBLANES, LANE)
    as_tiles = lambda a: a.reshape(-1, LANE)

    out = None
    nb = b // MOE_BATCH_RANGES
    for b0 in range(0, b, nb):
        x1, h2, route, counts = _mix_route(attn, lru, x, mod3, g1, w_gate, wa, wl, wo,
                                           norm2_g[0].reshape(1, d), wr, br, b0, nb)
        dest, n_slots, plan = _slot_plan(route, counts, nb * s)
        xp = as_tiles(_sc_scatter_rows(as_rows(h2), dest, n_slots))
        yp = _experts(xp, plan, w1[0], w3[0], w2[0])
        yg = as_tiles(_sc_gather_rows(as_rows(yp), dest))
        out = _combine(yg, route, x1, mod3, norm_f_g.reshape(1, d), b0, out)
    return out
```

```python
import functools
import math

import jax
import jax.numpy as jnp
from jax import lax
from jax.experimental import pallas as pl
from jax.experimental.pallas import tpu as pltpu
from jax.experimental.pallas import tpu_sc as plsc

F32 = jnp.float32
BF16 = jnp.bfloat16

D_MODEL = 1024
HEAD_DIM = 64
N_SLOTS = 8
SPAN = 128
DILATIONS = (1, 4, 16)
GROUP_COLS = 3 * N_SLOTS * HEAD_DIM
ATTN_WIDTH = len(DILATIONS) * N_SLOTS * HEAD_DIM
ATTN_OUT = N_SLOTS * HEAD_DIM
LRU_WIDTH = D_MODEL
CONV_WIDTH = 4
CONV_TAIL = 8
LRU_C = 8.0
N_GROUPS = 4
EXPERTS_PER_GROUP = 8
N_EXPERTS = N_GROUPS * EXPERTS_PER_GROUP
EPS = 1e-6
LOG2_E = 1.4426950408889634
LANE = 128
VMEM_LIMIT = 56 * 1024 * 1024


def _cparams(sem, vmem=None):
    return pltpu.CompilerParams(dimension_semantics=sem, vmem_limit_bytes=vmem)


def _resident(shape):
    nd = len(shape)
    return pl.BlockSpec(shape, lambda *_: (0,) * nd, pipeline_mode=pl.Buffered(1))


def _mod_kernel(c_ref, w_ref, b_ref, o_ref):
    c = c_ref[...]
    ca = c * jax.nn.sigmoid(c)
    o_ref[...] = jnp.dot(ca.astype(BF16), w_ref[...].astype(BF16),
                         preferred_element_type=F32) + b_ref[...]


def _modulation(c, w_mod, b_mod):
    b, d = c.shape
    n = w_mod.shape[1]
    tn = n // 4
    return pl.pallas_call(
        _mod_kernel,
        grid=(n // tn,),
        in_specs=[pl.BlockSpec((b, d), lambda j: (0, 0)),
                  pl.BlockSpec((d, tn), lambda j: (0, j)),
                  pl.BlockSpec((1, tn), lambda j: (0, j))],
        out_specs=pl.BlockSpec((b, tn), lambda j: (0, j)),
        out_shape=jax.ShapeDtypeStruct((b, n), F32),
        compiler_params=_cparams(("arbitrary",)),
        name="modulation",
    )(c, w_mod, b_mod.reshape(1, n))


def _rms_mod(x, g, scale, shift):
    ms = jnp.mean(x * x, axis=-1, keepdims=True)
    return x * lax.rsqrt(ms + EPS) * g * (1.0 + scale) + shift


def _gelu_tanh(y):
    return y * (0.5 * (1.0 + jnp.tanh(0.7978845608028654 * (y + 0.044715 * (y * y * y)))))


def _proj_kernel(x_ref, mod_ref, g_ref, w_ref, cw_ref, cb_ref, qkv0_ref, qkv1_ref, qkv2_ref,
                 xc_ref, gy_ref, hs_ref, xe_ref, *, tm):
    @pl.when(pl.program_id(1) == 0)
    def _():
        xe_ref[0:CONV_TAIL, :] = jnp.zeros((CONV_TAIL, LRU_WIDTH), F32)

    @pl.when(pl.program_id(1) > 0)
    def _():
        xe_ref[0:CONV_TAIL, :] = xe_ref[tm:tm + CONV_TAIL, :]

    d_model = x_ref.shape[-1]
    m = mod_ref[0]
    h = _rms_mod(x_ref[0], g_ref[...], m[:, d_model:2 * d_model], m[:, 0:d_model])

    def mm(hv, lo, hi):
        return jnp.dot(hv, w_ref[:, lo:hi], preferred_element_type=F32)

    hb = h.astype(BF16)
    c0 = len(DILATIONS) * GROUP_COLS
    qkv0_ref[0] = mm(hb, 0, GROUP_COLS).astype(BF16)
    xr = mm(hb, c0, c0 + LRU_WIDTH)
    xe_ref[CONV_TAIL:, :] = xr
    cw = cw_ref[...]
    xc = xr * cw[CONV_WIDTH - 1:CONV_WIDTH] + cb_ref[...]
    for k in range(1, CONV_WIDTH):
        xc = xc + xe_ref[CONV_TAIL - k:CONV_TAIL - k + tm, :] * cw[CONV_WIDTH - 1 - k:CONV_WIDTH - k]
    xc_ref[0] = xc.astype(BF16)
    gy_ref[0] = _gelu_tanh(mm(hb, c0 + LRU_WIDTH, c0 + 2 * LRU_WIDTH)).astype(BF16)

    n_slab = d_model // LANE
    for j in range(n_slab):
        hs_ref[j] = h[:, j * LANE:(j + 1) * LANE]
    for g, out_ref in ((1, qkv1_ref), (2, qkv2_ref)):
        d = DILATIONS[g]
        rows = tm // d
        hp = jnp.concatenate(
            [jnp.concatenate([hs_ref[j, pl.ds(p, rows, stride=d), :] for j in range(n_slab)], axis=1)
             for p in range(d)], axis=0).astype(BF16)
        res = mm(hp, g * GROUP_COLS, (g + 1) * GROUP_COLS).astype(BF16)
        for p in range(d):
            out_ref[p] = res[p * rows:(p + 1) * rows]


def _projection(x, mod3, g1, w_r, conv_w, conv_b, *, tm=512):
    b, s, d = x.shape
    n = w_r.shape[1]
    assert s % tm == 0 and tm % (16 * DILATIONS[-1]) == 0 and CONV_TAIL >= CONV_WIDTH - 1
    out_shape = [jax.ShapeDtypeStruct((b * dd, s // dd, GROUP_COLS), BF16) for dd in DILATIONS]
    out_shape += [jax.ShapeDtypeStruct((b, s, LRU_WIDTH), BF16),
                  jax.ShapeDtypeStruct((b, s, LRU_WIDTH), BF16)]
    out_specs = [pl.BlockSpec((dd, tm // dd, GROUP_COLS), lambda bi, i: (bi, i, 0)) for dd in DILATIONS]
    out_specs += [pl.BlockSpec((1, tm, LRU_WIDTH), lambda bi, i: (bi, i, 0)),
                  pl.BlockSpec((1, tm, LRU_WIDTH), lambda bi, i: (bi, i, 0))]
    return pl.pallas_call(
        functools.partial(_proj_kernel, tm=tm),
        grid=(b, s // tm),
        in_specs=[pl.BlockSpec((1, tm, d), lambda bi, i: (bi, i, 0)),
                  pl.BlockSpec((1, 1, mod3.shape[-1]), lambda bi, i: (bi, 0, 0)),
                  pl.BlockSpec((1, d), lambda bi, i: (0, 0)),
                  _resident((d, n)),
                  pl.BlockSpec((CONV_WIDTH, LRU_WIDTH), lambda bi, i: (0, 0)),
                  pl.BlockSpec((1, LRU_WIDTH), lambda bi, i: (0, 0))],
        out_specs=out_specs,
        out_shape=out_shape,
        scratch_shapes=[pltpu.VMEM((d // LANE, tm, LANE), F32), pltpu.VMEM((CONV_TAIL + tm, LRU_WIDTH), F32)],
        compiler_params=_cparams(("parallel", "arbitrary"), VMEM_LIMIT),
        name="projection",
    )(x, mod3, g1, w_r, conv_w, conv_b.reshape(1, LRU_WIDTH))


def _attn_kernel(q0, k0, v0, q1, k1, v1, q2, k2, v2, o_ref, acc_ref, lse_ref, bias_ref, *, seq):
    hcols = o_ref.shape[-1]
    n_head = hcols // HEAD_DIM
    head_of_lane = lax.broadcasted_iota(jnp.int32, (SPAN, hcols), 1) // HEAD_DIM
    head_mask_b = [jnp.where(head_of_lane == h, 1.0, 0.0).astype(BF16) for h in range(n_head)]
    ones_blk = jnp.ones((2 * SPAN, hcols), BF16)

    def by_head(parts):
        out = parts[n_head - 1]
        for h in range(n_head - 2, -1, -1):
            out = jnp.where(head_of_lane == h, parts[h], out)
        return out

    @pl.when((pl.program_id(0) == 0) & (pl.program_id(1) == 0))
    def _():
        qi = lax.broadcasted_iota(jnp.int32, (n_head * SPAN, 2 * SPAN), 0) % SPAN
        ki = lax.broadcasted_iota(jnp.int32, (n_head * SPAN, 2 * SPAN), 1)
        band = (ki >= qi) & (ki <= qi + SPAN)
        bias_ref[0] = jnp.where(band, 0.0, -jnp.inf)
        bias_ref[1] = jnp.where(band & (ki >= SPAN), 0.0, -jnp.inf)

    for g, (q_ref, k_ref, v_ref) in enumerate(((q0, k0, v0), (q1, k1, v1), (q2, k2, v2))):
        d = DILATIONS[g]
        n_blk = seq // d // SPAN

        def tile(n, carry, q_ref=q_ref, k_ref=k_ref, v_ref=v_ref, d=d, n_blk=n_blk, g=g):
            p = n // n_blk
            blk = n % n_blk
            r0 = pl.multiple_of(blk * SPAN, SPAN)
            rp = pl.multiple_of(jnp.maximum(blk - 1, 0) * SPAN, SPAN)
            q = q_ref[p, pl.ds(r0, SPAN), :]
            kk = jnp.concatenate([k_ref[p, pl.ds(rp, SPAN), :], k_ref[p, pl.ds(r0, SPAN), :]], axis=0)
            vv = jnp.concatenate([v_ref[p, pl.ds(rp, SPAN), :], v_ref[p, pl.ds(r0, SPAN), :]], axis=0)
            qs = jnp.concatenate([q * head_mask_b[h] for h in range(n_head)], axis=0)
            sc = lax.dot_general(qs, kk, (((1,), (1,)), ((), ())), preferred_element_type=F32)
            sc = sc + bias_ref[jnp.where(blk > 0, 0, 1)]
            mx = jnp.max(sc, axis=-1, keepdims=True)
            e = jnp.exp2(sc - mx)
            pv = jnp.dot(e.astype(BF16), jnp.concatenate([vv, ones_blk], axis=1), preferred_element_type=F32)
            rows_of = lambda a: [a[h * SPAN:(h + 1) * SPAN] for h in range(n_head)]
            den_l = by_head(rows_of(pv[:, hcols:]))
            o = by_head(rows_of(pv[:, :hcols])) / den_l
            l = by_head(rows_of(mx)) + jnp.log(den_l) * LOG2_E
            start = p + d * r0
            for j in range(hcols // LANE):
                rows = pl.ds(start, SPAN, stride=d) if d > 1 else pl.ds(start, SPAN)
                acc_ref[g, j, rows, :] = o[:, j * LANE:(j + 1) * LANE]
                lse_ref[g, j, rows, :] = l[:, j * LANE:(j + 1) * LANE]
            return carry

        lax.fori_loop(0, seq // SPAN, tile, 0, unroll=16)

    chunk = 256

    def combine(c, carry):
        r = pl.multiple_of(c * chunk, chunk)
        for j in range(hcols // LANE):
            ls = [lse_ref[g, j, pl.ds(r, chunk), :] for g in range(len(DILATIONS))]
            mx = jnp.maximum(jnp.maximum(ls[0], ls[1]), ls[2])
            ws = [jnp.exp2(v - mx) for v in ls]
            num = ws[0] * acc_ref[0, j, pl.ds(r, chunk), :]
            for g in range(1, len(DILATIONS)):
                num = num + ws[g] * acc_ref[g, j, pl.ds(r, chunk), :]
            o_ref[0, pl.ds(r, chunk), j * LANE:(j + 1) * LANE] = (num / (ws[0] + ws[1] + ws[2])).astype(BF16)
        return carry

    lax.fori_loop(0, seq // chunk, combine, 0)


def _attention(qkvs, b, s):
    hcols = 2 * HEAD_DIM
    n_hg = ATTN_OUT // hcols
    ncb = ATTN_OUT // hcols
    in_specs, args = [], []
    for g, d in enumerate(DILATIONS):
        for part in range(3):
            in_specs.append(pl.BlockSpec((d, s // d, hcols),
                                         lambda bi, hg, part=part: (bi, 0, part * ncb + hg)))
            args.append(qkvs[g])
    return pl.pallas_call(
        functools.partial(_attn_kernel, seq=s),
        grid=(b, n_hg),
        in_specs=in_specs,
        out_specs=pl.BlockSpec((1, s, hcols), lambda bi, hg: (bi, 0, hg)),
        out_shape=jax.ShapeDtypeStruct((b, s, ATTN_OUT), BF16),
        scratch_shapes=[pltpu.VMEM((len(DILATIONS), hcols // LANE, s, LANE), F32),
                        pltpu.VMEM((len(DILATIONS), hcols // LANE, s, LANE), F32),
                        pltpu.VMEM((2, (hcols // HEAD_DIM) * SPAN, 2 * SPAN), F32)],
        compiler_params=_cparams(("arbitrary", "arbitrary"), VMEM_LIMIT),
        name="dilated_attention",
    )(*args)


def _lru_kernel(xc_ref, gy_ref, wg_ref, bx_ref, ba_ref, lam_ref, o_ref, a_ref, b_ref, h_ref, *, pitch):
    nb, ts, tc = xc_ref.shape
    nl = tc // LANE

    @pl.when(pl.program_id(1) == 0)
    def _():
        h_ref[...] = jnp.zeros_like(h_ref)

    xb = xc_ref[...].reshape(nb * ts, tc)
    xc = xb.astype(F32)
    gates = jnp.dot(xb, wg_ref[0], preferred_element_type=F32)
    sigmoid = lambda v: 0.5 * jnp.tanh(0.5 * v) + 0.5
    gate_i = sigmoid(gates[:, :tc] + bx_ref[...])
    gate_r = sigmoid(gates[:, tc:] + ba_ref[...])
    neg_lam = -lam_ref[...]
    softplus = jnp.maximum(neg_lam, 0.0) + jnp.log1p(jnp.exp(-jnp.abs(neg_lam)))
    log_a = (-LRU_C) * gate_r * softplus
    a = jnp.exp(log_a)
    one_m_a2 = jnp.tanh(-log_a) * (1.0 + a * a)
    mult = jnp.where(one_m_a2 > 0.0, one_m_a2 * lax.rsqrt(one_m_a2), 0.0)
    bv = mult * gate_i * xc
    for bi in range(nb):
        for j in range(nl):
            a_ref[j, pl.ds(bi * pitch, ts), :] = a[bi * ts:(bi + 1) * ts, j * LANE:(j + 1) * LANE]
            b_ref[j, pl.ds(bi * pitch, ts), :] = bv[bi * ts:(bi + 1) * ts, j * LANE:(j + 1) * LANE]

    def step(t, hs):
        out = []
        for j in range(nl):
            rows = pl.ds(t, nb, stride=pitch)
            h = a_ref[j, rows, :] * hs[j] + b_ref[j, rows, :]
            b_ref[j, rows, :] = h
            out.append(h)
        return tuple(out)

    hs = lax.fori_loop(0, ts, step, tuple(h_ref[j] for j in range(nl)), unroll=8)
    for j in range(nl):
        h_ref[j] = hs[j]
    for bi in range(nb):
        h = jnp.concatenate([b_ref[j, pl.ds(bi * pitch, ts), :] for j in range(nl)], axis=1)
        o_ref[bi] = (h * gy_ref[bi].astype(F32)).astype(BF16)


def _lru_gate_weights(wx, wa, tc):
    nb, bd, _ = wx.shape
    per = tc // bd
    eye = jnp.eye(per, dtype=wx.dtype)

    def bdiag(w):
        w = w.reshape(nb // per, per, bd, bd)
        return jnp.einsum('cpio,pq->cpiqo', w, eye).reshape(nb // per, tc, tc)

    return jnp.concatenate([bdiag(wx), bdiag(wa)], axis=-1).astype(BF16)


def _lru_branch(xc, gy, wx, bx, wa, ba, lam, *, tc=256, ts=256):
    b, s, c = xc.shape
    assert s % ts == 0 and c % tc == 0
    wg = _lru_gate_weights(wx, wa, tc)
    row = lambda v: v.reshape(1, c)
    tile = pl.BlockSpec((b, ts, tc), lambda ci, ti: (0, ti, ci))
    vec = pl.BlockSpec((1, tc), lambda ci, ti: (0, ci))
    pitch = ts + 8
    return pl.pallas_call(
        functools.partial(_lru_kernel, pitch=pitch),
        grid=(c // tc, s // ts),
        in_specs=[tile, tile,
                  pl.BlockSpec((1, tc, 2 * tc), lambda ci, ti: (ci, 0, 0)),
                  vec, vec, vec],
        out_specs=tile,
        out_shape=jax.ShapeDtypeStruct((b, s, c), BF16),
        scratch_shapes=[pltpu.VMEM((tc // LANE, b * pitch, LANE), F32),
                        pltpu.VMEM((tc // LANE, b * pitch, LANE), F32),
                        pltpu.VMEM((tc // LANE, b, LANE), F32)],
        compiler_params=_cparams(("parallel", "arbitrary"), VMEM_LIMIT),
        name="rg_lru",
    )(xc, gy, wg, row(bx), row(ba), row(lam))


def _prep_w_in(w_in):
    a = ATTN_WIDTH
    gw = N_SLOTS * HEAD_DIM
    q = w_in[:, :a] * (HEAD_DIM ** -0.5 * LOG2_E)
    k = w_in[:, a:2 * a]
    v = w_in[:, 2 * a:3 * a]
    parts = []
    for g in range(len(DILATIONS)):
        sl = slice(g * gw, (g + 1) * gw)
        parts += [q[:, sl], k[:, sl], v[:, sl]]
    parts.append(w_in[:, 3 * a:3 * a + 2 * LRU_WIDTH])
    return jnp.concatenate(parts, axis=1).astype(BF16), w_in[:, 3 * a + 2 * LRU_WIDTH:].astype(BF16)


ROW_SUBLANES = D_MODEL // 2 // LANE


def _store_tile_rows(ref, v, row0=0):
    n, half = v.shape[0], v.shape[1] // 2
    lo = pltpu.bitcast(v[:, :half].astype(BF16).astype(F32), jnp.uint32)
    hi = pltpu.bitcast(v[:, half:].astype(BF16).astype(F32), jnp.uint32)
    words = (hi & jnp.uint32(0xFFFF0000)) | (lo >> 16)
    for j in range(ROW_SUBLANES):
        ref[pl.ds(row0 * ROW_SUBLANES + j, n, stride=ROW_SUBLANES), :] = words[:, j * LANE:(j + 1) * LANE]


def _load_tile_rows(ref, n=None):
    n = ref.shape[0] // ROW_SUBLANES if n is None else n
    words = [ref[pl.ds(j, n, stride=ROW_SUBLANES), :] for j in range(ROW_SUBLANES)]
    lo = [pltpu.bitcast(w << 16, F32) for w in words]
    hi = [pltpu.bitcast(w & jnp.uint32(0xFFFF0000), F32) for w in words]
    return jnp.concatenate(lo + hi, axis=-1)


SC_CORES, SC_SUBCORES = 2, 16
SC_CHUNK = 128


def _sc_gather_rows(table, idx):
    n = idx.shape[0]
    per_worker = n // (SC_CORES * SC_SUBCORES)
    n_chunks = per_worker // SC_CHUNK
    assert n_chunks * SC_CHUNK * SC_CORES * SC_SUBCORES == n
    mesh = plsc.VectorSubcoreMesh(core_axis_name="c", subcore_axis_name="s")

    def body(table_hbm, idx_hbm, out_hbm, idx_v, rows_v, sem):
        base = (lax.axis_index("s") * SC_CORES + lax.axis_index("c")) * per_worker

        @pl.loop(0, n_chunks)
        def _(i):
            off = pl.multiple_of(base + i * SC_CHUNK, SC_CHUNK)
            pltpu.sync_copy(idx_hbm.at[pl.ds(off, SC_CHUNK)], idx_v)
            pltpu.async_copy(table_hbm.at[idx_v], rows_v, sem).wait()
            pltpu.sync_copy(rows_v, out_hbm.at[pl.ds(off, SC_CHUNK)])

    return pl.kernel(
        body, mesh=mesh,
        out_type=jax.ShapeDtypeStruct((n,) + table.shape[1:], table.dtype),
        scratch_types=[pltpu.VMEM((SC_CHUNK,), jnp.int32),
                       pltpu.VMEM((SC_CHUNK,) + table.shape[1:], table.dtype),
                       pltpu.SemaphoreType.DMA],
        name="sc_gather_rows",
    )(table, idx)


def _sc_scatter_rows(rows, idx, n_out):
    n_rows = rows.shape[0]
    n_choice = idx.shape[0] // n_rows
    per_worker = n_rows // (SC_CORES * SC_SUBCORES)
    n_chunks = per_worker // SC_CHUNK
    assert n_chunks * SC_CHUNK * SC_CORES * SC_SUBCORES == n_rows and n_choice * n_rows == idx.shape[0]
    mesh = plsc.VectorSubcoreMesh(core_axis_name="c", subcore_axis_name="s")

    def body(rows_hbm, idx_hbm, out_hbm, idx_v, rows_v):
        base = (lax.axis_index("s") * SC_CORES + lax.axis_index("c")) * per_worker

        @pl.loop(0, n_chunks)
        def _(i):
            off = pl.multiple_of(base + i * SC_CHUNK, SC_CHUNK)
            pltpu.sync_copy(rows_hbm.at[pl.ds(off, SC_CHUNK)], rows_v)
            for k in range(n_choice):
                pltpu.sync_copy(idx_hbm.at[pl.ds(k * n_rows + off, SC_CHUNK)], idx_v)
                pltpu.sync_copy(rows_v, out_hbm.at[idx_v])

    return pl.kernel(
        body, mesh=mesh,
        out_type=jax.ShapeDtypeStruct((n_out,) + rows.shape[1:], rows.dtype),
        scratch_types=[pltpu.VMEM((SC_CHUNK,), jnp.int32),
                       pltpu.VMEM((SC_CHUNK,) + rows.shape[1:], rows.dtype)],
        name="sc_scatter_rows",
    )(rows, idx)


ROUTE_ROWS = 8
EXPERT_ROW0 = N_GROUPS
ROUTER_ROWS = 48


def _mix_kernel(attn_ref, lru_ref, x_ref, mod_ref, g1_ref, wg_ref, wa_ref, wl_ref, wo_ref, g2_ref,
                wrt_ref, brt_ref, x1_ref, h2_ref, route_ref, cnt_ref, cnt_acc):
    d = x_ref.shape[-1]
    tm = x_ref.shape[1]

    @pl.when((pl.program_id(0) == 0) & (pl.program_id(1) == 0))
    def _():
        cnt_acc[...] = jnp.zeros_like(cnt_acc)

    m = mod_ref[0]
    gate1, shift2, scale2 = m[:, 2 * d:3 * d], m[:, 3 * d:4 * d], m[:, 4 * d:5 * d]
    x = x_ref[0]
    h1 = _rms_mod(x, g1_ref[...], m[:, d:2 * d], m[:, 0:d]).astype(BF16)
    gates = 0.5 * jnp.tanh(0.5 * jnp.dot(h1, wg_ref[...], preferred_element_type=F32)) + 0.5
    ya = jnp.dot(attn_ref[0], wa_ref[...], preferred_element_type=F32)
    yl = jnp.dot(lru_ref[0], wl_ref[...], preferred_element_type=F32)
    mixed = gates[:, :d] * ya + gates[:, d:] * yl
    y = jnp.dot(mixed.astype(BF16), wo_ref[...], preferred_element_type=F32)
    x1 = x + (1.0 + gate1) * y
    x1_ref[0] = x1.astype(BF16)
    h2 = _rms_mod(x1, g2_ref[...], scale2, shift2)
    _store_tile_rows(h2_ref, h2)
    logits = lax.dot_general(wrt_ref[...], h2.astype(BF16), (((1,), (1,)), ((), ())),
                             preferred_element_type=F32) + brt_ref[...]

    row = lax.broadcasted_iota(jnp.int32, logits.shape, 0)
    neg = -jnp.inf

    def top(vals):
        mx = jnp.max(vals, axis=0, keepdims=True)
        idx = jnp.min(jnp.where(vals == mx, row, ROUTER_ROWS), axis=0, keepdims=True)
        return mx, idx

    is_grp = row < N_GROUPS
    gmax, gidx = top(jnp.where(is_grp, logits, neg))
    grp_gate = 1.0 / jnp.sum(jnp.where(is_grp, jnp.exp(logits - gmax), 0.0), axis=0, keepdims=True)
    lo = EXPERT_ROW0 + EXPERTS_PER_GROUP * gidx
    el = jnp.where((row >= lo) & (row < lo + EXPERTS_PER_GROUP), logits, neg)
    v1, i1 = top(el)
    v2, i2 = top(jnp.where(row == i1, neg, el))
    e21 = jnp.exp(v2 - v1)
    wt1 = grp_gate / (1.0 + e21)
    wt2 = wt1 * e21

    oh1 = jnp.where(row == i1, 1.0, 0.0)
    oh2 = jnp.where(row == i2, 1.0, 0.0)
    ohs = oh1 + oh2
    rr = lax.broadcasted_iota(jnp.int32, (tm, tm), 0)
    cc = lax.broadcasted_iota(jnp.int32, (tm, tm), 1)
    earlier = jnp.where(rr < cc, 1.0, 0.0).astype(BF16)
    before = jnp.dot(ohs.astype(BF16), earlier, preferred_element_type=F32) + cnt_acc[...]
    rank1 = jnp.sum(oh1 * before, axis=0, keepdims=True)
    rank2 = jnp.sum(oh2 * before, axis=0, keepdims=True)
    cnt_acc[...] = cnt_acc[...] + jnp.sum(ohs, axis=1, keepdims=True)
    cnt_ref[...] = cnt_acc[...]

    vals = [(i1 - EXPERT_ROW0).astype(F32), (i2 - EXPERT_ROW0).astype(F32), rank1, rank2, wt1, wt2]
    out_row = lax.broadcasted_iota(jnp.int32, (ROUTE_ROWS, tm), 0)
    slab = jnp.zeros((ROUTE_ROWS, tm), F32)
    for j, v in enumerate(vals):
        slab = jnp.where(out_row == j, v, slab)
    route_ref[...] = slab


def _mix_route(attn, lru, x, mod3, g1, wg, wa, wl, wo, g2, wrt, brt, b0, nb, *, tm=512):
    _, s, d = x.shape
    spt = s // tm
    tok_in = lambda w: pl.BlockSpec((1, tm, w), lambda bi, i: (b0 + bi, i, 0))
    return pl.pallas_call(
        _mix_kernel,
        grid=(nb, spt),
        in_specs=[tok_in(attn.shape[-1]), tok_in(d), tok_in(d),
                  pl.BlockSpec((1, 1, mod3.shape[-1]), lambda bi, i: (b0 + bi, 0, 0)),
                  pl.BlockSpec((1, d), lambda bi, i: (0, 0)),
                  _resident(wg.shape), _resident(wa.shape), _resident(wl.shape), _resident(wo.shape),
                  pl.BlockSpec((1, d), lambda bi, i: (0, 0)),
                  _resident(wrt.shape),
                  pl.BlockSpec((ROUTER_ROWS, 1), lambda bi, i: (0, 0))],
        out_specs=[pl.BlockSpec((1, tm, d), lambda bi, i: (bi, i, 0)),
                   pl.BlockSpec((tm * ROW_SUBLANES, LANE), lambda bi, i: (bi * spt + i, 0)),
                   pl.BlockSpec((ROUTE_ROWS, tm), lambda bi, i: (0, bi * spt + i)),
                   pl.BlockSpec((ROUTER_ROWS, 1), lambda bi, i: (0, 0))],
        out_shape=[jax.ShapeDtypeStruct((nb, s, d), BF16),
                   jax.ShapeDtypeStruct((nb * s * ROW_SUBLANES, LANE), jnp.uint32),
                   jax.ShapeDtypeStruct((ROUTE_ROWS, nb * s), F32),
                   jax.ShapeDtypeStruct((ROUTER_ROWS, 1), F32)],
        scratch_shapes=[pltpu.VMEM((ROUTER_ROWS, 1), F32)],
        compiler_params=_cparams(("arbitrary", "arbitrary"), VMEM_LIMIT),
        name="mix_route",
    )(attn, lru, x, mod3, g1, wg, wa, wl, wo, g2, wrt, brt)


TOP_K = 2
EXPERT_BLOCK = 512
MOE_BATCH_RANGES = 2


def _expert_kernel(be_ref, bv_ref, first_ref, slot_ref, next_ref, nu_ref, x_ref, w1_hbm, w3_hbm, w2_hbm,
                   y_ref, wf1, wf3, wf2, wb1, wb3, wb2, sem):
    j = pl.program_id(0)
    half = EXPERT_BLOCK // 2

    def fetch(e, s):
        return [pltpu.make_async_copy(w1_hbm.at[e], wf1.at[s], sem.at[s, 0]),
                pltpu.make_async_copy(w3_hbm.at[e], wf3.at[s], sem.at[s, 1]),
                pltpu.make_async_copy(w2_hbm.at[e], wf2.at[s], sem.at[s, 2])]

    def ffn(n):
        xb = _load_tile_rows(x_ref, n).astype(BF16)
        a = jnp.dot(xb, wb1[...], preferred_element_type=F32)
        g = jnp.dot(xb, wb3[...], preferred_element_type=F32)
        hm = (a * jax.nn.sigmoid(a) * g).astype(BF16)
        _store_tile_rows(y_ref, jnp.dot(hm, wb2[...], preferred_element_type=F32))

    @pl.when(j < nu_ref[0])
    def _():
        @pl.when(first_ref[j] == 1)
        def _():
            e, s = be_ref[j], slot_ref[j]

            @pl.when(j == 0)
            def _():
                for c in fetch(e, s):
                    c.start()

            @pl.when(next_ref[j] >= 0)
            def _():
                for c in fetch(next_ref[j], 1 - s):
                    c.start()

            for c in fetch(e, s):
                c.wait()
            wb1[...] = wf1[s].astype(BF16)
            wb3[...] = wf3[s].astype(BF16)
            wb2[...] = wf2[s].astype(BF16)

        @pl.when(bv_ref[j] > half)
        def _():
            ffn(EXPERT_BLOCK)

        @pl.when(bv_ref[j] <= half)
        def _():
            ffn(half)
            y_ref[half * ROW_SUBLANES:, :] = jnp.zeros((half * ROW_SUBLANES, LANE), y_ref.dtype)

    @pl.when(j >= nu_ref[0])
    def _():
        y_ref[...] = jnp.zeros_like(y_ref)


def _experts(xp, plan, w1, w3, w2):
    ne, d, de = w1.shape
    nb = xp.shape[0] // (EXPERT_BLOCK * ROW_SUBLANES)
    rows = (EXPERT_BLOCK * ROW_SUBLANES, LANE)
    grid_spec = pltpu.PrefetchScalarGridSpec(
        num_scalar_prefetch=len(plan),
        grid=(nb,),
        in_specs=[pl.BlockSpec(rows, lambda j, *p: (jnp.minimum(j, p[-1][0] - 1), 0)),
                  pl.BlockSpec(memory_space=pl.ANY),
                  pl.BlockSpec(memory_space=pl.ANY),
                  pl.BlockSpec(memory_space=pl.ANY)],
        out_specs=pl.BlockSpec(rows, lambda j, *p: (j, 0)),
        scratch_shapes=[pltpu.VMEM((2, d, de), F32), pltpu.VMEM((2, d, de), F32), pltpu.VMEM((2, de, d), F32),
                        pltpu.VMEM((d, de), BF16), pltpu.VMEM((d, de), BF16), pltpu.VMEM((de, d), BF16),
                        pltpu.SemaphoreType.DMA((2, 3))])
    return pl.pallas_call(
        _expert_kernel,
        grid_spec=grid_spec,
        out_shape=jax.ShapeDtypeStruct(xp.shape, xp.dtype),
        compiler_params=_cparams(("arbitrary",), VMEM_LIMIT),
        name="experts",
    )(*plan, xp, w1, w3, w2)


def _combine_kernel(y0_ref, y1_ref, route_ref, x1_ref, mod_ref, gf_ref, *rest):
    o_ref = rest[-1]
    tm, d = x1_ref.shape[1], x1_ref.shape[2]
    route = jnp.concatenate([route_ref[...], jnp.zeros((LANE - ROUTE_ROWS, tm), F32)], axis=0).T
    moe = _load_tile_rows(y0_ref) * route[:, 4:5] + _load_tile_rows(y1_ref) * route[:, 5:6]
    gate2 = mod_ref[0][:, 5 * d:6 * d]
    xo = x1_ref[0].astype(F32) + (1.0 + gate2) * moe
    ms = jnp.mean(xo * xo, axis=-1, keepdims=True)
    o_ref[0] = xo * lax.rsqrt(ms + EPS) * gf_ref[...]


def _combine(yg, route, x1, mod3, gf, b0, out_prev, *, tm=1024):
    nb, s, d = x1.shape
    b_all = mod3.shape[0]
    spt = s // tm
    nt = nb * spt
    rows = (tm * ROW_SUBLANES, LANE)
    in_specs = [pl.BlockSpec(rows, lambda bi, i: (bi * spt + i, 0)),
                pl.BlockSpec(rows, lambda bi, i: (nt + bi * spt + i, 0)),
                pl.BlockSpec((ROUTE_ROWS, tm), lambda bi, i: (0, bi * spt + i)),
                pl.BlockSpec((1, tm, d), lambda bi, i: (bi, i, 0)),
                pl.BlockSpec((1, 1, mod3.shape[-1]), lambda bi, i: (b0 + bi, 0, 0)),
                pl.BlockSpec((1, d), lambda bi, i: (0, 0))]
    args = [yg, yg, route, x1, mod3, gf]
    aliases = {}
    if out_prev is not None:
        in_specs.append(pl.BlockSpec(memory_space=pl.ANY))
        aliases = {len(args): 0}
        args.append(out_prev)
    return pl.pallas_call(
        _combine_kernel,
        grid=(nb, spt),
        in_specs=in_specs,
        out_specs=pl.BlockSpec((1, tm, d), lambda bi, i: (b0 + bi, i, 0)),
        out_shape=jax.ShapeDtypeStruct((b_all, s, d), F32),
        input_output_aliases=aliases,
        compiler_params=_cparams(("parallel", "parallel"), VMEM_LIMIT),
        name="combine",
    )(*args)


def _slot_plan(route, counts, n_tok):
    sizes = counts[EXPERT_ROW0:EXPERT_ROW0 + N_EXPERTS, 0].astype(jnp.int32)
    padded = (sizes + EXPERT_BLOCK - 1) // EXPERT_BLOCK * EXPERT_BLOCK
    pad_ends = jnp.cumsum(padded)
    pad_starts = pad_ends - padded
    eid = route[0:TOP_K].astype(jnp.int32)
    rank = route[TOP_K:2 * TOP_K].astype(jnp.int32)
    start = jnp.sum(jnp.where(eid[..., None] == jnp.arange(N_EXPERTS), pad_starts, 0), axis=-1)
    dest = (start + rank).reshape(TOP_K * n_tok)
    n_blocks = (n_tok * TOP_K + N_EXPERTS * (EXPERT_BLOCK - 1) + EXPERT_BLOCK - 1) // EXPERT_BLOCK
    gran = SC_CORES * SC_SUBCORES * SC_CHUNK // math.gcd(SC_CORES * SC_SUBCORES * SC_CHUNK, EXPERT_BLOCK)
    n_blocks = (n_blocks + gran - 1) // gran * gran
    n_used = pad_ends[-1] // EXPERT_BLOCK
    blk = jnp.minimum(jnp.arange(n_blocks), n_used - 1)
    blk_e = jnp.minimum(jnp.sum(pad_ends[None, :] <= (blk * EXPERT_BLOCK)[:, None], axis=1), N_EXPERTS - 1)
    blk_valid = jnp.clip(sizes[blk_e] - (blk * EXPERT_BLOCK - pad_starts[blk_e]), 0, EXPERT_BLOCK)
    idx = jnp.arange(n_blocks)
    first = (idx < n_used) & ((idx == 0) | (blk_e != jnp.roll(blk_e, 1)))
    slot = (jnp.cumsum(first) - 1) % 2
    later_first = lax.cummin(jnp.where(first, idx, n_blocks), reverse=True)
    next_first = jnp.concatenate([later_first[1:], jnp.full((1,), n_blocks)])
    next_e = jnp.where(next_first < n_blocks, blk_e[jnp.minimum(next_first, n_blocks - 1)], -1)
    i32 = lambda a: a.astype(jnp.int32)
    plan = (i32(blk_e), i32(blk_valid), i32(first), i32(slot), i32(next_e), i32(n_used.reshape(1)))
    return dest, n_blocks * EXPERT_BLOCK, plan


def kernel(x, c, w_mod, b_mod, norm1_g, w_in, conv_w, conv_b, lru_wx, lru_bx, lru_wa, lru_ba, lru_lambda, w_attn_o, w_lru_o, w_out, norm2_g, w_grp, b_grp, w_exp, b_exp, w1, w3, w2, norm_f_g):
    b, s, d = x.shape
    assert d == D_MODEL and s == SPAN * DILATIONS[-1] and w_mod.shape[0] == 1
    mod3 = _modulation(c, w_mod[0], b_mod[0]).reshape(b, 1, 6 * d)
    w_proj, w_gate = _prep_w_in(w_in[0])
    g1 = norm1_g[0].reshape(1, d)
    qkv0, qkv1, qkv2, xc, gy = _projection(x, mod3, g1, w_proj, conv_w[0], conv_b[0])
    attn = _attention((qkv0, qkv1, qkv2), b, s)
    lru = _lru_branch(xc, gy, lru_wx[0], lru_bx[0], lru_wa[0], lru_ba[0], lru_lambda[0])

    n_pad = ROUTER_ROWS - N_GROUPS - N_EXPERTS
    wr = jnp.pad(jnp.concatenate([w_grp[0], w_exp[0]], axis=1).T, ((0, n_pad), (0, 0))).astype(BF16)
    br = jnp.pad(jnp.concatenate([b_grp[0], b_exp[0]]), (0, n_pad)).reshape(ROUTER_ROWS, 1)
    wa, wl, wo = w_attn_o[0].astype(BF16), w_lru_o[0].astype(BF16), w_out[0].astype(BF16)
    as_rows = lambda a: a.reshape(-1, ROW_SUBLANES, LANE)
    as_tiles = lambda a: a.reshape(-1, LANE)

    out = None
    nb = b // MOE_BATCH_RANGES
    for b0 in range(0, b, nb):
        x1, h2, route, counts = _mix_route(attn, lru, x, mod3, g1, w_gate, wa, wl, wo,
                                           norm2_g[0].reshape(1, d), wr, br, b0, nb)
        dest, n_slots, plan = _slot_plan(route, counts, nb * s)
        xp = as_tiles(_sc_scatter_rows(as_rows(h2), dest, n_slots))
        yp = _experts(xp, plan, w1[0], w3[0], w2[0])
        yg = as_tiles(_sc_gather_rows(as_rows(yp), dest))
        out = _combine(yg, route, x1, mod3, norm_f_g.reshape(1, d), b0, out)
    return out
```

```python
import functools
import math

import jax
import jax.numpy as jnp
from jax import lax
from jax.experimental import pallas as pl
from jax.experimental.pallas import tpu as pltpu
from jax.experimental.pallas import tpu_sc as plsc

F32 = jnp.float32
BF16 = jnp.bfloat16

D_MODEL = 1024
HEAD_DIM = 64
N_SLOTS = 8
SPAN = 128
DILATIONS = (1, 4, 16)
GROUP_COLS = 3 * N_SLOTS * HEAD_DIM
ATTN_WIDTH = len(DILATIONS) * N_SLOTS * HEAD_DIM
ATTN_OUT = N_SLOTS * HEAD_DIM
LRU_WIDTH = D_MODEL
CONV_WIDTH = 4
CONV_TAIL = 8
LRU_C = 8.0
N_GROUPS = 4
EXPERTS_PER_GROUP = 8
N_EXPERTS = N_GROUPS * EXPERTS_PER_GROUP
EPS = 1e-6
LOG2_E = 1.4426950408889634
LANE = 128
VMEM_LIMIT = 56 * 1024 * 1024


def _cparams(sem, vmem=None):
    return pltpu.CompilerParams(dimension_semantics=sem, vmem_limit_bytes=vmem)


def _resident(shape):
    nd = len(shape)
    return pl.BlockSpec(shape, lambda *_: (0,) * nd, pipeline_mode=pl.Buffered(1))


def _mod_kernel(c_ref, w_ref, b_ref, o_ref):
    c = c_ref[...]
    ca = c * jax.nn.sigmoid(c)
    o_ref[...] = jnp.dot(ca.astype(BF16), w_ref[...].astype(BF16),
                         preferred_element_type=F32) + b_ref[...]


def _modulation(c, w_mod, b_mod):
    b, d = c.shape
    n = w_mod.shape[1]
    tn = n // 4
    return pl.pallas_call(
        _mod_kernel,
        grid=(n // tn,),
        in_specs=[pl.BlockSpec((b, d), lambda j: (0, 0)),
                  pl.BlockSpec((d, tn), lambda j: (0, j)),
                  pl.BlockSpec((1, tn), lambda j: (0, j))],
        out_specs=pl.BlockSpec((b, tn), lambda j: (0, j)),
        out_shape=jax.ShapeDtypeStruct((b, n), F32),
        compiler_params=_cparams(("arbitrary",)),
        name="modulation",
    )(c, w_mod, b_mod.reshape(1, n))


def _rms_mod(x, g, scale, shift):
    ms = jnp.mean(x * x, axis=-1, keepdims=True)
    return x * lax.rsqrt(ms + EPS) * g * (1.0 + scale) + shift


def _gelu_tanh(y):
    return y * (0.5 * (1.0 + jnp.tanh(0.7978845608028654 * (y + 0.044715 * (y * y * y)))))


def _proj_kernel(x_ref, mod_ref, g_ref, w_ref, cw_ref, cb_ref, qkv0_ref, qkv1_ref, qkv2_ref,
                 xc_ref, gy_ref, hs_ref, xe_ref, *, tm):
    @pl.when(pl.program_id(1) == 0)
    def _():
        xe_ref[0:CONV_TAIL, :] = jnp.zeros((CONV_TAIL, LRU_WIDTH), F32)

    @pl.when(pl.program_id(1) > 0)
    def _():
        xe_ref[0:CONV_TAIL, :] = xe_ref[tm:tm + CONV_TAIL, :]

    d_model = x_ref.shape[-1]
    m = mod_ref[0]
    h = _rms_mod(x_ref[0], g_ref[...], m[:, d_model:2 * d_model], m[:, 0:d_model])

    def mm(hv, lo, hi):
        return jnp.dot(hv, w_ref[:, lo:hi], preferred_element_type=F32)

    hb = h.astype(BF16)
    c0 = len(DILATIONS) * GROUP_COLS
    qkv0_ref[0] = mm(hb, 0, GROUP_COLS).astype(BF16)
    xr = mm(hb, c0, c0 + LRU_WIDTH)
    xe_ref[CONV_TAIL:, :] = xr
    cw = cw_ref[...]
    xc = xr * cw[CONV_WIDTH - 1:CONV_WIDTH] + cb_ref[...]
    for k in range(1, CONV_WIDTH):
        xc = xc + xe_ref[CONV_TAIL - k:CONV_TAIL - k + tm, :] * cw[CONV_WIDTH - 1 - k:CONV_WIDTH - k]
    xc_ref[0] = xc.astype(BF16)
    gy_ref[0] = _gelu_tanh(mm(hb, c0 + LRU_WIDTH, c0 + 2 * LRU_WIDTH)).astype(BF16)

    n_slab = d_model // LANE
    for j in range(n_slab):
        hs_ref[j] = h[:, j * LANE:(j + 1) * LANE]
    for g, out_ref in ((1, qkv1_ref), (2, qkv2_ref)):
        d = DILATIONS[g]
        rows = tm // d
        hp = jnp.concatenate(
            [jnp.concatenate([hs_ref[j, pl.ds(p, rows, stride=d), :] for j in range(n_slab)], axis=1)
             for p in range(d)], axis=0).astype(BF16)
        res = mm(hp, g * GROUP_COLS, (g + 1) * GROUP_COLS).astype(BF16)
        for p in range(d):
            out_ref[p] = res[p * rows:(p + 1) * rows]


def _projection(x, mod3, g1, w_r, conv_w, conv_b, *, tm=512):
    b, s, d = x.shape
    n = w_r.shape[1]
    assert s % tm == 0 and tm % (16 * DILATIONS[-1]) == 0 and CONV_TAIL >= CONV_WIDTH - 1
    out_shape = [jax.ShapeDtypeStruct((b * dd, s // dd, GROUP_COLS), BF16) for dd in DILATIONS]
    out_shape += [jax.ShapeDtypeStruct((b, s, LRU_WIDTH), BF16),
                  jax.ShapeDtypeStruct((b, s, LRU_WIDTH), BF16)]
    out_specs = [pl.BlockSpec((dd, tm // dd, GROUP_COLS), lambda bi, i: (bi, i, 0)) for dd in DILATIONS]
    out_specs += [pl.BlockSpec((1, tm, LRU_WIDTH), lambda bi, i: (bi, i, 0)),
                  pl.BlockSpec((1, tm, LRU_WIDTH), lambda bi, i: (bi, i, 0))]
    return pl.pallas_call(
        functools.partial(_proj_kernel, tm=tm),
        grid=(b, s // tm),
        in_specs=[pl.BlockSpec((1, tm, d), lambda bi, i: (bi, i, 0)),
                  pl.BlockSpec((1, 1, mod3.shape[-1]), lambda bi, i: (bi, 0, 0)),
                  pl.BlockSpec((1, d), lambda bi, i: (0, 0)),
                  _resident((d, n)),
                  pl.BlockSpec((CONV_WIDTH, LRU_WIDTH), lambda bi, i: (0, 0)),
                  pl.BlockSpec((1, LRU_WIDTH), lambda bi, i: (0, 0))],
        out_specs=out_specs,
        out_shape=out_shape,
        scratch_shapes=[pltpu.VMEM((d // LANE, tm, LANE), F32), pltpu.VMEM((CONV_TAIL + tm, LRU_WIDTH), F32)],
        compiler_params=_cparams(("parallel", "arbitrary"), VMEM_LIMIT),
        name="projection",
    )(x, mod3, g1, w_r, conv_w, conv_b.reshape(1, LRU_WIDTH))


def _attn_kernel(q0, k0, v0, q1, k1, v1, q2, k2, v2, o_ref, acc_ref, lse_ref, bias_ref, *, seq):
    hcols = o_ref.shape[-1]
    n_head = hcols // HEAD_DIM
    head_of_lane = lax.broadcasted_iota(jnp.int32, (SPAN, hcols), 1) // HEAD_DIM
    head_mask_b = [jnp.where(head_of_lane == h, 1.0, 0.0).astype(BF16) for h in range(n_head)]
    ones_blk = jnp.ones((2 * SPAN, hcols), BF16)

    def by_head(parts):
        out = parts[n_head - 1]
        for h in range(n_head - 2, -1, -1):
            out = jnp.where(head_of_lane == h, parts[h], out)
        return out

    @pl.when((pl.program_id(0) == 0) & (pl.program_id(1) == 0))
    def _():
        qi = lax.broadcasted_iota(jnp.int32, (n_head * SPAN, 2 * SPAN), 0) % SPAN
        ki = lax.broadcasted_iota(jnp.int32, (n_head * SPAN, 2 * SPAN), 1)
        band = (ki >= qi) & (ki <= qi + SPAN)
        bias_ref[0] = jnp.where(band, 0.0, -jnp.inf)
        bias_ref[1] = jnp.where(band & (ki >= SPAN), 0.0, -jnp.inf)

    for g, (q_ref, k_ref, v_ref) in enumerate(((q0, k0, v0), (q1, k1, v1), (q2, k2, v2))):
        d = DILATIONS[g]
        n_blk = seq // d // SPAN

        def tile(n, carry, q_ref=q_ref, k_ref=k_ref, v_ref=v_ref, d=d, n_blk=n_blk, g=g):
            p = n // n_blk
            blk = n % n_blk
            r0 = pl.multiple_of(blk * SPAN, SPAN)
            rp = pl.multiple_of(jnp.maximum(blk - 1, 0) * SPAN, SPAN)
            q = q_ref[p, pl.ds(r0, SPAN), :]
            kk = jnp.concatenate([k_ref[p, pl.ds(rp, SPAN), :], k_ref[p, pl.ds(r0, SPAN), :]], axis=0)
            vv = jnp.concatenate([v_ref[p, pl.ds(rp, SPAN), :], v_ref[p, pl.ds(r0, SPAN), :]], axis=0)
            qs = jnp.concatenate([q * head_mask_b[h] for h in range(n_head)], axis=0)
            sc = lax.dot_general(qs, kk, (((1,), (1,)), ((), ())), preferred_element_type=F32)
            sc = sc + bias_ref[jnp.where(blk > 0, 0, 1)]
            mx = jnp.max(sc, axis=-1, keepdims=True)
            e = jnp.exp2(sc - mx)
            pv = jnp.dot(e.astype(BF16), jnp.concatenate([vv, ones_blk], axis=1), preferred_element_type=F32)
            rows_of = lambda a: [a[h * SPAN:(h + 1) * SPAN] for h in range(n_head)]
            den_l = by_head(rows_of(pv[:, hcols:]))
            o = by_head(rows_of(pv[:, :hcols])) / den_l
            l = by_head(rows_of(mx)) + jnp.log(den_l) * LOG2_E
            start = p + d * r0
            for j in range(hcols // LANE):
                rows = pl.ds(start, SPAN, stride=d) if d > 1 else pl.ds(start, SPAN)
                acc_ref[g, j, rows, :] = o[:, j * LANE:(j + 1) * LANE]
                lse_ref[g, j, rows, :] = l[:, j * LANE:(j + 1) * LANE]
            return carry

        lax.fori_loop(0, seq // SPAN, tile, 0, unroll=16)

    chunk = 256

    def combine(c, carry):
        r = pl.multiple_of(c * chunk, chunk)
        for j in range(hcols // LANE):
            ls = [lse_ref[g, j, pl.ds(r, chunk), :] for g in range(len(DILATIONS))]
            mx = jnp.maximum(jnp.maximum(ls[0], ls[1]), ls[2])
            ws = [jnp.exp2(v - mx) for v in ls]
            num = ws[0] * acc_ref[0, j, pl.ds(r, chunk), :]
            for g in range(1, len(DILATIONS)):
                num = num + ws[g] * acc_ref[g, j, pl.ds(r, chunk), :]
            o_ref[0, pl.ds(r, chunk), j * LANE:(j + 1) * LANE] = (num / (ws[0] + ws[1] + ws[2])).astype(BF16)
        return carry

    lax.fori_loop(0, seq // chunk, combine, 0)


def _attention(qkvs, b, s):
    hcols = 2 * HEAD_DIM
    n_hg = ATTN_OUT // hcols
    ncb = ATTN_OUT // hcols
    in_specs, args = [], []
    for g, d in enumerate(DILATIONS):
        for part in range(3):
            in_specs.append(pl.BlockSpec((d, s // d, hcols),
                                         lambda bi, hg, part=part: (bi, 0, part * ncb + hg)))
            args.append(qkvs[g])
    return pl.pallas_call(
        functools.partial(_attn_kernel, seq=s),
        grid=(b, n_hg),
        in_specs=in_specs,
        out_specs=pl.BlockSpec((1, s, hcols), lambda bi, hg: (bi, 0, hg)),
        out_shape=jax.ShapeDtypeStruct((b, s, ATTN_OUT), BF16),
        scratch_shapes=[pltpu.VMEM((len(DILATIONS), hcols // LANE, s, LANE), F32),
                        pltpu.VMEM((len(DILATIONS), hcols // LANE, s, LANE), F32),
                        pltpu.VMEM((2, (hcols // HEAD_DIM) * SPAN, 2 * SPAN), F32)],
        compiler_params=_cparams(("arbitrary", "arbitrary"), VMEM_LIMIT),
        name="dilated_attention",
    )(*args)


def _lru_kernel(xc_ref, gy_ref, wg_ref, bx_ref, ba_ref, lam_ref, o_ref, a_ref, b_ref, h_ref, *, pitch):
    nb, ts, tc = xc_ref.shape
    nl = tc // LANE

    @pl.when(pl.program_id(1) == 0)
    def _():
        h_ref[...] = jnp.zeros_like(h_ref)

    xb = xc_ref[...].reshape(nb * ts, tc)
    xc = xb.astype(F32)
    gates = jnp.dot(xb, wg_ref[0], preferred_element_type=F32)
    sigmoid = lambda v: 0.5 * jnp.tanh(0.5 * v) + 0.5
    gate_i = sigmoid(gates[:, :tc] + bx_ref[...])
    gate_r = sigmoid(gates[:, tc:] + ba_ref[...])
    neg_lam = -lam_ref[...]
    softplus = jnp.maximum(neg_lam, 0.0) + jnp.log1p(jnp.exp(-jnp.abs(neg_lam)))
    log_a = (-LRU_C) * gate_r * softplus
    a = jnp.exp(log_a)
    one_m_a2 = jnp.tanh(-log_a) * (1.0 + a * a)
    mult = jnp.where(one_m_a2 > 0.0, one_m_a2 * lax.rsqrt(one_m_a2), 0.0)
    bv = mult * gate_i * xc
    for bi in range(nb):
        for j in range(nl):
            a_ref[j, pl.ds(bi * pitch, ts), :] = a[bi * ts:(bi + 1) * ts, j * LANE:(j + 1) * LANE]
            b_ref[j, pl.ds(bi * pitch, ts), :] = bv[bi * ts:(bi + 1) * ts, j * LANE:(j + 1) * LANE]

    def step(t, hs):
        out = []
        for j in range(nl):
            rows = pl.ds(t, nb, stride=pitch)
            h = a_ref[j, rows, :] * hs[j] + b_ref[j, rows, :]
            b_ref[j, rows, :] = h
            out.append(h)
        return tuple(out)

    hs = lax.fori_loop(0, ts, step, tuple(h_ref[j] for j in range(nl)), unroll=8)
    for j in range(nl):
        h_ref[j] = hs[j]
    for bi in range(nb):
        h = jnp.concatenate([b_ref[j, pl.ds(bi * pitch, ts), :] for j in range(nl)], axis=1)
        o_ref[bi] = (h * gy_ref[bi].astype(F32)).astype(BF16)


def _lru_gate_weights(wx, wa, tc):
    nb, bd, _ = wx.shape
    per = tc // bd
    eye = jnp.eye(per, dtype=wx.dtype)

    def bdiag(w):
        w = w.reshape(nb // per, per, bd, bd)
        return jnp.einsum('cpio,pq->cpiqo', w, eye).reshape(nb // per, tc, tc)

    return jnp.concatenate([bdiag(wx), bdiag(wa)], axis=-1).astype(BF16)


def _lru_branch(xc, gy, wx, bx, wa, ba, lam, *, tc=256, ts=256):
    b, s, c = xc.shape
    assert s % ts == 0 and c % tc == 0
    wg = _lru_gate_weights(wx, wa, tc)
    row = lambda v: v.reshape(1, c)
    tile = pl.BlockSpec((b, ts, tc), lambda ci, ti: (0, ti, ci))
    vec = pl.BlockSpec((1, tc), lambda ci, ti: (0, ci))
    pitch = ts + 8
    return pl.pallas_call(
        functools.partial(_lru_kernel, pitch=pitch),
        grid=(c // tc, s // ts),
        in_specs=[tile, tile,
                  pl.BlockSpec((1, tc, 2 * tc), lambda ci, ti: (ci, 0, 0)),
                  vec, vec, vec],
        out_specs=tile,
        out_shape=jax.ShapeDtypeStruct((b, s, c), BF16),
        scratch_shapes=[pltpu.VMEM((tc // LANE, b * pitch, LANE), F32),
                        pltpu.VMEM((tc // LANE, b * pitch, LANE), F32),
                        pltpu.VMEM((tc // LANE, b, LANE), F32)],
        compiler_params=_cparams(("parallel", "arbitrary"), VMEM_LIMIT),
        name="rg_lru",
    )(xc, gy, wg, row(bx), row(ba), row(lam))


def _prep_w_in(w_in):
    a = ATTN_WIDTH
    gw = N_SLOTS * HEAD_DIM
    q = w_in[:, :a] * (HEAD_DIM ** -0.5 * LOG2_E)
    k = w_in[:, a:2 * a]
    v = w_in[:, 2 * a:3 * a]
    parts = []
    for g in range(len(DILATIONS)):
        sl = slice(g * gw, (g + 1) * gw)
        parts += [q[:, sl], k[:, sl], v[:, sl]]
    parts.append(w_in[:, 3 * a:3 * a + 2 * LRU_WIDTH])
    return jnp.concatenate(parts, axis=1).astype(BF16), w_in[:, 3 * a + 2 * LRU_WIDTH:].astype(BF16)


ROW_SUBLANES = D_MODEL // 2 // LANE


def _store_tile_rows(ref, v, row0=0):
    n, half = v.shape[0], v.shape[1] // 2
    lo = pltpu.bitcast(v[:, :half].astype(BF16).astype(F32), jnp.uint32)
    hi = pltpu.bitcast(v[:, half:].astype(BF16).astype(F32), jnp.uint32)
    words = (hi & jnp.uint32(0xFFFF0000)) | (lo >> 16)
    for j in range(ROW_SUBLANES):
        ref[pl.ds(row0 * ROW_SUBLANES + j, n, stride=ROW_SUBLANES), :] = words[:, j * LANE:(j + 1) * LANE]


def _load_tile_rows(ref, n=None):
    n = ref.shape[0] // ROW_SUBLANES if n is None else n
    words = [ref[pl.ds(j, n, stride=ROW_SUBLANES), :] for j in range(ROW_SUBLANES)]
    lo = [pltpu.bitcast(w << 16, F32) for w in words]
    hi = [pltpu.bitcast(w & jnp.uint32(0xFFFF0000), F32) for w in words]
    return jnp.concatenate(lo + hi, axis=-1)


SC_CORES, SC_SUBCORES = 2, 16
SC_CHUNK = 128


def _sc_gather_rows(table, idx):
    n = idx.shape[0]
    per_worker = n // (SC_CORES * SC_SUBCORES)
    n_chunks = per_worker // SC_CHUNK
    assert n_chunks * SC_CHUNK * SC_CORES * SC_SUBCORES == n
    mesh = plsc.VectorSubcoreMesh(core_axis_name="c", subcore_axis_name="s")

    def body(table_hbm, idx_hbm, out_hbm, idx_v, rows_v, sem):
        base = (lax.axis_index("s") * SC_CORES + lax.axis_index("c")) * per_worker

        @pl.loop(0, n_chunks)
        def _(i):
            off = pl.multiple_of(base + i * SC_CHUNK, SC_CHUNK)
            pltpu.sync_copy(idx_hbm.at[pl.ds(off, SC_CHUNK)], idx_v)
            pltpu.async_copy(table_hbm.at[idx_v], rows_v, sem).wait()
            pltpu.sync_copy(rows_v, out_hbm.at[pl.ds(off, SC_CHUNK)])

    return pl.kernel(
        body, mesh=mesh,
        out_type=jax.ShapeDtypeStruct((n,) + table.shape[1:], table.dtype),
        scratch_types=[pltpu.VMEM((SC_CHUNK,), jnp.int32),
                       pltpu.VMEM((SC_CHUNK,) + table.shape[1:], table.dtype),
                       pltpu.SemaphoreType.DMA],
        name="sc_gather_rows",
    )(table, idx)


def _sc_scatter_rows(rows, idx, n_out):
    n_rows = rows.shape[0]
    n_choice = idx.shape[0] // n_rows
    per_worker = n_rows // (SC_CORES * SC_SUBCORES)
    n_chunks = per_worker // SC_CHUNK
    assert n_chunks * SC_CHUNK * SC_CORES * SC_SUBCORES == n_rows and n_choice * n_rows == idx.shape[0]
    mesh = plsc.VectorSubcoreMesh(core_axis_name="c", subcore_axis_name="s")

    def body(rows_hbm, idx_hbm, out_hbm, idx_v, rows_v):
        base = (lax.axis_index("s") * SC_CORES + lax.axis_index("c")) * per_worker

        @pl.loop(0, n_chunks)
        def _(i):
            off = pl.multiple_of(base + i * SC_CHUNK, SC_CHUNK)
            pltpu.sync_copy(rows_hbm.at[pl.ds(off, SC_CHUNK)], rows_v)
            for k in range(n_choice):
                pltpu.sync_copy(idx_hbm.at[pl.ds(k * n_rows + off, SC_CHUNK)], idx_v)
                pltpu.sync_copy(rows_v, out_hbm.at[idx_v])

    return pl.kernel(
        body, mesh=mesh,
        out_type=jax.ShapeDtypeStruct((n_out,) + rows.shape[1:], rows.dtype),
        scratch_types=[pltpu.VMEM((SC_CHUNK,), jnp.int32),
                       pltpu.VMEM((SC_CHUNK,) + rows.shape[1:], rows.dtype)],
        name="sc_scatter_rows",
    )(rows, idx)


ROUTE_ROWS = 8
EXPERT_ROW0 = N_GROUPS
ROUTER_ROWS = 48


def _mix_kernel(attn_ref, lru_ref, x_ref, mod_ref, g1_ref, wg_ref, wa_ref, wl_ref, wo_ref, g2_ref,
                wrt_ref, brt_ref, x1_ref, h2_ref, route_ref, cnt_ref, cnt_acc):
    d = x_ref.shape[-1]
    tm = x_ref.shape[1]

    @pl.when((pl.program_id(0) == 0) & (pl.program_id(1) == 0))
    def _():
        cnt_acc[...] = jnp.zeros_like(cnt_acc)

    m = mod_ref[0]
    gate1, shift2, scale2 = m[:, 2 * d:3 * d], m[:, 3 * d:4 * d], m[:, 4 * d:5 * d]
    x = x_ref[0]
    h1 = _rms_mod(x, g1_ref[...], m[:, d:2 * d], m[:, 0:d]).astype(BF16)
    gates = 0.5 * jnp.tanh(0.5 * jnp.dot(h1, wg_ref[...], preferred_element_type=F32)) + 0.5
    ya = jnp.dot(attn_ref[0], wa_ref[...], preferred_element_type=F32)
    yl = jnp.dot(lru_ref[0], wl_ref[...], preferred_element_type=F32)
    mixed = gates[:, :d] * ya + gates[:, d:] * yl
    y = jnp.dot(mixed.astype(BF16), wo_ref[...], preferred_element_type=F32)
    x1 = x + (1.0 + gate1) * y
    x1_ref[0] = x1.astype(BF16)
    h2 = _rms_mod(x1, g2_ref[...], scale2, shift2)
    _store_tile_rows(h2_ref, h2)
    logits = lax.dot_general(wrt_ref[...], h2.astype(BF16), (((1,), (1,)), ((), ())),
                             preferred_element_type=F32) + brt_ref[...]

    row = lax.broadcasted_iota(jnp.int32, logits.shape, 0)
    neg = -jnp.inf

    def top(vals):
        mx = jnp.max(vals, axis=0, keepdims=True)
        idx = jnp.min(jnp.where(vals == mx, row, ROUTER_ROWS), axis=0, keepdims=True)
        return mx, idx

    is_grp = row < N_GROUPS
    gmax, gidx = top(jnp.where(is_grp, logits, neg))
    grp_gate = 1.0 / jnp.sum(jnp.where(is_grp, jnp.exp(logits - gmax), 0.0), axis=0, keepdims=True)
    lo = EXPERT_ROW0 + EXPERTS_PER_GROUP * gidx
    el = jnp.where((row >= lo) & (row < lo + EXPERTS_PER_GROUP), logits, neg)
    v1, i1 = top(el)
    v2, i2 = top(jnp.where(row == i1, neg, el))
    e21 = jnp.exp(v2 - v1)
    wt1 = grp_gate / (1.0 + e21)
    wt2 = wt1 * e21

    oh1 = jnp.where(row == i1, 1.0, 0.0)
    oh2 = jnp.where(row == i2, 1.0, 0.0)
    ohs = oh1 + oh2
    rr = lax.broadcasted_iota(jnp.int32, (tm, tm), 0)
    cc = lax.broadcasted_iota(jnp.int32, (tm, tm), 1)
    earlier = jnp.where(rr < cc, 1.0, 0.0).astype(BF16)
    before = jnp.dot(ohs.astype(BF16), earlier, preferred_element_type=F32) + cnt_acc[...]
    rank1 = jnp.sum(oh1 * before, axis=0, keepdims=True)
    rank2 = jnp.sum(oh2 * before, axis=0, keepdims=True)
    cnt_acc[...] = cnt_acc[...] + jnp.sum(ohs, axis=1, keepdims=True)
    cnt_ref[...] = cnt_acc[...]

    vals = [(i1 - EXPERT_ROW0).astype(F32), (i2 - EXPERT_ROW0).astype(F32), rank1, rank2, wt1, wt2]
    out_row = lax.broadcasted_iota(jnp.int32, (ROUTE_ROWS, tm), 0)
    slab = jnp.zeros((ROUTE_ROWS, tm), F32)
    for j, v in enumerate(vals):
        slab = jnp.where(out_row == j, v, slab)
    route_ref[...] = slab


def _mix_route(attn, lru, x, mod3, g1, wg, wa, wl, wo, g2, wrt, brt, b0, nb, *, tm=512):
    _, s, d = x.shape
    spt = s // tm
    tok_in = lambda w: pl.BlockSpec((1, tm, w), lambda bi, i: (b0 + bi, i, 0))
    return pl.pallas_call(
        _mix_kernel,
        grid=(nb, spt),
        in_specs=[tok_in(attn.shape[-1]), tok_in(d), tok_in(d),
                  pl.BlockSpec((1, 1, mod3.shape[-1]), lambda bi, i: (b0 + bi, 0, 0)),
                  pl.BlockSpec((1, d), lambda bi, i: (0, 0)),
                  _resident(wg.shape), _resident(wa.shape), _resident(wl.shape), _resident(wo.shape),
                  pl.BlockSpec((1, d), lambda bi, i: (0, 0)),
                  _resident(wrt.shape),
                  pl.BlockSpec((ROUTER_ROWS, 1), lambda bi, i: (0, 0))],
        out_specs=[pl.BlockSpec((1, tm, d), lambda bi, i: (bi, i, 0)),
                   pl.BlockSpec((tm * ROW_SUBLANES, LANE), lambda bi, i: (bi * spt + i, 0)),
                   pl.BlockSpec((ROUTE_ROWS, tm), lambda bi, i: (0, bi * spt + i)),
                   pl.BlockSpec((ROUTER_ROWS, 1), lambda bi, i: (0, 0))],
        out_shape=[jax.ShapeDtypeStruct((nb, s, d), BF16),
                   jax.ShapeDtypeStruct((nb * s * ROW_SUBLANES, LANE), jnp.uint32),
                   jax.ShapeDtypeStruct((ROUTE_ROWS, nb * s), F32),
                   jax.ShapeDtypeStruct((ROUTER_ROWS, 1), F32)],
        scratch_shapes=[pltpu.VMEM((ROUTER_ROWS, 1), F32)],
        compiler_params=_cparams(("arbitrary", "arbitrary"), VMEM_LIMIT),
        name="mix_route",
    )(attn, lru, x, mod3, g1, wg, wa, wl, wo, g2, wrt, brt)


TOP_K = 2
EXPERT_BLOCK = 512
MOE_BATCH_RANGES = 2


def _expert_kernel(be_ref, bv_ref, first_ref, slot_ref, next_ref, nu_ref, x_ref, w1_hbm, w3_hbm, w2_hbm,
                   y_ref, wf1, wf3, wf2, wb1, wb3, wb2, sem):
    j = pl.program_id(0)
    half = EXPERT_BLOCK // 2

    def fetch(e, s):
        return [pltpu.make_async_copy(w1_hbm.at[e], wf1.at[s], sem.at[s, 0]),
                pltpu.make_async_copy(w3_hbm.at[e], wf3.at[s], sem.at[s, 1]),
                pltpu.make_async_copy(w2_hbm.at[e], wf2.at[s], sem.at[s, 2])]

    def ffn(n):
        xb = _load_tile_rows(x_ref, n).astype(BF16)
        de = wb1.shape[1]
        y = None
        for c in range(0, de, de // 2):
            cols = slice(c, c + de // 2)
            a = jnp.dot(xb, wb1[:, cols], preferred_element_type=F32)
            g = jnp.dot(xb, wb3[:, cols], preferred_element_type=F32)
            hm = (a * jax.nn.sigmoid(a) * g).astype(BF16)
            part = jnp.dot(hm, wb2[cols, :], preferred_element_type=F32)
            y = part if y is None else y + part
        _store_tile_rows(y_ref, y)

    @pl.when(j < nu_ref[0])
    def _():
        @pl.when(first_ref[j] == 1)
        def _():
            e, s = be_ref[j], slot_ref[j]

            @pl.when(j == 0)
            def _():
                for c in fetch(e, s):
                    c.start()

            @pl.when(next_ref[j] >= 0)
            def _():
                for c in fetch(next_ref[j], 1 - s):
                    c.start()

            for c in fetch(e, s):
                c.wait()
            wb1[...] = wf1[s].astype(BF16)
            wb3[...] = wf3[s].astype(BF16)
            wb2[...] = wf2[s].astype(BF16)

        @pl.when(bv_ref[j] > half)
        def _():
            ffn(EXPERT_BLOCK)

        @pl.when(bv_ref[j] <= half)
        def _():
            ffn(half)
            y_ref[half * ROW_SUBLANES:, :] = jnp.zeros((half * ROW_SUBLANES, LANE), y_ref.dtype)

    @pl.when(j >= nu_ref[0])
    def _():
        y_ref[...] = jnp.zeros_like(y_ref)


def _experts(xp, plan, w1, w3, w2):
    ne, d, de = w1.shape
    nb = xp.shape[0] // (EXPERT_BLOCK * ROW_SUBLANES)
    rows = (EXPERT_BLOCK * ROW_SUBLANES, LANE)
    grid_spec = pltpu.PrefetchScalarGridSpec(
        num_scalar_prefetch=len(plan),
        grid=(nb,),
        in_specs=[pl.BlockSpec(rows, lambda j, *p: (jnp.minimum(j, p[-1][0] - 1), 0)),
                  pl.BlockSpec(memory_space=pl.ANY),
                  pl.BlockSpec(memory_space=pl.ANY),
                  pl.BlockSpec(memory_space=pl.ANY)],
        out_specs=pl.BlockSpec(rows, lambda j, *p: (j, 0)),
        scratch_shapes=[pltpu.VMEM((2, d, de), F32), pltpu.VMEM((2, d, de), F32), pltpu.VMEM((2, de, d), F32),
                        pltpu.VMEM((d, de), BF16), pltpu.VMEM((d, de), BF16), pltpu.VMEM((de, d), BF16),
                        pltpu.SemaphoreType.DMA((2, 3))])
    return pl.pallas_call(
        _expert_kernel,
        grid_spec=grid_spec,
        out_shape=jax.ShapeDtypeStruct(xp.shape, xp.dtype),
        compiler_params=_cparams(("arbitrary",), VMEM_LIMIT),
        name="experts",
    )(*plan, xp, w1, w3, w2)


def _combine_kernel(y0_ref, y1_ref, route_ref, x1_ref, mod_ref, gf_ref, *rest):
    o_ref = rest[-1]
    tm, d = x1_ref.shape[1], x1_ref.shape[2]
    route = jnp.concatenate([route_ref[...], jnp.zeros((LANE - ROUTE_ROWS, tm), F32)], axis=0).T
    moe = _load_tile_rows(y0_ref) * route[:, 4:5] + _load_tile_rows(y1_ref) * route[:, 5:6]
    gate2 = mod_ref[0][:, 5 * d:6 * d]
    xo = x1_ref[0].astype(F32) + (1.0 + gate2) * moe
    ms = jnp.mean(xo * xo, axis=-1, keepdims=True)
    o_ref[0] = xo * lax.rsqrt(ms + EPS) * gf_ref[...]


def _combine(yg, route, x1, mod3, gf, b0, out_prev, *, tm=1024):
    nb, s, d = x1.shape
    b_all = mod3.shape[0]
    spt = s // tm
    nt = nb * spt
    rows = (tm * ROW_SUBLANES, LANE)
    in_specs = [pl.BlockSpec(rows, lambda bi, i: (bi * spt + i, 0)),
                pl.BlockSpec(rows, lambda bi, i: (nt + bi * spt + i, 0)),
                pl.BlockSpec((ROUTE_ROWS, tm), lambda bi, i: (0, bi * spt + i)),
                pl.BlockSpec((1, tm, d), lambda bi, i: (bi, i, 0)),
                pl.BlockSpec((1, 1, mod3.shape[-1]), lambda bi, i: (b0 + bi, 0, 0)),
                pl.BlockSpec((1, d), lambda bi, i: (0, 0))]
    args = [yg, yg, route, x1, mod3, gf]
    aliases = {}
    if out_prev is not None:
        in_specs.append(pl.BlockSpec(memory_space=pl.ANY))
        aliases = {len(args): 0}
        args.append(out_prev)
    return pl.pallas_call(
        _combine_kernel,
        grid=(nb, spt),
        in_specs=in_specs,
        out_specs=pl.BlockSpec((1, tm, d), lambda bi, i: (b0 + bi, i, 0)),
        out_shape=jax.ShapeDtypeStruct((b_all, s, d), F32),
        input_output_aliases=aliases,
        compiler_params=_cparams(("parallel", "parallel"), VMEM_LIMIT),
        name="combine",
    )(*args)


def _slot_plan(route, counts, n_tok):
    sizes = counts[EXPERT_ROW0:EXPERT_ROW0 + N_EXPERTS, 0].astype(jnp.int32)
    padded = (sizes + EXPERT_BLOCK - 1) // EXPERT_BLOCK * EXPERT_BLOCK
    pad_ends = jnp.cumsum(padded)
    pad_starts = pad_ends - padded
    eid = route[0:TOP_K].astype(jnp.int32)
    rank = route[TOP_K:2 * TOP_K].astype(jnp.int32)
    start = jnp.sum(jnp.where(eid[..., None] == jnp.arange(N_EXPERTS), pad_starts, 0), axis=-1)
    dest = (start + rank).reshape(TOP_K * n_tok)
    n_blocks = (n_tok * TOP_K + N_EXPERTS * (EXPERT_BLOCK - 1) + EXPERT_BLOCK - 1) // EXPERT_BLOCK
    gran = SC_CORES * SC_SUBCORES * SC_CHUNK // math.gcd(SC_CORES * SC_SUBCORES * SC_CHUNK, EXPERT_BLOCK)
    n_blocks = (n_blocks + gran - 1) // gran * gran
    n_used = pad_ends[-1] // EXPERT_BLOCK
    blk = jnp.minimum(jnp.arange(n_blocks), n_used - 1)
    blk_e = jnp.minimum(jnp.sum(pad_ends[None, :] <= (blk * EXPERT_BLOCK)[:, None], axis=1), N_EXPERTS - 1)
    blk_valid = jnp.clip(sizes[blk_e] - (blk * EXPERT_BLOCK - pad_starts[blk_e]), 0, EXPERT_BLOCK)
    idx = jnp.arange(n_blocks)
    first = (idx < n_used) & ((idx == 0) | (blk_e != jnp.roll(blk_e, 1)))
    slot = (jnp.cumsum(first) - 1) % 2
    later_first = lax.cummin(jnp.where(first, idx, n_blocks), reverse=True)
    next_first = jnp.concatenate([later_first[1:], jnp.full((1,), n_blocks)])
    next_e = jnp.where(next_first < n_blocks, blk_e[jnp.minimum(next_first, n_blocks - 1)], -1)
    i32 = lambda a: a.astype(jnp.int32)
    plan = (i32(blk_e), i32(blk_valid), i32(first), i32(slot), i32(next_e), i32(n_used.reshape(1)))
    return dest, n_blocks * EXPERT_BLOCK, plan


def kernel(x, c, w_mod, b_mod, norm1_g, w_in, conv_w, conv_b, lru_wx, lru_bx, lru_wa, lru_ba, lru_lambda, w_attn_o, w_lru_o, w_out, norm2_g, w_grp, b_grp, w_exp, b_exp, w1, w3, w2, norm_f_g):
    b, s, d = x.shape
    assert d == D_MODEL and s == SPAN * DILATIONS[-1] and w_mod.shape[0] == 1
    mod3 = _modulation(c, w_mod[0], b_mod[0]).reshape(b, 1, 6 * d)
    w_proj, w_gate = _prep_w_in(w_in[0])
    g1 = norm1_g[0].reshape(1, d)
    qkv0, qkv1, qkv2, xc, gy = _projection(x, mod3, g1, w_proj, conv_w[0], conv_b[0])
    attn = _attention((qkv0, qkv1, qkv2), b, s)
    lru = _lru_branch(xc, gy, lru_wx[0], lru_bx[0], lru_wa[0], lru_ba[0], lru_lambda[0])

    n_pad = ROUTER_ROWS - N_GROUPS - N_EXPERTS
    wr = jnp.pad(jnp.concatenate([w_grp[0], w_exp[0]], axis=1).T, ((0, n_pad), (0, 0))).astype(BF16)
    br = jnp.pad(jnp.concatenate([b_grp[0], b_exp[0]]), (0, n_pad)).reshape(ROUTER_ROWS, 1)
    wa, wl, wo = w_attn_o[0].astype(BF16), w_lru_o[0].astype(BF16), w_out[0].astype(BF16)
    as_rows = lambda a: a.reshape(-1, ROW_SUBLANES, LANE)
    as_tiles = lambda a: a.reshape(-1, LANE)

    out = None
    nb = b // MOE_BATCH_RANGES
    for b0 in range(0, b, nb):
        x1, h2, route, counts = _mix_route(attn, lru, x, mod3, g1, w_gate, wa, wl, wo,
                                           norm2_g[0].reshape(1, d), wr, br, b0, nb)
        dest, n_slots, plan = _slot_plan(route, counts, nb * s)
        xp = as_tiles(_sc_scatter_rows(as_rows(h2), dest, n_slots))
        yp = _experts(xp, plan, w1[0], w3[0], w2[0])
        yg = as_tiles(_sc_gather_rows(as_rows(yp), dest))
        out = _combine(yg, route, x1, mod3, norm_f_g.reshape(1, d), b0, out)
    return out
```
